```python
import math
import jax
import jax.numpy as jnp
from jax import lax
import numpy as np

D_MODEL = 2048
BATCH = 2
SEQ = 8192
DEPTH = 1
DEC_BATCH = 32
DEC_SEQ = 4
PAST_LEN = 16384
PAGE_SIZE = 128

D_MIX = D_MODEL
D_ATTN = D_MIX // 2
D_SSM = D_MIX - D_ATTN
HEAD_DIM = 64
N_HEADS = D_ATTN // HEAD_DIM
SSM_CH = 16
SSM_GROUPS = D_SSM // SSM_CH
SSM_STATE = 64
DILATED = ((128, 1), (512, 4), (2048, 16))
W_MAX = 2048
BLK = 128
NUM_BUCKETS = 32
MAX_DISTANCE = W_MAX
ATTN_SCALE = HEAD_DIM ** -0.5
LN_EPS = 1e-5
DN_ALPHA = (2 * DEPTH) ** 0.25
DN_BETA = (8 * DEPTH) ** -0.25
N_PROJ = 4 * D_ATTN + 2 * D_SSM

kernel_name = 'hymba_longnet_s5_deepnorm_step'


def t5_bucket(dist):
    exact = NUM_BUCKETS // 2
    d_f = jnp.maximum(dist, 1).astype(jnp.float32)
    large = exact + (jnp.log(d_f / exact) / math.log(MAX_DISTANCE / exact)
                     * (NUM_BUCKETS - exact)).astype(jnp.int32)
    large = jnp.minimum(large, NUM_BUCKETS - 1)
    return jnp.where(dist < exact, dist, large)


def pattern_bias(rel_bias, dil, kper):
    dist = jnp.arange(kper + 1, dtype=jnp.int32) * dil
    return rel_bias[t5_bucket(dist)].astype(jnp.float32)


def band_dilated_attention(q, k, v, bias_k, dil, kper):
    b, l, h, e = q.shape
    span = BLK * dil
    lp = -(-l // span) * span
    nb = lp // span
    padw = ((0, 0), (0, lp - l), (0, 0), (0, 0))

    def blocks(t):
        return jnp.pad(t, padw).reshape(b, nb, BLK, dil, h, e)

    def with_prev(t):
        prev = jnp.pad(t, ((0, 0), (1, 0), (0, 0), (0, 0), (0, 0), (0, 0)))[:, :-1]
        return jnp.concatenate([prev, t], axis=2)

    qb = blocks(q)
    kb = with_prev(blocks(k))
    vb = with_prev(blocks(v))
    s = jnp.einsum('bnqrhe,bnkrhe->bnrhqk', qb, kb,
                   preferred_element_type=jnp.float32) * ATTN_SCALE
    rel = jnp.arange(BLK)[:, None] + BLK - jnp.arange(2 * BLK)[None, :]
    band = (rel >= 0) & (rel <= kper)
    in_cur = jnp.arange(2 * BLK) >= BLK
    valid = band[None] & ((jnp.arange(nb) > 0)[:, None, None] | in_cur[None, None, :])
    bias = jnp.transpose(bias_k[jnp.clip(rel, 0, kper)], (2, 0, 1))
    s = jnp.where(valid[None, :, None, None], s + bias, -jnp.inf)
    lse = jax.nn.logsumexp(s, axis=-1)
    p = jnp.exp(s - lse[..., None])
    o = jnp.einsum('bnrhqk,bnkrhe->bnqrhe', p, vb.astype(jnp.float32))
    o = o.reshape(b, lp, h, e)[:, :l]
    lse = jnp.transpose(lse, (0, 1, 4, 2, 3)).reshape(b, lp, h)[:, :l]
    return o, lse


def cached_dilated_attention(q, k_all, v_all, bias_k, dil, kper, buf):
    s_len = q.shape[1]
    idx = buf + jnp.arange(s_len)[:, None] - jnp.arange(kper + 1)[None, :] * dil
    valid = idx >= 0
    idx = jnp.maximum(idx, 0)
    kg = jnp.take(k_all, idx, axis=1)
    vg = jnp.take(v_all, idx, axis=1)
    s = jnp.einsum('bshe,bskhe->bhsk', q, kg, preferred_element_type=jnp.float32) * ATTN_SCALE
    s = jnp.where(valid[None, None], s + bias_k.T[None, :, None, :], -jnp.inf)
    lse = jax.nn.logsumexp(s, axis=-1)
    p = jnp.exp(s - lse[..., None])
    o = jnp.einsum('bhsk,bskhe->bshe', p, vg.astype(jnp.float32))
    return o, jnp.transpose(lse, (0, 2, 1))


def merge_by_denominator(outs, lses):
    w = jax.nn.softmax(jnp.stack(lses), axis=0)
    return jnp.sum(w[..., None] * jnp.stack(outs), axis=0)


def prompt_attention(q, k, v, rel_bias):
    outs, lses = [], []
    for win, dil in DILATED:
        kper = win // dil
        o, l = band_dilated_attention(q, k, v, pattern_bias(rel_bias, dil, kper), dil, kper)
        outs.append(o)
        lses.append(l)
    return merge_by_denominator(outs, lses)


def sample_attention(q, k, v, cache_k, cache_v, rel_bias):
    buf = cache_k.shape[1]
    k_all = jnp.concatenate([cache_k, k.astype(cache_k.dtype)], axis=1)
    v_all = jnp.concatenate([cache_v, v.astype(cache_v.dtype)], axis=1)
    outs, lses = [], []
    for win, dil in DILATED:
        kper = win // dil
        o, l = cached_dilated_attention(q, k_all, v_all, pattern_bias(rel_bias, dil, kper), dil, kper, buf)
        outs.append(o)
        lses.append(l)
    return merge_by_denominator(outs, lses)


def s5_ssm(u, x0_re, x0_im, lam_re, lam_im, log_dt, b_re, b_im, c_re, c_im, d_skip):
    bsz, l, _ = u.shape
    f32 = jnp.float32
    u = u.astype(f32).reshape(bsz, l, SSM_GROUPS, SSM_CH)
    lr = jnp.minimum(lam_re.astype(f32), -1e-4)
    li = lam_im.astype(f32)
    dt = jnp.exp(log_dt.astype(f32))[:, None]
    mag = jnp.exp(lr * dt)
    ab_re, ab_im = mag * jnp.cos(li * dt), mag * jnp.sin(li * dt)
    den = lr * lr + li * li
    inv_re, inv_im = lr / den, -li / den
    n_re, n_im = ab_re - 1.0, ab_im
    cf_re = n_re * inv_re - n_im * inv_im
    cf_im = n_re * inv_im + n_im * inv_re
    br, bi = b_re.astype(f32), b_im.astype(f32)
    bb_re = cf_re[..., None] * br - cf_im[..., None] * bi
    bb_im = cf_re[..., None] * bi + cf_im[..., None] * br
    bu_re = jnp.einsum('blgc,gnc->lbgn', u, bb_re)
    bu_im = jnp.einsum('blgc,gnc->lbgn', u, bb_im)
    bu_re = bu_re.at[0].add(ab_re * x0_re - ab_im * x0_im)
    bu_im = bu_im.at[0].add(ab_re * x0_im + ab_im * x0_re)
    a_re = jnp.broadcast_to(ab_re, (l, 1, SSM_GROUPS, SSM_STATE))
    a_im = jnp.broadcast_to(ab_im, (l, 1, SSM_GROUPS, SSM_STATE))

    def combine(e1, e2):
        a1r, a1i, b1r, b1i = e1
        a2r, a2i, b2r, b2i = e2
        return (a2r * a1r - a2i * a1i, a2r * a1i + a2i * a1r,
                a2r * b1r - a2i * b1i + b2r, a2r * b1i + a2i * b1r + b2i)

    _, _, xr, xi = lax.associative_scan(combine, (a_re, a_im, bu_re, bu_im), axis=0)
    y = (jnp.einsum('lbgn,gcn->blgc', xr, c_re.astype(f32))
         - jnp.einsum('lbgn,gcn->blgc', xi, c_im.astype(f32)))
    y = y + d_skip.astype(f32).reshape(SSM_GROUPS, SSM_CH) * u
    return y.reshape(bsz, l, D_SSM), xr[-1], xi[-1]


def layer_norm(x, g, b):
    xf = x.astype(jnp.float32)
    mu = jnp.mean(xf, axis=-1, keepdims=True)
    var = jnp.mean(jnp.square(xf - mu), axis=-1, keepdims=True)
    return ((xf - mu) * lax.rsqrt(var + LN_EPS) * g.astype(jnp.float32)
            + b.astype(jnp.float32)).astype(x.dtype)


def mixer_layer(x, attend, x0_re, x0_im, w_in, w_out, b_out, lam_re, lam_im, log_dt,
                b_re, b_im, c_re, c_im, d_skip, w_glu, b_glu, ln_g, ln_b):
    bsz, l, _ = x.shape
    h = jnp.einsum('bld,df->blf', x, w_in)
    q, k, v, g_attn, u, g_ssm = jnp.split(
        h, [D_ATTN, 2 * D_ATTN, 3 * D_ATTN, 4 * D_ATTN, 4 * D_ATTN + D_SSM], axis=-1)
    heads = lambda t: t.reshape(bsz, l, N_HEADS, HEAD_DIM)
    k_h, v_h = heads(k), heads(v)
    attn = attend(heads(q), k_h, v_h).reshape(bsz, l, D_ATTN).astype(x.dtype)
    y, s_re, s_im = s5_ssm(u, x0_re, x0_im, lam_re, lam_im, log_dt, b_re, b_im, c_re, c_im, d_skip)
    z = jax.nn.gelu(y)
    z = (z * jax.nn.sigmoid(z @ w_glu.astype(jnp.float32) + b_glu.astype(jnp.float32))).astype(x.dtype)
    branches = jnp.concatenate([attn * jax.nn.silu(g_attn), z * jax.nn.silu(g_ssm)], axis=-1)
    mix = jnp.einsum('blf,fd->bld', branches, w_out) + b_out
    out = layer_norm(DN_ALPHA * x + mix, ln_g, ln_b)
    return out, k_h, v_h, s_re, s_im


def setup_inputs(seed: int = 0) -> dict:
    key = jax.random.key(seed)
    ks = jax.random.split(key, 24)
    nrm = lambda k, shape, s: jax.random.normal(k, shape, jnp.float32) * s
    cache_len = min(W_MAX, PAST_LEN)
    g, n = SSM_GROUPS, SSM_STATE
    lam_im0 = jnp.pi * jnp.arange(n, dtype=jnp.float32)
    return {
        'x_prompt': nrm(ks[0], (BATCH, SEQ, D_MODEL), 1.0),
        'x_sample': nrm(ks[1], (DEC_BATCH, DEC_SEQ, D_MODEL), 1.0),
        'cache_k': nrm(ks[2], (DEPTH, DEC_BATCH, cache_len, N_HEADS, HEAD_DIM), 1.0),
        'cache_v': nrm(ks[3], (DEPTH, DEC_BATCH, cache_len, N_HEADS, HEAD_DIM), 1.0),
        'state_ssm_re': nrm(ks[4], (DEPTH, DEC_BATCH, g, n), 0.1),
        'state_ssm_im': nrm(ks[5], (DEPTH, DEC_BATCH, g, n), 0.1),
        'w_in': nrm(ks[6], (DEPTH, D_MODEL, N_PROJ), D_MODEL ** -0.5),
        'w_out': nrm(ks[7], (DEPTH, D_MIX, D_MODEL), D_MIX ** -0.5 * DN_BETA),
        'b_out': nrm(ks[8], (DEPTH, D_MODEL), 0.01),
        'rel_bias': nrm(ks[9], (NUM_BUCKETS, N_HEADS), 0.5),
        'lam_re': -0.5 + nrm(ks[10], (DEPTH, g, n), 0.01),
        'lam_im': lam_im0 + nrm(ks[11], (DEPTH, g, n), 0.01),
        'log_dt': jax.random.uniform(ks[12], (DEPTH, g), jnp.float32,
                                     minval=math.log(1e-3), maxval=math.log(1e-1)),
        'b_re': nrm(ks[13], (DEPTH, g, n, SSM_CH), (2 * SSM_CH) ** -0.5),
        'b_im': nrm(ks[14], (DEPTH, g, n, SSM_CH), (2 * SSM_CH) ** -0.5),
        'c_re': nrm(ks[15], (DEPTH, g, SSM_CH, n), n ** -0.5),
        'c_im': nrm(ks[16], (DEPTH, g, SSM_CH, n), n ** -0.5),
        'd_skip': nrm(ks[17], (DEPTH, D_SSM), 1.0),
        'w_glu': nrm(ks[18], (DEPTH, D_SSM, D_SSM), D_SSM ** -0.5),
        'b_glu': nrm(ks[19], (DEPTH, D_SSM), 0.01),
        'ln_g': 1.0 + nrm(ks[20], (DEPTH, D_MODEL), 0.01),
        'ln_b': nrm(ks[21], (DEPTH, D_MODEL), 0.01),
    }


def reference(x_prompt, x_sample, cache_k, cache_v, state_ssm_re, state_ssm_im, w_in, w_out, b_out,
              rel_bias, lam_re, lam_im, log_dt, b_re, b_im, c_re, c_im, d_skip, w_glu, b_glu, ln_g, ln_b):
    xp, xs = x_prompt, x_sample
    keep = min(W_MAX, x_prompt.shape[1])
    kp_l, vp_l, srp_l, sip_l = [], [], [], []
    ks_l, vs_l, srs_l, sis_l = [], [], [], []
    for layer in range(DEPTH):
        params = (w_in[layer], w_out[layer], b_out[layer], lam_re[layer], lam_im[layer], log_dt[layer],
                  b_re[layer], b_im[layer], c_re[layer], c_im[layer], d_skip[layer], w_glu[layer],
                  b_glu[layer], ln_g[layer], ln_b[layer])
        zeros = jnp.zeros((xp.shape[0], SSM_GROUPS, SSM_STATE), jnp.float32)
        xp, kp, vp, srp, sip = mixer_layer(
            xp, lambda q, k, v: prompt_attention(q, k, v, rel_bias), zeros, zeros, *params)
        ck, cv = cache_k[layer], cache_v[layer]
        xs, kn, vn, srs, sis = mixer_layer(
            xs, lambda q, k, v, ck=ck, cv=cv: sample_attention(q, k, v, ck, cv, rel_bias),
            state_ssm_re[layer].astype(jnp.float32), state_ssm_im[layer].astype(jnp.float32), *params)
        kp_l.append(kp[:, -keep:])
        vp_l.append(vp[:, -keep:])
        srp_l.append(srp)
        sip_l.append(sip)
        ks_l.append(kn)
        vs_l.append(vn)
        srs_l.append(srs)
        sis_l.append(sis)
    return (xp, xs, jnp.stack(kp_l), jnp.stack(vp_l), jnp.stack(srp_l), jnp.stack(sip_l),
            jnp.stack(ks_l), jnp.stack(vs_l), jnp.stack(srs_l), jnp.stack(sis_l))
```

```python
import functools
import math

import jax
import jax.numpy as jnp
import numpy as np
from jax import lax
from jax.experimental import pallas as pl
from jax.experimental.pallas import tpu as pltpu

F32 = jnp.float32
BF16 = jnp.bfloat16

HEAD_DIM = 64
SSM_CH = 16
SSM_STATE = 64
NUM_BUCKETS = 32
MAX_DISTANCE = 2048
KPER = 128
BLK = 128
DILATIONS = (16, 4, 1)
LN_EPS = 1e-5
NEG = -1e30

LANES = 128
SUBLANES = 8
PLANES = 16
GROUPS_PER_TILE = LANES // SSM_CH
STATE_HALF = GROUPS_PER_TILE * SSM_STATE
VMEM_LIMIT = 56 * 1024 * 1024


def _cparams(sem, vmem=VMEM_LIMIT):
    return pltpu.CompilerParams(dimension_semantics=sem, vmem_limit_bytes=vmem)


def _bucket_np(dist):
    exact = NUM_BUCKETS // 2
    d_f = np.maximum(dist, 1).astype(np.float32)
    large = exact + (np.log(d_f / np.float32(exact)) / np.float32(math.log(MAX_DISTANCE / exact))
                     * np.float32(NUM_BUCKETS - exact)).astype(np.int32)
    large = np.minimum(large, NUM_BUCKETS - 1)
    return np.where(dist < exact, dist, large).astype(np.int32)


def _prompt_rel(dil, first):
    i = np.arange(BLK)[:, None]
    j = np.arange(2 * BLK)[None, :]
    npl = PLANES // dil
    qrows = BLK // npl
    pq, ml = i // qrows, i % qrows
    pk, jl = j // (2 * qrows), j % (2 * qrows)
    back = 0 if first else qrows
    return npl * (ml - jl + back) + (pq - pk)


def _prompt_bucket_tables():
    tabs = []
    for dil in DILATIONS:
        for first in (False, True):
            rel = _prompt_rel(dil, first)
            valid = (rel >= 0) & (rel <= KPER)
            tabs.append(np.where(valid, _bucket_np(np.clip(rel, 0, KPER) * dil), NUM_BUCKETS))
    return np.stack(tabs).reshape(len(DILATIONS) * 2, BLK * 2 * BLK).astype(np.int32)


def _sample_key_positions(buf, s_len, n_far, tail):
    far = (np.arange(s_len)[:, None] + PLANES * np.arange(n_far)[None, :]).reshape(-1)
    pos = np.concatenate([far, buf - tail + np.arange(tail), buf + np.arange(s_len)])
    pad = (-len(pos)) % SUBLANES
    return np.concatenate([pos, np.full(pad, -1)])


def _sample_bucket_tables(buf, s_len, n_far, tail, width):
    pos = _sample_key_positions(buf, s_len, n_far, tail)
    tabs = np.full((len(DILATIONS), s_len, width), NUM_BUCKETS, np.int32)
    for a, dil in enumerate(DILATIONS):
        for s in range(s_len):
            dist = buf + s - pos
            valid = (pos >= 0) & (dist >= 0) & (dist % dil == 0) & (dist // dil <= KPER)
            tabs[a, s, :len(pos)] = np.where(valid, _bucket_np(np.maximum(dist, 0)), NUM_BUCKETS)
    return tabs.reshape(1, -1), len(pos)


def _bias_kernel(idx_ref, rbt_ref, o_ref):
    idx = idx_ref[...]
    onehot = (lax.broadcasted_iota(jnp.int32, (NUM_BUCKETS, idx.shape[1]), 0) == idx).astype(F32)
    tab = jnp.dot(rbt_ref[...], onehot, precision=lax.Precision.HIGHEST,
                  preferred_element_type=F32)
    o_ref[...] = jnp.where(idx < NUM_BUCKETS, tab, NEG)


def _bias_tables(idx, rel_bias_t, chunk):
    n_heads = rel_bias_t.shape[0]
    total = idx.shape[1]
    return pl.pallas_call(
        _bias_kernel,
        grid=(total // chunk,),
        in_specs=[pl.BlockSpec((1, chunk), lambda c: (0, c)),
                  pl.BlockSpec((n_heads, NUM_BUCKETS), lambda c: (0, 0))],
        out_specs=pl.BlockSpec((n_heads, chunk), lambda c: (0, c)),
        out_shape=jax.ShapeDtypeStruct((n_heads, total), F32),
        compiler_params=_cparams(("arbitrary",)),
        name="bias_tables",
    )(idx, rel_bias_t)


def _discretize(lr, li, ldt):
    lr = jnp.minimum(lr, -1e-4)
    dt = jnp.exp(ldt)
    mag = jnp.exp(lr * dt)
    ab_re, ab_im = mag * jnp.cos(li * dt), mag * jnp.sin(li * dt)
    den = lr * lr + li * li
    inv_re, inv_im = lr / den, -li / den
    n_re, n_im = ab_re - 1.0, ab_im
    cf_re = n_re * inv_re - n_im * inv_im
    cf_im = n_re * inv_im + n_im * inv_re
    return ab_re, ab_im, cf_re, cf_im


def _ssm_prep_kernel(lam_row_ref, lam_col_ref, bt_re_ref, bt_im_ref, ct_re_ref, ct_im_ref,
                     ws_ref, wx_ref, m_ref, c0_ref, coef_ref):
    row = lam_row_ref[0]
    ab_re, ab_im, cf_re, cf_im = _discretize(row[0:1], row[1:2], row[2:3])
    bt_re, bt_im = bt_re_ref[0], bt_im_ref[0]
    bb_re = cf_re * bt_re - cf_im * bt_im
    bb_im = cf_re * bt_im + cf_im * bt_re
    col = lam_col_ref[0]
    abc_re, abc_im, _, _ = _discretize(col[:, 0:1], col[:, 1:2], col[:, 2:3])
    ct_re, ct_im = ct_re_ref[0], ct_im_ref[0]
    c0 = jnp.concatenate([ct_re, -ct_im], axis=0)
    c0_ref[0] = c0.astype(BF16)

    pr, pi = jnp.ones_like(ab_re), jnp.zeros_like(ab_re)
    qr, qi = abc_re, abc_im
    zero_blk = jnp.zeros((LANES, LANES), BF16)
    for lag in range(PLANES):
        w = jnp.concatenate([pr * bb_re - pi * bb_im, pr * bb_im + pi * bb_re], axis=1)
        s = PLANES - 1 - lag
        ws_ref[0, s * LANES:(s + 1) * LANES, :] = w.astype(BF16)
        m_lag = jnp.dot(w, c0, precision=lax.Precision.HIGHEST, preferred_element_type=F32).astype(BF16)
        for t in range(PLANES):
            src = t - lag
            if src >= 0:
                m_ref[0, src * LANES:(src + 1) * LANES, t * LANES:(t + 1) * LANES] = m_lag
            else:
                m_ref[0, (PLANES + src) * LANES:(PLANES + src + 1) * LANES, t * LANES:(t + 1) * LANES] = zero_blk
        wx_ref[0, 0:STATE_HALF, lag * LANES:(lag + 1) * LANES] = (ct_re * qr - ct_im * qi).astype(BF16)
        wx_ref[0, STATE_HALF:2 * STATE_HALF, lag * LANES:(lag + 1) * LANES] = (-(ct_re * qi + ct_im * qr)).astype(BF16)
        pr, pi = pr * ab_re - pi * ab_im, pr * ab_im + pi * ab_re
        qr, qi = qr * abc_re - qi * abc_im, qr * abc_im + qi * abc_re
    coef_ref[0] = jnp.concatenate([
        jnp.concatenate([ab_re, ab_im], axis=1),
        jnp.concatenate([pr, pi], axis=1),
        jnp.zeros((SUBLANES - 2, 2 * STATE_HALF), F32)], axis=0)


def _ssm_prep(lam_re, lam_im, log_dt, b_re, b_im, c_re, c_im):
    groups, n = lam_re.shape
    tiles = groups // GROUPS_PER_TILE
    gpt = GROUPS_PER_TILE
    eye = jnp.eye(gpt, dtype=F32)

    def rows(v):
        return v.reshape(tiles, gpt * n)

    ldt = jnp.broadcast_to(log_dt[:, None], (groups, n))
    lam_row = jnp.stack([rows(lam_re), rows(lam_im), rows(ldt)], axis=1)
    lam_col = jnp.transpose(lam_row, (0, 2, 1))

    def bt(b):
        b = jnp.transpose(b.reshape(tiles, gpt, n, SSM_CH), (0, 1, 3, 2))
        return (b[:, :, :, None, :] * eye[None, :, None, :, None]).reshape(tiles, gpt * SSM_CH, gpt * n)

    def ct(c):
        c = jnp.transpose(c.reshape(tiles, gpt, SSM_CH, n), (0, 1, 3, 2))
        return (c[:, :, :, None, :] * eye[None, :, None, :, None]).reshape(tiles, gpt * n, gpt * SSM_CH)

    sh, ln = STATE_HALF, LANES
    tile3 = lambda a, b: pl.BlockSpec((1, a, b), lambda t: (t, 0, 0))
    return pl.pallas_call(
        _ssm_prep_kernel,
        grid=(tiles,),
        in_specs=[tile3(3, sh), tile3(sh, 3), tile3(ln, sh), tile3(ln, sh), tile3(sh, ln), tile3(sh, ln)],
        out_specs=[tile3(PLANES * ln, 2 * sh), tile3(2 * sh, PLANES * ln), tile3(PLANES * ln, PLANES * ln),
                   tile3(2 * sh, ln), tile3(SUBLANES, 2 * sh)],
        out_shape=[jax.ShapeDtypeStruct((tiles, PLANES * ln, 2 * sh), BF16),
                   jax.ShapeDtypeStruct((tiles, 2 * sh, PLANES * ln), BF16),
                   jax.ShapeDtypeStruct((tiles, PLANES * ln, PLANES * ln), BF16),
                   jax.ShapeDtypeStruct((tiles, 2 * sh, ln), BF16),
                   jax.ShapeDtypeStruct((tiles, SUBLANES, 2 * sh), F32)],
        compiler_params=_cparams(("arbitrary",)),
        name="ssm_prep",
    )(lam_row, lam_col, bt(b_re), bt(b_im), ct(c_re), ct(c_im))


def _inproj_kernel(keep_rows, scale, x_ref, w_ref, q_ref, k_ref, v_ref, ga_ref, u_ref, gs_ref,
                   kl_ref, vl_ref, xb_ref):
    j = pl.program_id(2)

    @pl.when(j == 0)
    def _():
        xb_ref[...] = x_ref[...].astype(BF16)

    acc = jnp.dot(xb_ref[...], w_ref[...], preferred_element_type=F32)

    @pl.when(j == 0)
    def _():
        q_ref[...] = acc * scale

    @pl.when(j == 1)
    def _():
        k_ref[...] = acc
        kl_ref[...] = acc[acc.shape[0] - keep_rows:, :]

    @pl.when(j == 2)
    def _():
        v_ref[...] = acc
        vl_ref[...] = acc[acc.shape[0] - keep_rows:, :]

    @pl.when(j == 3)
    def _():
        ga_ref[...] = acc.astype(BF16)

    @pl.when(j == 4)
    def _():
        u_ref[...] = acc.astype(BF16)

    @pl.when(j == 5)
    def _():
        gs_ref[...] = acc.astype(BF16)


def _inproj_prompt(x, w_bf, keep):
    bsz, seq, d_model = x.shape
    width = w_bf.shape[1] // 6
    rows = seq // PLANES
    keep_rows = keep // PLANES
    xv = x.reshape(bsz, rows, PLANES * d_model)
    plane = lambda dt: jax.ShapeDtypeStruct((bsz, PLANES, rows, width), dt)
    plane_spec = pl.BlockSpec((None, None, rows, width), lambda b, r, j: (b, r, 0, 0))
    last = jax.ShapeDtypeStruct((bsz, keep_rows, PLANES * width), F32)
    last_spec = pl.BlockSpec((None, keep_rows, width), lambda b, r, j: (b, 0, r))
    return pl.pallas_call(
        functools.partial(_inproj_kernel, keep_rows, HEAD_DIM ** -0.5),
        grid=(bsz, PLANES, 6),
        in_specs=[pl.BlockSpec((None, rows, d_model), lambda b, r, j: (b, 0, r)),
                  pl.BlockSpec((d_model, width), lambda b, r, j: (0, j))],
        out_specs=[plane_spec] * 6 + [last_spec] * 2,
        out_shape=[plane(F32), plane(F32), plane(F32), plane(BF16), plane(BF16), plane(BF16), last, last],
        scratch_shapes=[pltpu.VMEM((rows, d_model), BF16)],
        compiler_params=_cparams(("arbitrary", "arbitrary", "arbitrary")),
        name="inproj_prompt",
    )(xv, w_bf)


def _attn_kernel(q_ref, k_ref, v_ref, bias_ref, o_ref, acc_ref, m_ref, l_ref):
    sup = pl.program_id(2)
    is_a = lax.broadcasted_iota(jnp.int32, (BLK, LANES), 1) < HEAD_DIM
    base = pl.multiple_of(sup * BLK, BLK)

    def scores(q, kb, table, head):
        qm = jnp.where(is_a if head == 0 else jnp.logical_not(is_a), q, 0.0).astype(BF16)
        s = lax.dot_general(qm, kb, (((1,), (1,)), ((), ())), preferred_element_type=F32)
        return s + bias_ref[head, table]

    def both(a, b):
        return jnp.where(is_a, a, b)

    def block(q, k, v, table, state):
        kb, vb = k.astype(BF16), v.astype(BF16)
        s = [scores(q, kb, table, h) for h in range(2)]
        mb = both(jnp.max(s[0], axis=1, keepdims=True), jnp.max(s[1], axis=1, keepdims=True))
        m_new = mb if state is None else jnp.maximum(state[0], mb)
        ls, os_ = [], []
        for h in range(2):
            p = jnp.exp(s[h] - m_new[:, h * HEAD_DIM:h * HEAD_DIM + 1])
            ls.append(jnp.sum(p, axis=1, keepdims=True))
            os_.append(jnp.dot(p.astype(BF16), vb, preferred_element_type=F32))
        l_new, acc_new = both(ls[0], ls[1]), both(os_[0], os_[1])
        if state is not None:
            alpha = jnp.exp(state[0] - m_new)
            l_new = alpha * state[1] + l_new
            acc_new = alpha * state[2] + acc_new
        return m_new, l_new, acc_new

    def run_pattern(a, dil, nblocks):
        npl = PLANES // dil
        qrows = BLK // npl
        per_res = BLK // qrows

        def body(blk, carry):
            res, sub = blk // per_res, blk % per_res
            off = pl.multiple_of(sub * qrows, qrows)
            kstart = pl.multiple_of(jnp.maximum(base + off - qrows, 0), qrows)
            first = jnp.logical_and(sup == 0, sub == 0).astype(jnp.int32)
            planes = [res + dil * i for i in range(npl)]
            cat = lambda parts: parts[0] if len(parts) == 1 else jnp.concatenate(parts, axis=0)
            q = cat([q_ref[pln, pl.ds(base + off, qrows), :] for pln in planes])
            k = cat([k_ref[pln, pl.ds(kstart, 2 * qrows), :] for pln in planes])
            v = cat([v_ref[pln, pl.ds(kstart, 2 * qrows), :] for pln in planes])
            if a == 0:
                state = None
            else:
                state = tuple(cat([ref[pln, pl.ds(off, qrows), :] for pln in planes])
                              for ref in (m_ref, l_ref, acc_ref))
            new = block(q, k, v, 2 * a + first, state)
            for ref, val in zip((m_ref, l_ref, acc_ref), new):
                for i, pln in enumerate(planes):
                    ref[pln, pl.ds(off, qrows), :] = val[i * qrows:(i + 1) * qrows]
            return carry

        lax.fori_loop(0, nblocks, body, 0)

    for a, dil in enumerate(DILATIONS):
        run_pattern(a, dil, PLANES)
    for pln in range(PLANES):
        o_ref[pln] = acc_ref[pln] / l_ref[pln]


def _attention_prompt(q, k, v, bias):
    bsz, _, rows, width = q.shape
    pairs = width // LANES
    ntab = bias.shape[1]
    qkv_spec = pl.BlockSpec((None, PLANES, rows, LANES), lambda b, h, s: (b, 0, 0, h))
    return pl.pallas_call(
        _attn_kernel,
        grid=(bsz, pairs, rows // BLK),
        in_specs=[qkv_spec, qkv_spec, qkv_spec,
                  pl.BlockSpec((2, ntab, BLK, 2 * BLK), lambda b, h, s: (h, 0, 0, 0))],
        out_specs=pl.BlockSpec((None, PLANES, BLK, LANES), lambda b, h, s: (b, 0, s, h)),
        out_shape=jax.ShapeDtypeStruct((bsz, PLANES, rows, width), F32),
        scratch_shapes=[pltpu.VMEM((PLANES, BLK, LANES), F32)] * 3,
        compiler_params=_cparams(("arbitrary", "arbitrary", "arbitrary")),
        name="attention_prompt",
    )(q, k, v, bias)


def _ssm_kernel(u_ref, ws_ref, wx_ref, m_ref, coef_ref, d_ref, y_ref, sre_ref, sim_ref,
                ucat_ref, s_ref, x_ref):
    tc = pl.program_id(2)
    chunks = u_ref.shape[1]
    per_step = y_ref.shape[0]

    @pl.when(tc == 0)
    def _():
        for s in range(PLANES):
            ucat_ref[:, s * LANES:(s + 1) * LANES] = u_ref[s]
        s_ref[...] = jnp.dot(ucat_ref[...], ws_ref[...], preferred_element_type=F32)
        a_re = coef_ref[1:2, 0:STATE_HALF]
        a_im = coef_ref[1:2, STATE_HALF:2 * STATE_HALF]

        def step(c, carry):
            xr, xi = carry
            x_ref[pl.ds(c, 1), 0:STATE_HALF] = xr
            x_ref[pl.ds(c, 1), STATE_HALF:2 * STATE_HALF] = xi
            sr = s_ref[pl.ds(c, 1), 0:STATE_HALF]
            si = s_ref[pl.ds(c, 1), STATE_HALF:2 * STATE_HALF]
            return a_re * xr - a_im * xi + sr, a_re * xi + a_im * xr + si

        zero = jnp.zeros((1, STATE_HALF), F32)
        xr, xi = lax.fori_loop(0, chunks, step, (zero, zero))
        sre_ref[...] = xr
        sim_ref[...] = xi

    y = jnp.dot(ucat_ref[...], m_ref[...], preferred_element_type=F32)
    y = y + jnp.dot(x_ref[...].astype(BF16), wx_ref[...], preferred_element_type=F32)
    d = d_ref[...]
    for t in range(per_step):
        u_t = u_ref[tc * per_step + t].astype(F32)
        y_ref[t] = (y[:, t * LANES:(t + 1) * LANES] + d * u_t).astype(BF16)


def _ssm_prompt(u, ws, wx, m, coef, d_skip):
    bsz, _, chunks, width = u.shape
    tiles = width // LANES
    per_step = 4
    steps = PLANES // per_step
    sh = STATE_HALF
    state = jax.ShapeDtypeStruct((bsz, tiles, 1, sh), F32)
    state_spec = pl.BlockSpec((None, None, 1, sh), lambda t, b, c: (b, t, 0, 0))
    return pl.pallas_call(
        _ssm_kernel,
        grid=(tiles, bsz, steps),
        in_specs=[pl.BlockSpec((None, PLANES, chunks, LANES), lambda t, b, c: (b, 0, 0, t)),
                  pl.BlockSpec((None, PLANES * LANES, 2 * sh), lambda t, b, c: (t, 0, 0)),
                  pl.BlockSpec((None, 2 * sh, per_step * LANES), lambda t, b, c: (t, 0, c)),
                  pl.BlockSpec((None, PLANES * LANES, per_step * LANES), lambda t, b, c: (t, 0, c)),
                  pl.BlockSpec((None, SUBLANES, 2 * sh), lambda t, b, c: (t, 0, 0)),
                  pl.BlockSpec((1, LANES), lambda t, b, c: (0, t))],
        out_specs=[pl.BlockSpec((None, per_step, chunks, LANES), lambda t, b, c: (b, c, 0, t)),
                   state_spec, state_spec],
        out_shape=[jax.ShapeDtypeStruct((bsz, PLANES, chunks, width), BF16), state, state],
        scratch_shapes=[pltpu.VMEM((chunks, PLANES * LANES), BF16),
                        pltpu.VMEM((chunks, 2 * sh), F32),
                        pltpu.VMEM((chunks, 2 * sh), F32)],
        compiler_params=_cparams(("arbitrary", "arbitrary", "arbitrary")),
        name="ssm_prompt",
    )(u, ws, wx, m, coef, d_skip)


def _epilogue_kernel(alpha, attn_ref, ga_ref, y_ref, gs_ref, x_ref, wglu_ref, bglu_ref, wo_ref, bo_ref,
                     g_ref, b_ref, o_ref):
    d_attn = attn_ref.shape[-1]
    br_a = (attn_ref[...].astype(F32) * jax.nn.silu(ga_ref[...].astype(F32))).astype(BF16)
    z = jax.nn.gelu(y_ref[...].astype(F32))
    gate = jax.nn.sigmoid(jnp.dot(z.astype(BF16), wglu_ref[...], preferred_element_type=F32) + bglu_ref[...])
    br_s = (z * gate * jax.nn.silu(gs_ref[...].astype(F32))).astype(BF16)
    mix = (jnp.dot(br_a, wo_ref[0:d_attn, :], preferred_element_type=F32)
           + jnp.dot(br_s, wo_ref[d_attn:, :], preferred_element_type=F32) + bo_ref[...])
    t = alpha * x_ref[...] + mix
    mu = jnp.mean(t, axis=-1, keepdims=True)
    var = jnp.mean(jnp.square(t - mu), axis=-1, keepdims=True)
    o_ref[...] = (t - mu) * lax.rsqrt(var + LN_EPS) * g_ref[...] + b_ref[...]


def _epilogue_weights_specs(d_mix, d_ssm, d_model, imap):
    return [pl.BlockSpec((d_ssm, d_ssm), imap), pl.BlockSpec((1, d_ssm), imap),
            pl.BlockSpec((d_mix, d_model), imap), pl.BlockSpec((1, d_model), imap),
            pl.BlockSpec((1, d_model), imap), pl.BlockSpec((1, d_model), imap)]


def _epilogue_prompt(alpha, attn, ga, y, gs, x, weights):
    bsz, _, rows, width = attn.shape
    d_model = x.shape[-1]
    tile = 256
    xv = x.reshape(bsz, rows, PLANES * d_model)
    pspec = pl.BlockSpec((None, None, tile, width), lambda b, r, h: (b, r, h, 0))
    xspec = pl.BlockSpec((None, tile, d_model), lambda b, r, h: (b, h, r))
    out = pl.pallas_call(
        functools.partial(_epilogue_kernel, alpha),
        grid=(bsz, PLANES, rows // tile),
        in_specs=[pspec, pspec, pspec, pspec, xspec]
        + _epilogue_weights_specs(2 * width, width, d_model, lambda b, r, h: (0, 0)),
        out_specs=xspec,
        out_shape=jax.ShapeDtypeStruct(xv.shape, F32),
        compiler_params=_cparams(("arbitrary", "arbitrary", "arbitrary")),
        name="epilogue_prompt",
    )(attn, ga, y, gs, xv, *weights)
    return out.reshape(x.shape)


def _epilogue_sample(alpha, attn, h, y, x, weights):
    rows, width = attn.shape
    d_model = x.shape[-1]
    col = lambda c: pl.BlockSpec((rows, width), lambda i: (0, c))
    return pl.pallas_call(
        functools.partial(_epilogue_kernel, alpha),
        grid=(1,),
        in_specs=[col(0), col(3), col(0), col(5), pl.BlockSpec((rows, d_model), lambda i: (0, 0))]
        + _epilogue_weights_specs(2 * width, width, d_model, lambda i: (0, 0)),
        out_specs=pl.BlockSpec((rows, d_model), lambda i: (0, 0)),
        out_shape=jax.ShapeDtypeStruct((rows, d_model), F32),
        compiler_params=_cparams(("arbitrary",)),
        name="epilogue_sample",
    )(attn, h, y, h, x, *weights)


def _inproj_sample_kernel(scale, x_ref, w_ref, o_ref):
    acc = jnp.dot(x_ref[...].astype(BF16), w_ref[...], preferred_element_type=F32)
    o_ref[...] = acc * jnp.where(pl.program_id(0) == 0, scale, 1.0)


def _inproj_sample(x, w_bf):
    rows, d_model = x.shape
    width = w_bf.shape[1] // 6
    return pl.pallas_call(
        functools.partial(_inproj_sample_kernel, HEAD_DIM ** -0.5),
        grid=(6,),
        in_specs=[pl.BlockSpec((rows, d_model), lambda j: (0, 0)),
                  pl.BlockSpec((d_model, width), lambda j: (0, j))],
        out_specs=pl.BlockSpec((rows, width), lambda j: (0, j)),
        out_shape=jax.ShapeDtypeStruct((rows, 6 * width), F32),
        compiler_params=_cparams(("arbitrary",)),
        name="inproj_sample",
    )(x, w_bf)


def _attn_sample_kernel(n_keys, q_ref, kn_ref, vn_ref, kf_ref, kt_ref, vf_ref, vt_ref, tb_ref, o_ref):
    s_len, width = q_ref.shape
    heads = width // HEAD_DIM
    pad = n_keys - kf_ref.shape[0] * s_len - kt_ref.shape[0] - s_len

    def keys(far_ref, tail_ref, new_ref):
        parts = [far_ref[:, s * width:(s + 1) * width] for s in range(s_len)]
        parts += [tail_ref[...], new_ref[...]]
        if pad:
            parts.append(jnp.zeros((pad, width), F32))
        return jnp.concatenate(parts, axis=0).astype(BF16)

    head_of_lane = lax.broadcasted_iota(jnp.int32, (heads, width), 1) // HEAD_DIM
    own = head_of_lane == lax.broadcasted_iota(jnp.int32, (heads, width), 0)
    q = q_ref[...]
    wq = jnp.concatenate([jnp.where(own, q[s:s + 1, :], 0.0) for s in range(s_len)], axis=0).astype(BF16)
    st = lax.dot_general(keys(kf_ref, kt_ref, kn_ref), wq, (((1,), (1,)), ((), ())),
                         preferred_element_type=F32)
    z = [st + tb_ref[a] for a in range(len(DILATIONS))]
    top = functools.reduce(jnp.maximum, [jnp.max(t, axis=0, keepdims=True) for t in z])
    wgt = functools.reduce(jnp.add, [jnp.exp(t - top) for t in z]).astype(BF16)
    vall = keys(vf_ref, vt_ref, vn_ref)
    contract0 = (((0,), (0,)), ((), ()))
    o = lax.dot_general(wgt, vall, contract0, preferred_element_type=F32)
    den = lax.dot_general(wgt, jnp.ones((n_keys, LANES), BF16), contract0, preferred_element_type=F32)
    o = o / den[:, 0:1]
    for s in range(s_len):
        o_ref[s:s + 1, :] = jnp.sum(jnp.where(own, o[s * heads:(s + 1) * heads, :], 0.0), axis=0, keepdims=True)


def _attention_sample(h, cache_k, cache_v, tb, n_keys, n_far, tail):
    bsz, s_len, _ = h.shape
    buf, width = cache_k.shape[1], cache_k.shape[2]
    new = lambda c: pl.BlockSpec((None, s_len, width), lambda b: (b, 0, c))
    far = pl.BlockSpec((None, n_far, s_len * width), lambda b: (b, 0, 0))
    tl = pl.BlockSpec((None, tail, width), lambda b: (b, buf // tail - 1, 0))
    ckv = cache_k.reshape(bsz, buf // PLANES, PLANES * width)
    cvv = cache_v.reshape(bsz, buf // PLANES, PLANES * width)
    return pl.pallas_call(
        functools.partial(_attn_sample_kernel, n_keys),
        grid=(bsz,),
        in_specs=[new(0), new(1), new(2), far, tl, far, tl,
                  pl.BlockSpec(tb.shape, lambda b: (0, 0, 0))],
        out_specs=pl.BlockSpec((None, s_len, width), lambda b: (b, 0, 0)),
        out_shape=jax.ShapeDtypeStruct((bsz, s_len, width), F32),
        compiler_params=_cparams(("arbitrary",)),
        name="attention_sample",
    )(h, h, h, ckv, cache_k, cvv, cache_v, tb)


def _ssm_sample_kernel(s_len, u_ref, b0_ref, c0_ref, coef_ref, d_ref, x0r_ref, x0i_ref,
                       y_ref, sre_ref, sim_ref, bu_ref, xs_ref):
    bsz = x0r_ref.shape[0]
    bu = jnp.dot(u_ref[...].astype(BF16), b0_ref[...], preferred_element_type=F32)
    slabs = bu_ref.shape[0]
    for c in range(slabs):
        bu_ref[c] = bu[:, c * LANES:(c + 1) * LANES]
    a_re = coef_ref[0:1, 0:STATE_HALF]
    a_im = coef_ref[0:1, STATE_HALF:2 * STATE_HALF]
    half = slabs // 2
    xr, xi = x0r_ref[...], x0i_ref[...]
    for s in range(s_len):
        step = lambda c: bu_ref[c, pl.ds(s, bsz, stride=s_len), :]
        br = jnp.concatenate([step(c) for c in range(half)], axis=1)
        bi = jnp.concatenate([step(c) for c in range(half, slabs)], axis=1)
        xr, xi = a_re * xr - a_im * xi + br, a_re * xi + a_im * xr + bi
        xs_ref[s * bsz:(s + 1) * bsz, :] = jnp.concatenate([xr, xi], axis=1)
    sre_ref[...] = xr
    sim_ref[...] = xi
    y = jnp.dot(xs_ref[...].astype(BF16), c0_ref[...], preferred_element_type=F32)
    d = d_ref[...]
    for s in range(s_len):
        y_ref[pl.ds(s, bsz, stride=s_len), :] = (y[s * bsz:(s + 1) * bsz, :]
                                                 + d * u_ref[pl.ds(s, bsz, stride=s_len), :])


def _ssm_sample(h, ws, c0, coef, d_skip, x0_re, x0_im, s_len):
    rows = h.shape[0]
    bsz = rows // s_len
    tiles = ws.shape[0]
    sh = STATE_HALF
    u_col0 = 4 * (h.shape[1] // 6) // LANES
    st_spec = pl.BlockSpec((bsz, sh), lambda t: (0, t))
    state = jax.ShapeDtypeStruct((bsz, tiles * sh), F32)
    return pl.pallas_call(
        functools.partial(_ssm_sample_kernel, s_len),
        grid=(tiles,),
        in_specs=[pl.BlockSpec((rows, LANES), lambda t: (0, u_col0 + t)),
                  pl.BlockSpec((None, LANES, 2 * sh), lambda t: (t, PLANES - 1, 0)),
                  pl.BlockSpec((None, 2 * sh, LANES), lambda t: (t, 0, 0)),
                  pl.BlockSpec((None, SUBLANES, 2 * sh), lambda t: (t, 0, 0)),
                  pl.BlockSpec((1, LANES), lambda t: (0, t)),
                  st_spec, st_spec],
        out_specs=[pl.BlockSpec((rows, LANES), lambda t: (0, t)), st_spec, st_spec],
        out_shape=[jax.ShapeDtypeStruct((rows, tiles * LANES), F32), state, state],
        scratch_shapes=[pltpu.VMEM((2 * sh // LANES, rows, LANES), F32),
                        pltpu.VMEM((rows, 2 * sh), F32)],
        compiler_params=_cparams(("arbitrary",)),
        name="ssm_sample",
    )(h, ws, c0, coef, d_skip, x0_re, x0_im)


def kernel(x_prompt, x_sample, cache_k, cache_v, state_ssm_re, state_ssm_im, w_in, w_out, b_out, rel_bias,
           lam_re, lam_im, log_dt, b_re, b_im, c_re, c_im, d_skip, w_glu, b_glu, ln_g, ln_b):
    depth = w_in.shape[0]
    assert depth == 1, "one layer per step"
    bsz, seq, d_model = x_prompt.shape
    dbsz, s_len, _ = x_sample.shape
    buf, heads = cache_k.shape[2], cache_k.shape[3]
    width = heads * HEAD_DIM
    groups, nstate = lam_re.shape[1], lam_re.shape[2]
    keep = min(MAX_DISTANCE, seq)
    assert seq % (PLANES * BLK) == 0 and seq >= 2 * PLANES * BLK and keep % PLANES == 0 and buf == KPER * PLANES
    assert nstate == SSM_STATE and width == groups * SSM_CH and s_len <= min(DILATIONS[:-1])
    alpha = (2 * depth) ** 0.25
    tail, n_far = 4 * BLK, (buf - 4 * BLK) // PLANES

    w_bf = w_in[0].astype(BF16)
    row = lambda v: v.reshape(1, -1)
    weights = (w_glu[0].astype(BF16), row(b_glu[0]), w_out[0].astype(BF16), row(b_out[0]),
               row(ln_g[0]), row(ln_b[0]))
    d_row = row(d_skip[0])

    rbt = rel_bias.T
    ptab = _bias_tables(jnp.asarray(_prompt_bucket_tables().reshape(1, -1)), rbt, 8192)
    ptab = ptab.reshape(heads, 2 * len(DILATIONS), BLK, 2 * BLK)
    key_w = 1024
    sidx, n_keys = _sample_bucket_tables(buf, s_len, n_far, tail, key_w)
    stab = _bias_tables(jnp.asarray(sidx), rbt, 4096).reshape(heads, len(DILATIONS), s_len, key_w)
    stab = jnp.transpose(stab[..., :n_keys], (1, 3, 2, 0)).reshape(len(DILATIONS), n_keys, s_len * heads)
    ws, wx, m_intra, c0, coef = _ssm_prep(lam_re[0], lam_im[0], log_dt[0], b_re[0], b_im[0], c_re[0], c_im[0])

    q, k, v, ga, u, gs, k_last, v_last = _inproj_prompt(x_prompt, w_bf, keep)
    attn = _attention_prompt(q, k, v, ptab)
    y, sre_p, sim_p = _ssm_prompt(u, ws, wx, m_intra, coef, d_row)
    y_prompt = _epilogue_prompt(alpha, attn, ga, y, gs, x_prompt, weights)
    kv_shape = (1, bsz, keep, heads, HEAD_DIM)
    st_shape = (1, bsz, groups, nstate)

    xs = x_sample.reshape(dbsz * s_len, d_model)
    hs = _inproj_sample(xs, w_bf)
    attn_s = _attention_sample(hs.reshape(dbsz, s_len, -1), cache_k[0].reshape(dbsz, buf, width),
                               cache_v[0].reshape(dbsz, buf, width), stab, n_keys, n_far, tail)
    y_s, sre_s, sim_s = _ssm_sample(hs, ws, c0, coef, d_row,
                                    state_ssm_re[0].astype(F32).reshape(dbsz, groups * nstate),
                                    state_ssm_im[0].astype(F32).reshape(dbsz, groups * nstate), s_len)
    y_sample = _epilogue_sample(alpha, attn_s.reshape(dbsz * s_len, width), hs, y_s, xs, weights)
    new_shape = (1, dbsz, s_len, heads, HEAD_DIM)
    sst_shape = (1, dbsz, groups, nstate)
    return (y_prompt, y_sample.reshape(dbsz, s_len, d_model),
            k_last.reshape(kv_shape), v_last.reshape(kv_shape),
            sre_p.reshape(st_shape), sim_p.reshape(st_shape),
            hs[:, width:2 * width].reshape(new_shape), hs[:, 2 * width:3 * width].reshape(new_shape),
            sre_s.reshape(sst_shape), sim_s.reshape(sst_shape))
```

```python
import functools
import math

import jax
import jax.numpy as jnp
import numpy as np
from jax import lax
from jax.experimental import pallas as pl
from jax.experimental.pallas import tpu as pltpu

F32 = jnp.float32
BF16 = jnp.bfloat16

HEAD_DIM = 64
SSM_CH = 16
SSM_STATE = 64
NUM_BUCKETS = 32
MAX_DISTANCE = 2048
KPER = 128
BLK = 128
DILATIONS = (16, 4, 1)
LN_EPS = 1e-5
NEG = -1e30

LANES = 128
SUBLANES = 8
PLANES = 16
GROUPS_PER_TILE = LANES // SSM_CH
STATE_HALF = GROUPS_PER_TILE * SSM_STATE
VMEM_LIMIT = 56 * 1024 * 1024
ROW_TILE = 512


def _cparams(sem, vmem=VMEM_LIMIT):
    return pltpu.CompilerParams(dimension_semantics=sem, vmem_limit_bytes=vmem)


def _bucket_np(dist):
    exact = NUM_BUCKETS // 2
    d_f = np.maximum(dist, 1).astype(np.float32)
    large = exact + (np.log(d_f / np.float32(exact)) / np.float32(math.log(MAX_DISTANCE / exact))
                     * np.float32(NUM_BUCKETS - exact)).astype(np.int32)
    large = np.minimum(large, NUM_BUCKETS - 1)
    return np.where(dist < exact, dist, large).astype(np.int32)


def _prompt_rel(dil, first):
    i = np.arange(BLK)[:, None]
    j = np.arange(2 * BLK)[None, :]
    npl = PLANES // dil
    qrows = BLK // npl
    pq, ml = i // qrows, i % qrows
    pk, jl = j // (2 * qrows), j % (2 * qrows)
    back = 0 if first else qrows
    return npl * (ml - jl + back) + (pq - pk)


def _prompt_bucket_tables():
    tabs = []
    for dil in DILATIONS:
        for first in (False, True):
            rel = _prompt_rel(dil, first)
            valid = (rel >= 0) & (rel <= KPER)
            tabs.append(np.where(valid, _bucket_np(np.clip(rel, 0, KPER) * dil), NUM_BUCKETS))
    return np.stack(tabs).reshape(len(DILATIONS) * 2, BLK * 2 * BLK).astype(np.int32)


def _sample_bucket_tables(buf, s_len, width):
    pos = np.arange(buf + s_len)
    tabs = np.full((len(DILATIONS), s_len, width), NUM_BUCKETS, np.int32)
    for a, dil in enumerate(DILATIONS):
        for s in range(s_len):
            dist = buf + s - pos
            valid = (dist >= 0) & (dist % dil == 0) & (dist // dil <= KPER)
            tabs[a, s, :len(pos)] = np.where(valid, _bucket_np(np.maximum(dist, 0)), NUM_BUCKETS)
    return tabs.reshape(1, -1)


def _bias_kernel(idx_ref, rbt_ref, o_ref):
    idx = idx_ref[...]
    onehot = (lax.broadcasted_iota(jnp.int32, (NUM_BUCKETS, idx.shape[1]), 0) == idx).astype(F32)
    tab = jnp.dot(rbt_ref[...], onehot, precision=lax.Precision.HIGHEST,
                  preferred_element_type=F32)
    o_ref[...] = jnp.where(idx < NUM_BUCKETS, tab, NEG)


def _bias_tables(idx, rel_bias_t, chunk):
    n_heads = rel_bias_t.shape[0]
    total = idx.shape[1]
    return pl.pallas_call(
        _bias_kernel,
        grid=(total // chunk,),
        in_specs=[pl.BlockSpec((1, chunk), lambda c: (0, c)),
                  pl.BlockSpec((n_heads, NUM_BUCKETS), lambda c: (0, 0))],
        out_specs=pl.BlockSpec((n_heads, chunk), lambda c: (0, c)),
        out_shape=jax.ShapeDtypeStruct((n_heads, total), F32),
        compiler_params=_cparams(("arbitrary",)),
        name="bias_tables",
    )(idx, rel_bias_t)


def _discretize(lr, li, ldt):
    lr = jnp.minimum(lr, -1e-4)
    dt = jnp.exp(ldt)
    mag = jnp.exp(lr * dt)
    ab_re, ab_im = mag * jnp.cos(li * dt), mag * jnp.sin(li * dt)
    den = lr * lr + li * li
    inv_re, inv_im = lr / den, -li / den
    n_re, n_im = ab_re - 1.0, ab_im
    cf_re = n_re * inv_re - n_im * inv_im
    cf_im = n_re * inv_im + n_im * inv_re
    return ab_re, ab_im, cf_re, cf_im


def _ssm_prep_kernel(lam_row_ref, lam_col_ref, bt_re_ref, bt_im_ref, ct_re_ref, ct_im_ref,
                     ws_ref, wx_ref, m_ref, c0_ref, coef_ref):
    row = lam_row_ref[0]
    ab_re, ab_im, cf_re, cf_im = _discretize(row[0:1], row[1:2], row[2:3])
    bt_re, bt_im = bt_re_ref[0], bt_im_ref[0]
    bb_re = cf_re * bt_re - cf_im * bt_im
    bb_im = cf_re * bt_im + cf_im * bt_re
    col = lam_col_ref[0]
    abc_re, abc_im, _, _ = _discretize(col[:, 0:1], col[:, 1:2], col[:, 2:3])
    ct_re, ct_im = ct_re_ref[0], ct_im_ref[0]
    c0 = jnp.concatenate([ct_re, -ct_im], axis=0)
    c0_ref[0] = c0.astype(BF16)

    pr, pi = jnp.ones_like(ab_re), jnp.zeros_like(ab_re)
    qr, qi = abc_re, abc_im
    zero_blk = jnp.zeros((LANES, LANES), BF16)
    for lag in range(PLANES):
        w = jnp.concatenate([pr * bb_re - pi * bb_im, pr * bb_im + pi * bb_re], axis=1)
        s = PLANES - 1 - lag
        ws_ref[0, s * LANES:(s + 1) * LANES, :] = w.astype(BF16)
        m_lag = jnp.dot(w, c0, precision=lax.Precision.HIGHEST, preferred_element_type=F32).astype(BF16)
        for t in range(PLANES):
            src = t - lag
            if src >= 0:
                m_ref[0, src * LANES:(src + 1) * LANES, t * LANES:(t + 1) * LANES] = m_lag
            else:
                m_ref[0, (PLANES + src) * LANES:(PLANES + src + 1) * LANES, t * LANES:(t + 1) * LANES] = zero_blk
        wx_ref[0, 0:STATE_HALF, lag * LANES:(lag + 1) * LANES] = (ct_re * qr - ct_im * qi).astype(BF16)
        wx_ref[0, STATE_HALF:2 * STATE_HALF, lag * LANES:(lag + 1) * LANES] = (-(ct_re * qi + ct_im * qr)).astype(BF16)
        pr, pi = pr * ab_re - pi * ab_im, pr * ab_im + pi * ab_re
        qr, qi = qr * abc_re - qi * abc_im, qr * abc_im + qi * abc_re
    coef_ref[0] = jnp.concatenate([
        jnp.concatenate([ab_re, ab_im], axis=1),
        jnp.concatenate([pr, pi], axis=1),
        jnp.zeros((SUBLANES - 2, 2 * STATE_HALF), F32)], axis=0)


def _ssm_prep(lam_re, lam_im, log_dt, b_re, b_im, c_re, c_im):
    groups, n = lam_re.shape
    tiles = groups // GROUPS_PER_TILE
    gpt = GROUPS_PER_TILE
    eye = jnp.eye(gpt, dtype=F32)

    def rows(v):
        return v.reshape(tiles, gpt * n)

    ldt = jnp.broadcast_to(log_dt[:, None], (groups, n))
    lam_row = jnp.stack([rows(lam_re), rows(lam_im), rows(ldt)], axis=1)
    lam_col = jnp.transpose(lam_row, (0, 2, 1))

    def bt(b):
        b = jnp.transpose(b.reshape(tiles, gpt, n, SSM_CH), (0, 1, 3, 2))
        return (b[:, :, :, None, :] * eye[None, :, None, :, None]).reshape(tiles, gpt * SSM_CH, gpt * n)

    def ct(c):
        c = jnp.transpose(c.reshape(tiles, gpt, SSM_CH, n), (0, 1, 3, 2))
        return (c[:, :, :, None, :] * eye[None, :, None, :, None]).reshape(tiles, gpt * n, gpt * SSM_CH)

    sh, ln = STATE_HALF, LANES
    tile3 = lambda a, b: pl.BlockSpec((1, a, b), lambda t: (t, 0, 0))
    return pl.pallas_call(
        _ssm_prep_kernel,
        grid=(tiles,),
        in_specs=[tile3(3, sh), tile3(sh, 3), tile3(ln, sh), tile3(ln, sh), tile3(sh, ln), tile3(sh, ln)],
        out_specs=[tile3(PLANES * ln, 2 * sh), tile3(2 * sh, PLANES * ln), tile3(PLANES * ln, PLANES * ln),
                   tile3(2 * sh, ln), tile3(SUBLANES, 2 * sh)],
        out_shape=[jax.ShapeDtypeStruct((tiles, PLANES * ln, 2 * sh), BF16),
                   jax.ShapeDtypeStruct((tiles, 2 * sh, PLANES * ln), BF16),
                   jax.ShapeDtypeStruct((tiles, PLANES * ln, PLANES * ln), BF16),
                   jax.ShapeDtypeStruct((tiles, 2 * sh, ln), BF16),
                   jax.ShapeDtypeStruct((tiles, SUBLANES, 2 * sh), F32)],
        compiler_params=_cparams(("arbitrary",)),
        name="ssm_prep",
    )(lam_row, lam_col, bt(b_re), bt(b_im), ct(c_re), ct(c_im))


def _inproj_kernel(first_keep, scale, x_ref, w_ref, q_ref, k_ref, v_ref, ga_ref, u_ref, gs_ref,
                   kl_ref, vl_ref, xb_ref, slab_ref):
    i, j = pl.program_id(1), pl.program_id(2)
    slabs = slab_ref.shape[0]
    prow = q_ref.shape[1]

    @pl.when(j == 0)
    def _():
        xb_ref[...] = x_ref[...].astype(BF16)

    acc = jnp.dot(xb_ref[...], w_ref[...], preferred_element_type=F32)

    def to_planes(val, out_ref):
        for c in range(slabs):
            slab_ref[c] = val[:, c * LANES:(c + 1) * LANES]
        for r in range(PLANES):
            rows = [slab_ref[c, pl.ds(r, prow, stride=PLANES), :] for c in range(slabs)]
            out_ref[r] = jnp.concatenate(rows, axis=1).astype(out_ref.dtype)

    @pl.when(j == 0)
    def _():
        to_planes(acc * scale, q_ref)

    @pl.when(j == 1)
    def _():
        to_planes(acc, k_ref)

    @pl.when(j == 2)
    def _():
        to_planes(acc, v_ref)

    @pl.when(jnp.logical_and(j == 1, i >= first_keep))
    def _():
        kl_ref[...] = acc.T

    @pl.when(jnp.logical_and(j == 2, i >= first_keep))
    def _():
        vl_ref[...] = acc.T

    @pl.when(j == 3)
    def _():
        ga_ref[...] = acc.astype(BF16)

    @pl.when(j == 4)
    def _():
        to_planes(acc, u_ref)

    @pl.when(j == 5)
    def _():
        gs_ref[...] = acc.astype(BF16)


def _inproj_prompt(x, w_bf, keep):
    bsz, seq, d_model = x.shape
    width = w_bf.shape[1] // 6
    rows = seq // PLANES
    tile = ROW_TILE
    prow = tile // PLANES
    first_keep = (seq - keep) // tile
    plane = lambda dt: jax.ShapeDtypeStruct((bsz, PLANES, rows, width), dt)
    plane_spec = pl.BlockSpec((None, PLANES, prow, width), lambda b, i, j: (b, 0, i, 0))
    token = jax.ShapeDtypeStruct((bsz, seq, width), BF16)
    token_spec = pl.BlockSpec((None, tile, width), lambda b, i, j: (b, i, 0))
    last = jax.ShapeDtypeStruct((bsz, width, keep), F32)
    last_spec = pl.BlockSpec((None, width, tile), lambda b, i, j: (b, 0, jnp.maximum(i - first_keep, 0)))
    return pl.pallas_call(
        functools.partial(_inproj_kernel, first_keep, HEAD_DIM ** -0.5),
        grid=(bsz, seq // tile, 6),
        in_specs=[pl.BlockSpec((None, tile, d_model), lambda b, i, j: (b, i, 0)),
                  pl.BlockSpec((d_model, width), lambda b, i, j: (0, j))],
        out_specs=[plane_spec, plane_spec, plane_spec, token_spec, plane_spec, token_spec, last_spec, last_spec],
        out_shape=[plane(F32), plane(F32), plane(F32), token, plane(BF16), token, last, last],
        scratch_shapes=[pltpu.VMEM((tile, d_model), BF16),
                        pltpu.VMEM((width // LANES, tile, LANES), F32)],
        compiler_params=_cparams(("arbitrary", "arbitrary", "arbitrary")),
        name="inproj_prompt",
    )(x, w_bf)


ATTN_UNROLL = 4


def _attn_kernel(q_ref, k_ref, v_ref, bias_ref, o_ref, acc_ref, m_ref, l_ref):
    sup = pl.program_id(2)
    is_a = lax.broadcasted_iota(jnp.int32, (BLK, LANES), 1) < HEAD_DIM
    base = pl.multiple_of(sup * BLK, BLK)

    def scores(q, kb, table, head):
        qm = jnp.where(is_a if head == 0 else jnp.logical_not(is_a), q, 0.0).astype(BF16)
        s = lax.dot_general(qm, kb, (((1,), (1,)), ((), ())), preferred_element_type=F32)
        return s + bias_ref[head, table]

    def both(a, b):
        return jnp.where(is_a, a, b)

    def block(q, k, v, table, state):
        kb, vb = k.astype(BF16), v.astype(BF16)
        s = [scores(q, kb, table, h) for h in range(2)]
        mb = both(jnp.max(s[0], axis=1, keepdims=True), jnp.max(s[1], axis=1, keepdims=True))
        m_new = mb if state is None else jnp.maximum(state[0], mb)
        ls, os_ = [], []
        for h in range(2):
            p = jnp.exp(s[h] - m_new[:, h * HEAD_DIM:h * HEAD_DIM + 1])
            ls.append(jnp.sum(p, axis=1, keepdims=True))
            os_.append(jnp.dot(p.astype(BF16), vb, preferred_element_type=F32))
        l_new, acc_new = both(ls[0], ls[1]), both(os_[0], os_[1])
        if state is not None:
            alpha = jnp.exp(state[0] - m_new)
            l_new = alpha * state[1] + l_new
            acc_new = alpha * state[2] + acc_new
        return m_new, l_new, acc_new

    cat = lambda parts: parts[0] if len(parts) == 1 else jnp.concatenate(parts, axis=0)

    def run_pattern(a, dil, nblocks):
        npl = PLANES // dil
        qrows = BLK // npl
        per_res = BLK // qrows

        def body(grp, carry):
            work = []
            for un in range(ATTN_UNROLL):
                blk = grp * ATTN_UNROLL + un
                res, sub = blk // per_res, blk % per_res
                off = pl.multiple_of(sub * qrows, qrows)
                kstart = pl.multiple_of(jnp.maximum(base + off - qrows, 0), qrows)
                first = jnp.logical_and(sup == 0, sub == 0).astype(jnp.int32)
                planes = [res + dil * i for i in range(npl)]
                q = cat([q_ref[pln, pl.ds(base + off, qrows), :] for pln in planes])
                k = cat([k_ref[pln, pl.ds(kstart, 2 * qrows), :] for pln in planes])
                v = cat([v_ref[pln, pl.ds(kstart, 2 * qrows), :] for pln in planes])
                state = None if a == 0 else tuple(cat([ref[pln, pl.ds(off, qrows), :] for pln in planes])
                                                  for ref in (m_ref, l_ref, acc_ref))
                work.append((planes, off, q, k, v, 2 * a + first, state))
            done = [(planes, off, block(q, k, v, table, state)) for planes, off, q, k, v, table, state in work]
            for planes, off, new in done:
                for ref, val in zip((m_ref, l_ref, acc_ref), new):
                    for i, pln in enumerate(planes):
                        ref[pln, pl.ds(off, qrows), :] = val[i * qrows:(i + 1) * qrows]
            return carry

        lax.fori_loop(0, nblocks // ATTN_UNROLL, body, 0)

    for a, dil in enumerate(DILATIONS):
        run_pattern(a, dil, PLANES)
    for pln in range(PLANES):
        o_ref[pl.ds(pln, BLK, stride=PLANES), :] = acc_ref[pln] / l_ref[pln]


def _attention_prompt(q, k, v, bias):
    bsz, _, rows, width = q.shape
    pairs = width // LANES
    ntab = bias.shape[1]
    qkv_spec = pl.BlockSpec((None, PLANES, rows, LANES), lambda b, h, s: (b, 0, 0, h))
    return pl.pallas_call(
        _attn_kernel,
        grid=(bsz, pairs, rows // BLK),
        in_specs=[qkv_spec, qkv_spec, qkv_spec,
                  pl.BlockSpec((2, ntab, BLK, 2 * BLK), lambda b, h, s: (h, 0, 0, 0))],
        out_specs=pl.BlockSpec((None, PLANES * BLK, LANES), lambda b, h, s: (b, s, h)),
        out_shape=jax.ShapeDtypeStruct((bsz, PLANES * rows, width), F32),
        scratch_shapes=[pltpu.VMEM((PLANES, BLK, LANES), F32)] * 3,
        compiler_params=_cparams(("arbitrary", "arbitrary", "arbitrary")),
        name="attention_prompt",
    )(q, k, v, bias)


def _ssm_kernel(u_ref, ws_ref, wx_ref, m_ref, coef_ref, d_ref, y_ref, sre_ref, sim_ref,
                ucat_ref, s_ref, x_ref):
    tc = pl.program_id(2)
    chunks = u_ref.shape[1]
    per_step = m_ref.shape[1] // LANES

    @pl.when(tc == 0)
    def _():
        for s in range(PLANES):
            ucat_ref[:, s * LANES:(s + 1) * LANES] = u_ref[s]
        s_ref[...] = jnp.dot(ucat_ref[...], ws_ref[...], preferred_element_type=F32)
        a_re = coef_ref[1:2, 0:STATE_HALF]
        a_im = coef_ref[1:2, STATE_HALF:2 * STATE_HALF]

        def step(c, carry):
            xr, xi = carry
            x_ref[pl.ds(c, 1), 0:STATE_HALF] = xr
            x_ref[pl.ds(c, 1), STATE_HALF:2 * STATE_HALF] = xi
            sr = s_ref[pl.ds(c, 1), 0:STATE_HALF]
            si = s_ref[pl.ds(c, 1), STATE_HALF:2 * STATE_HALF]
            return a_re * xr - a_im * xi + sr, a_re * xi + a_im * xr + si

        zero = jnp.zeros((1, STATE_HALF), F32)
        xr, xi = lax.fori_loop(0, chunks, step, (zero, zero))
        sre_ref[...] = xr
        sim_ref[...] = xi

    y = jnp.dot(ucat_ref[...], m_ref[...], preferred_element_type=F32)
    y = y + jnp.dot(x_ref[...].astype(BF16), wx_ref[...], preferred_element_type=F32)
    d = d_ref[...]
    for t in range(per_step):
        tok = tc * per_step + t
        y_ref[pl.ds(tok, chunks, stride=PLANES), :] = y[:, t * LANES:(t + 1) * LANES] + d * u_ref[tok].astype(F32)


def _ssm_prompt(u, ws, wx, m, coef, d_skip):
    bsz, _, chunks, width = u.shape
    tiles = width // LANES
    per_step = 4
    steps = PLANES // per_step
    sh = STATE_HALF
    state = jax.ShapeDtypeStruct((bsz, tiles, 1, sh), F32)
    state_spec = pl.BlockSpec((None, None, 1, sh), lambda t, b, c: (b, t, 0, 0))
    return pl.pallas_call(
        _ssm_kernel,
        grid=(tiles, bsz, steps),
        in_specs=[pl.BlockSpec((None, PLANES, chunks, LANES), lambda t, b, c: (b, 0, 0, t)),
                  pl.BlockSpec((None, PLANES * LANES, 2 * sh), lambda t, b, c: (t, 0, 0)),
                  pl.BlockSpec((None, 2 * sh, per_step * LANES), lambda t, b, c: (t, 0, c)),
                  pl.BlockSpec((None, PLANES * LANES, per_step * LANES), lambda t, b, c: (t, 0, c)),
                  pl.BlockSpec((None, SUBLANES, 2 * sh), lambda t, b, c: (t, 0, 0)),
                  pl.BlockSpec((1, LANES), lambda t, b, c: (0, t))],
        out_specs=[pl.BlockSpec((None, PLANES * chunks, LANES), lambda t, b, c: (b, 0, t)),
                   state_spec, state_spec],
        out_shape=[jax.ShapeDtypeStruct((bsz, PLANES * chunks, width), F32), state, state],
        scratch_shapes=[pltpu.VMEM((chunks, PLANES * LANES), BF16),
                        pltpu.VMEM((chunks, 2 * sh), F32),
                        pltpu.VMEM((chunks, 2 * sh), F32)],
        compiler_params=_cparams(("arbitrary", "arbitrary", "arbitrary")),
        name="ssm_prompt",
    )(u, ws, wx, m, coef, d_skip)


def _epilogue_kernel(alpha, attn_ref, ga_ref, y_ref, gs_ref, x_ref, wglu_ref, bglu_ref, wo_ref, bo_ref,
                     g_ref, b_ref, o_ref):
    d_attn = attn_ref.shape[-1]
    br_a = (attn_ref[...].astype(F32) * jax.nn.silu(ga_ref[...].astype(F32))).astype(BF16)
    z = jax.nn.gelu(y_ref[...].astype(F32))
    gate = jax.nn.sigmoid(jnp.dot(z.astype(BF16), wglu_ref[...], preferred_element_type=F32) + bglu_ref[...])
    br_s = (z * gate * jax.nn.silu(gs_ref[...].astype(F32))).astype(BF16)
    mix = (jnp.dot(br_a, wo_ref[0:d_attn, :], preferred_element_type=F32)
           + jnp.dot(br_s, wo_ref[d_attn:, :], preferred_element_type=F32) + bo_ref[...])
    t = alpha * x_ref[...] + mix
    mu = jnp.mean(t, axis=-1, keepdims=True)
    var = jnp.mean(jnp.square(t - mu), axis=-1, keepdims=True)
    o_ref[...] = (t - mu) * lax.rsqrt(var + LN_EPS) * g_ref[...] + b_ref[...]


def _epilogue_weights_specs(d_mix, d_ssm, d_model, imap):
    return [pl.BlockSpec((d_ssm, d_ssm), imap), pl.BlockSpec((1, d_ssm), imap),
            pl.BlockSpec((d_mix, d_model), imap), pl.BlockSpec((1, d_model), imap),
            pl.BlockSpec((1, d_model), imap), pl.BlockSpec((1, d_model), imap)]


def _epilogue_prompt(alpha, attn, ga, y, gs, x, weights):
    bsz, seq, width = attn.shape
    d_model = x.shape[-1]
    tile = ROW_TILE // 2
    wspec = pl.BlockSpec((None, tile, width), lambda b, i: (b, i, 0))
    xspec = pl.BlockSpec((None, tile, d_model), lambda b, i: (b, i, 0))
    return pl.pallas_call(
        functools.partial(_epilogue_kernel, alpha),
        grid=(bsz, seq // tile),
        in_specs=[wspec, wspec, wspec, wspec, xspec]
        + _epilogue_weights_specs(2 * width, width, d_model, lambda b, i: (0, 0)),
        out_specs=xspec,
        out_shape=jax.ShapeDtypeStruct(x.shape, F32),
        compiler_params=_cparams(("arbitrary", "arbitrary")),
        name="epilogue_prompt",
    )(attn, ga, y, gs, x, *weights)


def _epilogue_sample(alpha, attn, h, y, x, weights):
    rows, width = attn.shape
    d_model = x.shape[-1]
    col = lambda c: pl.BlockSpec((rows, width), lambda i: (0, c))
    return pl.pallas_call(
        functools.partial(_epilogue_kernel, alpha),
        grid=(1,),
        in_specs=[col(0), col(3), col(0), col(5), pl.BlockSpec((rows, d_model), lambda i: (0, 0))]
        + _epilogue_weights_specs(2 * width, width, d_model, lambda i: (0, 0)),
        out_specs=pl.BlockSpec((rows, d_model), lambda i: (0, 0)),
        out_shape=jax.ShapeDtypeStruct((rows, d_model), F32),
        compiler_params=_cparams(("arbitrary",)),
        name="epilogue_sample",
    )(attn, h, y, h, x, *weights)


def _inproj_sample_kernel(scale, x_ref, w_ref, o_ref):
    acc = jnp.dot(x_ref[...].astype(BF16), w_ref[...], preferred_element_type=F32)
    o_ref[...] = acc * jnp.where(pl.program_id(0) == 0, scale, 1.0)


def _inproj_sample(x, w_bf):
    rows, d_model = x.shape
    width = w_bf.shape[1] // 6
    return pl.pallas_call(
        functools.partial(_inproj_sample_kernel, HEAD_DIM ** -0.5),
        grid=(6,),
        in_specs=[pl.BlockSpec((rows, d_model), lambda j: (0, 0)),
                  pl.BlockSpec((d_model, width), lambda j: (0, j))],
        out_specs=pl.BlockSpec((rows, width), lambda j: (0, j)),
        out_shape=jax.ShapeDtypeStruct((rows, 6 * width), F32),
        compiler_params=_cparams(("arbitrary",)),
        name="inproj_sample",
    )(x, w_bf)


SAMPLE_PAIRS = 4


def _attn_sample_kernel(q_ref, kn_ref, vn_ref, kt_ref, vt_ref, tb_ref, tn_ref, o_ref):
    s_len = q_ref.shape[0]
    buf = kt_ref.shape[2]
    npat = tb_ref.shape[0]
    is_a = lax.broadcasted_iota(jnp.int32, (s_len, LANES), 1) < HEAD_DIM
    nt = (((1,), (1,)), ((), ()))
    pad = jnp.zeros((2 * s_len - s_len, LANES), F32)
    for pp in range(SAMPLE_PAIRS):
        lanes = slice(pp * LANES, (pp + 1) * LANES)
        rows = slice(pp * 2 * s_len, (pp + 1) * 2 * s_len)
        qp = q_ref[:, lanes]
        q2 = jnp.concatenate([jnp.where(is_a, qp, 0.0), jnp.where(is_a, 0.0, qp)], axis=0).astype(BF16)
        kt = kt_ref[2 * pp:2 * pp + 2].reshape(2 * HEAD_DIM, buf).astype(BF16)
        vt = vt_ref[2 * pp:2 * pp + 2].reshape(2 * HEAD_DIM, buf).astype(BF16)
        kn = jnp.concatenate([kn_ref[:, lanes], pad], axis=0).astype(BF16)
        vn = jnp.concatenate([vn_ref[:, lanes], pad], axis=0).astype(BF16)
        s_buf = jnp.dot(q2, kt, preferred_element_type=F32)
        s_new = lax.dot_general(q2, kn, nt, preferred_element_type=F32)
        z_buf = [s_buf + tb_ref[a, rows, :] for a in range(npat)]
        z_new = [s_new + tn_ref[a, rows, 0:2 * s_len] for a in range(npat)]
        top = functools.reduce(jnp.maximum, [jnp.max(z, axis=1, keepdims=True) for z in z_buf + z_new])
        w_buf = functools.reduce(jnp.add, [jnp.exp(z - top) for z in z_buf])
        w_new = functools.reduce(jnp.add, [jnp.exp(z - top) for z in z_new])
        den = jnp.sum(w_buf, axis=1, keepdims=True) + jnp.sum(w_new, axis=1, keepdims=True)
        o = lax.dot_general(w_buf.astype(BF16), vt, nt, preferred_element_type=F32)
        o = (o + jnp.dot(w_new.astype(BF16), vn, preferred_element_type=F32)) / den
        o_ref[:, lanes] = jnp.where(is_a, o[0:s_len], o[s_len:2 * s_len])


def _attention_sample(h, kt, vt, tb, tn):
    bsz, s_len, _ = h.shape
    heads, _, buf = kt.shape[1:]
    width = heads * HEAD_DIM
    gw = SAMPLE_PAIRS * LANES
    per_w = width // gw
    new = lambda c: pl.BlockSpec((None, s_len, gw), lambda b, g: (b, 0, c * per_w + g))
    cache = pl.BlockSpec((None, 2 * SAMPLE_PAIRS, HEAD_DIM, buf), lambda b, g: (b, g, 0, 0))
    trows = SAMPLE_PAIRS * 2 * s_len
    return pl.pallas_call(
        _attn_sample_kernel,
        grid=(bsz, per_w),
        in_specs=[new(0), new(1), new(2), cache, cache,
                  pl.BlockSpec((tb.shape[0], trows, buf), lambda b, g: (0, g, 0)),
                  pl.BlockSpec((tn.shape[0], trows, LANES), lambda b, g: (0, g, 0))],
        out_specs=pl.BlockSpec((None, s_len, gw), lambda b, g: (b, 0, g)),
        out_shape=jax.ShapeDtypeStruct((bsz, s_len, width), F32),
        compiler_params=_cparams(("arbitrary", "arbitrary")),
        name="attention_sample",
    )(h, h, h, kt, vt, tb, tn)


def _ssm_sample_kernel(s_len, u_ref, b0_ref, c0_ref, coef_ref, d_ref, x0r_ref, x0i_ref,
                       y_ref, sre_ref, sim_ref, bu_ref, xs_ref):
    bsz = x0r_ref.shape[0]
    bu = jnp.dot(u_ref[...].astype(BF16), b0_ref[...], preferred_element_type=F32)
    slabs = bu_ref.shape[0]
    for c in range(slabs):
        bu_ref[c] = bu[:, c * LANES:(c + 1) * LANES]
    a_re = coef_ref[0:1, 0:STATE_HALF]
    a_im = coef_ref[0:1, STATE_HALF:2 * STATE_HALF]
    half = slabs // 2
    xr, xi = x0r_ref[...], x0i_ref[...]
    for s in range(s_len):
        step = lambda c: bu_ref[c, pl.ds(s, bsz, stride=s_len), :]
        br = jnp.concatenate([step(c) for c in range(half)], axis=1)
        bi = jnp.concatenate([step(c) for c in range(half, slabs)], axis=1)
        xr, xi = a_re * xr - a_im * xi + br, a_re * xi + a_im * xr + bi
        xs_ref[s * bsz:(s + 1) * bsz, :] = jnp.concatenate([xr, xi], axis=1)
    sre_ref[...] = xr
    sim_ref[...] = xi
    y = jnp.dot(xs_ref[...].astype(BF16), c0_ref[...], preferred_element_type=F32)
    d = d_ref[...]
    for s in range(s_len):
        y_ref[pl.ds(s, bsz, stride=s_len), :] = (y[s * bsz:(s + 1) * bsz, :]
                                                 + d * u_ref[pl.ds(s, bsz, stride=s_len), :])


def _ssm_sample(h, ws, c0, coef, d_skip, x0_re, x0_im, s_len):
    rows = h.shape[0]
    bsz = rows // s_len
    tiles = ws.shape[0]
    sh = STATE_HALF
    u_col0 = 4 * (h.shape[1] // 6) // LANES
    st_spec = pl.BlockSpec((bsz, sh), lambda t: (0, t))
    state = jax.ShapeDtypeStruct((bsz, tiles * sh), F32)
    return pl.pallas_call(
        functools.partial(_ssm_sample_kernel, s_len),
        grid=(tiles,),
        in_specs=[pl.BlockSpec((rows, LANES), lambda t: (0, u_col0 + t)),
                  pl.BlockSpec((None, LANES, 2 * sh), lambda t: (t, PLANES - 1, 0)),
                  pl.BlockSpec((None, 2 * sh, LANES), lambda t: (t, 0, 0)),
                  pl.BlockSpec((None, SUBLANES, 2 * sh), lambda t: (t, 0, 0)),
                  pl.BlockSpec((1, LANES), lambda t: (0, t)),
                  st_spec, st_spec],
        out_specs=[pl.BlockSpec((rows, LANES), lambda t: (0, t)), st_spec, st_spec],
        out_shape=[jax.ShapeDtypeStruct((rows, tiles * LANES), F32), state, state],
        scratch_shapes=[pltpu.VMEM((2 * sh // LANES, rows, LANES), F32),
                        pltpu.VMEM((rows, 2 * sh), F32)],
        compiler_params=_cparams(("arbitrary",)),
        name="ssm_sample",
    )(h, ws, c0, coef, d_skip, x0_re, x0_im)


def kernel(x_prompt, x_sample, cache_k, cache_v, state_ssm_re, state_ssm_im, w_in, w_out, b_out, rel_bias,
           lam_re, lam_im, log_dt, b_re, b_im, c_re, c_im, d_skip, w_glu, b_glu, ln_g, ln_b):
    depth = w_in.shape[0]
    assert depth == 1, "one layer per step"
    bsz, seq, d_model = x_prompt.shape
    dbsz, s_len, _ = x_sample.shape
    buf, heads = cache_k.shape[2], cache_k.shape[3]
    width = heads * HEAD_DIM
    groups, nstate = lam_re.shape[1], lam_re.shape[2]
    keep = min(MAX_DISTANCE, seq)
    assert seq % (PLANES * BLK) == 0 and seq >= 2 * PLANES * BLK and keep % ROW_TILE == 0
    assert buf == KPER * max(DILATIONS) and s_len <= min(DILATIONS[:-1])
    assert nstate == SSM_STATE and width == groups * SSM_CH and heads % (2 * SAMPLE_PAIRS) == 0
    alpha = (2 * depth) ** 0.25
    npat = len(DILATIONS)

    w_bf = w_in[0].astype(BF16)
    row = lambda v: v.reshape(1, -1)
    weights = (w_glu[0].astype(BF16), row(b_glu[0]), w_out[0].astype(BF16), row(b_out[0]),
               row(ln_g[0]), row(ln_b[0]))
    d_row = row(d_skip[0])

    rbt = rel_bias.T
    ptab = _bias_tables(jnp.asarray(_prompt_bucket_tables().reshape(1, -1)), rbt, 8192)
    ptab = ptab.reshape(heads, 2 * npat, BLK, 2 * BLK)
    key_w = buf + 2 * LANES
    stab = _bias_tables(jnp.asarray(_sample_bucket_tables(buf, s_len, key_w)), rbt, key_w)
    stab = jnp.transpose(stab.reshape(heads, npat, s_len, key_w), (1, 0, 2, 3)).reshape(npat, heads * s_len, key_w)
    stab_buf, stab_new = stab[:, :, :buf], stab[:, :, buf:buf + LANES]
    ws, wx, m_intra, c0, coef = _ssm_prep(lam_re[0], lam_im[0], log_dt[0], b_re[0], b_im[0], c_re[0], c_im[0])

    q, k, v, ga, u, gs, kl_t, vl_t = _inproj_prompt(x_prompt, w_bf, keep)
    attn = _attention_prompt(q, k, v, ptab)
    y, sre_p, sim_p = _ssm_prompt(u, ws, wx, m_intra, coef, d_row)
    y_prompt = _epilogue_prompt(alpha, attn, ga, y, gs, x_prompt, weights)
    last = lambda t: jnp.transpose(t.reshape(bsz, heads, HEAD_DIM, keep), (0, 3, 1, 2))[None]
    st_shape = (1, bsz, groups, nstate)

    xs = x_sample.reshape(dbsz * s_len, d_model)
    hs = _inproj_sample(xs, w_bf)
    pos_minor = lambda c: jnp.transpose(c[0], (0, 2, 3, 1))
    attn_s = _attention_sample(hs.reshape(dbsz, s_len, -1), pos_minor(cache_k), pos_minor(cache_v),
                               stab_buf, stab_new)
    y_s, sre_s, sim_s = _ssm_sample(hs, ws, c0, coef, d_row,
                                    state_ssm_re[0].astype(F32).reshape(dbsz, groups * nstate),
                                    state_ssm_im[0].astype(F32).reshape(dbsz, groups * nstate), s_len)
    y_sample = _epilogue_sample(alpha, attn_s.reshape(dbsz * s_len, width), hs, y_s, xs, weights)
    new_shape = (1, dbsz, s_len, heads, HEAD_DIM)
    sst_shape = (1, dbsz, groups, nstate)
    return (y_prompt, y_sample.reshape(dbsz, s_len, d_model), last(kl_t), last(vl_t),
            sre_p.reshape(st_shape), sim_p.reshape(st_shape),
            hs[:, width:2 * width].reshape(new_shape), hs[:, 2 * width:3 * width].reshape(new_shape),
            sre_s.reshape(sst_shape), sim_s.reshape(sst_shape))
```

```python
import functools
import math

import jax
import jax.numpy as jnp
import numpy as np
from jax import lax
from jax.experimental import pallas as pl
from jax.experimental.pallas import tpu as pltpu

F32 = jnp.float32
BF16 = jnp.bfloat16

HEAD_DIM = 64
SSM_CH = 16
SSM_STATE = 64
NUM_BUCKETS = 32
MAX_DISTANCE = 2048
KPER = 128
BLK = 128
DILATIONS = (16, 4, 1)
LN_EPS = 1e-5
NEG = -1e30

LANES = 128
SUBLANES = 8
PLANES = 16
GROUPS_PER_TILE = LANES // SSM_CH
STATE_HALF = GROUPS_PER_TILE * SSM_STATE
VMEM_LIMIT = 56 * 1024 * 1024
ROW_TILE = 512


def _cparams(sem, vmem=VMEM_LIMIT):
    return pltpu.CompilerParams(dimension_semantics=sem, vmem_limit_bytes=vmem)


def _bucket_np(dist):
    exact = NUM_BUCKETS // 2
    d_f = np.maximum(dist, 1).astype(np.float32)
    large = exact + (np.log(d_f / np.float32(exact)) / np.float32(math.log(MAX_DISTANCE / exact))
                     * np.float32(NUM_BUCKETS - exact)).astype(np.int32)
    large = np.minimum(large, NUM_BUCKETS - 1)
    return np.where(dist < exact, dist, large).astype(np.int32)


def _prompt_rel(dil, first):
    i = np.arange(BLK)[:, None]
    j = np.arange(2 * BLK)[None, :]
    npl = PLANES // dil
    qrows = BLK // npl
    pq, ml = i // qrows, i % qrows
    pk, jl = j // (2 * qrows), j % (2 * qrows)
    back = 0 if first else qrows
    return npl * (ml - jl + back) + (pq - pk)


def _prompt_bucket_tables():
    tabs = []
    for dil in DILATIONS:
        for first in (False, True):
            rel = _prompt_rel(dil, first)
            valid = (rel >= 0) & (rel <= KPER)
            tabs.append(np.where(valid, _bucket_np(np.clip(rel, 0, KPER) * dil), NUM_BUCKETS))
    return np.stack(tabs).reshape(len(DILATIONS) * 2, BLK * 2 * BLK).astype(np.int32)


def _sample_bucket_tables(buf, s_len, width):
    pos = np.arange(buf + s_len)
    tabs = np.full((len(DILATIONS), s_len, width), NUM_BUCKETS, np.int32)
    for a, dil in enumerate(DILATIONS):
        for s in range(s_len):
            dist = buf + s - pos
            valid = (dist >= 0) & (dist % dil == 0) & (dist // dil <= KPER)
            tabs[a, s, :len(pos)] = np.where(valid, _bucket_np(np.maximum(dist, 0)), NUM_BUCKETS)
    return tabs.reshape(1, -1)


def _bias_kernel(idx_ref, rbt_ref, o_ref):
    idx = idx_ref[...]
    onehot = (lax.broadcasted_iota(jnp.int32, (NUM_BUCKETS, idx.shape[1]), 0) == idx).astype(F32)
    tab = jnp.dot(rbt_ref[...], onehot, precision=lax.Precision.HIGHEST,
                  preferred_element_type=F32)
    o_ref[...] = jnp.where(idx < NUM_BUCKETS, tab, NEG)


def _bias_tables(idx, rel_bias_t, chunk):
    n_heads = rel_bias_t.shape[0]
    total = idx.shape[1]
    return pl.pallas_call(
        _bias_kernel,
        grid=(total // chunk,),
        in_specs=[pl.BlockSpec((1, chunk), lambda c: (0, c)),
                  pl.BlockSpec((n_heads, NUM_BUCKETS), lambda c: (0, 0))],
        out_specs=pl.BlockSpec((n_heads, chunk), lambda c: (0, c)),
        out_shape=jax.ShapeDtypeStruct((n_heads, total), F32),
        compiler_params=_cparams(("arbitrary",)),
        name="bias_tables",
    )(idx, rel_bias_t)


def _discretize(lr, li, ldt):
    lr = jnp.minimum(lr, -1e-4)
    dt = jnp.exp(ldt)
    mag = jnp.exp(lr * dt)
    ab_re, ab_im = mag * jnp.cos(li * dt), mag * jnp.sin(li * dt)
    den = lr * lr + li * li
    inv_re, inv_im = lr / den, -li / den
    n_re, n_im = ab_re - 1.0, ab_im
    cf_re = n_re * inv_re - n_im * inv_im
    cf_im = n_re * inv_im + n_im * inv_re
    return ab_re, ab_im, cf_re, cf_im


def _ssm_prep_kernel(lam_row_ref, lam_col_ref, bt_re_ref, bt_im_ref, ct_re_ref, ct_im_ref,
                     ws_ref, wx_ref, m_ref, c0_ref, coef_ref):
    row = lam_row_ref[0]
    ab_re, ab_im, cf_re, cf_im = _discretize(row[0:1], row[1:2], row[2:3])
    bt_re, bt_im = bt_re_ref[0], bt_im_ref[0]
    bb_re = cf_re * bt_re - cf_im * bt_im
    bb_im = cf_re * bt_im + cf_im * bt_re
    col = lam_col_ref[0]
    abc_re, abc_im, _, _ = _discretize(col[:, 0:1], col[:, 1:2], col[:, 2:3])
    ct_re, ct_im = ct_re_ref[0], ct_im_ref[0]
    c0 = jnp.concatenate([ct_re, -ct_im], axis=0)
    c0_ref[0] = c0.astype(BF16)

    pr, pi = jnp.ones_like(ab_re), jnp.zeros_like(ab_re)
    qr, qi = abc_re, abc_im
    zero_blk = jnp.zeros((LANES, LANES), BF16)
    for lag in range(PLANES):
        w = jnp.concatenate([pr * bb_re - pi * bb_im, pr * bb_im + pi * bb_re], axis=1)
        s = PLANES - 1 - lag
        ws_ref[0, s * LANES:(s + 1) * LANES, :] = w.astype(BF16)
        m_lag = jnp.dot(w, c0, precision=lax.Precision.HIGHEST, preferred_element_type=F32).astype(BF16)
        for t in range(PLANES):
            src = t - lag
            if src >= 0:
                m_ref[0, src * LANES:(src + 1) * LANES, t * LANES:(t + 1) * LANES] = m_lag
            else:
                m_ref[0, (PLANES + src) * LANES:(PLANES + src + 1) * LANES, t * LANES:(t + 1) * LANES] = zero_blk
        wx_ref[0, 0:STATE_HALF, lag * LANES:(lag + 1) * LANES] = (ct_re * qr - ct_im * qi).astype(BF16)
        wx_ref[0, STATE_HALF:2 * STATE_HALF, lag * LANES:(lag + 1) * LANES] = (-(ct_re * qi + ct_im * qr)).astype(BF16)
        pr, pi = pr * ab_re - pi * ab_im, pr * ab_im + pi * ab_re
        qr, qi = qr * abc_re - qi * abc_im, qr * abc_im + qi * abc_re
    coef_ref[0] = jnp.concatenate([
        jnp.concatenate([ab_re, ab_im], axis=1),
        jnp.concatenate([pr, pi], axis=1),
        jnp.zeros((SUBLANES - 2, 2 * STATE_HALF), F32)], axis=0)


def _ssm_prep(lam_re, lam_im, log_dt, b_re, b_im, c_re, c_im):
    groups, n = lam_re.shape
    tiles = groups // GROUPS_PER_TILE
    gpt = GROUPS_PER_TILE
    eye = jnp.eye(gpt, dtype=F32)

    def rows(v):
        return v.reshape(tiles, gpt * n)

    ldt = jnp.broadcast_to(log_dt[:, None], (groups, n))
    lam_row = jnp.stack([rows(lam_re), rows(lam_im), rows(ldt)], axis=1)
    lam_col = jnp.transpose(lam_row, (0, 2, 1))

    def bt(b):
        b = jnp.transpose(b.reshape(tiles, gpt, n, SSM_CH), (0, 1, 3, 2))
        return (b[:, :, :, None, :] * eye[None, :, None, :, None]).reshape(tiles, gpt * SSM_CH, gpt * n)

    def ct(c):
        c = jnp.transpose(c.reshape(tiles, gpt, SSM_CH, n), (0, 1, 3, 2))
        return (c[:, :, :, None, :] * eye[None, :, None, :, None]).reshape(tiles, gpt * n, gpt * SSM_CH)

    sh, ln = STATE_HALF, LANES
    tile3 = lambda a, b: pl.BlockSpec((1, a, b), lambda t: (t, 0, 0))
    return pl.pallas_call(
        _ssm_prep_kernel,
        grid=(tiles,),
        in_specs=[tile3(3, sh), tile3(sh, 3), tile3(ln, sh), tile3(ln, sh), tile3(sh, ln), tile3(sh, ln)],
        out_specs=[tile3(PLANES * ln, 2 * sh), tile3(2 * sh, PLANES * ln), tile3(PLANES * ln, PLANES * ln),
                   tile3(2 * sh, ln), tile3(SUBLANES, 2 * sh)],
        out_shape=[jax.ShapeDtypeStruct((tiles, PLANES * ln, 2 * sh), BF16),
                   jax.ShapeDtypeStruct((tiles, 2 * sh, PLANES * ln), BF16),
                   jax.ShapeDtypeStruct((tiles, PLANES * ln, PLANES * ln), BF16),
                   jax.ShapeDtypeStruct((tiles, 2 * sh, ln), BF16),
                   jax.ShapeDtypeStruct((tiles, SUBLANES, 2 * sh), F32)],
        compiler_params=_cparams(("arbitrary",)),
        name="ssm_prep",
    )(lam_row, lam_col, bt(b_re), bt(b_im), ct(c_re), ct(c_im))


def _inproj_kernel(first_keep, scale, x_ref, w_ref, q_ref, k_ref, v_ref, ga_ref, u_ref, gs_ref,
                   kl_ref, vl_ref, xb_ref, slab_ref):
    i, j = pl.program_id(1), pl.program_id(2)
    slabs = slab_ref.shape[0]
    prow = q_ref.shape[1]

    @pl.when(j == 0)
    def _():
        xb_ref[...] = x_ref[...].astype(BF16)

    acc = jnp.dot(xb_ref[...], w_ref[...], preferred_element_type=F32)

    def to_planes(val, out_ref):
        for c in range(slabs):
            slab_ref[c] = val[:, c * LANES:(c + 1) * LANES]
        for r in range(PLANES):
            rows = [slab_ref[c, pl.ds(r, prow, stride=PLANES), :] for c in range(slabs)]
            out_ref[r] = jnp.concatenate(rows, axis=1).astype(out_ref.dtype)

    @pl.when(j == 0)
    def _():
        to_planes(acc * scale, q_ref)

    @pl.when(j == 1)
    def _():
        to_planes(acc, k_ref)

    @pl.when(j == 2)
    def _():
        to_planes(acc, v_ref)

    @pl.when(jnp.logical_and(j == 1, i >= first_keep))
    def _():
        kl_ref[...] = acc.T

    @pl.when(jnp.logical_and(j == 2, i >= first_keep))
    def _():
        vl_ref[...] = acc.T

    @pl.when(j == 3)
    def _():
        ga_ref[...] = acc.astype(BF16)

    @pl.when(j == 4)
    def _():
        to_planes(acc, u_ref)

    @pl.when(j == 5)
    def _():
        gs_ref[...] = acc.astype(BF16)


def _inproj_prompt(x, w_bf, keep):
    bsz, seq, d_model = x.shape
    width = w_bf.shape[1] // 6
    rows = seq // PLANES
    tile = ROW_TILE
    prow = tile // PLANES
    first_keep = (seq - keep) // tile
    plane = lambda dt: jax.ShapeDtypeStruct((bsz, PLANES, rows, width), dt)
    plane_spec = pl.BlockSpec((None, PLANES, prow, width), lambda b, i, j: (b, 0, i, 0))
    token = jax.ShapeDtypeStruct((bsz, seq, width), BF16)
    token_spec = pl.BlockSpec((None, tile, width), lambda b, i, j: (b, i, 0))
    last = jax.ShapeDtypeStruct((bsz, width, keep), F32)
    last_spec = pl.BlockSpec((None, width, tile), lambda b, i, j: (b, 0, jnp.maximum(i - first_keep, 0)))
    return pl.pallas_call(
        functools.partial(_inproj_kernel, first_keep, HEAD_DIM ** -0.5),
        grid=(bsz, seq // tile, 6),
        in_specs=[pl.BlockSpec((None, tile, d_model), lambda b, i, j: (b, i, 0)),
                  pl.BlockSpec((d_model, width), lambda b, i, j: (0, j))],
        out_specs=[plane_spec, plane_spec, plane_spec, token_spec, plane_spec, token_spec, last_spec, last_spec],
        out_shape=[plane(F32), plane(F32), plane(F32), token, plane(BF16), token, last, last],
        scratch_shapes=[pltpu.VMEM((tile, d_model), BF16),
                        pltpu.VMEM((width // LANES, tile, LANES), F32)],
        compiler_params=_cparams(("arbitrary", "arbitrary", "arbitrary")),
        name="inproj_prompt",
    )(x, w_bf)


ATTN_UNROLL = 8


def _attn_kernel(q_ref, k_ref, v_ref, bias_ref, o_ref, acc_ref, m_ref, l_ref):
    sup = pl.program_id(2)
    is_a = lax.broadcasted_iota(jnp.int32, (BLK, LANES), 1) < HEAD_DIM
    base = pl.multiple_of(sup * BLK, BLK)

    def raw_scores(q, k):
        kb = k.astype(BF16)
        out = []
        for head in range(2):
            qm = jnp.where(is_a if head == 0 else jnp.logical_not(is_a), q, 0.0).astype(BF16)
            out.append(lax.dot_general(qm, kb, (((1,), (1,)), ((), ())), preferred_element_type=F32))
        return out

    def both(a, b):
        return jnp.where(is_a, a, b)

    def weights(s, table, state):
        m_new, p = [], []
        for h in range(2):
            sh = s[h] + bias_ref[h, table]
            mb = jnp.broadcast_to(jnp.max(sh, axis=1, keepdims=True), (BLK, LANES))
            mh = mb if state is None else jnp.maximum(state[0][h], mb)
            m_new.append(mh)
            p.append(jnp.exp(sh - jnp.concatenate([mh, mh], axis=1)).astype(BF16))
        return m_new, p

    ones = jnp.ones((2 * BLK, LANES), BF16)

    def combine(m_new, p, v, state):
        vext = jnp.concatenate([v.astype(BF16), ones], axis=1)
        res = [jnp.dot(p[h], vext, preferred_element_type=F32) for h in range(2)]
        acc_new = both(res[0][:, :LANES], res[1][:, :LANES])
        l_new = [res[h][:, LANES:] for h in range(2)]
        if state is not None:
            alpha = [jnp.exp(state[0][h] - m_new[h]) for h in range(2)]
            l_new = [alpha[h] * state[1][h] + l_new[h] for h in range(2)]
            acc_new = both(alpha[0], alpha[1]) * state[2] + acc_new
        return m_new, l_new, acc_new

    cat = lambda parts: parts[0] if len(parts) == 1 else jnp.concatenate(parts, axis=0)

    def run_pattern(a, dil, nblocks):
        npl = PLANES // dil
        qrows = BLK // npl
        per_res = BLK // qrows

        def body(grp, carry):
            work = []
            for un in range(ATTN_UNROLL):
                blk = grp * ATTN_UNROLL + un
                res, sub = blk // per_res, blk % per_res
                off = pl.multiple_of(sub * qrows, qrows)
                kstart = pl.multiple_of(jnp.maximum(base + off - qrows, 0), qrows)
                first = jnp.logical_and(sup == 0, sub == 0).astype(jnp.int32)
                planes = [res + dil * i for i in range(npl)]
                q = cat([q_ref[pln, pl.ds(base + off, qrows), :] for pln in planes])
                k = cat([k_ref[pln, pl.ds(kstart, 2 * qrows), :] for pln in planes])
                v = cat([v_ref[pln, pl.ds(kstart, 2 * qrows), :] for pln in planes])
                rows = lambda ref, *lead: cat([ref[(*lead, pln, pl.ds(off, qrows), slice(None))] for pln in planes])
                state = None if a == 0 else ([rows(m_ref, h) for h in range(2)],
                                             [rows(l_ref, h) for h in range(2)], rows(acc_ref))
                work.append((planes, off, q, k, v, 2 * a + first, state))
            raw = [raw_scores(q, k) for _, _, q, k, _, _, _ in work]
            soft = [weights(s, w[5], w[6]) for s, w in zip(raw, work)]
            done = [(w[0], w[1], combine(*sm, w[4], w[6])) for sm, w in zip(soft, work)]
            for planes, off, (m_new, l_new, acc_new) in done:
                for i, pln in enumerate(planes):
                    part = slice(i * qrows, (i + 1) * qrows)
                    for h in range(2):
                        m_ref[h, pln, pl.ds(off, qrows), :] = m_new[h][part]
                        l_ref[h, pln, pl.ds(off, qrows), :] = l_new[h][part]
                    acc_ref[pln, pl.ds(off, qrows), :] = acc_new[part]
            return carry

        lax.fori_loop(0, nblocks // ATTN_UNROLL, body, 0)

    for a, dil in enumerate(DILATIONS):
        run_pattern(a, dil, PLANES)
    for pln in range(PLANES):
        o_ref[pl.ds(pln, BLK, stride=PLANES), :] = acc_ref[pln] / both(l_ref[0, pln], l_ref[1, pln])


def _attention_prompt(q, k, v, bias):
    bsz, _, rows, width = q.shape
    pairs = width // LANES
    ntab = bias.shape[1]
    qkv_spec = pl.BlockSpec((None, PLANES, rows, LANES), lambda b, h, s: (b, 0, 0, h))
    return pl.pallas_call(
        _attn_kernel,
        grid=(bsz, pairs, rows // BLK),
        in_specs=[qkv_spec, qkv_spec, qkv_spec,
                  pl.BlockSpec((2, ntab, BLK, 2 * BLK), lambda b, h, s: (h, 0, 0, 0))],
        out_specs=pl.BlockSpec((None, PLANES * BLK, LANES), lambda b, h, s: (b, s, h)),
        out_shape=jax.ShapeDtypeStruct((bsz, PLANES * rows, width), F32),
        scratch_shapes=[pltpu.VMEM((PLANES, BLK, LANES), F32),
                        pltpu.VMEM((2, PLANES, BLK, LANES), F32),
                        pltpu.VMEM((2, PLANES, BLK, LANES), F32)],
        compiler_params=_cparams(("arbitrary", "arbitrary", "arbitrary")),
        name="attention_prompt",
    )(q, k, v, bias)


def _ssm_kernel(u_ref, ws_ref, wx_ref, m_ref, coef_ref, d_ref, y_ref, sre_ref, sim_ref,
                ucat_ref, s_ref, x_ref):
    tc = pl.program_id(2)
    chunks = u_ref.shape[1]
    per_step = m_ref.shape[1] // LANES

    @pl.when(tc == 0)
    def _():
        for s in range(PLANES):
            ucat_ref[:, s * LANES:(s + 1) * LANES] = u_ref[s]
        s_ref[...] = jnp.dot(ucat_ref[...], ws_ref[...], preferred_element_type=F32)
        a_re = coef_ref[1:2, 0:STATE_HALF]
        a_im = coef_ref[1:2, STATE_HALF:2 * STATE_HALF]

        def step(c, carry):
            xr, xi = carry
            x_ref[pl.ds(c, 1), 0:STATE_HALF] = xr
            x_ref[pl.ds(c, 1), STATE_HALF:2 * STATE_HALF] = xi
            sr = s_ref[pl.ds(c, 1), 0:STATE_HALF]
            si = s_ref[pl.ds(c, 1), STATE_HALF:2 * STATE_HALF]
            return a_re * xr - a_im * xi + sr, a_re * xi + a_im * xr + si

        zero = jnp.zeros((1, STATE_HALF), F32)
        xr, xi = lax.fori_loop(0, chunks, step, (zero, zero))
        sre_ref[...] = xr
        sim_ref[...] = xi

    y = jnp.dot(ucat_ref[...], m_ref[...], preferred_element_type=F32)
    y = y + jnp.dot(x_ref[...].astype(BF16), wx_ref[...], preferred_element_type=F32)
    d = d_ref[...]
    for t in range(per_step):
        tok = tc * per_step + t
        y_ref[pl.ds(tok, chunks, stride=PLANES), :] = y[:, t * LANES:(t + 1) * LANES] + d * u_ref[tok].astype(F32)


def _ssm_prompt(u, ws, wx, m, coef, d_skip):
    bsz, _, chunks, width = u.shape
    tiles = width // LANES
    per_step = 4
    steps = PLANES // per_step
    sh = STATE_HALF
    state = jax.ShapeDtypeStruct((bsz, tiles, 1, sh), F32)
    state_spec = pl.BlockSpec((None, None, 1, sh), lambda t, b, c: (b, t, 0, 0))
    return pl.pallas_call(
        _ssm_kernel,
        grid=(tiles, bsz, steps),
        in_specs=[pl.BlockSpec((None, PLANES, chunks, LANES), lambda t, b, c: (b, 0, 0, t)),
                  pl.BlockSpec((None, PLANES * LANES, 2 * sh), lambda t, b, c: (t, 0, 0)),
                  pl.BlockSpec((None, 2 * sh, per_step * LANES), lambda t, b, c: (t, 0, c)),
                  pl.BlockSpec((None, PLANES * LANES, per_step * LANES), lambda t, b, c: (t, 0, c)),
                  pl.BlockSpec((None, SUBLANES, 2 * sh), lambda t, b, c: (t, 0, 0)),
                  pl.BlockSpec((1, LANES), lambda t, b, c: (0, t))],
        out_specs=[pl.BlockSpec((None, PLANES * chunks, LANES), lambda t, b, c: (b, 0, t)),
                   state_spec, state_spec],
        out_shape=[jax.ShapeDtypeStruct((bsz, PLANES * chunks, width), F32), state, state],
        scratch_shapes=[pltpu.VMEM((chunks, PLANES * LANES), BF16),
                        pltpu.VMEM((chunks, 2 * sh), F32),
                        pltpu.VMEM((chunks, 2 * sh), F32)],
        compiler_params=_cparams(("arbitrary", "arbitrary", "arbitrary")),
        name="ssm_prompt",
    )(u, ws, wx, m, coef, d_skip)


def _epilogue_kernel(alpha, attn_ref, ga_ref, y_ref, gs_ref, x_ref, wglu_ref, bglu_ref, wo_ref, bo_ref,
                     g_ref, b_ref, o_ref):
    d_attn = attn_ref.shape[-1]
    br_a = (attn_ref[...].astype(F32) * jax.nn.silu(ga_ref[...].astype(F32))).astype(BF16)
    z = jax.nn.gelu(y_ref[...].astype(F32))
    gate = jax.nn.sigmoid(jnp.dot(z.astype(BF16), wglu_ref[...], preferred_element_type=F32) + bglu_ref[...])
    br_s = (z * gate * jax.nn.silu(gs_ref[...].astype(F32))).astype(BF16)
    mix = (jnp.dot(br_a, wo_ref[0:d_attn, :], preferred_element_type=F32)
           + jnp.dot(br_s, wo_ref[d_attn:, :], preferred_element_type=F32) + bo_ref[...])
    t = alpha * x_ref[...] + mix
    mu = jnp.mean(t, axis=-1, keepdims=True)
    var = jnp.mean(jnp.square(t - mu), axis=-1, keepdims=True)
    o_ref[...] = (t - mu) * lax.rsqrt(var + LN_EPS) * g_ref[...] + b_ref[...]


def _epilogue_weights_specs(d_mix, d_ssm, d_model, imap):
    return [pl.BlockSpec((d_ssm, d_ssm), imap), pl.BlockSpec((1, d_ssm), imap),
            pl.BlockSpec((d_mix, d_model), imap), pl.BlockSpec((1, d_model), imap),
            pl.BlockSpec((1, d_model), imap), pl.BlockSpec((1, d_model), imap)]


def _epilogue_prompt(alpha, attn, ga, y, gs, x, weights):
    bsz, seq, width = attn.shape
    d_model = x.shape[-1]
    tile = ROW_TILE // 2
    wspec = pl.BlockSpec((None, tile, width), lambda b, i: (b, i, 0))
    xspec = pl.BlockSpec((None, tile, d_model), lambda b, i: (b, i, 0))
    return pl.pallas_call(
        functools.partial(_epilogue_kernel, alpha),
        grid=(bsz, seq // tile),
        in_specs=[wspec, wspec, wspec, wspec, xspec]
        + _epilogue_weights_specs(2 * width, width, d_model, lambda b, i: (0, 0)),
        out_specs=xspec,
        out_shape=jax.ShapeDtypeStruct(x.shape, F32),
        compiler_params=_cparams(("arbitrary", "arbitrary")),
        name="epilogue_prompt",
    )(attn, ga, y, gs, x, *weights)


def _epilogue_sample(alpha, attn, h, y, x, weights):
    rows, width = attn.shape
    d_model = x.shape[-1]
    col = lambda c: pl.BlockSpec((rows, width), lambda i: (0, c))
    return pl.pallas_call(
        functools.partial(_epilogue_kernel, alpha),
        grid=(1,),
        in_specs=[col(0), col(3), col(0), col(5), pl.BlockSpec((rows, d_model), lambda i: (0, 0))]
        + _epilogue_weights_specs(2 * width, width, d_model, lambda i: (0, 0)),
        out_specs=pl.BlockSpec((rows, d_model), lambda i: (0, 0)),
        out_shape=jax.ShapeDtypeStruct((rows, d_model), F32),
        compiler_params=_cparams(("arbitrary",)),
        name="epilogue_sample",
    )(attn, h, y, h, x, *weights)


def _inproj_sample_kernel(scale, x_ref, w_ref, o_ref):
    acc = jnp.dot(x_ref[...].astype(BF16), w_ref[...], preferred_element_type=F32)
    o_ref[...] = acc * jnp.where(pl.program_id(0) == 0, scale, 1.0)


def _inproj_sample(x, w_bf):
    rows, d_model = x.shape
    width = w_bf.shape[1] // 6
    return pl.pallas_call(
        functools.partial(_inproj_sample_kernel, HEAD_DIM ** -0.5),
        grid=(6,),
        in_specs=[pl.BlockSpec((rows, d_model), lambda j: (0, 0)),
                  pl.BlockSpec((d_model, width), lambda j: (0, j))],
        out_specs=pl.BlockSpec((rows, width), lambda j: (0, j)),
        out_shape=jax.ShapeDtypeStruct((rows, 6 * width), F32),
        compiler_params=_cparams(("arbitrary",)),
        name="inproj_sample",
    )(x, w_bf)


SAMPLE_PAIRS = 4


def _attn_sample_kernel(q_ref, kn_ref, vn_ref, kt_ref, vt_ref, tb_ref, tn_ref, o_ref):
    s_len = q_ref.shape[0]
    buf = kt_ref.shape[2]
    npat = tb_ref.shape[0]
    is_a = lax.broadcasted_iota(jnp.int32, (s_len, LANES), 1) < HEAD_DIM
    nt = (((1,), (1,)), ((), ()))
    pad = jnp.zeros((2 * s_len - s_len, LANES), F32)
    for pp in range(SAMPLE_PAIRS):
        lanes = slice(pp * LANES, (pp + 1) * LANES)
        rows = slice(pp * 2 * s_len, (pp + 1) * 2 * s_len)
        qp = q_ref[:, lanes]
        q2 = jnp.concatenate([jnp.where(is_a, qp, 0.0), jnp.where(is_a, 0.0, qp)], axis=0).astype(BF16)
        kt = kt_ref[2 * pp:2 * pp + 2].reshape(2 * HEAD_DIM, buf).astype(BF16)
        vt = vt_ref[2 * pp:2 * pp + 2].reshape(2 * HEAD_DIM, buf).astype(BF16)
        kn = jnp.concatenate([kn_ref[:, lanes], pad], axis=0).astype(BF16)
        vn = jnp.concatenate([vn_ref[:, lanes], pad], axis=0).astype(BF16)
        s_buf = jnp.dot(q2, kt, preferred_element_type=F32)
        s_new = lax.dot_general(q2, kn, nt, preferred_element_type=F32)
        z_buf = [s_buf + tb_ref[a, rows, :] for a in range(npat)]
        z_new = [s_new + tn_ref[a, rows, 0:2 * s_len] for a in range(npat)]
        top = functools.reduce(jnp.maximum, [jnp.max(z, axis=1, keepdims=True) for z in z_buf + z_new])
        w_buf = functools.reduce(jnp.add, [jnp.exp(z - top) for z in z_buf])
        w_new = functools.reduce(jnp.add, [jnp.exp(z - top) for z in z_new])
        den = jnp.sum(w_buf, axis=1, keepdims=True) + jnp.sum(w_new, axis=1, keepdims=True)
        o = lax.dot_general(w_buf.astype(BF16), vt, nt, preferred_element_type=F32)
        o = (o + jnp.dot(w_new.astype(BF16), vn, preferred_element_type=F32)) / den
        o_ref[:, lanes] = jnp.where(is_a, o[0:s_len], o[s_len:2 * s_len])


def _attention_sample(h, kt, vt, tb, tn):
    bsz, s_len, _ = h.shape
    heads, _, buf = kt.shape[1:]
    width = heads * HEAD_DIM
    gw = SAMPLE_PAIRS * LANES
    per_w = width // gw
    new = lambda c: pl.BlockSpec((None, s_len, gw), lambda b, g: (b, 0, c * per_w + g))
    cache = pl.BlockSpec((None, 2 * SAMPLE_PAIRS, HEAD_DIM, buf), lambda b, g: (b, g, 0, 0))
    trows = SAMPLE_PAIRS * 2 * s_len
    return pl.pallas_call(
        _attn_sample_kernel,
        grid=(bsz, per_w),
        in_specs=[new(0), new(1), new(2), cache, cache,
                  pl.BlockSpec((tb.shape[0], trows, buf), lambda b, g: (0, g, 0)),
                  pl.BlockSpec((tn.shape[0], trows, LANES), lambda b, g: (0, g, 0))],
        out_specs=pl.BlockSpec((None, s_len, gw), lambda b, g: (b, 0, g)),
        out_shape=jax.ShapeDtypeStruct((bsz, s_len, width), F32),
        compiler_params=_cparams(("arbitrary", "arbitrary")),
        name="attention_sample",
    )(h, h, h, kt, vt, tb, tn)


def _ssm_sample_kernel(s_len, u_ref, b0_ref, c0_ref, coef_ref, d_ref, x0r_ref, x0i_ref,
                       y_ref, sre_ref, sim_ref, bu_ref, xs_ref):
    bsz = x0r_ref.shape[0]
    bu = jnp.dot(u_ref[...].astype(BF16), b0_ref[...], preferred_element_type=F32)
    slabs = bu_ref.shape[0]
    for c in range(slabs):
        bu_ref[c] = bu[:, c * LANES:(c + 1) * LANES]
    a_re = coef_ref[0:1, 0:STATE_HALF]
    a_im = coef_ref[0:1, STATE_HALF:2 * STATE_HALF]
    half = slabs // 2
    xr, xi = x0r_ref[...], x0i_ref[...]
    for s in range(s_len):
        step = lambda c: bu_ref[c, pl.ds(s, bsz, stride=s_len), :]
        br = jnp.concatenate([step(c) for c in range(half)], axis=1)
        bi = jnp.concatenate([step(c) for c in range(half, slabs)], axis=1)
        xr, xi = a_re * xr - a_im * xi + br, a_re * xi + a_im * xr + bi
        xs_ref[s * bsz:(s + 1) * bsz, :] = jnp.concatenate([xr, xi], axis=1)
    sre_ref[...] = xr
    sim_ref[...] = xi
    y = jnp.dot(xs_ref[...].astype(BF16), c0_ref[...], preferred_element_type=F32)
    d = d_ref[...]
    for s in range(s_len):
        y_ref[pl.ds(s, bsz, stride=s_len), :] = (y[s * bsz:(s + 1) * bsz, :]
                                                 + d * u_ref[pl.ds(s, bsz, stride=s_len), :])


def _ssm_sample(h, ws, c0, coef, d_skip, x0_re, x0_im, s_len):
    rows = h.shape[0]
    bsz = rows // s_len
    tiles = ws.shape[0]
    sh = STATE_HALF
    u_col0 = 4 * (h.shape[1] // 6) // LANES
    st_spec = pl.BlockSpec((bsz, sh), lambda t: (0, t))
    state = jax.ShapeDtypeStruct((bsz, tiles * sh), F32)
    return pl.pallas_call(
        functools.partial(_ssm_sample_kernel, s_len),
        grid=(tiles,),
        in_specs=[pl.BlockSpec((rows, LANES), lambda t: (0, u_col0 + t)),
                  pl.BlockSpec((None, LANES, 2 * sh), lambda t: (t, PLANES - 1, 0)),
                  pl.BlockSpec((None, 2 * sh, LANES), lambda t: (t, 0, 0)),
                  pl.BlockSpec((None, SUBLANES, 2 * sh), lambda t: (t, 0, 0)),
                  pl.BlockSpec((1, LANES), lambda t: (0, t)),
                  st_spec, st_spec],
        out_specs=[pl.BlockSpec((rows, LANES), lambda t: (0, t)), st_spec, st_spec],
        out_shape=[jax.ShapeDtypeStruct((rows, tiles * LANES), F32), state, state],
        scratch_shapes=[pltpu.VMEM((2 * sh // LANES, rows, LANES), F32),
                        pltpu.VMEM((rows, 2 * sh), F32)],
        compiler_params=_cparams(("arbitrary",)),
        name="ssm_sample",
    )(h, ws, c0, coef, d_skip, x0_re, x0_im)


def kernel(x_prompt, x_sample, cache_k, cache_v, state_ssm_re, state_ssm_im, w_in, w_out, b_out, rel_bias,
           lam_re, lam_im, log_dt, b_re, b_im, c_re, c_im, d_skip, w_glu, b_glu, ln_g, ln_b):
    depth = w_in.shape[0]
    assert depth == 1, "one layer per step"
    bsz, seq, d_model = x_prompt.shape
    dbsz, s_len, _ = x_sample.shape
    buf, heads = cache_k.shape[2], cache_k.shape[3]
    width = heads * HEAD_DIM
    groups, nstate = lam_re.shape[1], lam_re.shape[2]
    keep = min(MAX_DISTANCE, seq)
    assert seq % (PLANES * BLK) == 0 and seq >= 2 * PLANES * BLK and keep % ROW_TILE == 0
    assert buf == KPER * max(DILATIONS) and s_len <= min(DILATIONS[:-1])
    assert nstate == SSM_STATE and width == groups * SSM_CH and heads % (2 * SAMPLE_PAIRS) == 0
    alpha = (2 * depth) ** 0.25
    npat = len(DILATIONS)

    w_bf = w_in[0].astype(BF16)
    row = lambda v: v.reshape(1, -1)
    weights = (w_glu[0].astype(BF16), row(b_glu[0]), w_out[0].astype(BF16), row(b_out[0]),
               row(ln_g[0]), row(ln_b[0]))
    d_row = row(d_skip[0])

    rbt = rel_bias.T
    ptab = _bias_tables(jnp.asarray(_prompt_bucket_tables().reshape(1, -1)), rbt, 8192)
    ptab = ptab.reshape(heads, 2 * npat, BLK, 2 * BLK)
    key_w = buf + 2 * LANES
    stab = _bias_tables(jnp.asarray(_sample_bucket_tables(buf, s_len, key_w)), rbt, key_w)
    stab = jnp.transpose(stab.reshape(heads, npat, s_len, key_w), (1, 0, 2, 3)).reshape(npat, heads * s_len, key_w)
    stab_buf, stab_new = stab[:, :, :buf], stab[:, :, buf:buf + LANES]
    ws, wx, m_intra, c0, coef = _ssm_prep(lam_re[0], lam_im[0], log_dt[0], b_re[0], b_im[0], c_re[0], c_im[0])

    q, k, v, ga, u, gs, kl_t, vl_t = _inproj_prompt(x_prompt, w_bf, keep)
    attn = _attention_prompt(q, k, v, ptab)
    y, sre_p, sim_p = _ssm_prompt(u, ws, wx, m_intra, coef, d_row)
    y_prompt = _epilogue_prompt(alpha, attn, ga, y, gs, x_prompt, weights)
    last = lambda t: jnp.transpose(t.reshape(bsz, heads, HEAD_DIM, keep), (0, 3, 1, 2))[None]
    st_shape = (1, bsz, groups, nstate)

    xs = x_sample.reshape(dbsz * s_len, d_model)
    hs = _inproj_sample(xs, w_bf)
    pos_minor = lambda c: jnp.transpose(c[0], (0, 2, 3, 1))
    attn_s = _attention_sample(hs.reshape(dbsz, s_len, -1), pos_minor(cache_k), pos_minor(cache_v),
                               stab_buf, stab_new)
    y_s, sre_s, sim_s = _ssm_sample(hs, ws, c0, coef, d_row,
                                    state_ssm_re[0].astype(F32).reshape(dbsz, groups * nstate),
                                    state_ssm_im[0].astype(F32).reshape(dbsz, groups * nstate), s_len)
    y_sample = _epilogue_sample(alpha, attn_s.reshape(dbsz * s_len, width), hs, y_s, xs, weights)
    new_shape = (1, dbsz, s_len, heads, HEAD_DIM)
    sst_shape = (1, dbsz, groups, nstate)
    return (y_prompt, y_sample.reshape(dbsz, s_len, d_model), last(kl_t), last(vl_t),
            sre_p.reshape(st_shape), sim_p.reshape(st_shape),
            hs[:, width:2 * width].reshape(new_shape), hs[:, 2 * width:3 * width].reshape(new_shape),
            sre_s.reshape(sst_shape), sim_s.reshape(sst_shape))
```

```python
import functools
import math

import jax
import jax.numpy as jnp
import numpy as np
from jax import lax
from jax.experimental import pallas as pl
from jax.experimental.pallas import tpu as pltpu

F32 = jnp.float32
BF16 = jnp.bfloat16

HEAD_DIM = 64
SSM_CH = 16
SSM_STATE = 64
NUM_BUCKETS = 32
MAX_DISTANCE = 2048
KPER = 128
BLK = 128
DILATIONS = (16, 4, 1)
LN_EPS = 1e-5
NEG = -1e30

LANES = 128
SUBLANES = 8
PLANES = 16
GROUPS_PER_TILE = LANES // SSM_CH
STATE_HALF = GROUPS_PER_TILE * SSM_STATE
VMEM_LIMIT = 56 * 1024 * 1024
ROW_TILE = 512
ONCE = pl.Buffered(1)


def _cparams(sem, vmem=VMEM_LIMIT):
    return pltpu.CompilerParams(dimension_semantics=sem, vmem_limit_bytes=vmem)


def _bucket_np(dist):
    exact = NUM_BUCKETS // 2
    d_f = np.maximum(dist, 1).astype(np.float32)
    large = exact + (np.log(d_f / np.float32(exact)) / np.float32(math.log(MAX_DISTANCE / exact))
                     * np.float32(NUM_BUCKETS - exact)).astype(np.int32)
    large = np.minimum(large, NUM_BUCKETS - 1)
    return np.where(dist < exact, dist, large).astype(np.int32)


def _prompt_rel(dil, first):
    i = np.arange(BLK)[:, None]
    j = np.arange(2 * BLK)[None, :]
    npl = PLANES // dil
    qrows = BLK // npl
    pq, ml = i // qrows, i % qrows
    pk, jl = j // (2 * qrows), j % (2 * qrows)
    back = 0 if first else qrows
    return npl * (ml - jl + back) + (pq - pk)


def _prompt_bucket_tables():
    tabs = []
    for dil in DILATIONS:
        for first in (False, True):
            rel = _prompt_rel(dil, first)
            valid = (rel >= 0) & (rel <= KPER)
            tabs.append(np.where(valid, _bucket_np(np.clip(rel, 0, KPER) * dil), NUM_BUCKETS))
    return np.stack(tabs).reshape(len(DILATIONS) * 2, BLK * 2 * BLK).astype(np.int32)


def _sample_bucket_tables(buf, s_len, width):
    pos = np.arange(buf + s_len)
    tabs = np.full((len(DILATIONS), s_len, width), NUM_BUCKETS, np.int32)
    for a, dil in enumerate(DILATIONS):
        for s in range(s_len):
            dist = buf + s - pos
            valid = (dist >= 0) & (dist % dil == 0) & (dist // dil <= KPER)
            tabs[a, s, :len(pos)] = np.where(valid, _bucket_np(np.maximum(dist, 0)), NUM_BUCKETS)
    return tabs.reshape(1, -1)


def _bias_kernel(idx_ref, rbt_ref, o_ref):
    idx = idx_ref[...]
    onehot = (lax.broadcasted_iota(jnp.int32, (NUM_BUCKETS, idx.shape[1]), 0) == idx).astype(F32)
    tab = jnp.dot(rbt_ref[...], onehot, precision=lax.Precision.HIGHEST,
                  preferred_element_type=F32)
    o_ref[...] = jnp.where(idx < NUM_BUCKETS, tab, NEG)


def _bias_tables(idx, rel_bias_t, chunk):
    n_heads = rel_bias_t.shape[0]
    total = idx.shape[1]
    return pl.pallas_call(
        _bias_kernel,
        grid=(total // chunk,),
        in_specs=[pl.BlockSpec((1, chunk), lambda c: (0, c)),
                  pl.BlockSpec((n_heads, NUM_BUCKETS), lambda c: (0, 0))],
        out_specs=pl.BlockSpec((n_heads, chunk), lambda c: (0, c)),
        out_shape=jax.ShapeDtypeStruct((n_heads, total), F32),
        compiler_params=_cparams(("arbitrary",)),
        name="bias_tables",
    )(idx, rel_bias_t)


def _discretize(lr, li, ldt):
    lr = jnp.minimum(lr, -1e-4)
    dt = jnp.exp(ldt)
    mag = jnp.exp(lr * dt)
    ab_re, ab_im = mag * jnp.cos(li * dt), mag * jnp.sin(li * dt)
    den = lr * lr + li * li
    inv_re, inv_im = lr / den, -li / den
    n_re, n_im = ab_re - 1.0, ab_im
    cf_re = n_re * inv_re - n_im * inv_im
    cf_im = n_re * inv_im + n_im * inv_re
    return ab_re, ab_im, cf_re, cf_im


def _ssm_prep_kernel(lam_row_ref, lam_col_ref, bt_re_ref, bt_im_ref, ct_re_ref, ct_im_ref,
                     ws_ref, wx_ref, m_ref, c0_ref, coef_ref):
    row = lam_row_ref[0]
    ab_re, ab_im, cf_re, cf_im = _discretize(row[0:1], row[1:2], row[2:3])
    bt_re, bt_im = bt_re_ref[0], bt_im_ref[0]
    bb_re = cf_re * bt_re - cf_im * bt_im
    bb_im = cf_re * bt_im + cf_im * bt_re
    col = lam_col_ref[0]
    abc_re, abc_im, _, _ = _discretize(col[:, 0:1], col[:, 1:2], col[:, 2:3])
    ct_re, ct_im = ct_re_ref[0], ct_im_ref[0]
    c0 = jnp.concatenate([ct_re, -ct_im], axis=0)
    c0_ref[0] = c0.astype(BF16)

    pr, pi = jnp.ones_like(ab_re), jnp.zeros_like(ab_re)
    qr, qi = abc_re, abc_im
    zero_blk = jnp.zeros((LANES, LANES), BF16)
    for lag in range(PLANES):
        w = jnp.concatenate([pr * bb_re - pi * bb_im, pr * bb_im + pi * bb_re], axis=1)
        s = PLANES - 1 - lag
        ws_ref[0, s * LANES:(s + 1) * LANES, :] = w.astype(BF16)
        m_lag = jnp.dot(w, c0, precision=lax.Precision.HIGHEST, preferred_element_type=F32).astype(BF16)
        for t in range(PLANES):
            src = t - lag
            if src >= 0:
                m_ref[0, src * LANES:(src + 1) * LANES, t * LANES:(t + 1) * LANES] = m_lag
            else:
                m_ref[0, (PLANES + src) * LANES:(PLANES + src + 1) * LANES, t * LANES:(t + 1) * LANES] = zero_blk
        wx_ref[0, 0:STATE_HALF, lag * LANES:(lag + 1) * LANES] = (ct_re * qr - ct_im * qi).astype(BF16)
        wx_ref[0, STATE_HALF:2 * STATE_HALF, lag * LANES:(lag + 1) * LANES] = (-(ct_re * qi + ct_im * qr)).astype(BF16)
        pr, pi = pr * ab_re - pi * ab_im, pr * ab_im + pi * ab_re
        qr, qi = qr * abc_re - qi * abc_im, qr * abc_im + qi * abc_re
    coef_ref[0] = jnp.concatenate([
        jnp.concatenate([ab_re, ab_im], axis=1),
        jnp.concatenate([pr, pi], axis=1),
        jnp.zeros((SUBLANES - 2, 2 * STATE_HALF), F32)], axis=0)


def _ssm_prep(lam_re, lam_im, log_dt, b_re, b_im, c_re, c_im):
    groups, n = lam_re.shape
    tiles = groups // GROUPS_PER_TILE
    gpt = GROUPS_PER_TILE
    eye = jnp.eye(gpt, dtype=F32)

    def rows(v):
        return v.reshape(tiles, gpt * n)

    ldt = jnp.broadcast_to(log_dt[:, None], (groups, n))
    lam_row = jnp.stack([rows(lam_re), rows(lam_im), rows(ldt)], axis=1)
    lam_col = jnp.transpose(lam_row, (0, 2, 1))

    def bt(b):
        b = jnp.transpose(b.reshape(tiles, gpt, n, SSM_CH), (0, 1, 3, 2))
        return (b[:, :, :, None, :] * eye[None, :, None, :, None]).reshape(tiles, gpt * SSM_CH, gpt * n)

    def ct(c):
        c = jnp.transpose(c.reshape(tiles, gpt, SSM_CH, n), (0, 1, 3, 2))
        return (c[:, :, :, None, :] * eye[None, :, None, :, None]).reshape(tiles, gpt * n, gpt * SSM_CH)

    sh, ln = STATE_HALF, LANES
    tile3 = lambda a, b: pl.BlockSpec((1, a, b), lambda t: (t, 0, 0))
    return pl.pallas_call(
        _ssm_prep_kernel,
        grid=(tiles,),
        in_specs=[tile3(3, sh), tile3(sh, 3), tile3(ln, sh), tile3(ln, sh), tile3(sh, ln), tile3(sh, ln)],
        out_specs=[tile3(PLANES * ln, 2 * sh), tile3(2 * sh, PLANES * ln), tile3(PLANES * ln, PLANES * ln),
                   tile3(2 * sh, ln), tile3(SUBLANES, 2 * sh)],
        out_shape=[jax.ShapeDtypeStruct((tiles, PLANES * ln, 2 * sh), BF16),
                   jax.ShapeDtypeStruct((tiles, 2 * sh, PLANES * ln), BF16),
                   jax.ShapeDtypeStruct((tiles, PLANES * ln, PLANES * ln), BF16),
                   jax.ShapeDtypeStruct((tiles, 2 * sh, ln), BF16),
                   jax.ShapeDtypeStruct((tiles, SUBLANES, 2 * sh), F32)],
        compiler_params=_cparams(("arbitrary",)),
        name="ssm_prep",
    )(lam_row, lam_col, bt(b_re), bt(b_im), ct(c_re), ct(c_im))


def _inproj_kernel(first_keep, scale, x_ref, wqkv_ref, wu_ref, q_ref, k_ref, v_ref, u_ref,
                   kl_ref, vl_ref, slab_ref):
    i = pl.program_id(1)
    slabs = slab_ref.shape[1]
    prow = q_ref.shape[1]
    width = u_ref.shape[-1]
    xb = x_ref[...].astype(BF16)

    def to_planes(val, out_ref, buf):
        for c in range(slabs):
            slab_ref[buf, c] = val[:, c * LANES:(c + 1) * LANES]
        for r in range(PLANES):
            rows = [slab_ref[buf, c, pl.ds(r, prow, stride=PLANES), :] for c in range(slabs)]
            out_ref[r] = jnp.concatenate(rows, axis=1).astype(out_ref.dtype)

    q = jnp.dot(xb, wqkv_ref[:, 0:width], preferred_element_type=F32) * scale
    to_planes(q, q_ref, 0)
    k = jnp.dot(xb, wqkv_ref[:, width:2 * width], preferred_element_type=F32)
    to_planes(k, k_ref, 1)
    v = jnp.dot(xb, wqkv_ref[:, 2 * width:3 * width], preferred_element_type=F32)
    to_planes(v, v_ref, 0)
    u = jnp.dot(xb, wu_ref[...], preferred_element_type=F32)
    to_planes(u, u_ref, 1)

    @pl.when(i >= first_keep)
    def _():
        kl_ref[...] = k.T
        vl_ref[...] = v.T


def _inproj_prompt(x, w_bf, keep):
    bsz, seq, d_model = x.shape
    width = w_bf.shape[1] // 6
    rows = seq // PLANES
    tile = ROW_TILE // 2
    prow = tile // PLANES
    first_keep = (seq - keep) // tile
    plane = lambda dt: jax.ShapeDtypeStruct((bsz, PLANES, rows, width), dt)
    plane_spec = pl.BlockSpec((None, PLANES, prow, width), lambda b, i: (b, 0, i, 0))
    last = jax.ShapeDtypeStruct((bsz, width, keep), F32)
    last_spec = pl.BlockSpec((None, width, tile), lambda b, i: (b, 0, jnp.maximum(i - first_keep, 0)))
    return pl.pallas_call(
        functools.partial(_inproj_kernel, first_keep, HEAD_DIM ** -0.5),
        grid=(bsz, seq // tile),
        in_specs=[pl.BlockSpec((None, tile, d_model), lambda b, i: (b, i, 0)),
                  pl.BlockSpec((d_model, 3 * width), lambda b, i: (0, 0), pipeline_mode=ONCE),
                  pl.BlockSpec((d_model, width), lambda b, i: (0, 4), pipeline_mode=ONCE)],
        out_specs=[plane_spec, plane_spec, plane_spec, plane_spec, last_spec, last_spec],
        out_shape=[plane(F32), plane(F32), plane(F32), plane(BF16), last, last],
        scratch_shapes=[pltpu.VMEM((2, width // LANES, tile, LANES), F32)],
        compiler_params=_cparams(("arbitrary", "arbitrary")),
        name="inproj_prompt",
    )(x, w_bf, w_bf)


ATTN_UNROLL = 8


def _attn_kernel(q_ref, k_ref, v_ref, bias_ref, o_ref, acc_ref, m_ref, l_ref):
    sup = pl.program_id(2)
    is_a = lax.broadcasted_iota(jnp.int32, (BLK, LANES), 1) < HEAD_DIM
    base = pl.multiple_of(sup * BLK, BLK)

    def raw_scores(q, k):
        kb = k.astype(BF16)
        out = []
        for head in range(2):
            qm = jnp.where(is_a if head == 0 else jnp.logical_not(is_a), q, 0.0).astype(BF16)
            out.append(lax.dot_general(qm, kb, (((1,), (1,)), ((), ())), preferred_element_type=F32))
        return out

    def both(a, b):
        return jnp.where(is_a, a, b)

    def weights(s, table, state):
        m_new, p = [], []
        for h in range(2):
            sh = s[h] + bias_ref[h, table]
            mb = jnp.broadcast_to(jnp.max(sh, axis=1, keepdims=True), (BLK, LANES))
            mh = mb if state is None else jnp.maximum(state[0][h], mb)
            m_new.append(mh)
            p.append(jnp.exp(sh - jnp.concatenate([mh, mh], axis=1)).astype(BF16))
        return m_new, p

    ones = jnp.ones((2 * BLK, LANES), BF16)

    def combine(m_new, p, v, state):
        vext = jnp.concatenate([v.astype(BF16), ones], axis=1)
        res = [jnp.dot(p[h], vext, preferred_element_type=F32) for h in range(2)]
        acc_new = both(res[0][:, :LANES], res[1][:, :LANES])
        l_new = [res[h][:, LANES:] for h in range(2)]
        if state is not None:
            alpha = [jnp.exp(state[0][h] - m_new[h]) for h in range(2)]
            l_new = [alpha[h] * state[1][h] + l_new[h] for h in range(2)]
            acc_new = both(alpha[0], alpha[1]) * state[2] + acc_new
        return m_new, l_new, acc_new

    cat = lambda parts: parts[0] if len(parts) == 1 else jnp.concatenate(parts, axis=0)

    def run_pattern(a, dil, nblocks):
        npl = PLANES // dil
        qrows = BLK // npl
        per_res = BLK // qrows

        def body(grp, carry):
            work = []
            for un in range(ATTN_UNROLL):
                blk = grp * ATTN_UNROLL + un
                res, sub = blk // per_res, blk % per_res
                off = pl.multiple_of(sub * qrows, qrows)
                kstart = pl.multiple_of(jnp.maximum(base + off - qrows, 0), qrows)
                first = jnp.logical_and(sup == 0, sub == 0).astype(jnp.int32)
                planes = [res + dil * i for i in range(npl)]
                q = cat([q_ref[pln, pl.ds(base + off, qrows), :] for pln in planes])
                k = cat([k_ref[pln, pl.ds(kstart, 2 * qrows), :] for pln in planes])
                v = cat([v_ref[pln, pl.ds(kstart, 2 * qrows), :] for pln in planes])
                rows = lambda ref, *lead: cat([ref[(*lead, pln, pl.ds(off, qrows), slice(None))] for pln in planes])
                state = None if a == 0 else ([rows(m_ref, h) for h in range(2)],
                                             [rows(l_ref, h) for h in range(2)], rows(acc_ref))
                work.append((planes, off, q, k, v, 2 * a + first, state))
            raw = [raw_scores(q, k) for _, _, q, k, _, _, _ in work]
            soft = [weights(s, w[5], w[6]) for s, w in zip(raw, work)]
            done = [(w[0], w[1], combine(*sm, w[4], w[6])) for sm, w in zip(soft, work)]
            for planes, off, (m_new, l_new, acc_new) in done:
                for i, pln in enumerate(planes):
                    part = slice(i * qrows, (i + 1) * qrows)
                    for h in range(2):
                        m_ref[h, pln, pl.ds(off, qrows), :] = m_new[h][part]
                        l_ref[h, pln, pl.ds(off, qrows), :] = l_new[h][part]
                    acc_ref[pln, pl.ds(off, qrows), :] = acc_new[part]
            return carry

        lax.fori_loop(0, nblocks // ATTN_UNROLL, body, 0)

    for a, dil in enumerate(DILATIONS):
        run_pattern(a, dil, PLANES)
    for pln in range(PLANES):
        o_ref[pl.ds(pln, BLK, stride=PLANES), :] = acc_ref[pln] / both(l_ref[0, pln], l_ref[1, pln])


def _attention_prompt(q, k, v, bias):
    bsz, _, rows, width = q.shape
    pairs = width // LANES
    ntab = bias.shape[1]
    qkv_spec = pl.BlockSpec((None, PLANES, rows, LANES), lambda b, h, s: (b, 0, 0, h))
    return pl.pallas_call(
        _attn_kernel,
        grid=(bsz, pairs, rows // BLK),
        in_specs=[qkv_spec, qkv_spec, qkv_spec,
                  pl.BlockSpec((2, ntab, BLK, 2 * BLK), lambda b, h, s: (h, 0, 0, 0))],
        out_specs=pl.BlockSpec((None, PLANES * BLK, LANES), lambda b, h, s: (b, s, h)),
        out_shape=jax.ShapeDtypeStruct((bsz, PLANES * rows, width), F32),
        scratch_shapes=[pltpu.VMEM((PLANES, BLK, LANES), F32),
                        pltpu.VMEM((2, PLANES, BLK, LANES), F32),
                        pltpu.VMEM((2, PLANES, BLK, LANES), F32)],
        compiler_params=_cparams(("arbitrary", "arbitrary", "arbitrary")),
        name="attention_prompt",
    )(q, k, v, bias)


def _ssm_kernel(u_ref, ws_ref, wx_ref, m_ref, coef_ref, d_ref, y_ref, sre_ref, sim_ref,
                ucat_ref, s_ref, x_ref):
    tc = pl.program_id(2)
    chunks = u_ref.shape[1]
    per_step = m_ref.shape[1] // LANES

    @pl.when(tc == 0)
    def _():
        for s in range(PLANES):
            ucat_ref[:, s * LANES:(s + 1) * LANES] = u_ref[s]
        s_ref[...] = jnp.dot(ucat_ref[...], ws_ref[...], preferred_element_type=F32)
        a_re = coef_ref[1:2, 0:STATE_HALF]
        a_im = coef_ref[1:2, STATE_HALF:2 * STATE_HALF]

        def step(c, carry):
            xr, xi = carry
            x_ref[pl.ds(c, 1), 0:STATE_HALF] = xr
            x_ref[pl.ds(c, 1), STATE_HALF:2 * STATE_HALF] = xi
            sr = s_ref[pl.ds(c, 1), 0:STATE_HALF]
            si = s_ref[pl.ds(c, 1), STATE_HALF:2 * STATE_HALF]
            return a_re * xr - a_im * xi + sr, a_re * xi + a_im * xr + si

        zero = jnp.zeros((1, STATE_HALF), F32)
        xr, xi = lax.fori_loop(0, chunks, step, (zero, zero))
        sre_ref[...] = xr
        sim_ref[...] = xi

    y = jnp.dot(ucat_ref[...], m_ref[...], preferred_element_type=F32)
    y = y + jnp.dot(x_ref[...].astype(BF16), wx_ref[...], preferred_element_type=F32)
    d = d_ref[...]
    for t in range(per_step):
        tok = tc * per_step + t
        y_ref[pl.ds(tok, chunks, stride=PLANES), :] = y[:, t * LANES:(t + 1) * LANES] + d * u_ref[tok].astype(F32)


def _ssm_prompt(u, ws, wx, m, coef, d_skip):
    bsz, _, chunks, width = u.shape
    tiles = width // LANES
    per_step = 4
    steps = PLANES // per_step
    sh = STATE_HALF
    state = jax.ShapeDtypeStruct((bsz, tiles, 1, sh), F32)
    state_spec = pl.BlockSpec((None, None, 1, sh), lambda t, b, c: (b, t, 0, 0))
    return pl.pallas_call(
        _ssm_kernel,
        grid=(tiles, bsz, steps),
        in_specs=[pl.BlockSpec((None, PLANES, chunks, LANES), lambda t, b, c: (b, 0, 0, t)),
                  pl.BlockSpec((None, PLANES * LANES, 2 * sh), lambda t, b, c: (t, 0, 0)),
                  pl.BlockSpec((None, 2 * sh, per_step * LANES), lambda t, b, c: (t, 0, c)),
                  pl.BlockSpec((None, PLANES * LANES, per_step * LANES), lambda t, b, c: (t, 0, c)),
                  pl.BlockSpec((None, SUBLANES, 2 * sh), lambda t, b, c: (t, 0, 0)),
                  pl.BlockSpec((1, LANES), lambda t, b, c: (0, t))],
        out_specs=[pl.BlockSpec((None, PLANES * chunks, LANES), lambda t, b, c: (b, 0, t)),
                   state_spec, state_spec],
        out_shape=[jax.ShapeDtypeStruct((bsz, PLANES * chunks, width), F32), state, state],
        scratch_shapes=[pltpu.VMEM((chunks, PLANES * LANES), BF16),
                        pltpu.VMEM((chunks, 2 * sh), F32),
                        pltpu.VMEM((chunks, 2 * sh), F32)],
        compiler_params=_cparams(("arbitrary", "arbitrary", "arbitrary")),
        name="ssm_prompt",
    )(u, ws, wx, m, coef, d_skip)


def _epilogue_kernel(alpha, attn_ref, y_ref, x_ref, wga_ref, wgs_ref, wglu_ref, bglu_ref, wo_ref, bo_ref,
                     g_ref, b_ref, o_ref):
    d_attn = attn_ref.shape[-1]
    x = x_ref[...]
    xb = x.astype(BF16)
    g_attn = jnp.dot(xb, wga_ref[...], preferred_element_type=F32)
    br_a = (attn_ref[...] * jax.nn.silu(g_attn)).astype(BF16)
    g_ssm = jnp.dot(xb, wgs_ref[...], preferred_element_type=F32)
    z = jax.nn.gelu(y_ref[...])
    gate = jax.nn.sigmoid(jnp.dot(z.astype(BF16), wglu_ref[...], preferred_element_type=F32) + bglu_ref[...])
    br_s = (z * gate * jax.nn.silu(g_ssm)).astype(BF16)
    mix = (jnp.dot(br_a, wo_ref[0:d_attn, :], preferred_element_type=F32)
           + jnp.dot(br_s, wo_ref[d_attn:, :], preferred_element_type=F32) + bo_ref[...])
    t = alpha * x + mix
    mu = jnp.mean(t, axis=-1, keepdims=True)
    var = jnp.mean(jnp.square(t - mu), axis=-1, keepdims=True)
    o_ref[...] = (t - mu) * lax.rsqrt(var + LN_EPS) * g_ref[...] + b_ref[...]


def _epilogue(alpha, name, attn, y, x, w_bf, weights, tile):
    lead = attn.shape[:-2]
    rows, width = attn.shape[-2:]
    d_model = x.shape[-1]
    grid = lead + (rows // tile,)
    none = (None,) * len(lead)
    tiled = lambda w: pl.BlockSpec(none + (tile, w), lambda *g: g + (0,))
    const = lambda shape, col=0: pl.BlockSpec(shape, lambda *g: (0, col), pipeline_mode=ONCE)
    return pl.pallas_call(
        functools.partial(_epilogue_kernel, alpha),
        grid=grid,
        in_specs=[tiled(width), tiled(width), tiled(d_model),
                  const((d_model, width), 3), const((d_model, width), 5),
                  const((width, width)), const((1, width)), const((2 * width, d_model)), const((1, d_model)),
                  const((1, d_model)), const((1, d_model))],
        out_specs=tiled(d_model),
        out_shape=jax.ShapeDtypeStruct(x.shape, F32),
        compiler_params=_cparams(("arbitrary",) * len(grid)),
        name=name,
    )(attn, y, x, w_bf, w_bf, *weights)


def _inproj_sample_kernel(scale, x_ref, w_ref, o_ref):
    acc = jnp.dot(x_ref[...].astype(BF16), w_ref[...], preferred_element_type=F32)
    o_ref[...] = acc * jnp.where(pl.program_id(0) == 0, scale, 1.0)


def _inproj_sample(x, w_bf):
    rows, d_model = x.shape
    width = w_bf.shape[1] // 6
    return pl.pallas_call(
        functools.partial(_inproj_sample_kernel, HEAD_DIM ** -0.5),
        grid=(4,),
        in_specs=[pl.BlockSpec((rows, d_model), lambda j: (0, 0)),
                  pl.BlockSpec((d_model, width), lambda j: (0, j + j // 3))],
        out_specs=pl.BlockSpec((rows, width), lambda j: (0, j)),
        out_shape=jax.ShapeDtypeStruct((rows, 4 * width), F32),
        compiler_params=_cparams(("arbitrary",)),
        name="inproj_sample",
    )(x, w_bf)


SAMPLE_PAIRS = 4


def _attn_sample_kernel(q_ref, kn_ref, vn_ref, kt_ref, vt_ref, tb_ref, tn_ref, o_ref):
    s_len = q_ref.shape[0]
    buf = kt_ref.shape[2]
    npat = tb_ref.shape[0]
    is_a = lax.broadcasted_iota(jnp.int32, (s_len, LANES), 1) < HEAD_DIM
    nt = (((1,), (1,)), ((), ()))
    pad = jnp.zeros((2 * s_len - s_len, LANES), F32)
    for pp in range(SAMPLE_PAIRS):
        lanes = slice(pp * LANES, (pp + 1) * LANES)
        rows = slice(pp * 2 * s_len, (pp + 1) * 2 * s_len)
        qp = q_ref[:, lanes]
        q2 = jnp.concatenate([jnp.where(is_a, qp, 0.0), jnp.where(is_a, 0.0, qp)], axis=0).astype(BF16)
        kt = kt_ref[2 * pp:2 * pp + 2].reshape(2 * HEAD_DIM, buf).astype(BF16)
        vt = vt_ref[2 * pp:2 * pp + 2].reshape(2 * HEAD_DIM, buf).astype(BF16)
        kn = jnp.concatenate([kn_ref[:, lanes], pad], axis=0).astype(BF16)
        vn = jnp.concatenate([vn_ref[:, lanes], pad], axis=0).astype(BF16)
        s_buf = jnp.dot(q2, kt, preferred_element_type=F32)
        s_new = lax.dot_general(q2, kn, nt, preferred_element_type=F32)
        z_buf = [s_buf + tb_ref[a, rows, :] for a in range(npat)]
        z_new = [s_new + tn_ref[a, rows, 0:2 * s_len] for a in range(npat)]
        top = functools.reduce(jnp.maximum, [jnp.max(z, axis=1, keepdims=True) for z in z_buf + z_new])
        w_buf = functools.reduce(jnp.add, [jnp.exp(z - top) for z in z_buf])
        w_new = functools.reduce(jnp.add, [jnp.exp(z - top) for z in z_new])
        den = jnp.sum(w_buf, axis=1, keepdims=True) + jnp.sum(w_new, axis=1, keepdims=True)
        o = lax.dot_general(w_buf.astype(BF16), vt, nt, preferred_element_type=F32)
        o = (o + jnp.dot(w_new.astype(BF16), vn, preferred_element_type=F32)) / den
        o_ref[:, lanes] = jnp.where(is_a, o[0:s_len], o[s_len:2 * s_len])


def _attention_sample(h, kt, vt, tb, tn):
    bsz, s_len, _ = h.shape
    heads, _, buf = kt.shape[1:]
    width = heads * HEAD_DIM
    gw = SAMPLE_PAIRS * LANES
    per_w = width // gw
    new = lambda c: pl.BlockSpec((None, s_len, gw), lambda b, g: (b, 0, c * per_w + g))
    cache = pl.BlockSpec((None, 2 * SAMPLE_PAIRS, HEAD_DIM, buf), lambda b, g: (b, g, 0, 0))
    trows = SAMPLE_PAIRS * 2 * s_len
    return pl.pallas_call(
        _attn_sample_kernel,
        grid=(bsz, per_w),
        in_specs=[new(0), new(1), new(2), cache, cache,
                  pl.BlockSpec((tb.shape[0], trows, buf), lambda b, g: (0, g, 0)),
                  pl.BlockSpec((tn.shape[0], trows, LANES), lambda b, g: (0, g, 0))],
        out_specs=pl.BlockSpec((None, s_len, gw), lambda b, g: (b, 0, g)),
        out_shape=jax.ShapeDtypeStruct((bsz, s_len, width), F32),
        compiler_params=_cparams(("arbitrary", "arbitrary")),
        name="attention_sample",
    )(h, h, h, kt, vt, tb, tn)


def _ssm_sample_kernel(s_len, u_ref, b0_ref, c0_ref, coef_ref, d_ref, x0r_ref, x0i_ref,
                       y_ref, sre_ref, sim_ref, bu_ref, xs_ref):
    bsz = x0r_ref.shape[0]
    bu = jnp.dot(u_ref[...].astype(BF16), b0_ref[...], preferred_element_type=F32)
    slabs = bu_ref.shape[0]
    for c in range(slabs):
        bu_ref[c] = bu[:, c * LANES:(c + 1) * LANES]
    a_re = coef_ref[0:1, 0:STATE_HALF]
    a_im = coef_ref[0:1, STATE_HALF:2 * STATE_HALF]
    half = slabs // 2
    xr, xi = x0r_ref[...], x0i_ref[...]
    for s in range(s_len):
        step = lambda c: bu_ref[c, pl.ds(s, bsz, stride=s_len), :]
        br = jnp.concatenate([step(c) for c in range(half)], axis=1)
        bi = jnp.concatenate([step(c) for c in range(half, slabs)], axis=1)
        xr, xi = a_re * xr - a_im * xi + br, a_re * xi + a_im * xr + bi
        xs_ref[s * bsz:(s + 1) * bsz, :] = jnp.concatenate([xr, xi], axis=1)
    sre_ref[...] = xr
    sim_ref[...] = xi
    y = jnp.dot(xs_ref[...].astype(BF16), c0_ref[...], preferred_element_type=F32)
    d = d_ref[...]
    for s in range(s_len):
        y_ref[pl.ds(s, bsz, stride=s_len), :] = (y[s * bsz:(s + 1) * bsz, :]
                                                 + d * u_ref[pl.ds(s, bsz, stride=s_len), :])


def _ssm_sample(h, ws, c0, coef, d_skip, x0_re, x0_im, s_len):
    rows = h.shape[0]
    bsz = rows // s_len
    tiles = ws.shape[0]
    sh = STATE_HALF
    u_col0 = 3 * (h.shape[1] // 4) // LANES
    st_spec = pl.BlockSpec((bsz, sh), lambda t: (0, t))
    state = jax.ShapeDtypeStruct((bsz, tiles * sh), F32)
    return pl.pallas_call(
        functools.partial(_ssm_sample_kernel, s_len),
        grid=(tiles,),
        in_specs=[pl.BlockSpec((rows, LANES), lambda t: (0, u_col0 + t)),
                  pl.BlockSpec((None, LANES, 2 * sh), lambda t: (t, PLANES - 1, 0)),
                  pl.BlockSpec((None, 2 * sh, LANES), lambda t: (t, 0, 0)),
                  pl.BlockSpec((None, SUBLANES, 2 * sh), lambda t: (t, 0, 0)),
                  pl.BlockSpec((1, LANES), lambda t: (0, t)),
                  st_spec, st_spec],
        out_specs=[pl.BlockSpec((rows, LANES), lambda t: (0, t)), st_spec, st_spec],
        out_shape=[jax.ShapeDtypeStruct((rows, tiles * LANES), F32), state, state],
        scratch_shapes=[pltpu.VMEM((2 * sh // LANES, rows, LANES), F32),
                        pltpu.VMEM((rows, 2 * sh), F32)],
        compiler_params=_cparams(("arbitrary",)),
        name="ssm_sample",
    )(h, ws, c0, coef, d_skip, x0_re, x0_im)


def kernel(x_prompt, x_sample, cache_k, cache_v, state_ssm_re, state_ssm_im, w_in, w_out, b_out, rel_bias,
           lam_re, lam_im, log_dt, b_re, b_im, c_re, c_im, d_skip, w_glu, b_glu, ln_g, ln_b):
    depth = w_in.shape[0]
    assert depth == 1, "one layer per step"
    bsz, seq, d_model = x_prompt.shape
    dbsz, s_len, _ = x_sample.shape
    buf, heads = cache_k.shape[2], cache_k.shape[3]
    width = heads * HEAD_DIM
    groups, nstate = lam_re.shape[1], lam_re.shape[2]
    keep = min(MAX_DISTANCE, seq)
    assert seq % (PLANES * BLK) == 0 and seq >= 2 * PLANES * BLK and keep % ROW_TILE == 0
    assert buf == KPER * max(DILATIONS) and s_len <= min(DILATIONS[:-1])
    assert nstate == SSM_STATE and width == groups * SSM_CH and heads % (2 * SAMPLE_PAIRS) == 0
    alpha = (2 * depth) ** 0.25
    npat = len(DILATIONS)

    w_bf = w_in[0].astype(BF16)
    row = lambda v: v.reshape(1, -1)
    weights = (w_glu[0].astype(BF16), row(b_glu[0]), w_out[0].astype(BF16), row(b_out[0]),
               row(ln_g[0]), row(ln_b[0]))
    d_row = row(d_skip[0])

    rbt = rel_bias.T
    ptab = _bias_tables(jnp.asarray(_prompt_bucket_tables().reshape(1, -1)), rbt, 8192)
    ptab = ptab.reshape(heads, 2 * npat, BLK, 2 * BLK)
    key_w = buf + 2 * LANES
    stab = _bias_tables(jnp.asarray(_sample_bucket_tables(buf, s_len, key_w)), rbt, key_w)
    stab = jnp.transpose(stab.reshape(heads, npat, s_len, key_w), (1, 0, 2, 3)).reshape(npat, heads * s_len, key_w)
    stab_buf, stab_new = stab[:, :, :buf], stab[:, :, buf:buf + LANES]
    ws, wx, m_intra, c0, coef = _ssm_prep(lam_re[0], lam_im[0], log_dt[0], b_re[0], b_im[0], c_re[0], c_im[0])

    q, k, v, u, kl_t, vl_t = _inproj_prompt(x_prompt, w_bf, keep)
    attn = _attention_prompt(q, k, v, ptab)
    y, sre_p, sim_p = _ssm_prompt(u, ws, wx, m_intra, coef, d_row)
    y_prompt = _epilogue(alpha, "epilogue_prompt", attn, y, x_prompt, w_bf, weights, ROW_TILE // 2)
    last = lambda t: jnp.transpose(t.reshape(bsz, heads, HEAD_DIM, keep), (0, 3, 1, 2))[None]
    st_shape = (1, bsz, groups, nstate)

    xs = x_sample.reshape(dbsz * s_len, d_model)
    hs = _inproj_sample(xs, w_bf)
    pos_minor = lambda c: jnp.transpose(c[0], (0, 2, 3, 1))
    attn_s = _attention_sample(hs.reshape(dbsz, s_len, -1), pos_minor(cache_k), pos_minor(cache_v),
                               stab_buf, stab_new)
    y_s, sre_s, sim_s = _ssm_sample(hs, ws, c0, coef, d_row,
                                    state_ssm_re[0].astype(F32).reshape(dbsz, groups * nstate),
                                    state_ssm_im[0].astype(F32).reshape(dbsz, groups * nstate), s_len)
    y_sample = _epilogue(alpha, "epilogue_sample", attn_s.reshape(dbsz * s_len, width), y_s, xs, w_bf, weights,
                         dbsz * s_len)
    new_shape = (1, dbsz, s_len, heads, HEAD_DIM)
    sst_shape = (1, dbsz, groups, nstate)
    return (y_prompt, y_sample.reshape(dbsz, s_len, d_model), last(kl_t), last(vl_t),
            sre_p.reshape(st_shape), sim_p.reshape(st_shape),
            hs[:, width:2 * width].reshape(new_shape), hs[:, 2 * width:3 * width].reshape(new_shape),
            sre_s.reshape(sst_shape), sim_s.reshape(sst_shape))
```

```python
import functools
import math

import jax
import jax.numpy as jnp
import numpy as np
from jax import lax
from jax.experimental import pallas as pl
from jax.experimental.pallas import tpu as pltpu

F32 = jnp.float32
BF16 = jnp.bfloat16

HEAD_DIM = 64
SSM_CH = 16
SSM_STATE = 64
NUM_BUCKETS = 32
MAX_DISTANCE = 2048
KPER = 128
BLK = 128
DILATIONS = (16, 4, 1)
LN_EPS = 1e-5
NEG = -1e30
LOG2E = math.log2(math.e)

LANES = 128
SUBLANES = 8
PLANES = 16
GROUPS_PER_TILE = LANES // SSM_CH
STATE_HALF = GROUPS_PER_TILE * SSM_STATE
VMEM_LIMIT = 56 * 1024 * 1024
ROW_TILE = 512
ONCE = pl.Buffered(1)


def _cparams(sem, vmem=VMEM_LIMIT):
    return pltpu.CompilerParams(dimension_semantics=sem, vmem_limit_bytes=vmem)


def _bucket_np(dist):
    exact = NUM_BUCKETS // 2
    d_f = np.maximum(dist, 1).astype(np.float32)
    large = exact + (np.log(d_f / np.float32(exact)) / np.float32(math.log(MAX_DISTANCE / exact))
                     * np.float32(NUM_BUCKETS - exact)).astype(np.int32)
    large = np.minimum(large, NUM_BUCKETS - 1)
    return np.where(dist < exact, dist, large).astype(np.int32)


def _prompt_rel(dil, first):
    i = np.arange(BLK)[:, None]
    j = np.arange(2 * BLK)[None, :]
    npl = PLANES // dil
    qrows = BLK // npl
    pq, ml = i // qrows, i % qrows
    pk, jl = j // (2 * qrows), j % (2 * qrows)
    back = 0 if first else qrows
    return npl * (ml - jl + back) + (pq - pk)


def _prompt_bucket_tables():
    tabs = []
    for dil in DILATIONS:
        for first in (False, True):
            rel = _prompt_rel(dil, first)
            valid = (rel >= 0) & (rel <= KPER)
            tabs.append(np.where(valid, _bucket_np(np.clip(rel, 0, KPER) * dil), NUM_BUCKETS))
    return np.stack(tabs).reshape(len(DILATIONS) * 2, BLK * 2 * BLK).astype(np.int32)


def _sample_bucket_tables(buf, s_len, width):
    pos = np.arange(buf + s_len)
    tabs = np.full((len(DILATIONS), s_len, width), NUM_BUCKETS, np.int32)
    for a, dil in enumerate(DILATIONS):
        for s in range(s_len):
            dist = buf + s - pos
            valid = (dist >= 0) & (dist % dil == 0) & (dist // dil <= KPER)
            tabs[a, s, :len(pos)] = np.where(valid, _bucket_np(np.maximum(dist, 0)), NUM_BUCKETS)
    return tabs.reshape(1, -1)


def _bias_kernel(scale, idx_ref, rbt_ref, o_ref):
    idx = idx_ref[...]
    onehot = (lax.broadcasted_iota(jnp.int32, (NUM_BUCKETS, idx.shape[1]), 0) == idx).astype(F32)
    tab = jnp.dot(rbt_ref[...], onehot, precision=lax.Precision.HIGHEST,
                  preferred_element_type=F32)
    o_ref[...] = jnp.where(idx < NUM_BUCKETS, tab * scale, NEG)


def _bias_tables(idx, rel_bias_t, chunk, scale=1.0):
    n_heads = rel_bias_t.shape[0]
    total = idx.shape[1]
    return pl.pallas_call(
        functools.partial(_bias_kernel, scale),
        grid=(total // chunk,),
        in_specs=[pl.BlockSpec((1, chunk), lambda c: (0, c)),
                  pl.BlockSpec((n_heads, NUM_BUCKETS), lambda c: (0, 0))],
        out_specs=pl.BlockSpec((n_heads, chunk), lambda c: (0, c)),
        out_shape=jax.ShapeDtypeStruct((n_heads, total), F32),
        compiler_params=_cparams(("arbitrary",)),
        name="bias_tables",
    )(idx, rel_bias_t)


def _discretize(lr, li, ldt):
    lr = jnp.minimum(lr, -1e-4)
    dt = jnp.exp(ldt)
    mag = jnp.exp(lr * dt)
    ab_re, ab_im = mag * jnp.cos(li * dt), mag * jnp.sin(li * dt)
    den = lr * lr + li * li
    inv_re, inv_im = lr / den, -li / den
    n_re, n_im = ab_re - 1.0, ab_im
    cf_re = n_re * inv_re - n_im * inv_im
    cf_im = n_re * inv_im + n_im * inv_re
    return ab_re, ab_im, cf_re, cf_im


def _ssm_prep_kernel(lam_row_ref, lam_col_ref, bt_re_ref, bt_im_ref, ct_re_ref, ct_im_ref,
                     ws_ref, wx_ref, m_ref, c0_ref, coef_ref):
    row = lam_row_ref[0]
    ab_re, ab_im, cf_re, cf_im = _discretize(row[0:1], row[1:2], row[2:3])
    bt_re, bt_im = bt_re_ref[0], bt_im_ref[0]
    bb_re = cf_re * bt_re - cf_im * bt_im
    bb_im = cf_re * bt_im + cf_im * bt_re
    col = lam_col_ref[0]
    abc_re, abc_im, _, _ = _discretize(col[:, 0:1], col[:, 1:2], col[:, 2:3])
    ct_re, ct_im = ct_re_ref[0], ct_im_ref[0]
    c0 = jnp.concatenate([ct_re, -ct_im], axis=0)
    c0_ref[0] = c0.astype(BF16)

    pr, pi = jnp.ones_like(ab_re), jnp.zeros_like(ab_re)
    qr, qi = abc_re, abc_im
    c0b = c0.astype(BF16)
    for lag in range(PLANES):
        w = jnp.concatenate([pr * bb_re - pi * bb_im, pr * bb_im + pi * bb_re], axis=1).astype(BF16)
        s = PLANES - 1 - lag
        ws_ref[0, s * LANES:(s + 1) * LANES, :] = w
        m_ref[0, lag * LANES:(lag + 1) * LANES, :] = jnp.dot(w, c0b, preferred_element_type=F32).astype(BF16)
        wx_ref[0, 0:STATE_HALF, lag * LANES:(lag + 1) * LANES] = (ct_re * qr - ct_im * qi).astype(BF16)
        wx_ref[0, STATE_HALF:2 * STATE_HALF, lag * LANES:(lag + 1) * LANES] = (-(ct_re * qi + ct_im * qr)).astype(BF16)
        pr, pi = pr * ab_re - pi * ab_im, pr * ab_im + pi * ab_re
        qr, qi = qr * abc_re - qi * abc_im, qr * abc_im + qi * abc_re
    coef_ref[0] = jnp.concatenate([
        jnp.concatenate([ab_re, ab_im], axis=1),
        jnp.concatenate([pr, pi], axis=1),
        jnp.zeros((SUBLANES - 2, 2 * STATE_HALF), F32)], axis=0)


def _ssm_prep(lam_re, lam_im, log_dt, b_re, b_im, c_re, c_im):
    groups, n = lam_re.shape
    tiles = groups // GROUPS_PER_TILE
    gpt = GROUPS_PER_TILE
    eye = jnp.eye(gpt, dtype=F32)

    def rows(v):
        return v.reshape(tiles, gpt * n)

    ldt = jnp.broadcast_to(log_dt[:, None], (groups, n))
    lam_row = jnp.stack([rows(lam_re), rows(lam_im), rows(ldt)], axis=1)
    lam_col = jnp.transpose(lam_row, (0, 2, 1))

    def bt(b):
        b = jnp.transpose(b.reshape(tiles, gpt, n, SSM_CH), (0, 1, 3, 2))
        return (b[:, :, :, None, :] * eye[None, :, None, :, None]).reshape(tiles, gpt * SSM_CH, gpt * n)

    def ct(c):
        c = jnp.transpose(c.reshape(tiles, gpt, SSM_CH, n), (0, 1, 3, 2))
        return (c[:, :, :, None, :] * eye[None, :, None, :, None]).reshape(tiles, gpt * n, gpt * SSM_CH)

    sh, ln = STATE_HALF, LANES
    tile3 = lambda a, b: pl.BlockSpec((1, a, b), lambda t: (t, 0, 0))
    return pl.pallas_call(
        _ssm_prep_kernel,
        grid=(tiles,),
        in_specs=[tile3(3, sh), tile3(sh, 3), tile3(ln, sh), tile3(ln, sh), tile3(sh, ln), tile3(sh, ln)],
        out_specs=[tile3(PLANES * ln, 2 * sh), tile3(2 * sh, PLANES * ln), tile3(PLANES * ln, ln),
                   tile3(2 * sh, ln), tile3(SUBLANES, 2 * sh)],
        out_shape=[jax.ShapeDtypeStruct((tiles, PLANES * ln, 2 * sh), BF16),
                   jax.ShapeDtypeStruct((tiles, 2 * sh, PLANES * ln), BF16),
                   jax.ShapeDtypeStruct((tiles, PLANES * ln, ln), BF16),
                   jax.ShapeDtypeStruct((tiles, 2 * sh, ln), BF16),
                   jax.ShapeDtypeStruct((tiles, SUBLANES, 2 * sh), F32)],
        compiler_params=_cparams(("arbitrary",)),
        name="ssm_prep",
    )(lam_row, lam_col, bt(b_re), bt(b_im), ct(c_re), ct(c_im))


def _inproj_kernel(first_keep, scale, x_ref, wqkv_ref, wu_ref, q_ref, k_ref, v_ref, u_ref,
                   kl_ref, vl_ref, slab_ref):
    i = pl.program_id(1)
    slabs = slab_ref.shape[1]
    prow = q_ref.shape[1]
    width = u_ref.shape[-1]
    xb = x_ref[...].astype(BF16)

    def to_planes(val, out_ref, buf):
        for c in range(slabs):
            slab_ref[buf, c] = val[:, c * LANES:(c + 1) * LANES]
        for r in range(PLANES):
            rows = [slab_ref[buf, c, pl.ds(r, prow, stride=PLANES), :] for c in range(slabs)]
            out_ref[r] = jnp.concatenate(rows, axis=1).astype(out_ref.dtype)

    q = jnp.dot(xb, wqkv_ref[:, 0:width], preferred_element_type=F32) * scale
    to_planes(q, q_ref, 0)
    k = jnp.dot(xb, wqkv_ref[:, width:2 * width], preferred_element_type=F32)
    to_planes(k, k_ref, 1)
    v = jnp.dot(xb, wqkv_ref[:, 2 * width:3 * width], preferred_element_type=F32)
    to_planes(v, v_ref, 0)
    u = jnp.dot(xb, wu_ref[...], preferred_element_type=F32)
    to_planes(u, u_ref, 1)

    @pl.when(i >= first_keep)
    def _():
        kl_ref[...] = k.T
        vl_ref[...] = v.T


def _inproj_prompt(x, w_bf, keep):
    bsz, seq, d_model = x.shape
    width = w_bf.shape[1] // 6
    rows = seq // PLANES
    tile = ROW_TILE // 2
    prow = tile // PLANES
    first_keep = (seq - keep) // tile
    plane = lambda dt: jax.ShapeDtypeStruct((bsz, PLANES, rows, width), dt)
    plane_spec = pl.BlockSpec((None, PLANES, prow, width), lambda b, i: (b, 0, i, 0))
    last = jax.ShapeDtypeStruct((bsz, width, keep), F32)
    last_spec = pl.BlockSpec((None, width, tile), lambda b, i: (b, 0, jnp.maximum(i - first_keep, 0)))
    return pl.pallas_call(
        functools.partial(_inproj_kernel, first_keep, HEAD_DIM ** -0.5 * LOG2E),
        grid=(bsz, seq // tile),
        in_specs=[pl.BlockSpec((None, tile, d_model), lambda b, i: (b, i, 0)),
                  pl.BlockSpec((d_model, 3 * width), lambda b, i: (0, 0), pipeline_mode=ONCE),
                  pl.BlockSpec((d_model, width), lambda b, i: (0, 4), pipeline_mode=ONCE)],
        out_specs=[plane_spec, plane_spec, plane_spec, plane_spec, last_spec, last_spec],
        out_shape=[plane(F32), plane(F32), plane(F32), plane(BF16), last, last],
        scratch_shapes=[pltpu.VMEM((2, width // LANES, tile, LANES), F32)],
        compiler_params=_cparams(("arbitrary", "arbitrary")),
        name="inproj_prompt",
    )(x, w_bf, w_bf)


ATTN_UNROLL = 8


def _attn_kernel(q_ref, k_ref, v_ref, bias_ref, o_ref, acc_ref, m_ref, l_ref):
    sup = pl.program_id(2)
    is_a = lax.broadcasted_iota(jnp.int32, (BLK, LANES), 1) < HEAD_DIM
    base = pl.multiple_of(sup * BLK, BLK)

    def raw_scores(q, k):
        kb = k.astype(BF16)
        out = []
        for head in range(2):
            qm = jnp.where(is_a if head == 0 else jnp.logical_not(is_a), q, 0.0).astype(BF16)
            out.append(lax.dot_general(qm, kb, (((1,), (1,)), ((), ())), preferred_element_type=F32))
        return out

    def both(a, b):
        return jnp.where(is_a, a, b)

    def weights(s, table, state):
        m_new, p = [], []
        for h in range(2):
            sh = s[h] + bias_ref[h, table]
            mb = jnp.broadcast_to(jnp.max(sh, axis=1, keepdims=True), (BLK, LANES))
            mh = mb if state is None else jnp.maximum(state[0][h], mb)
            m_new.append(mh)
            p.append(jnp.exp2(sh - jnp.concatenate([mh, mh], axis=1)).astype(BF16))
        return m_new, p

    ones = jnp.ones((2 * BLK, LANES), BF16)

    def combine(m_new, p, v, state):
        vext = jnp.concatenate([v.astype(BF16), ones], axis=1)
        res = [jnp.dot(p[h], vext, preferred_element_type=F32) for h in range(2)]
        acc_new = both(res[0][:, :LANES], res[1][:, :LANES])
        l_new = [res[h][:, LANES:] for h in range(2)]
        if state is not None:
            alpha = [jnp.exp2(state[0][h] - m_new[h]) for h in range(2)]
            l_new = [alpha[h] * state[1][h] + l_new[h] for h in range(2)]
            acc_new = both(alpha[0], alpha[1]) * state[2] + acc_new
        return m_new, l_new, acc_new

    cat = lambda parts: parts[0] if len(parts) == 1 else jnp.concatenate(parts, axis=0)

    def run_pattern(a, dil, nblocks):
        npl = PLANES // dil
        qrows = BLK // npl
        per_res = BLK // qrows

        def body(grp, carry):
            work = []
            for un in range(ATTN_UNROLL):
                blk = grp * ATTN_UNROLL + un
                res, sub = blk // per_res, blk % per_res
                off = pl.multiple_of(sub * qrows, qrows)
                kstart = pl.multiple_of(jnp.maximum(base + off - qrows, 0), qrows)
                first = jnp.logical_and(sup == 0, sub == 0).astype(jnp.int32)
                planes = [res + dil * i for i in range(npl)]
                q = cat([q_ref[pln, pl.ds(base + off, qrows), :] for pln in planes])
                k = cat([k_ref[pln, pl.ds(kstart, 2 * qrows), :] for pln in planes])
                v = cat([v_ref[pln, pl.ds(kstart, 2 * qrows), :] for pln in planes])
                rows = lambda ref, *lead: cat([ref[(*lead, pln, pl.ds(off, qrows), slice(None))] for pln in planes])
                state = None if a == 0 else ([rows(m_ref, h) for h in range(2)],
                                             [rows(l_ref, h) for h in range(2)], rows(acc_ref))
                work.append((planes, off, q, k, v, 2 * a + first, state))
            raw = [raw_scores(q, k) for _, _, q, k, _, _, _ in work]
            soft = [weights(s, w[5], w[6]) for s, w in zip(raw, work)]
            done = [(w[0], w[1], combine(*sm, w[4], w[6])) for sm, w in zip(soft, work)]
            for planes, off, (m_new, l_new, acc_new) in done:
                for i, pln in enumerate(planes):
                    part = slice(i * qrows, (i + 1) * qrows)
                    for h in range(2):
                        m_ref[h, pln, pl.ds(off, qrows), :] = m_new[h][part]
                        l_ref[h, pln, pl.ds(off, qrows), :] = l_new[h][part]
                    acc_ref[pln, pl.ds(off, qrows), :] = acc_new[part]
            return carry

        lax.fori_loop(0, nblocks // ATTN_UNROLL, body, 0)

    for a, dil in enumerate(DILATIONS):
        run_pattern(a, dil, PLANES)
    for pln in range(PLANES):
        o_ref[pl.ds(pln, BLK, stride=PLANES), :] = acc_ref[pln] / both(l_ref[0, pln], l_ref[1, pln])


def _attention_prompt(q, k, v, bias):
    bsz, _, rows, width = q.shape
    pairs = width // LANES
    ntab = bias.shape[1]
    qkv_spec = pl.BlockSpec((None, PLANES, rows, LANES), lambda b, h, s: (b, 0, 0, h))
    return pl.pallas_call(
        _attn_kernel,
        grid=(bsz, pairs, rows // BLK),
        in_specs=[qkv_spec, qkv_spec, qkv_spec,
                  pl.BlockSpec((2, ntab, BLK, 2 * BLK), lambda b, h, s: (h, 0, 0, 0))],
        out_specs=pl.BlockSpec((None, PLANES * BLK, LANES), lambda b, h, s: (b, s, h)),
        out_shape=jax.ShapeDtypeStruct((bsz, PLANES * rows, width), F32),
        scratch_shapes=[pltpu.VMEM((PLANES, BLK, LANES), F32),
                        pltpu.VMEM((2, PLANES, BLK, LANES), F32),
                        pltpu.VMEM((2, PLANES, BLK, LANES), F32)],
        compiler_params=_cparams(("arbitrary", "arbitrary", "arbitrary")),
        name="attention_prompt",
    )(q, k, v, bias)


SSM_COLS = 4


def _ssm_kernel(u_ref, ws_ref, wx_ref, mlag_ref, coef_ref, d_ref, y_ref, sre_ref, sim_ref,
                ucat_ref, s_ref, x_ref, m_ref):
    chunks = u_ref.shape[1]

    @pl.when(pl.program_id(1) == 0)
    def _():
        zero_blk = jnp.zeros((LANES, LANES), BF16)
        for s in range(PLANES):
            for t in range(PLANES):
                blk = mlag_ref[(t - s) * LANES:(t - s + 1) * LANES, :] if t >= s else zero_blk
                m_ref[s * LANES:(s + 1) * LANES, t * LANES:(t + 1) * LANES] = blk

    for s in range(PLANES):
        ucat_ref[:, s * LANES:(s + 1) * LANES] = u_ref[s]
    s_ref[...] = jnp.dot(ucat_ref[...], ws_ref[...], preferred_element_type=F32)
    a_re = coef_ref[1:2, 0:STATE_HALF]
    a_im = coef_ref[1:2, STATE_HALF:2 * STATE_HALF]

    def step(c, carry):
        xr, xi = carry
        x_ref[pl.ds(c, 1), 0:STATE_HALF] = xr
        x_ref[pl.ds(c, 1), STATE_HALF:2 * STATE_HALF] = xi
        sr = s_ref[pl.ds(c, 1), 0:STATE_HALF]
        si = s_ref[pl.ds(c, 1), STATE_HALF:2 * STATE_HALF]
        return a_re * xr - a_im * xi + sr, a_re * xi + a_im * xr + si

    zero = jnp.zeros((1, STATE_HALF), F32)
    xr, xi = lax.fori_loop(0, chunks, step, (zero, zero))
    sre_ref[...] = xr
    sim_ref[...] = xi

    xb = x_ref[...].astype(BF16)
    d = d_ref[...]
    for g in range(PLANES // SSM_COLS):
        cols = slice(g * SSM_COLS * LANES, (g + 1) * SSM_COLS * LANES)
        used = (g + 1) * SSM_COLS * LANES
        y = jnp.dot(ucat_ref[:, :used], m_ref[:used, cols], preferred_element_type=F32)
        y = y + jnp.dot(xb, wx_ref[:, cols], preferred_element_type=F32)
        for t in range(SSM_COLS):
            tok = g * SSM_COLS + t
            y_ref[pl.ds(tok, chunks, stride=PLANES), :] = (y[:, t * LANES:(t + 1) * LANES]
                                                          + d * u_ref[tok].astype(F32))


def _ssm_prompt(u, ws, wx, mlag, coef, d_skip):
    bsz, _, chunks, width = u.shape
    tiles = width // LANES
    sh = STATE_HALF
    state = jax.ShapeDtypeStruct((bsz, tiles, 1, sh), F32)
    state_spec = pl.BlockSpec((None, None, 1, sh), lambda t, b: (b, t, 0, 0))
    per_tile = lambda r, c: pl.BlockSpec((None, r, c), lambda t, b: (t, 0, 0))
    return pl.pallas_call(
        _ssm_kernel,
        grid=(tiles, bsz),
        in_specs=[pl.BlockSpec((None, PLANES, chunks, LANES), lambda t, b: (b, 0, 0, t)),
                  per_tile(PLANES * LANES, 2 * sh), per_tile(2 * sh, PLANES * LANES),
                  per_tile(PLANES * LANES, LANES), per_tile(SUBLANES, 2 * sh),
                  pl.BlockSpec((1, LANES), lambda t, b: (0, t))],
        out_specs=[pl.BlockSpec((None, PLANES * chunks, LANES), lambda t, b: (b, 0, t)),
                   state_spec, state_spec],
        out_shape=[jax.ShapeDtypeStruct((bsz, PLANES * chunks, width), F32), state, state],
        scratch_shapes=[pltpu.VMEM((chunks, PLANES * LANES), BF16),
                        pltpu.VMEM((chunks, 2 * sh), F32),
                        pltpu.VMEM((chunks, 2 * sh), F32),
                        pltpu.VMEM((PLANES * LANES, PLANES * LANES), BF16)],
        compiler_params=_cparams(("arbitrary", "arbitrary")),
        name="ssm_prompt",
    )(u, ws, wx, mlag, coef, d_skip)


def _epilogue_kernel(alpha, attn_ref, y_ref, x_ref, wga_ref, wgs_ref, wglu_ref, bglu_ref, wo_ref, bo_ref,
                     g_ref, b_ref, o_ref):
    d_attn = attn_ref.shape[-1]
    x = x_ref[...]
    xb = x.astype(BF16)
    g_attn = jnp.dot(xb, wga_ref[...], preferred_element_type=F32)
    br_a = (attn_ref[...] * jax.nn.silu(g_attn)).astype(BF16)
    g_ssm = jnp.dot(xb, wgs_ref[...], preferred_element_type=F32)
    z = jax.nn.gelu(y_ref[...])
    gate = jax.nn.sigmoid(jnp.dot(z.astype(BF16), wglu_ref[...], preferred_element_type=F32) + bglu_ref[...])
    br_s = (z * gate * jax.nn.silu(g_ssm)).astype(BF16)
    mix = (jnp.dot(br_a, wo_ref[0:d_attn, :], preferred_element_type=F32)
           + jnp.dot(br_s, wo_ref[d_attn:, :], preferred_element_type=F32) + bo_ref[...])
    t = alpha * x + mix
    mu = jnp.mean(t, axis=-1, keepdims=True)
    var = jnp.mean(jnp.square(t - mu), axis=-1, keepdims=True)
    o_ref[...] = (t - mu) * lax.rsqrt(var + LN_EPS) * g_ref[...] + b_ref[...]


def _epilogue(alpha, name, attn, y, x, w_bf, weights, tile):
    lead = attn.shape[:-2]
    rows, width = attn.shape[-2:]
    d_model = x.shape[-1]
    grid = lead + (rows // tile,)
    none = (None,) * len(lead)
    tiled = lambda w: pl.BlockSpec(none + (tile, w), lambda *g: g + (0,))
    const = lambda shape, col=0: pl.BlockSpec(shape, lambda *g: (0, col), pipeline_mode=ONCE)
    return pl.pallas_call(
        functools.partial(_epilogue_kernel, alpha),
        grid=grid,
        in_specs=[tiled(width), tiled(width), tiled(d_model),
                  const((d_model, width), 3), const((d_model, width), 5),
                  const((width, width)), const((1, width)), const((2 * width, d_model)), const((1, d_model)),
                  const((1, d_model)), const((1, d_model))],
        out_specs=tiled(d_model),
        out_shape=jax.ShapeDtypeStruct(x.shape, F32),
        compiler_params=_cparams(("arbitrary",) * len(grid)),
        name=name,
    )(attn, y, x, w_bf, w_bf, *weights)


def _inproj_sample_kernel(scale, x_ref, w_ref, o_ref):
    acc = jnp.dot(x_ref[...].astype(BF16), w_ref[...], preferred_element_type=F32)
    o_ref[...] = acc * jnp.where(pl.program_id(0) == 0, scale, 1.0)


def _inproj_sample(x, w_bf):
    rows, d_model = x.shape
    width = w_bf.shape[1] // 6
    return pl.pallas_call(
        functools.partial(_inproj_sample_kernel, HEAD_DIM ** -0.5),
        grid=(4,),
        in_specs=[pl.BlockSpec((rows, d_model), lambda j: (0, 0)),
                  pl.BlockSpec((d_model, width), lambda j: (0, j + j // 3))],
        out_specs=pl.BlockSpec((rows, width), lambda j: (0, j)),
        out_shape=jax.ShapeDtypeStruct((rows, 4 * width), F32),
        compiler_params=_cparams(("arbitrary",)),
        name="inproj_sample",
    )(x, w_bf)


SAMPLE_PAIRS = 4


def _attn_sample_kernel(q_ref, kn_ref, vn_ref, kt_ref, vt_ref, tb_ref, tn_ref, o_ref):
    s_len = q_ref.shape[0]
    buf = kt_ref.shape[2]
    npat = tb_ref.shape[0]
    is_a = lax.broadcasted_iota(jnp.int32, (s_len, LANES), 1) < HEAD_DIM
    nt = (((1,), (1,)), ((), ()))
    pad = jnp.zeros((2 * s_len - s_len, LANES), F32)
    for pp in range(SAMPLE_PAIRS):
        lanes = slice(pp * LANES, (pp + 1) * LANES)
        rows = slice(pp * 2 * s_len, (pp + 1) * 2 * s_len)
        qp = q_ref[:, lanes]
        q2 = jnp.concatenate([jnp.where(is_a, qp, 0.0), jnp.where(is_a, 0.0, qp)], axis=0).astype(BF16)
        kt = kt_ref[2 * pp:2 * pp + 2].reshape(2 * HEAD_DIM, buf).astype(BF16)
        vt = vt_ref[2 * pp:2 * pp + 2].reshape(2 * HEAD_DIM, buf).astype(BF16)
        kn = jnp.concatenate([kn_ref[:, lanes], pad], axis=0).astype(BF16)
        vn = jnp.concatenate([vn_ref[:, lanes], pad], axis=0).astype(BF16)
        s_buf = jnp.dot(q2, kt, preferred_element_type=F32)
        s_new = lax.dot_general(q2, kn, nt, preferred_element_type=F32)
        z_buf = [s_buf + tb_ref[a, rows, :] for a in range(npat)]
        z_new = [s_new + tn_ref[a, rows, 0:2 * s_len] for a in range(npat)]
        top = functools.reduce(jnp.maximum, [jnp.max(z, axis=1, keepdims=True) for z in z_buf + z_new])
        w_buf = functools.reduce(jnp.add, [jnp.exp(z - top) for z in z_buf])
        w_new = functools.reduce(jnp.add, [jnp.exp(z - top) for z in z_new])
        den = jnp.sum(w_buf, axis=1, keepdims=True) + jnp.sum(w_new, axis=1, keepdims=True)
        o = lax.dot_general(w_buf.astype(BF16), vt, nt, preferred_element_type=F32)
        o = (o + jnp.dot(w_new.astype(BF16), vn, preferred_element_type=F32)) / den
        o_ref[:, lanes] = jnp.where(is_a, o[0:s_len], o[s_len:2 * s_len])


def _attention_sample(h, kt, vt, tb, tn):
    bsz, s_len, _ = h.shape
    heads, _, buf = kt.shape[1:]
    width = heads * HEAD_DIM
    gw = SAMPLE_PAIRS * LANES
    per_w = width // gw
    new = lambda c: pl.BlockSpec((None, s_len, gw), lambda b, g: (b, 0, c * per_w + g))
    cache = pl.BlockSpec((None, 2 * SAMPLE_PAIRS, HEAD_DIM, buf), lambda b, g: (b, g, 0, 0))
    trows = SAMPLE_PAIRS * 2 * s_len
    return pl.pallas_call(
        _attn_sample_kernel,
        grid=(bsz, per_w),
        in_specs=[new(0), new(1), new(2), cache, cache,
                  pl.BlockSpec((tb.shape[0], trows, buf), lambda b, g: (0, g, 0)),
                  pl.BlockSpec((tn.shape[0], trows, LANES), lambda b, g: (0, g, 0))],
        out_specs=pl.BlockSpec((None, s_len, gw), lambda b, g: (b, 0, g)),
        out_shape=jax.ShapeDtypeStruct((bsz, s_len, width), F32),
        compiler_params=_cparams(("arbitrary", "arbitrary")),
        name="attention_sample",
    )(h, h, h, kt, vt, tb, tn)


def _ssm_sample_kernel(s_len, u_ref, b0_ref, c0_ref, coef_ref, d_ref, x0r_ref, x0i_ref,
                       y_ref, sre_ref, sim_ref, bu_ref, xs_ref):
    bsz = x0r_ref.shape[0]
    bu = jnp.dot(u_ref[...].astype(BF16), b0_ref[...], preferred_element_type=F32)
    slabs = bu_ref.shape[0]
    for c in range(slabs):
        bu_ref[c] = bu[:, c * LANES:(c + 1) * LANES]
    a_re = coef_ref[0:1, 0:STATE_HALF]
    a_im = coef_ref[0:1, STATE_HALF:2 * STATE_HALF]
    half = slabs // 2
    xr, xi = x0r_ref[...], x0i_ref[...]
    for s in range(s_len):
        step = lambda c: bu_ref[c, pl.ds(s, bsz, stride=s_len), :]
        br = jnp.concatenate([step(c) for c in range(half)], axis=1)
        bi = jnp.concatenate([step(c) for c in range(half, slabs)], axis=1)
        xr, xi = a_re * xr - a_im * xi + br, a_re * xi + a_im * xr + bi
        xs_ref[s * bsz:(s + 1) * bsz, :] = jnp.concatenate([xr, xi], axis=1)
    sre_ref[...] = xr
    sim_ref[...] = xi
    y = jnp.dot(xs_ref[...].astype(BF16), c0_ref[...], preferred_element_type=F32)
    d = d_ref[...]
    for s in range(s_len):
        y_ref[pl.ds(s, bsz, stride=s_len), :] = (y[s * bsz:(s + 1) * bsz, :]
                                                 + d * u_ref[pl.ds(s, bsz, stride=s_len), :])


def _ssm_sample(h, ws, c0, coef, d_skip, x0_re, x0_im, s_len):
    rows = h.shape[0]
    bsz = rows // s_len
    tiles = ws.shape[0]
    sh = STATE_HALF
    u_col0 = 3 * (h.shape[1] // 4) // LANES
    st_spec = pl.BlockSpec((bsz, sh), lambda t: (0, t))
    state = jax.ShapeDtypeStruct((bsz, tiles * sh), F32)
    return pl.pallas_call(
        functools.partial(_ssm_sample_kernel, s_len),
        grid=(tiles,),
        in_specs=[pl.BlockSpec((rows, LANES), lambda t: (0, u_col0 + t)),
                  pl.BlockSpec((None, LANES, 2 * sh), lambda t: (t, PLANES - 1, 0)),
                  pl.BlockSpec((None, 2 * sh, LANES), lambda t: (t, 0, 0)),
                  pl.BlockSpec((None, SUBLANES, 2 * sh), lambda t: (t, 0, 0)),
                  pl.BlockSpec((1, LANES), lambda t: (0, t)),
                  st_spec, st_spec],
        out_specs=[pl.BlockSpec((rows, LANES), lambda t: (0, t)), st_spec, st_spec],
        out_shape=[jax.ShapeDtypeStruct((rows, tiles * LANES), F32), state, state],
        scratch_shapes=[pltpu.VMEM((2 * sh // LANES, rows, LANES), F32),
                        pltpu.VMEM((rows, 2 * sh), F32)],
        compiler_params=_cparams(("arbitrary",)),
        name="ssm_sample",
    )(h, ws, c0, coef, d_skip, x0_re, x0_im)


def kernel(x_prompt, x_sample, cache_k, cache_v, state_ssm_re, state_ssm_im, w_in, w_out, b_out, rel_bias,
           lam_re, lam_im, log_dt, b_re, b_im, c_re, c_im, d_skip, w_glu, b_glu, ln_g, ln_b):
    depth = w_in.shape[0]
    assert depth == 1, "one layer per step"
    bsz, seq, d_model = x_prompt.shape
    dbsz, s_len, _ = x_sample.shape
    buf, heads = cache_k.shape[2], cache_k.shape[3]
    width = heads * HEAD_DIM
    groups, nstate = lam_re.shape[1], lam_re.shape[2]
    keep = min(MAX_DISTANCE, seq)
    assert seq % (PLANES * BLK) == 0 and seq >= 2 * PLANES * BLK and keep % ROW_TILE == 0
    assert buf == KPER * max(DILATIONS) and s_len <= min(DILATIONS[:-1])
    assert nstate == SSM_STATE and width == groups * SSM_CH and heads % (2 * SAMPLE_PAIRS) == 0
    alpha = (2 * depth) ** 0.25
    npat = len(DILATIONS)

    w_bf = w_in[0].astype(BF16)
    row = lambda v: v.reshape(1, -1)
    weights = (w_glu[0].astype(BF16), row(b_glu[0]), w_out[0].astype(BF16), row(b_out[0]),
               row(ln_g[0]), row(ln_b[0]))
    d_row = row(d_skip[0])

    rbt = rel_bias.T
    ptab = _bias_tables(jnp.asarray(_prompt_bucket_tables().reshape(1, -1)), rbt, 8192, LOG2E)
    ptab = ptab.reshape(heads, 2 * npat, BLK, 2 * BLK)
    key_w = buf + 2 * LANES
    stab = _bias_tables(jnp.asarray(_sample_bucket_tables(buf, s_len, key_w)), rbt, key_w)
    stab = jnp.transpose(stab.reshape(heads, npat, s_len, key_w), (1, 0, 2, 3)).reshape(npat, heads * s_len, key_w)
    stab_buf, stab_new = stab[:, :, :buf], stab[:, :, buf:buf + LANES]
    ws, wx, m_intra, c0, coef = _ssm_prep(lam_re[0], lam_im[0], log_dt[0], b_re[0], b_im[0], c_re[0], c_im[0])

    q, k, v, u, kl_t, vl_t = _inproj_prompt(x_prompt, w_bf, keep)
    attn = _attention_prompt(q, k, v, ptab)
    y, sre_p, sim_p = _ssm_prompt(u, ws, wx, m_intra, coef, d_row)
    y_prompt = _epilogue(alpha, "epilogue_prompt", attn, y, x_prompt, w_bf, weights, ROW_TILE // 2)
    last = lambda t: jnp.transpose(t.reshape(bsz, heads, HEAD_DIM, keep), (0, 3, 1, 2))[None]
    st_shape = (1, bsz, groups, nstate)

    xs = x_sample.reshape(dbsz * s_len, d_model)
    hs = _inproj_sample(xs, w_bf)
    pos_minor = lambda c: jnp.transpose(c[0], (0, 2, 3, 1))
    attn_s = _attention_sample(hs.reshape(dbsz, s_len, -1), pos_minor(cache_k), pos_minor(cache_v),
                               stab_buf, stab_new)
    y_s, sre_s, sim_s = _ssm_sample(hs, ws, c0, coef, d_row,
                                    state_ssm_re[0].astype(F32).reshape(dbsz, groups * nstate),
                                    state_ssm_im[0].astype(F32).reshape(dbsz, groups * nstate), s_len)
    y_sample = _epilogue(alpha, "epilogue_sample", attn_s.reshape(dbsz * s_len, width), y_s, xs, w_bf, weights,
                         dbsz * s_len)
    new_shape = (1, dbsz, s_len, heads, HEAD_DIM)
    sst_shape = (1, dbsz, groups, nstate)
    return (y_prompt, y_sample.reshape(dbsz, s_len, d_model), last(kl_t), last(vl_t),
            sre_p.reshape(st_shape), sim_p.reshape(st_shape),
            hs[:, width:2 * width].reshape(new_shape), hs[:, 2 * width:3 * width].reshape(new_shape),
            sre_s.reshape(sst_shape), sim_s.reshape(sst_shape))
```

```python
import functools
import math

import jax
import jax.numpy as jnp
import numpy as np
from jax import lax
from jax.experimental import pallas as pl
from jax.experimental.pallas import tpu as pltpu

F32 = jnp.float32
BF16 = jnp.bfloat16

HEAD_DIM = 64
SSM_CH = 16
SSM_STATE = 64
NUM_BUCKETS = 32
MAX_DISTANCE = 2048
KPER = 128
BLK = 128
DILATIONS = (16, 4, 1)
LN_EPS = 1e-5
NEG = -1e30
LOG2E = math.log2(math.e)

LANES = 128
SUBLANES = 8
PLANES = 16
GROUPS_PER_TILE = LANES // SSM_CH
STATE_HALF = GROUPS_PER_TILE * SSM_STATE
VMEM_LIMIT = 56 * 1024 * 1024
ROW_TILE = 512
ONCE = pl.Buffered(1)


def _cparams(sem, vmem=VMEM_LIMIT):
    return pltpu.CompilerParams(dimension_semantics=sem, vmem_limit_bytes=vmem)


def _bucket_np(dist):
    exact = NUM_BUCKETS // 2
    d_f = np.maximum(dist, 1).astype(np.float32)
    large = exact + (np.log(d_f / np.float32(exact)) / np.float32(math.log(MAX_DISTANCE / exact))
                     * np.float32(NUM_BUCKETS - exact)).astype(np.int32)
    large = np.minimum(large, NUM_BUCKETS - 1)
    return np.where(dist < exact, dist, large).astype(np.int32)


def _prompt_rel(dil, first):
    i = np.arange(BLK)[:, None]
    j = np.arange(2 * BLK)[None, :]
    npl = PLANES // dil
    qrows = BLK // npl
    pq, ml = i // qrows, i % qrows
    pk, jl = j // (2 * qrows), j % (2 * qrows)
    back = 0 if first else qrows
    return npl * (ml - jl + back) + (pq - pk)


def _prompt_bucket_tables():
    tabs = []
    for dil in DILATIONS:
        for first in (False, True):
            rel = _prompt_rel(dil, first)
            valid = (rel >= 0) & (rel <= KPER)
            tabs.append(np.where(valid, _bucket_np(np.clip(rel, 0, KPER) * dil), NUM_BUCKETS))
    return np.stack(tabs).reshape(len(DILATIONS) * 2, BLK * 2 * BLK).astype(np.int32)


def _sample_bucket_tables(buf, s_len, width):
    pos = np.arange(buf + s_len)
    tabs = np.full((len(DILATIONS), s_len, width), NUM_BUCKETS, np.int32)
    for a, dil in enumerate(DILATIONS):
        for s in range(s_len):
            dist = buf + s - pos
            valid = (dist >= 0) & (dist % dil == 0) & (dist // dil <= KPER)
            tabs[a, s, :len(pos)] = np.where(valid, _bucket_np(np.maximum(dist, 0)), NUM_BUCKETS)
    return tabs.reshape(1, -1)


def _bias_kernel(scale, idx_ref, rbt_ref, o_ref):
    idx = idx_ref[...]
    onehot = (lax.broadcasted_iota(jnp.int32, (NUM_BUCKETS, idx.shape[1]), 0) == idx).astype(F32)
    tab = jnp.dot(rbt_ref[...], onehot, precision=lax.Precision.HIGHEST,
                  preferred_element_type=F32)
    o_ref[...] = jnp.where(idx < NUM_BUCKETS, tab * scale, NEG)


def _bias_tables(idx, rel_bias_t, chunk, scale=1.0):
    n_heads = rel_bias_t.shape[0]
    total = idx.shape[1]
    return pl.pallas_call(
        functools.partial(_bias_kernel, scale),
        grid=(total // chunk,),
        in_specs=[pl.BlockSpec((1, chunk), lambda c: (0, c)),
                  pl.BlockSpec((n_heads, NUM_BUCKETS), lambda c: (0, 0))],
        out_specs=pl.BlockSpec((n_heads, chunk), lambda c: (0, c)),
        out_shape=jax.ShapeDtypeStruct((n_heads, total), F32),
        compiler_params=_cparams(("arbitrary",)),
        name="bias_tables",
    )(idx, rel_bias_t)


def _discretize(lr, li, ldt):
    lr = jnp.minimum(lr, -1e-4)
    dt = jnp.exp(ldt)
    mag = jnp.exp(lr * dt)
    ab_re, ab_im = mag * jnp.cos(li * dt), mag * jnp.sin(li * dt)
    den = lr * lr + li * li
    inv_re, inv_im = lr / den, -li / den
    n_re, n_im = ab_re - 1.0, ab_im
    cf_re = n_re * inv_re - n_im * inv_im
    cf_im = n_re * inv_im + n_im * inv_re
    return ab_re, ab_im, cf_re, cf_im


def _ssm_prep_kernel(lam_row_ref, lam_col_ref, bt_re_ref, bt_im_ref, ct_re_ref, ct_im_ref,
                     ws_ref, wx_ref, m_ref, c0_ref, coef_ref):
    row = lam_row_ref[0]
    ab_re, ab_im, cf_re, cf_im = _discretize(row[0:1], row[1:2], row[2:3])
    bt_re, bt_im = bt_re_ref[0], bt_im_ref[0]
    bb_re = cf_re * bt_re - cf_im * bt_im
    bb_im = cf_re * bt_im + cf_im * bt_re
    col = lam_col_ref[0]
    abc_re, abc_im, _, _ = _discretize(col[:, 0:1], col[:, 1:2], col[:, 2:3])
    ct_re, ct_im = ct_re_ref[0], ct_im_ref[0]
    c0 = jnp.concatenate([ct_re, -ct_im], axis=0)
    c0_ref[0] = c0.astype(BF16)

    pr, pi = jnp.ones_like(ab_re), jnp.zeros_like(ab_re)
    qr, qi = abc_re, abc_im
    c0b = c0.astype(BF16)
    for lag in range(PLANES):
        w = jnp.concatenate([pr * bb_re - pi * bb_im, pr * bb_im + pi * bb_re], axis=1).astype(BF16)
        s = PLANES - 1 - lag
        ws_ref[0, s * LANES:(s + 1) * LANES, :] = w
        m_ref[0, lag * LANES:(lag + 1) * LANES, :] = jnp.dot(w, c0b, preferred_element_type=F32).astype(BF16)
        wx_ref[0, 0:STATE_HALF, lag * LANES:(lag + 1) * LANES] = (ct_re * qr - ct_im * qi).astype(BF16)
        wx_ref[0, STATE_HALF:2 * STATE_HALF, lag * LANES:(lag + 1) * LANES] = (-(ct_re * qi + ct_im * qr)).astype(BF16)
        pr, pi = pr * ab_re - pi * ab_im, pr * ab_im + pi * ab_re
        qr, qi = qr * abc_re - qi * abc_im, qr * abc_im + qi * abc_re
    coef_ref[0] = jnp.concatenate([
        jnp.concatenate([ab_re, ab_im], axis=1),
        jnp.concatenate([pr, pi], axis=1),
        jnp.zeros((SUBLANES - 2, 2 * STATE_HALF), F32)], axis=0)


def _ssm_prep(lam_re, lam_im, log_dt, b_re, b_im, c_re, c_im):
    groups, n = lam_re.shape
    tiles = groups // GROUPS_PER_TILE
    gpt = GROUPS_PER_TILE
    eye = jnp.eye(gpt, dtype=F32)

    def rows(v):
        return v.reshape(tiles, gpt * n)

    ldt = jnp.broadcast_to(log_dt[:, None], (groups, n))
    lam_row = jnp.stack([rows(lam_re), rows(lam_im), rows(ldt)], axis=1)
    lam_col = jnp.transpose(lam_row, (0, 2, 1))

    def bt(b):
        b = jnp.transpose(b.reshape(tiles, gpt, n, SSM_CH), (0, 1, 3, 2))
        return (b[:, :, :, None, :] * eye[None, :, None, :, None]).reshape(tiles, gpt * SSM_CH, gpt * n)

    def ct(c):
        c = jnp.transpose(c.reshape(tiles, gpt, SSM_CH, n), (0, 1, 3, 2))
        return (c[:, :, :, None, :] * eye[None, :, None, :, None]).reshape(tiles, gpt * n, gpt * SSM_CH)

    sh, ln = STATE_HALF, LANES
    tile3 = lambda a, b: pl.BlockSpec((1, a, b), lambda t: (t, 0, 0))
    return pl.pallas_call(
        _ssm_prep_kernel,
        grid=(tiles,),
        in_specs=[tile3(3, sh), tile3(sh, 3), tile3(ln, sh), tile3(ln, sh), tile3(sh, ln), tile3(sh, ln)],
        out_specs=[tile3(PLANES * ln, 2 * sh), tile3(2 * sh, PLANES * ln), tile3(PLANES * ln, ln),
                   tile3(2 * sh, ln), tile3(SUBLANES, 2 * sh)],
        out_shape=[jax.ShapeDtypeStruct((tiles, PLANES * ln, 2 * sh), BF16),
                   jax.ShapeDtypeStruct((tiles, 2 * sh, PLANES * ln), BF16),
                   jax.ShapeDtypeStruct((tiles, PLANES * ln, ln), BF16),
                   jax.ShapeDtypeStruct((tiles, 2 * sh, ln), BF16),
                   jax.ShapeDtypeStruct((tiles, SUBLANES, 2 * sh), F32)],
        compiler_params=_cparams(("arbitrary",)),
        name="ssm_prep",
    )(lam_row, lam_col, bt(b_re), bt(b_im), ct(c_re), ct(c_im))


def _inproj_kernel(first_keep, scale, x_ref, wqkv_ref, wu_ref, q_ref, k_ref, v_ref, u_ref,
                   kl_ref, vl_ref, slab_ref):
    slabs = slab_ref.shape[1]
    prow = q_ref.shape[1]
    width = u_ref.shape[-1]
    xb = x_ref[...].astype(BF16)

    def to_planes(val, out_ref, buf):
        for c in range(slabs):
            slab_ref[buf, c] = val[:, c * LANES:(c + 1) * LANES]
        for r in range(PLANES):
            rows = [slab_ref[buf, c, pl.ds(r, prow, stride=PLANES), :] for c in range(slabs)]
            out_ref[r] = jnp.concatenate(rows, axis=1).astype(out_ref.dtype)

    q = jnp.dot(xb, wqkv_ref[:, 0:width], preferred_element_type=F32) * scale
    to_planes(q, q_ref, 0)
    k = jnp.dot(xb, wqkv_ref[:, width:2 * width], preferred_element_type=F32)
    to_planes(k, k_ref, 1)
    v = jnp.dot(xb, wqkv_ref[:, 2 * width:3 * width], preferred_element_type=F32)
    to_planes(v, v_ref, 0)
    u = jnp.dot(xb, wu_ref[...], preferred_element_type=F32)
    to_planes(u, u_ref, 1)

    @pl.when(pl.program_id(1) >= first_keep)
    def _():
        kl_ref[...] = k.T
        vl_ref[...] = v.T


def _inproj_prompt(x, w_bf, keep):
    bsz, seq, d_model = x.shape
    width = w_bf.shape[1] // 6
    rows = seq // PLANES
    tile = ROW_TILE // 2
    prow = tile // PLANES
    first_keep = (seq - keep) // tile
    plane = lambda dt: jax.ShapeDtypeStruct((bsz, PLANES, rows, width), dt)
    plane_spec = pl.BlockSpec((None, PLANES, prow, width), lambda b, i: (b, 0, i, 0))
    last = jax.ShapeDtypeStruct((bsz, width, keep), F32)
    last_spec = pl.BlockSpec((None, width, tile), lambda b, i: (b, 0, jnp.maximum(i - first_keep, 0)))
    return pl.pallas_call(
        functools.partial(_inproj_kernel, first_keep, HEAD_DIM ** -0.5 * LOG2E),
        grid=(bsz, seq // tile),
        in_specs=[pl.BlockSpec((None, tile, d_model), lambda b, i: (b, i, 0)),
                  pl.BlockSpec((d_model, 3 * width), lambda b, i: (0, 0), pipeline_mode=ONCE),
                  pl.BlockSpec((d_model, width), lambda b, i: (0, 4), pipeline_mode=ONCE)],
        out_specs=[plane_spec, plane_spec, plane_spec, plane_spec, last_spec, last_spec],
        out_shape=[plane(F32), plane(F32), plane(F32), plane(BF16), last, last],
        scratch_shapes=[pltpu.VMEM((2, width // LANES, tile, LANES), F32)],
        compiler_params=_cparams(("arbitrary", "arbitrary")),
        name="inproj_prompt",
    )(x, w_bf, w_bf)


ATTN_UNROLL = 2


def _attn_kernel(q_ref, k_ref, v_ref, bias_ref, o_ref, acc_ref, m_ref, l_ref, s_ref, p_ref, mn_ref):
    sup = pl.program_id(2)
    is_a = lax.broadcasted_iota(jnp.int32, (BLK, LANES), 1) < HEAD_DIM
    base = pl.multiple_of(sup * BLK, BLK)

    def raw_scores(q, k):
        kb = k.astype(BF16)
        out = []
        for head in range(2):
            qm = jnp.where(is_a if head == 0 else jnp.logical_not(is_a), q, 0.0).astype(BF16)
            out.append(lax.dot_general(qm, kb, (((1,), (1,)), ((), ())), preferred_element_type=F32))
        return out

    def both(a, b):
        return jnp.where(is_a, a, b)

    def weights(s, table, m_old):
        m_new, p = [], []
        for h in range(2):
            sh = s[h] + bias_ref[h, table]
            mb = jnp.broadcast_to(jnp.max(sh, axis=1, keepdims=True), (BLK, LANES))
            mh = mb if m_old is None else jnp.maximum(m_old[h], mb)
            m_new.append(mh)
            p.append(jnp.exp2(sh - jnp.concatenate([mh, mh], axis=1)).astype(BF16))
        return m_new, p

    ones = jnp.ones((2 * BLK, LANES), BF16)

    def combine(m_new, p, v, state):
        vext = jnp.concatenate([v.astype(BF16), ones], axis=1)
        res = [jnp.dot(p[h], vext, preferred_element_type=F32) for h in range(2)]
        acc_new = both(res[0][:, :LANES], res[1][:, :LANES])
        l_new = [res[h][:, LANES:] for h in range(2)]
        if state is not None:
            alpha = [jnp.exp2(state[0][h] - m_new[h]) for h in range(2)]
            l_new = [alpha[h] * state[1][h] + l_new[h] for h in range(2)]
            acc_new = both(alpha[0], alpha[1]) * state[2] + acc_new
        return m_new, l_new, acc_new

    cat = lambda parts: parts[0] if len(parts) == 1 else jnp.concatenate(parts, axis=0)

    def gather(ref, planes, start, size, *lead):
        return cat([ref[(*lead, pln, pl.ds(start, size), slice(None))] for pln in planes])

    def run_pattern(a, dil):
        npl = PLANES // dil
        qrows = BLK // npl
        per_res = BLK // qrows
        groups = PLANES // ATTN_UNROLL

        def geometry(grp, un):
            res, sub = divmod(grp * ATTN_UNROLL + un, per_res)
            off = sub * qrows
            kstart = pl.multiple_of(jnp.maximum(base + off - qrows, 0), qrows)
            table = 2 * a + (jnp.where(sup == 0, 1, 0) if sub == 0 else 0)
            return [res + dil * i for i in range(npl)], off, kstart, table

        def stage_scores(grp):
            stores = []
            for un in range(ATTN_UNROLL):
                planes, off, kstart, _ = geometry(grp, un)
                s = raw_scores(gather(q_ref, planes, base + off, qrows), gather(k_ref, planes, kstart, 2 * qrows))
                stores += [(s_ref, (grp % 2, un, h), s[h]) for h in range(2)]
            return stores

        def stage_softmax(grp):
            stores = []
            for un in range(ATTN_UNROLL):
                planes, off, _, table = geometry(grp, un)
                m_old = None if a == 0 else [gather(m_ref, planes, off, qrows, h) for h in range(2)]
                m_new, p = weights([s_ref[grp % 2, un, h] for h in range(2)], table, m_old)
                for h in range(2):
                    stores += [(p_ref, (grp % 2, un, h), p[h]), (mn_ref, (grp % 2, un, h), m_new[h])]
            return stores

        def stage_values(grp):
            stores = []
            for un in range(ATTN_UNROLL):
                planes, off, kstart, _ = geometry(grp, un)
                old = None if a == 0 else ([gather(m_ref, planes, off, qrows, h) for h in range(2)],
                                           [gather(l_ref, planes, off, qrows, h) for h in range(2)],
                                           gather(acc_ref, planes, off, qrows))
                m_new, l_new, acc_new = combine([mn_ref[grp % 2, un, h] for h in range(2)],
                                                [p_ref[grp % 2, un, h] for h in range(2)],
                                                gather(v_ref, planes, kstart, 2 * qrows), old)
                for i, pln in enumerate(planes):
                    part = slice(i * qrows, (i + 1) * qrows)
                    rows = (pln, pl.ds(off, qrows), slice(None))
                    for h in range(2):
                        stores += [(m_ref, (h,) + rows, m_new[h][part]), (l_ref, (h,) + rows, l_new[h][part])]
                    stores.append((acc_ref, rows, acc_new[part]))
            return stores

        for step in range(groups + 2):
            stores = []
            if 0 <= step - 2 < groups:
                stores += stage_values(step - 2)
            if 0 <= step - 1 < groups:
                stores += stage_softmax(step - 1)
            if step < groups:
                stores += stage_scores(step)
            for ref, idx, val in stores:
                ref[idx] = val

    for a, dil in enumerate(DILATIONS):
        run_pattern(a, dil)
    for pln in range(PLANES):
        o_ref[pl.ds(pln, BLK, stride=PLANES), :] = acc_ref[pln] / both(l_ref[0, pln], l_ref[1, pln])


def _attention_prompt(q, k, v, bias):
    bsz, _, rows, width = q.shape
    pairs = width // LANES
    ntab = bias.shape[1]
    qkv_spec = pl.BlockSpec((None, PLANES, rows, LANES), lambda b, h, s: (b, 0, 0, h))
    return pl.pallas_call(
        _attn_kernel,
        grid=(bsz, pairs, rows // BLK),
        in_specs=[qkv_spec, qkv_spec, qkv_spec,
                  pl.BlockSpec((2, ntab, BLK, 2 * BLK), lambda b, h, s: (h, 0, 0, 0))],
        out_specs=pl.BlockSpec((None, PLANES * BLK, LANES), lambda b, h, s: (b, s, h)),
        out_shape=jax.ShapeDtypeStruct((bsz, PLANES * rows, width), F32),
        scratch_shapes=[pltpu.VMEM((PLANES, BLK, LANES), F32),
                        pltpu.VMEM((2, PLANES, BLK, LANES), F32),
                        pltpu.VMEM((2, PLANES, BLK, LANES), F32),
                        pltpu.VMEM((2, ATTN_UNROLL, 2, BLK, 2 * BLK), F32),
                        pltpu.VMEM((2, ATTN_UNROLL, 2, BLK, 2 * BLK), BF16),
                        pltpu.VMEM((2, ATTN_UNROLL, 2, BLK, LANES), F32)],
        compiler_params=_cparams(("arbitrary", "arbitrary", "arbitrary")),
        name="attention_prompt",
    )(q, k, v, bias)


SSM_COLS = 4


def _ssm_kernel(u_ref, ws_ref, wx_ref, mlag_ref, coef_ref, d_ref, y_ref, sre_ref, sim_ref,
                ucat_ref, s_ref, x_ref, m_ref):
    chunks = u_ref.shape[1]

    @pl.when(pl.program_id(1) == 0)
    def _():
        zero_blk = jnp.zeros((LANES, LANES), BF16)
        for s in range(PLANES):
            for t in range(PLANES):
                blk = mlag_ref[(t - s) * LANES:(t - s + 1) * LANES, :] if t >= s else zero_blk
                m_ref[s * LANES:(s + 1) * LANES, t * LANES:(t + 1) * LANES] = blk

    for s in range(PLANES):
        ucat_ref[:, s * LANES:(s + 1) * LANES] = u_ref[s]
    s_ref[...] = jnp.dot(ucat_ref[...], ws_ref[...], preferred_element_type=F32)
    a_re = coef_ref[1:2, 0:STATE_HALF]
    a_im = coef_ref[1:2, STATE_HALF:2 * STATE_HALF]

    def step(c, carry):
        xr, xi = carry
        x_ref[pl.ds(c, 1), 0:STATE_HALF] = xr
        x_ref[pl.ds(c, 1), STATE_HALF:2 * STATE_HALF] = xi
        sr = s_ref[pl.ds(c, 1), 0:STATE_HALF]
        si = s_ref[pl.ds(c, 1), STATE_HALF:2 * STATE_HALF]
        return a_re * xr - a_im * xi + sr, a_re * xi + a_im * xr + si

    zero = jnp.zeros((1, STATE_HALF), F32)
    xr, xi = lax.fori_loop(0, chunks, step, (zero, zero))
    sre_ref[...] = xr
    sim_ref[...] = xi

    xb = x_ref[...].astype(BF16)
    d = d_ref[...]
    for g in range(PLANES // SSM_COLS):
        cols = slice(g * SSM_COLS * LANES, (g + 1) * SSM_COLS * LANES)
        used = (g + 1) * SSM_COLS * LANES
        y = jnp.dot(ucat_ref[:, :used], m_ref[:used, cols], preferred_element_type=F32)
        y = y + jnp.dot(xb, wx_ref[:, cols], preferred_element_type=F32)
        for t in range(SSM_COLS):
            tok = g * SSM_COLS + t
            y_ref[pl.ds(tok, chunks, stride=PLANES), :] = (y[:, t * LANES:(t + 1) * LANES]
                                                          + d * u_ref[tok].astype(F32))


def _ssm_prompt(u, ws, wx, mlag, coef, d_skip):
    bsz, _, chunks, width = u.shape
    tiles = width // LANES
    sh = STATE_HALF
    state = jax.ShapeDtypeStruct((bsz, tiles, 1, sh), F32)
    state_spec = pl.BlockSpec((None, None, 1, sh), lambda t, b: (b, t, 0, 0))
    per_tile = lambda r, c: pl.BlockSpec((None, r, c), lambda t, b: (t, 0, 0))
    return pl.pallas_call(
        _ssm_kernel,
        grid=(tiles, bsz),
        in_specs=[pl.BlockSpec((None, PLANES, chunks, LANES), lambda t, b: (b, 0, 0, t)),
                  per_tile(PLANES * LANES, 2 * sh), per_tile(2 * sh, PLANES * LANES),
                  per_tile(PLANES * LANES, LANES), per_tile(SUBLANES, 2 * sh),
                  pl.BlockSpec((1, LANES), lambda t, b: (0, t))],
        out_specs=[pl.BlockSpec((None, PLANES * chunks, LANES), lambda t, b: (b, 0, t)),
                   state_spec, state_spec],
        out_shape=[jax.ShapeDtypeStruct((bsz, PLANES * chunks, width), F32), state, state],
        scratch_shapes=[pltpu.VMEM((chunks, PLANES * LANES), BF16),
                        pltpu.VMEM((chunks, 2 * sh), F32),
                        pltpu.VMEM((chunks, 2 * sh), F32),
                        pltpu.VMEM((PLANES * LANES, PLANES * LANES), BF16)],
        compiler_params=_cparams(("arbitrary", "arbitrary")),
        name="ssm_prompt",
    )(u, ws, wx, mlag, coef, d_skip)


def _epilogue_kernel(alpha, attn_ref, y_ref, x_ref, wga_ref, wgs_ref, wglu_ref, bglu_ref, wo_ref, bo_ref,
                     g_ref, b_ref, o_ref):
    d_attn = attn_ref.shape[-1]
    x = x_ref[...]
    xb = x.astype(BF16)
    g_attn = jnp.dot(xb, wga_ref[...], preferred_element_type=F32)
    br_a = (attn_ref[...] * jax.nn.silu(g_attn)).astype(BF16)
    g_ssm = jnp.dot(xb, wgs_ref[...], preferred_element_type=F32)
    z = jax.nn.gelu(y_ref[...])
    gate = jax.nn.sigmoid(jnp.dot(z.astype(BF16), wglu_ref[...], preferred_element_type=F32) + bglu_ref[...])
    br_s = (z * gate * jax.nn.silu(g_ssm)).astype(BF16)
    mix = (jnp.dot(br_a, wo_ref[0:d_attn, :], preferred_element_type=F32)
           + jnp.dot(br_s, wo_ref[d_attn:, :], preferred_element_type=F32) + bo_ref[...])
    t = alpha * x + mix
    mu = jnp.mean(t, axis=-1, keepdims=True)
    var = jnp.mean(jnp.square(t - mu), axis=-1, keepdims=True)
    o_ref[...] = (t - mu) * lax.rsqrt(var + LN_EPS) * g_ref[...] + b_ref[...]


def _epilogue(alpha, name, attn, y, x, w_bf, weights, tile):
    lead = attn.shape[:-2]
    rows, width = attn.shape[-2:]
    d_model = x.shape[-1]
    grid = lead + (rows // tile,)
    none = (None,) * len(lead)
    tiled = lambda w: pl.BlockSpec(none + (tile, w), lambda *g: g + (0,))
    const = lambda shape, col=0: pl.BlockSpec(shape, lambda *g: (0, col), pipeline_mode=ONCE)
    return pl.pallas_call(
        functools.partial(_epilogue_kernel, alpha),
        grid=grid,
        in_specs=[tiled(width), tiled(width), tiled(d_model),
                  const((d_model, width), 3), const((d_model, width), 5),
                  const((width, width)), const((1, width)), const((2 * width, d_model)), const((1, d_model)),
                  const((1, d_model)), const((1, d_model))],
        out_specs=tiled(d_model),
        out_shape=jax.ShapeDtypeStruct(x.shape, F32),
        compiler_params=_cparams(("arbitrary",) * len(grid)),
        name=name,
    )(attn, y, x, w_bf, w_bf, *weights)


def _inproj_sample_kernel(scale, x_ref, w_ref, o_ref):
    acc = jnp.dot(x_ref[...].astype(BF16), w_ref[...], preferred_element_type=F32)
    o_ref[...] = acc * jnp.where(pl.program_id(0) == 0, scale, 1.0)


def _inproj_sample(x, w_bf):
    rows, d_model = x.shape
    width = w_bf.shape[1] // 6
    return pl.pallas_call(
        functools.partial(_inproj_sample_kernel, HEAD_DIM ** -0.5),
        grid=(4,),
        in_specs=[pl.BlockSpec((rows, d_model), lambda j: (0, 0)),
                  pl.BlockSpec((d_model, width), lambda j: (0, j + j // 3))],
        out_specs=pl.BlockSpec((rows, width), lambda j: (0, j)),
        out_shape=jax.ShapeDtypeStruct((rows, 4 * width), F32),
        compiler_params=_cparams(("arbitrary",)),
        name="inproj_sample",
    )(x, w_bf)


SAMPLE_PAIRS = 4


def _attn_sample_kernel(q_ref, kn_ref, vn_ref, kt_ref, vt_ref, tb_ref, tn_ref, o_ref):
    s_len = q_ref.shape[0]
    buf = kt_ref.shape[2]
    npat = tb_ref.shape[0]
    is_a = lax.broadcasted_iota(jnp.int32, (s_len, LANES), 1) < HEAD_DIM
    nt = (((1,), (1,)), ((), ()))
    pad = jnp.zeros((2 * s_len - s_len, LANES), F32)
    for pp in range(SAMPLE_PAIRS):
        lanes = slice(pp * LANES, (pp + 1) * LANES)
        rows = slice(pp * 2 * s_len, (pp + 1) * 2 * s_len)
        qp = q_ref[:, lanes]
        q2 = jnp.concatenate([jnp.where(is_a, qp, 0.0), jnp.where(is_a, 0.0, qp)], axis=0).astype(BF16)
        kt = kt_ref[2 * pp:2 * pp + 2].reshape(2 * HEAD_DIM, buf).astype(BF16)
        vt = vt_ref[2 * pp:2 * pp + 2].reshape(2 * HEAD_DIM, buf).astype(BF16)
        kn = jnp.concatenate([kn_ref[:, lanes], pad], axis=0).astype(BF16)
        vn = jnp.concatenate([vn_ref[:, lanes], pad], axis=0).astype(BF16)
        s_buf = jnp.dot(q2, kt, preferred_element_type=F32)
        s_new = lax.dot_general(q2, kn, nt, preferred_element_type=F32)
        z_buf = [s_buf + tb_ref[a, rows, :] for a in range(npat)]
        z_new = [s_new + tn_ref[a, rows, 0:2 * s_len] for a in range(npat)]
        top = functools.reduce(jnp.maximum, [jnp.max(z, axis=1, keepdims=True) for z in z_buf + z_new])
        w_buf = functools.reduce(jnp.add, [jnp.exp(z - top) for z in z_buf])
        w_new = functools.reduce(jnp.add, [jnp.exp(z - top) for z in z_new])
        den = jnp.sum(w_buf, axis=1, keepdims=True) + jnp.sum(w_new, axis=1, keepdims=True)
        o = lax.dot_general(w_buf.astype(BF16), vt, nt, preferred_element_type=F32)
        o = (o + jnp.dot(w_new.astype(BF16), vn, preferred_element_type=F32)) / den
        o_ref[:, lanes] = jnp.where(is_a, o[0:s_len], o[s_len:2 * s_len])


def _attention_sample(h, kt, vt, tb, tn):
    bsz, s_len, _ = h.shape
    heads, _, buf = kt.shape[1:]
    width = heads * HEAD_DIM
    gw = SAMPLE_PAIRS * LANES
    per_w = width // gw
    new = lambda c: pl.BlockSpec((None, s_len, gw), lambda b, g: (b, 0, c * per_w + g))
    cache = pl.BlockSpec((None, 2 * SAMPLE_PAIRS, HEAD_DIM, buf), lambda b, g: (b, g, 0, 0))
    trows = SAMPLE_PAIRS * 2 * s_len
    return pl.pallas_call(
        _attn_sample_kernel,
        grid=(bsz, per_w),
        in_specs=[new(0), new(1), new(2), cache, cache,
                  pl.BlockSpec((tb.shape[0], trows, buf), lambda b, g: (0, g, 0)),
                  pl.BlockSpec((tn.shape[0], trows, LANES), lambda b, g: (0, g, 0))],
        out_specs=pl.BlockSpec((None, s_len, gw), lambda b, g: (b, 0, g)),
        out_shape=jax.ShapeDtypeStruct((bsz, s_len, width), F32),
        compiler_params=_cparams(("arbitrary", "arbitrary")),
        name="attention_sample",
    )(h, h, h, kt, vt, tb, tn)


def _ssm_sample_kernel(s_len, u_ref, b0_ref, c0_ref, coef_ref, d_ref, x0r_ref, x0i_ref,
                       y_ref, sre_ref, sim_ref, bu_ref, xs_ref):
    bsz = x0r_ref.shape[0]
    bu = jnp.dot(u_ref[...].astype(BF16), b0_ref[...], preferred_element_type=F32)
    slabs = bu_ref.shape[0]
    for c in range(slabs):
        bu_ref[c] = bu[:, c * LANES:(c + 1) * LANES]
    a_re = coef_ref[0:1, 0:STATE_HALF]
    a_im = coef_ref[0:1, STATE_HALF:2 * STATE_HALF]
    half = slabs // 2
    xr, xi = x0r_ref[...], x0i_ref[...]
    for s in range(s_len):
        step = lambda c: bu_ref[c, pl.ds(s, bsz, stride=s_len), :]
        br = jnp.concatenate([step(c) for c in range(half)], axis=1)
        bi = jnp.concatenate([step(c) for c in range(half, slabs)], axis=1)
        xr, xi = a_re * xr - a_im * xi + br, a_re * xi + a_im * xr + bi
        xs_ref[s * bsz:(s + 1) * bsz, :] = jnp.concatenate([xr, xi], axis=1)
    sre_ref[...] = xr
    sim_ref[...] = xi
    y = jnp.dot(xs_ref[...].astype(BF16), c0_ref[...], preferred_element_type=F32)
    d = d_ref[...]
    for s in range(s_len):
        y_ref[pl.ds(s, bsz, stride=s_len), :] = (y[s * bsz:(s + 1) * bsz, :]
                                                 + d * u_ref[pl.ds(s, bsz, stride=s_len), :])


def _ssm_sample(h, ws, c0, coef, d_skip, x0_re, x0_im, s_len):
    rows = h.shape[0]
    bsz = rows // s_len
    tiles = ws.shape[0]
    sh = STATE_HALF
    u_col0 = 3 * (h.shape[1] // 4) // LANES
    st_spec = pl.BlockSpec((bsz, sh), lambda t: (0, t))
    state = jax.ShapeDtypeStruct((bsz, tiles * sh), F32)
    return pl.pallas_call(
        functools.partial(_ssm_sample_kernel, s_len),
        grid=(tiles,),
        in_specs=[pl.BlockSpec((rows, LANES), lambda t: (0, u_col0 + t)),
                  pl.BlockSpec((None, LANES, 2 * sh), lambda t: (t, PLANES - 1, 0)),
                  pl.BlockSpec((None, 2 * sh, LANES), lambda t: (t, 0, 0)),
                  pl.BlockSpec((None, SUBLANES, 2 * sh), lambda t: (t, 0, 0)),
                  pl.BlockSpec((1, LANES), lambda t: (0, t)),
                  st_spec, st_spec],
        out_specs=[pl.BlockSpec((rows, LANES), lambda t: (0, t)), st_spec, st_spec],
        out_shape=[jax.ShapeDtypeStruct((rows, tiles * LANES), F32), state, state],
        scratch_shapes=[pltpu.VMEM((2 * sh // LANES, rows, LANES), F32),
                        pltpu.VMEM((rows, 2 * sh), F32)],
        compiler_params=_cparams(("arbitrary",)),
        name="ssm_sample",
    )(h, ws, c0, coef, d_skip, x0_re, x0_im)


def kernel(x_prompt, x_sample, cache_k, cache_v, state_ssm_re, state_ssm_im, w_in, w_out, b_out, rel_bias,
           lam_re, lam_im, log_dt, b_re, b_im, c_re, c_im, d_skip, w_glu, b_glu, ln_g, ln_b):
    depth = w_in.shape[0]
    assert depth == 1, "one layer per step"
    bsz, seq, d_model = x_prompt.shape
    dbsz, s_len, _ = x_sample.shape
    buf, heads = cache_k.shape[2], cache_k.shape[3]
    width = heads * HEAD_DIM
    groups, nstate = lam_re.shape[1], lam_re.shape[2]
    keep = min(MAX_DISTANCE, seq)
    assert seq % (PLANES * BLK) == 0 and seq >= 2 * PLANES * BLK and keep % ROW_TILE == 0
    assert buf == KPER * max(DILATIONS) and s_len <= min(DILATIONS[:-1])
    assert nstate == SSM_STATE and width == groups * SSM_CH and heads % (2 * SAMPLE_PAIRS) == 0
    alpha = (2 * depth) ** 0.25
    npat = len(DILATIONS)

    w_bf = w_in[0].astype(BF16)
    row = lambda v: v.reshape(1, -1)
    weights = (w_glu[0].astype(BF16), row(b_glu[0]), w_out[0].astype(BF16), row(b_out[0]),
               row(ln_g[0]), row(ln_b[0]))
    d_row = row(d_skip[0])

    rbt = rel_bias.T
    ptab = _bias_tables(jnp.asarray(_prompt_bucket_tables().reshape(1, -1)), rbt, 8192, LOG2E)
    ptab = ptab.reshape(heads, 2 * npat, BLK, 2 * BLK)
    key_w = buf + 2 * LANES
    stab = _bias_tables(jnp.asarray(_sample_bucket_tables(buf, s_len, key_w)), rbt, key_w)
    stab = jnp.transpose(stab.reshape(heads, npat, s_len, key_w), (1, 0, 2, 3)).reshape(npat, heads * s_len, key_w)
    stab_buf, stab_new = stab[:, :, :buf], stab[:, :, buf:buf + LANES]
    ws, wx, m_intra, c0, coef = _ssm_prep(lam_re[0], lam_im[0], log_dt[0], b_re[0], b_im[0], c_re[0], c_im[0])

    q, k, v, u, kl_t, vl_t = _inproj_prompt(x_prompt, w_bf, keep)
    attn = _attention_prompt(q, k, v, ptab)
    y, sre_p, sim_p = _ssm_prompt(u, ws, wx, m_intra, coef, d_row)
    y_prompt = _epilogue(alpha, "epilogue_prompt", attn, y, x_prompt, w_bf, weights, ROW_TILE // 2)
    last = lambda t: jnp.transpose(t.reshape(bsz, heads, HEAD_DIM, keep), (0, 3, 1, 2))[None]
    st_shape = (1, bsz, groups, nstate)

    xs = x_sample.reshape(dbsz * s_len, d_model)
    hs = _inproj_sample(xs, w_bf)
    pos_minor = lambda c: jnp.transpose(c[0], (0, 2, 3, 1))
    attn_s = _attention_sample(hs.reshape(dbsz, s_len, -1), pos_minor(cache_k), pos_minor(cache_v),
                               stab_buf, stab_new)
    y_s, sre_s, sim_s = _ssm_sample(hs, ws, c0, coef, d_row,
                                    state_ssm_re[0].astype(F32).reshape(dbsz, groups * nstate),
                                    state_ssm_im[0].astype(F32).reshape(dbsz, groups * nstate), s_len)
    y_sample = _epilogue(alpha, "epilogue_sample", attn_s.reshape(dbsz * s_len, width), y_s, xs, w_bf, weights,
                         dbsz * s_len)
    new_shape = (1, dbsz, s_len, heads, HEAD_DIM)
    sst_shape = (1, dbsz, groups, nstate)
    return (y_prompt, y_sample.reshape(dbsz, s_len, d_model), last(kl_t), last(vl_t),
            sre_p.reshape(st_shape), sim_p.reshape(st_shape),
            hs[:, width:2 * width].reshape(new_shape), hs[:, 2 * width:3 * width].reshape(new_shape),
            sre_s.reshape(sst_shape), sim_s.reshape(sst_shape))
```

```python
import functools
import math

import jax
import jax.numpy as jnp
import numpy as np
from jax import lax
from jax.experimental import pallas as pl
from jax.experimental.pallas import tpu as pltpu

F32 = jnp.float32
BF16 = jnp.bfloat16

HEAD_DIM = 64
SSM_CH = 16
SSM_STATE = 64
NUM_BUCKETS = 32
MAX_DISTANCE = 2048
KPER = 128
BLK = 128
DILATIONS = (16, 4, 1)
LN_EPS = 1e-5
NEG = -1e30
LOG2E = math.log2(math.e)

LANES = 128
SUBLANES = 8
PLANES = 16
GROUPS_PER_TILE = LANES // SSM_CH
STATE_HALF = GROUPS_PER_TILE * SSM_STATE
VMEM_LIMIT = 56 * 1024 * 1024
ROW_TILE = 512
ONCE = pl.Buffered(1)


def _cparams(sem, vmem=VMEM_LIMIT):
    return pltpu.CompilerParams(dimension_semantics=sem, vmem_limit_bytes=vmem)


def _bucket_np(dist):
    exact = NUM_BUCKETS // 2
    d_f = np.maximum(dist, 1).astype(np.float32)
    large = exact + (np.log(d_f / np.float32(exact)) / np.float32(math.log(MAX_DISTANCE / exact))
                     * np.float32(NUM_BUCKETS - exact)).astype(np.int32)
    large = np.minimum(large, NUM_BUCKETS - 1)
    return np.where(dist < exact, dist, large).astype(np.int32)


def _prompt_rel(dil, first):
    i = np.arange(BLK)[:, None]
    j = np.arange(2 * BLK)[None, :]
    npl = PLANES // dil
    qrows = BLK // npl
    pq, ml = i // qrows, i % qrows
    pk, jl = j // (2 * qrows), j % (2 * qrows)
    back = 0 if first else qrows
    return npl * (ml - jl + back) + (pq - pk)


def _prompt_bucket_tables():
    tabs = []
    for dil in DILATIONS:
        for first in (False, True):
            rel = _prompt_rel(dil, first)
            valid = (rel >= 0) & (rel <= KPER)
            tabs.append(np.where(valid, _bucket_np(np.clip(rel, 0, KPER) * dil), NUM_BUCKETS))
    return np.stack(tabs).reshape(len(DILATIONS) * 2, BLK * 2 * BLK).astype(np.int32)


def _sample_bucket_tables(buf, s_len, width):
    pos = np.arange(buf + s_len)
    tabs = np.full((len(DILATIONS), s_len, width), NUM_BUCKETS, np.int32)
    for a, dil in enumerate(DILATIONS):
        for s in range(s_len):
            dist = buf + s - pos
            valid = (dist >= 0) & (dist % dil == 0) & (dist // dil <= KPER)
            tabs[a, s, :len(pos)] = np.where(valid, _bucket_np(np.maximum(dist, 0)), NUM_BUCKETS)
    return tabs.reshape(1, -1)


def _bias_kernel(scale, idx_ref, rbt_ref, o_ref):
    idx = idx_ref[...]
    onehot = (lax.broadcasted_iota(jnp.int32, (NUM_BUCKETS, idx.shape[1]), 0) == idx).astype(F32)
    tab = jnp.dot(rbt_ref[...], onehot, precision=lax.Precision.HIGHEST,
                  preferred_element_type=F32)
    o_ref[...] = jnp.where(idx < NUM_BUCKETS, tab * scale, NEG)


def _bias_tables(idx, rel_bias_t, chunk, scale=1.0):
    n_heads = rel_bias_t.shape[0]
    total = idx.shape[1]
    return pl.pallas_call(
        functools.partial(_bias_kernel, scale),
        grid=(total // chunk,),
        in_specs=[pl.BlockSpec((1, chunk), lambda c: (0, c)),
                  pl.BlockSpec((n_heads, NUM_BUCKETS), lambda c: (0, 0))],
        out_specs=pl.BlockSpec((n_heads, chunk), lambda c: (0, c)),
        out_shape=jax.ShapeDtypeStruct((n_heads, total), F32),
        compiler_params=_cparams(("arbitrary",)),
        name="bias_tables",
    )(idx, rel_bias_t)


def _discretize(lr, li, ldt):
    lr = jnp.minimum(lr, -1e-4)
    dt = jnp.exp(ldt)
    mag = jnp.exp(lr * dt)
    ab_re, ab_im = mag * jnp.cos(li * dt), mag * jnp.sin(li * dt)
    den = lr * lr + li * li
    inv_re, inv_im = lr / den, -li / den
    n_re, n_im = ab_re - 1.0, ab_im
    cf_re = n_re * inv_re - n_im * inv_im
    cf_im = n_re * inv_im + n_im * inv_re
    return ab_re, ab_im, cf_re, cf_im


def _ssm_prep_kernel(lam_row_ref, lam_col_ref, bt_re_ref, bt_im_ref, ct_re_ref, ct_im_ref,
                     ws_ref, wx_ref, m_ref, c0_ref, coef_ref):
    row = lam_row_ref[0]
    ab_re, ab_im, cf_re, cf_im = _discretize(row[0:1], row[1:2], row[2:3])
    bt_re, bt_im = bt_re_ref[0], bt_im_ref[0]
    bb_re = cf_re * bt_re - cf_im * bt_im
    bb_im = cf_re * bt_im + cf_im * bt_re
    col = lam_col_ref[0]
    abc_re, abc_im, _, _ = _discretize(col[:, 0:1], col[:, 1:2], col[:, 2:3])
    ct_re, ct_im = ct_re_ref[0], ct_im_ref[0]
    c0 = jnp.concatenate([ct_re, -ct_im], axis=0)
    c0_ref[0] = c0.astype(BF16)

    pr, pi = jnp.ones_like(ab_re), jnp.zeros_like(ab_re)
    qr, qi = abc_re, abc_im
    c0b = c0.astype(BF16)
    for lag in range(PLANES):
        w = jnp.concatenate([pr * bb_re - pi * bb_im, pr * bb_im + pi * bb_re], axis=1).astype(BF16)
        s = PLANES - 1 - lag
        ws_ref[0, s * LANES:(s + 1) * LANES, :] = w
        m_ref[0, lag * LANES:(lag + 1) * LANES, :] = jnp.dot(w, c0b, preferred_element_type=F32).astype(BF16)
        wx_ref[0, 0:STATE_HALF, lag * LANES:(lag + 1) * LANES] = (ct_re * qr - ct_im * qi).astype(BF16)
        wx_ref[0, STATE_HALF:2 * STATE_HALF, lag * LANES:(lag + 1) * LANES] = (-(ct_re * qi + ct_im * qr)).astype(BF16)
        pr, pi = pr * ab_re - pi * ab_im, pr * ab_im + pi * ab_re
        qr, qi = qr * abc_re - qi * abc_im, qr * abc_im + qi * abc_re
    coef_ref[0] = jnp.concatenate([
        jnp.concatenate([ab_re, ab_im], axis=1),
        jnp.concatenate([pr, pi], axis=1),
        jnp.zeros((SUBLANES - 2, 2 * STATE_HALF), F32)], axis=0)


def _ssm_prep(lam_re, lam_im, log_dt, b_re, b_im, c_re, c_im):
    groups, n = lam_re.shape
    tiles = groups // GROUPS_PER_TILE
    gpt = GROUPS_PER_TILE
    eye = jnp.eye(gpt, dtype=F32)

    def rows(v):
        return v.reshape(tiles, gpt * n)

    ldt = jnp.broadcast_to(log_dt[:, None], (groups, n))
    lam_row = jnp.stack([rows(lam_re), rows(lam_im), rows(ldt)], axis=1)
    lam_col = jnp.transpose(lam_row, (0, 2, 1))

    def bt(b):
        b = jnp.transpose(b.reshape(tiles, gpt, n, SSM_CH), (0, 1, 3, 2))
        return (b[:, :, :, None, :] * eye[None, :, None, :, None]).reshape(tiles, gpt * SSM_CH, gpt * n)

    def ct(c):
        c = jnp.transpose(c.reshape(tiles, gpt, SSM_CH, n), (0, 1, 3, 2))
        return (c[:, :, :, None, :] * eye[None, :, None, :, None]).reshape(tiles, gpt * n, gpt * SSM_CH)

    sh, ln = STATE_HALF, LANES
    tile3 = lambda a, b: pl.BlockSpec((1, a, b), lambda t: (t, 0, 0))
    return pl.pallas_call(
        _ssm_prep_kernel,
        grid=(tiles,),
        in_specs=[tile3(3, sh), tile3(sh, 3), tile3(ln, sh), tile3(ln, sh), tile3(sh, ln), tile3(sh, ln)],
        out_specs=[tile3(PLANES * ln, 2 * sh), tile3(2 * sh, PLANES * ln), tile3(PLANES * ln, ln),
                   tile3(2 * sh, ln), tile3(SUBLANES, 2 * sh)],
        out_shape=[jax.ShapeDtypeStruct((tiles, PLANES * ln, 2 * sh), BF16),
                   jax.ShapeDtypeStruct((tiles, 2 * sh, PLANES * ln), BF16),
                   jax.ShapeDtypeStruct((tiles, PLANES * ln, ln), BF16),
                   jax.ShapeDtypeStruct((tiles, 2 * sh, ln), BF16),
                   jax.ShapeDtypeStruct((tiles, SUBLANES, 2 * sh), F32)],
        compiler_params=_cparams(("arbitrary",)),
        name="ssm_prep",
    )(lam_row, lam_col, bt(b_re), bt(b_im), ct(c_re), ct(c_im))


def _inproj_kernel(first_keep, scale, x_ref, perm_ref, wqkv_ref, wu_ref, q_ref, k_ref, v_ref, u_ref,
                   kl_ref, vl_ref, slab_ref):
    slabs = slab_ref.shape[0]
    prow = q_ref.shape[1]
    width = u_ref.shape[-1]
    xp = jnp.dot(perm_ref[...], x_ref[...].astype(BF16), preferred_element_type=F32).astype(BF16)

    def to_planes(val, out_ref):
        for r in range(PLANES):
            out_ref[r] = val[r * prow:(r + 1) * prow].astype(out_ref.dtype)

    to_planes(jnp.dot(xp, wqkv_ref[:, 0:width], preferred_element_type=F32) * scale, q_ref)
    k = jnp.dot(xp, wqkv_ref[:, width:2 * width], preferred_element_type=F32)
    to_planes(k, k_ref)
    v = jnp.dot(xp, wqkv_ref[:, 2 * width:3 * width], preferred_element_type=F32)
    to_planes(v, v_ref)
    to_planes(jnp.dot(xp, wu_ref[...], preferred_element_type=F32), u_ref)

    @pl.when(pl.program_id(1) >= first_keep)
    def _():
        def token_order_t(val):
            for r in range(PLANES):
                for c in range(slabs):
                    slab_ref[c, pl.ds(r, prow, stride=PLANES), :] = val[r * prow:(r + 1) * prow,
                                                                        c * LANES:(c + 1) * LANES]
            return jnp.concatenate([slab_ref[c] for c in range(slabs)], axis=1).T
        kl_ref[...] = token_order_t(k)
        vl_ref[...] = token_order_t(v)


def _inproj_prompt(x, w_bf, keep):
    bsz, seq, d_model = x.shape
    width = w_bf.shape[1] // 6
    rows = seq // PLANES
    tile = ROW_TILE // 2
    prow = tile // PLANES
    first_keep = (seq - keep) // tile
    plane = lambda dt: jax.ShapeDtypeStruct((bsz, PLANES, rows, width), dt)
    plane_spec = pl.BlockSpec((None, PLANES, prow, width), lambda b, i: (b, 0, i, 0))
    last = jax.ShapeDtypeStruct((bsz, width, keep), F32)
    last_spec = pl.BlockSpec((None, width, tile), lambda b, i: (b, 0, jnp.maximum(i - first_keep, 0)))
    perm = np.zeros((tile, tile), np.float32)
    perm[np.arange(tile), PLANES * (np.arange(tile) % prow) + np.arange(tile) // prow] = 1.0
    return pl.pallas_call(
        functools.partial(_inproj_kernel, first_keep, HEAD_DIM ** -0.5 * LOG2E),
        grid=(bsz, seq // tile),
        in_specs=[pl.BlockSpec((None, tile, d_model), lambda b, i: (b, i, 0)),
                  pl.BlockSpec((tile, tile), lambda b, i: (0, 0), pipeline_mode=ONCE),
                  pl.BlockSpec((d_model, 3 * width), lambda b, i: (0, 0), pipeline_mode=ONCE),
                  pl.BlockSpec((d_model, width), lambda b, i: (0, 4), pipeline_mode=ONCE)],
        out_specs=[plane_spec, plane_spec, plane_spec, plane_spec, last_spec, last_spec],
        out_shape=[plane(F32), plane(F32), plane(F32), plane(BF16), last, last],
        scratch_shapes=[pltpu.VMEM((width // LANES, tile, LANES), F32)],
        compiler_params=_cparams(("arbitrary", "arbitrary")),
        name="inproj_prompt",
    )(x, jnp.asarray(perm, BF16), w_bf, w_bf)


ATTN_UNROLL = 2


def _attn_kernel(q_ref, k_ref, v_ref, bias_ref, o_ref, acc_ref, m_ref, l_ref, s_ref, p_ref, mn_ref):
    sup = pl.program_id(2)
    is_a = lax.broadcasted_iota(jnp.int32, (BLK, LANES), 1) < HEAD_DIM
    base = pl.multiple_of(sup * BLK, BLK)

    def raw_scores(q, k):
        kb = k.astype(BF16)
        out = []
        for head in range(2):
            qm = jnp.where(is_a if head == 0 else jnp.logical_not(is_a), q, 0.0).astype(BF16)
            out.append(lax.dot_general(qm, kb, (((1,), (1,)), ((), ())), preferred_element_type=F32))
        return out

    def both(a, b):
        return jnp.where(is_a, a, b)

    def weights(s, table, m_old):
        m_new, p = [], []
        for h in range(2):
            sh = s[h] + bias_ref[h, table]
            mb = jnp.broadcast_to(jnp.max(sh, axis=1, keepdims=True), (BLK, LANES))
            mh = mb if m_old is None else jnp.maximum(m_old[h], mb)
            m_new.append(mh)
            p.append(jnp.exp2(sh - jnp.concatenate([mh, mh], axis=1)).astype(BF16))
        return m_new, p

    ones = jnp.ones((2 * BLK, LANES), BF16)

    def combine(m_new, p, v, state):
        vext = jnp.concatenate([v.astype(BF16), ones], axis=1)
        res = [jnp.dot(p[h], vext, preferred_element_type=F32) for h in range(2)]
        acc_new = both(res[0][:, :LANES], res[1][:, :LANES])
        l_new = [res[h][:, LANES:] for h in range(2)]
        if state is not None:
            alpha = [jnp.exp2(state[0][h] - m_new[h]) for h in range(2)]
            l_new = [alpha[h] * state[1][h] + l_new[h] for h in range(2)]
            acc_new = both(alpha[0], alpha[1]) * state[2] + acc_new
        return m_new, l_new, acc_new

    cat = lambda parts: parts[0] if len(parts) == 1 else jnp.concatenate(parts, axis=0)

    def gather(ref, planes, start, size, *lead):
        return cat([ref[(*lead, pln, pl.ds(start, size), slice(None))] for pln in planes])

    def run_pattern(a, dil):
        npl = PLANES // dil
        qrows = BLK // npl
        per_res = BLK // qrows
        groups = PLANES // ATTN_UNROLL

        def geometry(grp, un):
            res, sub = divmod(grp * ATTN_UNROLL + un, per_res)
            off = sub * qrows
            kstart = pl.multiple_of(jnp.maximum(base + off - qrows, 0), qrows)
            table = 2 * a + (jnp.where(sup == 0, 1, 0) if sub == 0 else 0)
            return [res + dil * i for i in range(npl)], off, kstart, table

        def stage_scores(grp):
            stores = []
            for un in range(ATTN_UNROLL):
                planes, off, kstart, _ = geometry(grp, un)
                s = raw_scores(gather(q_ref, planes, base + off, qrows), gather(k_ref, planes, kstart, 2 * qrows))
                stores += [(s_ref, (grp % 2, un, h), s[h]) for h in range(2)]
            return stores

        def stage_softmax(grp):
            stores = []
            for un in range(ATTN_UNROLL):
                planes, off, _, table = geometry(grp, un)
                m_old = None if a == 0 else [gather(m_ref, planes, off, qrows, h) for h in range(2)]
                m_new, p = weights([s_ref[grp % 2, un, h] for h in range(2)], table, m_old)
                for h in range(2):
                    stores += [(p_ref, (grp % 2, un, h), p[h]), (mn_ref, (grp % 2, un, h), m_new[h])]
            return stores

        def stage_values(grp):
            stores = []
            for un in range(ATTN_UNROLL):
                planes, off, kstart, _ = geometry(grp, un)
                old = None if a == 0 else ([gather(m_ref, planes, off, qrows, h) for h in range(2)],
                                           [gather(l_ref, planes, off, qrows, h) for h in range(2)],
                                           gather(acc_ref, planes, off, qrows))
                m_new, l_new, acc_new = combine([mn_ref[grp % 2, un, h] for h in range(2)],
                                                [p_ref[grp % 2, un, h] for h in range(2)],
                                                gather(v_ref, planes, kstart, 2 * qrows), old)
                for i, pln in enumerate(planes):
                    part = slice(i * qrows, (i + 1) * qrows)
                    rows = (pln, pl.ds(off, qrows), slice(None))
                    for h in range(2):
                        stores += [(m_ref, (h,) + rows, m_new[h][part]), (l_ref, (h,) + rows, l_new[h][part])]
                    stores.append((acc_ref, rows, acc_new[part]))
            return stores

        for step in range(groups + 2):
            stores = []
            if 0 <= step - 2 < groups:
                stores += stage_values(step - 2)
            if 0 <= step - 1 < groups:
                stores += stage_softmax(step - 1)
            if step < groups:
                stores += stage_scores(step)
            for ref, idx, val in stores:
                ref[idx] = val

    for a, dil in enumerate(DILATIONS):
        run_pattern(a, dil)
    for pln in range(PLANES):
        o_ref[pl.ds(pln, BLK, stride=PLANES), :] = acc_ref[pln] / both(l_ref[0, pln], l_ref[1, pln])


def _attention_prompt(q, k, v, bias):
    bsz, _, rows, width = q.shape
    pairs = width // LANES
    ntab = bias.shape[1]
    qkv_spec = pl.BlockSpec((None, PLANES, rows, LANES), lambda b, h, s: (b, 0, 0, h))
    return pl.pallas_call(
        _attn_kernel,
        grid=(bsz, pairs, rows // BLK),
        in_specs=[qkv_spec, qkv_spec, qkv_spec,
                  pl.BlockSpec((2, ntab, BLK, 2 * BLK), lambda b, h, s: (h, 0, 0, 0))],
        out_specs=pl.BlockSpec((None, PLANES * BLK, LANES), lambda b, h, s: (b, s, h)),
        out_shape=jax.ShapeDtypeStruct((bsz, PLANES * rows, width), F32),
        scratch_shapes=[pltpu.VMEM((PLANES, BLK, LANES), F32),
                        pltpu.VMEM((2, PLANES, BLK, LANES), F32),
                        pltpu.VMEM((2, PLANES, BLK, LANES), F32),
                        pltpu.VMEM((2, ATTN_UNROLL, 2, BLK, 2 * BLK), F32),
                        pltpu.VMEM((2, ATTN_UNROLL, 2, BLK, 2 * BLK), BF16),
                        pltpu.VMEM((2, ATTN_UNROLL, 2, BLK, LANES), F32)],
        compiler_params=_cparams(("arbitrary", "arbitrary", "arbitrary")),
        name="attention_prompt",
    )(q, k, v, bias)


SSM_COLS = 4


def _ssm_kernel(u_ref, ws_ref, wx_ref, mlag_ref, coef_ref, d_ref, y_ref, sre_ref, sim_ref,
                ucat_ref, s_ref, x_ref, m_ref):
    chunks = u_ref.shape[1]

    @pl.when(pl.program_id(1) == 0)
    def _():
        zero_blk = jnp.zeros((LANES, LANES), BF16)
        for s in range(PLANES):
            for t in range(PLANES):
                blk = mlag_ref[(t - s) * LANES:(t - s + 1) * LANES, :] if t >= s else zero_blk
                m_ref[s * LANES:(s + 1) * LANES, t * LANES:(t + 1) * LANES] = blk

    for s in range(PLANES):
        ucat_ref[:, s * LANES:(s + 1) * LANES] = u_ref[s]
    s_ref[...] = jnp.dot(ucat_ref[...], ws_ref[...], preferred_element_type=F32)
    a_re = coef_ref[1:2, 0:STATE_HALF]
    a_im = coef_ref[1:2, STATE_HALF:2 * STATE_HALF]

    def step(c, carry):
        xr, xi = carry
        x_ref[pl.ds(c, 1), 0:STATE_HALF] = xr
        x_ref[pl.ds(c, 1), STATE_HALF:2 * STATE_HALF] = xi
        sr = s_ref[pl.ds(c, 1), 0:STATE_HALF]
        si = s_ref[pl.ds(c, 1), STATE_HALF:2 * STATE_HALF]
        return a_re * xr - a_im * xi + sr, a_re * xi + a_im * xr + si

    zero = jnp.zeros((1, STATE_HALF), F32)
    xr, xi = lax.fori_loop(0, chunks, step, (zero, zero))
    sre_ref[...] = xr
    sim_ref[...] = xi

    xb = x_ref[...].astype(BF16)
    d = d_ref[...]
    for g in range(PLANES // SSM_COLS):
        cols = slice(g * SSM_COLS * LANES, (g + 1) * SSM_COLS * LANES)
        used = (g + 1) * SSM_COLS * LANES
        y = jnp.dot(ucat_ref[:, :used], m_ref[:used, cols], preferred_element_type=F32)
        y = y + jnp.dot(xb, wx_ref[:, cols], preferred_element_type=F32)
        for t in range(SSM_COLS):
            tok = g * SSM_COLS + t
            y_ref[pl.ds(tok, chunks, stride=PLANES), :] = (y[:, t * LANES:(t + 1) * LANES]
                                                          + d * u_ref[tok].astype(F32))


def _ssm_prompt(u, ws, wx, mlag, coef, d_skip):
    bsz, _, chunks, width = u.shape
    tiles = width // LANES
    sh = STATE_HALF
    state = jax.ShapeDtypeStruct((bsz, tiles, 1, sh), F32)
    state_spec = pl.BlockSpec((None, None, 1, sh), lambda t, b: (b, t, 0, 0))
    per_tile = lambda r, c: pl.BlockSpec((None, r, c), lambda t, b: (t, 0, 0))
    return pl.pallas_call(
        _ssm_kernel,
        grid=(tiles, bsz),
        in_specs=[pl.BlockSpec((None, PLANES, chunks, LANES), lambda t, b: (b, 0, 0, t)),
                  per_tile(PLANES * LANES, 2 * sh), per_tile(2 * sh, PLANES * LANES),
                  per_tile(PLANES * LANES, LANES), per_tile(SUBLANES, 2 * sh),
                  pl.BlockSpec((1, LANES), lambda t, b: (0, t))],
        out_specs=[pl.BlockSpec((None, PLANES * chunks, LANES), lambda t, b: (b, 0, t)),
                   state_spec, state_spec],
        out_shape=[jax.ShapeDtypeStruct((bsz, PLANES * chunks, width), F32), state, state],
        scratch_shapes=[pltpu.VMEM((chunks, PLANES * LANES), BF16),
                        pltpu.VMEM((chunks, 2 * sh), F32),
                        pltpu.VMEM((chunks, 2 * sh), F32),
                        pltpu.VMEM((PLANES * LANES, PLANES * LANES), BF16)],
        compiler_params=_cparams(("arbitrary", "arbitrary")),
        name="ssm_prompt",
    )(u, ws, wx, mlag, coef, d_skip)


def _epilogue_kernel(alpha, attn_ref, y_ref, x_ref, wga_ref, wgs_ref, wglu_ref, bglu_ref, wo_ref, bo_ref,
                     g_ref, b_ref, o_ref):
    d_attn = attn_ref.shape[-1]
    x = x_ref[...]
    xb = x.astype(BF16)
    g_attn = jnp.dot(xb, wga_ref[...], preferred_element_type=F32)
    br_a = (attn_ref[...] * jax.nn.silu(g_attn)).astype(BF16)
    g_ssm = jnp.dot(xb, wgs_ref[...], preferred_element_type=F32)
    z = jax.nn.gelu(y_ref[...])
    gate = jax.nn.sigmoid(jnp.dot(z.astype(BF16), wglu_ref[...], preferred_element_type=F32) + bglu_ref[...])
    br_s = (z * gate * jax.nn.silu(g_ssm)).astype(BF16)
    mix = (jnp.dot(br_a, wo_ref[0:d_attn, :], preferred_element_type=F32)
           + jnp.dot(br_s, wo_ref[d_attn:, :], preferred_element_type=F32) + bo_ref[...])
    t = alpha * x + mix
    mu = jnp.mean(t, axis=-1, keepdims=True)
    var = jnp.mean(jnp.square(t - mu), axis=-1, keepdims=True)
    o_ref[...] = (t - mu) * lax.rsqrt(var + LN_EPS) * g_ref[...] + b_ref[...]


def _epilogue(alpha, name, attn, y, x, w_bf, weights, tile):
    lead = attn.shape[:-2]
    rows, width = attn.shape[-2:]
    d_model = x.shape[-1]
    grid = lead + (rows // tile,)
    none = (None,) * len(lead)
    tiled = lambda w: pl.BlockSpec(none + (tile, w), lambda *g: g + (0,))
    const = lambda shape, col=0: pl.BlockSpec(shape, lambda *g: (0, col), pipeline_mode=ONCE)
    return pl.pallas_call(
        functools.partial(_epilogue_kernel, alpha),
        grid=grid,
        in_specs=[tiled(width), tiled(width), tiled(d_model),
                  const((d_model, width), 3), const((d_model, width), 5),
                  const((width, width)), const((1, width)), const((2 * width, d_model)), const((1, d_model)),
                  const((1, d_model)), const((1, d_model))],
        out_specs=tiled(d_model),
        out_shape=jax.ShapeDtypeStruct(x.shape, F32),
        compiler_params=_cparams(("arbitrary",) * len(grid)),
        name=name,
    )(attn, y, x, w_bf, w_bf, *weights)


def _inproj_sample_kernel(scale, x_ref, w_ref, o_ref):
    acc = jnp.dot(x_ref[...].astype(BF16), w_ref[...], preferred_element_type=F32)
    o_ref[...] = acc * jnp.where(pl.program_id(0) == 0, scale, 1.0)


def _inproj_sample(x, w_bf):
    rows, d_model = x.shape
    width = w_bf.shape[1] // 6
    return pl.pallas_call(
        functools.partial(_inproj_sample_kernel, HEAD_DIM ** -0.5),
        grid=(4,),
        in_specs=[pl.BlockSpec((rows, d_model), lambda j: (0, 0)),
                  pl.BlockSpec((d_model, width), lambda j: (0, j + j // 3))],
        out_specs=pl.BlockSpec((rows, width), lambda j: (0, j)),
        out_shape=jax.ShapeDtypeStruct((rows, 4 * width), F32),
        compiler_params=_cparams(("arbitrary",)),
        name="inproj_sample",
    )(x, w_bf)


SAMPLE_PAIRS = 4


def _attn_sample_kernel(q_ref, kn_ref, vn_ref, kt_ref, vt_ref, tb_ref, tn_ref, o_ref):
    s_len = q_ref.shape[0]
    buf = kt_ref.shape[2]
    npat = tb_ref.shape[0]
    is_a = lax.broadcasted_iota(jnp.int32, (s_len, LANES), 1) < HEAD_DIM
    nt = (((1,), (1,)), ((), ()))
    pad = jnp.zeros((2 * s_len - s_len, LANES), F32)
    for pp in range(SAMPLE_PAIRS):
        lanes = slice(pp * LANES, (pp + 1) * LANES)
        rows = slice(pp * 2 * s_len, (pp + 1) * 2 * s_len)
        qp = q_ref[:, lanes]
        q2 = jnp.concatenate([jnp.where(is_a, qp, 0.0), jnp.where(is_a, 0.0, qp)], axis=0).astype(BF16)
        kt = kt_ref[2 * pp:2 * pp + 2].reshape(2 * HEAD_DIM, buf).astype(BF16)
        vt = vt_ref[2 * pp:2 * pp + 2].reshape(2 * HEAD_DIM, buf).astype(BF16)
        kn = jnp.concatenate([kn_ref[:, lanes], pad], axis=0).astype(BF16)
        vn = jnp.concatenate([vn_ref[:, lanes], pad], axis=0).astype(BF16)
        s_buf = jnp.dot(q2, kt, preferred_element_type=F32)
        s_new = lax.dot_general(q2, kn, nt, preferred_element_type=F32)
        z_buf = [s_buf + tb_ref[a, rows, :] for a in range(npat)]
        z_new = [s_new + tn_ref[a, rows, 0:2 * s_len] for a in range(npat)]
        top = functools.reduce(jnp.maximum, [jnp.max(z, axis=1, keepdims=True) for z in z_buf + z_new])
        w_buf = functools.reduce(jnp.add, [jnp.exp(z - top) for z in z_buf])
        w_new = functools.reduce(jnp.add, [jnp.exp(z - top) for z in z_new])
        den = jnp.sum(w_buf, axis=1, keepdims=True) + jnp.sum(w_new, axis=1, keepdims=True)
        o = lax.dot_general(w_buf.astype(BF16), vt, nt, preferred_element_type=F32)
        o = (o + jnp.dot(w_new.astype(BF16), vn, preferred_element_type=F32)) / den
        o_ref[:, lanes] = jnp.where(is_a, o[0:s_len], o[s_len:2 * s_len])


def _attention_sample(h, kt, vt, tb, tn):
    bsz, s_len, _ = h.shape
    heads, _, buf = kt.shape[1:]
    width = heads * HEAD_DIM
    gw = SAMPLE_PAIRS * LANES
    per_w = width // gw
    new = lambda c: pl.BlockSpec((None, s_len, gw), lambda b, g: (b, 0, c * per_w + g))
    cache = pl.BlockSpec((None, 2 * SAMPLE_PAIRS, HEAD_DIM, buf), lambda b, g: (b, g, 0, 0))
    trows = SAMPLE_PAIRS * 2 * s_len
    return pl.pallas_call(
        _attn_sample_kernel,
        grid=(bsz, per_w),
        in_specs=[new(0), new(1), new(2), cache, cache,
                  pl.BlockSpec((tb.shape[0], trows, buf), lambda b, g: (0, g, 0)),
                  pl.BlockSpec((tn.shape[0], trows, LANES), lambda b, g: (0, g, 0))],
        out_specs=pl.BlockSpec((None, s_len, gw), lambda b, g: (b, 0, g)),
        out_shape=jax.ShapeDtypeStruct((bsz, s_len, width), F32),
        compiler_params=_cparams(("arbitrary", "arbitrary")),
        name="attention_sample",
    )(h, h, h, kt, vt, tb, tn)


def _ssm_sample_kernel(s_len, u_ref, b0_ref, c0_ref, coef_ref, d_ref, x0r_ref, x0i_ref,
                       y_ref, sre_ref, sim_ref, bu_ref, xs_ref):
    bsz = x0r_ref.shape[0]
    bu = jnp.dot(u_ref[...].astype(BF16), b0_ref[...], preferred_element_type=F32)
    slabs = bu_ref.shape[0]
    for c in range(slabs):
        bu_ref[c] = bu[:, c * LANES:(c + 1) * LANES]
    a_re = coef_ref[0:1, 0:STATE_HALF]
    a_im = coef_ref[0:1, STATE_HALF:2 * STATE_HALF]
    half = slabs // 2
    xr, xi = x0r_ref[...], x0i_ref[...]
    for s in range(s_len):
        step = lambda c: bu_ref[c, pl.ds(s, bsz, stride=s_len), :]
        br = jnp.concatenate([step(c) for c in range(half)], axis=1)
        bi = jnp.concatenate([step(c) for c in range(half, slabs)], axis=1)
        xr, xi = a_re * xr - a_im * xi + br, a_re * xi + a_im * xr + bi
        xs_ref[s * bsz:(s + 1) * bsz, :] = jnp.concatenate([xr, xi], axis=1)
    sre_ref[...] = xr
    sim_ref[...] = xi
    y = jnp.dot(xs_ref[...].astype(BF16), c0_ref[...], preferred_element_type=F32)
    d = d_ref[...]
    for s in range(s_len):
        y_ref[pl.ds(s, bsz, stride=s_len), :] = (y[s * bsz:(s + 1) * bsz, :]
                                                 + d * u_ref[pl.ds(s, bsz, stride=s_len), :])


def _ssm_sample(h, ws, c0, coef, d_skip, x0_re, x0_im, s_len):
    rows = h.shape[0]
    bsz = rows // s_len
    tiles = ws.shape[0]
    sh = STATE_HALF
    u_col0 = 3 * (h.shape[1] // 4) // LANES
    st_spec = pl.BlockSpec((bsz, sh), lambda t: (0, t))
    state = jax.ShapeDtypeStruct((bsz, tiles * sh), F32)
    return pl.pallas_call(
        functools.partial(_ssm_sample_kernel, s_len),
        grid=(tiles,),
        in_specs=[pl.BlockSpec((rows, LANES), lambda t: (0, u_col0 + t)),
                  pl.BlockSpec((None, LANES, 2 * sh), lambda t: (t, PLANES - 1, 0)),
                  pl.BlockSpec((None, 2 * sh, LANES), lambda t: (t, 0, 0)),
                  pl.BlockSpec((None, SUBLANES, 2 * sh), lambda t: (t, 0, 0)),
                  pl.BlockSpec((1, LANES), lambda t: (0, t)),
                  st_spec, st_spec],
        out_specs=[pl.BlockSpec((rows, LANES), lambda t: (0, t)), st_spec, st_spec],
        out_shape=[jax.ShapeDtypeStruct((rows, tiles * LANES), F32), state, state],
        scratch_shapes=[pltpu.VMEM((2 * sh // LANES, rows, LANES), F32),
                        pltpu.VMEM((rows, 2 * sh), F32)],
        compiler_params=_cparams(("arbitrary",)),
        name="ssm_sample",
    )(h, ws, c0, coef, d_skip, x0_re, x0_im)


def kernel(x_prompt, x_sample, cache_k, cache_v, state_ssm_re, state_ssm_im, w_in, w_out, b_out, rel_bias,
           lam_re, lam_im, log_dt, b_re, b_im, c_re, c_im, d_skip, w_glu, b_glu, ln_g, ln_b):
    depth = w_in.shape[0]
    assert depth == 1, "one layer per step"
    bsz, seq, d_model = x_prompt.shape
    dbsz, s_len, _ = x_sample.shape
    buf, heads = cache_k.shape[2], cache_k.shape[3]
    width = heads * HEAD_DIM
    groups, nstate = lam_re.shape[1], lam_re.shape[2]
    keep = min(MAX_DISTANCE, seq)
    assert seq % (PLANES * BLK) == 0 and seq >= 2 * PLANES * BLK and keep % ROW_TILE == 0
    assert buf == KPER * max(DILATIONS) and s_len <= min(DILATIONS[:-1])
    assert nstate == SSM_STATE and width == groups * SSM_CH and heads % (2 * SAMPLE_PAIRS) == 0
    alpha = (2 * depth) ** 0.25
    npat = len(DILATIONS)

    w_bf = w_in[0].astype(BF16)
    row = lambda v: v.reshape(1, -1)
    weights = (w_glu[0].astype(BF16), row(b_glu[0]), w_out[0].astype(BF16), row(b_out[0]),
               row(ln_g[0]), row(ln_b[0]))
    d_row = row(d_skip[0])

    rbt = rel_bias.T
    ptab = _bias_tables(jnp.asarray(_prompt_bucket_tables().reshape(1, -1)), rbt, 8192, LOG2E)
    ptab = ptab.reshape(heads, 2 * npat, BLK, 2 * BLK)
    key_w = buf + 2 * LANES
    stab = _bias_tables(jnp.asarray(_sample_bucket_tables(buf, s_len, key_w)), rbt, key_w)
    stab = jnp.transpose(stab.reshape(heads, npat, s_len, key_w), (1, 0, 2, 3)).reshape(npat, heads * s_len, key_w)
    stab_buf, stab_new = stab[:, :, :buf], stab[:, :, buf:buf + LANES]
    ws, wx, m_intra, c0, coef = _ssm_prep(lam_re[0], lam_im[0], log_dt[0], b_re[0], b_im[0], c_re[0], c_im[0])

    q, k, v, u, kl_t, vl_t = _inproj_prompt(x_prompt, w_bf, keep)
    attn = _attention_prompt(q, k, v, ptab)
    y, sre_p, sim_p = _ssm_prompt(u, ws, wx, m_intra, coef, d_row)
    y_prompt = _epilogue(alpha, "epilogue_prompt", attn, y, x_prompt, w_bf, weights, ROW_TILE // 2)
    last = lambda t: jnp.transpose(t.reshape(bsz, heads, HEAD_DIM, keep), (0, 3, 1, 2))[None]
    st_shape = (1, bsz, groups, nstate)

    xs = x_sample.reshape(dbsz * s_len, d_model)
    hs = _inproj_sample(xs, w_bf)
    pos_minor = lambda c: jnp.transpose(c[0], (0, 2, 3, 1))
    attn_s = _attention_sample(hs.reshape(dbsz, s_len, -1), pos_minor(cache_k), pos_minor(cache_v),
                               stab_buf, stab_new)
    y_s, sre_s, sim_s = _ssm_sample(hs, ws, c0, coef, d_row,
                                    state_ssm_re[0].astype(F32).reshape(dbsz, groups * nstate),
                                    state_ssm_im[0].astype(F32).reshape(dbsz, groups * nstate), s_len)
    y_sample = _epilogue(alpha, "epilogue_sample", attn_s.reshape(dbsz * s_len, width), y_s, xs, w_bf, weights,
                         dbsz * s_len)
    new_shape = (1, dbsz, s_len, heads, HEAD_DIM)
    sst_shape = (1, dbsz, groups, nstate)
    return (y_prompt, y_sample.reshape(dbsz, s_len, d_model), last(kl_t), last(vl_t),
            sre_p.reshape(st_shape), sim_p.reshape(st_shape),
            hs[:, width:2 * width].reshape(new_shape), hs[:, 2 * width:3 * width].reshape(new_shape),
            sre_s.reshape(sst_shape), sim_s.reshape(sst_shape))
```

```python
import functools
import math

import jax
import jax.numpy as jnp
import numpy as np
from jax import lax
from jax.experimental import pallas as pl
from jax.experimental.pallas import tpu as pltpu

F32 = jnp.float32
BF16 = jnp.bfloat16

HEAD_DIM = 64
SSM_CH = 16
SSM_STATE = 64
NUM_BUCKETS = 32
MAX_DISTANCE = 2048
KPER = 128
BLK = 128
DILATIONS = (16, 4, 1)
LN_EPS = 1e-5
NEG = -1e30
LOG2E = math.log2(math.e)

LANES = 128
SUBLANES = 8
PLANES = 16
GROUPS_PER_TILE = LANES // SSM_CH
STATE_HALF = GROUPS_PER_TILE * SSM_STATE
VMEM_LIMIT = 56 * 1024 * 1024
ROW_TILE = 512
ONCE = pl.Buffered(1)


def _cparams(sem, vmem=VMEM_LIMIT):
    return pltpu.CompilerParams(dimension_semantics=sem, vmem_limit_bytes=vmem)


def _bucket_np(dist):
    exact = NUM_BUCKETS // 2
    d_f = np.maximum(dist, 1).astype(np.float32)
    large = exact + (np.log(d_f / np.float32(exact)) / np.float32(math.log(MAX_DISTANCE / exact))
                     * np.float32(NUM_BUCKETS - exact)).astype(np.int32)
    large = np.minimum(large, NUM_BUCKETS - 1)
    return np.where(dist < exact, dist, large).astype(np.int32)


def _prompt_rel(dil, first):
    i = np.arange(BLK)[:, None]
    j = np.arange(2 * BLK)[None, :]
    npl = PLANES // dil
    qrows = BLK // npl
    pq, ml = i // qrows, i % qrows
    pk, jl = j // (2 * qrows), j % (2 * qrows)
    back = 0 if first else qrows
    return npl * (ml - jl + back) + (pq - pk)


def _prompt_bucket_tables():
    tabs = []
    for dil in DILATIONS:
        for first in (False, True):
            rel = _prompt_rel(dil, first)
            valid = (rel >= 0) & (rel <= KPER)
            tabs.append(np.where(valid, _bucket_np(np.clip(rel, 0, KPER) * dil), NUM_BUCKETS))
    return np.stack(tabs).reshape(len(DILATIONS) * 2, BLK * 2 * BLK).astype(np.int32)


def _sample_bucket_tables(buf, s_len, width):
    pos = np.arange(buf + s_len)
    tabs = np.full((len(DILATIONS), s_len, width), NUM_BUCKETS, np.int32)
    for a, dil in enumerate(DILATIONS):
        for s in range(s_len):
            dist = buf + s - pos
            valid = (dist >= 0) & (dist % dil == 0) & (dist // dil <= KPER)
            tabs[a, s, :len(pos)] = np.where(valid, _bucket_np(np.maximum(dist, 0)), NUM_BUCKETS)
    return tabs.reshape(1, -1)


def _bias_kernel(scale, idx_ref, rbt_ref, o_ref):
    idx = idx_ref[...]
    onehot = (lax.broadcasted_iota(jnp.int32, (NUM_BUCKETS, idx.shape[1]), 0) == idx).astype(F32)
    tab = jnp.dot(rbt_ref[...], onehot, precision=lax.Precision.HIGHEST,
                  preferred_element_type=F32)
    o_ref[...] = jnp.where(idx < NUM_BUCKETS, tab * scale, NEG)


def _bias_tables(idx, rel_bias_t, chunk, scale=1.0):
    n_heads = rel_bias_t.shape[0]
    total = idx.shape[1]
    return pl.pallas_call(
        functools.partial(_bias_kernel, scale),
        grid=(total // chunk,),
        in_specs=[pl.BlockSpec((1, chunk), lambda c: (0, c)),
                  pl.BlockSpec((n_heads, NUM_BUCKETS), lambda c: (0, 0))],
        out_specs=pl.BlockSpec((n_heads, chunk), lambda c: (0, c)),
        out_shape=jax.ShapeDtypeStruct((n_heads, total), F32),
        compiler_params=_cparams(("arbitrary",)),
        name="bias_tables",
    )(idx, rel_bias_t)


def _discretize(lr, li, ldt):
    lr = jnp.minimum(lr, -1e-4)
    dt = jnp.exp(ldt)
    mag = jnp.exp(lr * dt)
    ab_re, ab_im = mag * jnp.cos(li * dt), mag * jnp.sin(li * dt)
    den = lr * lr + li * li
    inv_re, inv_im = lr / den, -li / den
    n_re, n_im = ab_re - 1.0, ab_im
    cf_re = n_re * inv_re - n_im * inv_im
    cf_im = n_re * inv_im + n_im * inv_re
    return ab_re, ab_im, cf_re, cf_im


def _ssm_prep_kernel(lam_row_ref, lam_col_ref, bt_re_ref, bt_im_ref, ct_re_ref, ct_im_ref,
                     ws_ref, wx_ref, m_ref, c0_ref, coef_ref):
    row = lam_row_ref[0]
    ab_re, ab_im, cf_re, cf_im = _discretize(row[0:1], row[1:2], row[2:3])
    bt_re, bt_im = bt_re_ref[0], bt_im_ref[0]
    bb_re = cf_re * bt_re - cf_im * bt_im
    bb_im = cf_re * bt_im + cf_im * bt_re
    col = lam_col_ref[0]
    abc_re, abc_im, _, _ = _discretize(col[:, 0:1], col[:, 1:2], col[:, 2:3])
    ct_re, ct_im = ct_re_ref[0], ct_im_ref[0]
    c0 = jnp.concatenate([ct_re, -ct_im], axis=0)
    c0_ref[0] = c0.astype(BF16)

    pr, pi = jnp.ones_like(ab_re), jnp.zeros_like(ab_re)
    qr, qi = abc_re, abc_im
    c0b = c0.astype(BF16)
    for lag in range(PLANES):
        w = jnp.concatenate([pr * bb_re - pi * bb_im, pr * bb_im + pi * bb_re], axis=1).astype(BF16)
        s = PLANES - 1 - lag
        ws_ref[0, s * LANES:(s + 1) * LANES, :] = w
        m_ref[0, lag * LANES:(lag + 1) * LANES, :] = jnp.dot(w, c0b, preferred_element_type=F32).astype(BF16)
        wx_ref[0, 0:STATE_HALF, lag * LANES:(lag + 1) * LANES] = (ct_re * qr - ct_im * qi).astype(BF16)
        wx_ref[0, STATE_HALF:2 * STATE_HALF, lag * LANES:(lag + 1) * LANES] = (-(ct_re * qi + ct_im * qr)).astype(BF16)
        pr, pi = pr * ab_re - pi * ab_im, pr * ab_im + pi * ab_re
        qr, qi = qr * abc_re - qi * abc_im, qr * abc_im + qi * abc_re
    coef_ref[0] = jnp.concatenate([
        jnp.concatenate([ab_re, ab_im], axis=1),
        jnp.concatenate([pr, pi], axis=1),
        jnp.zeros((SUBLANES - 2, 2 * STATE_HALF), F32)], axis=0)


def _ssm_prep(lam_re, lam_im, log_dt, b_re, b_im, c_re, c_im):
    groups, n = lam_re.shape
    tiles = groups // GROUPS_PER_TILE
    gpt = GROUPS_PER_TILE
    eye = jnp.eye(gpt, dtype=F32)

    def rows(v):
        return v.reshape(tiles, gpt * n)

    ldt = jnp.broadcast_to(log_dt[:, None], (groups, n))
    lam_row = jnp.stack([rows(lam_re), rows(lam_im), rows(ldt)], axis=1)
    lam_col = jnp.transpose(lam_row, (0, 2, 1))

    def bt(b):
        b = jnp.transpose(b.reshape(tiles, gpt, n, SSM_CH), (0, 1, 3, 2))
        return (b[:, :, :, None, :] * eye[None, :, None, :, None]).reshape(tiles, gpt * SSM_CH, gpt * n)

    def ct(c):
        c = jnp.transpose(c.reshape(tiles, gpt, SSM_CH, n), (0, 1, 3, 2))
        return (c[:, :, :, None, :] * eye[None, :, None, :, None]).reshape(tiles, gpt * n, gpt * SSM_CH)

    sh, ln = STATE_HALF, LANES
    tile3 = lambda a, b: pl.BlockSpec((1, a, b), lambda t: (t, 0, 0))
    return pl.pallas_call(
        _ssm_prep_kernel,
        grid=(tiles,),
        in_specs=[tile3(3, sh), tile3(sh, 3), tile3(ln, sh), tile3(ln, sh), tile3(sh, ln), tile3(sh, ln)],
        out_specs=[tile3(PLANES * ln, 2 * sh), tile3(2 * sh, PLANES * ln), tile3(PLANES * ln, ln),
                   tile3(2 * sh, ln), tile3(SUBLANES, 2 * sh)],
        out_shape=[jax.ShapeDtypeStruct((tiles, PLANES * ln, 2 * sh), BF16),
                   jax.ShapeDtypeStruct((tiles, 2 * sh, PLANES * ln), BF16),
                   jax.ShapeDtypeStruct((tiles, PLANES * ln, ln), BF16),
                   jax.ShapeDtypeStruct((tiles, 2 * sh, ln), BF16),
                   jax.ShapeDtypeStruct((tiles, SUBLANES, 2 * sh), F32)],
        compiler_params=_cparams(("arbitrary",)),
        name="ssm_prep",
    )(lam_row, lam_col, bt(b_re), bt(b_im), ct(c_re), ct(c_im))


def _inproj_kernel(first_keep, scale, x_ref, perm_ref, wqkv_ref, wu_ref, *rest):
    sample_in, (q_ref, k_ref, v_ref, u_ref, kl_ref, vl_ref, sample_out, slab_ref) = rest[:7], rest[7:]
    slabs = slab_ref.shape[0]
    prow = q_ref.shape[1]
    width = u_ref.shape[-1]
    xp = jnp.dot(perm_ref[...], x_ref[...].astype(BF16), preferred_element_type=F32).astype(BF16)

    def to_planes(val, out_ref):
        for r in range(PLANES):
            out_ref[r] = val[r * prow:(r + 1) * prow].astype(out_ref.dtype)

    sq_ref, skn_ref, svn_ref, skt_ref, svt_ref, stb_ref, stn_ref = sample_in
    to_planes(jnp.dot(xp, wqkv_ref[:, 0:width], preferred_element_type=F32) * scale, q_ref)
    sample_scores = _sample_scores(sq_ref, skn_ref, skt_ref)
    k = jnp.dot(xp, wqkv_ref[:, width:2 * width], preferred_element_type=F32)
    to_planes(k, k_ref)
    v = jnp.dot(xp, wqkv_ref[:, 2 * width:3 * width], preferred_element_type=F32)
    to_planes(v, v_ref)
    _sample_outputs(sample_scores, svn_ref, svt_ref, stb_ref, stn_ref, sample_out)
    to_planes(jnp.dot(xp, wu_ref[...], preferred_element_type=F32), u_ref)

    @pl.when(pl.program_id(1) >= first_keep)
    def _():
        def token_order_t(val):
            for r in range(PLANES):
                for c in range(slabs):
                    slab_ref[c, pl.ds(r, prow, stride=PLANES), :] = val[r * prow:(r + 1) * prow,
                                                                        c * LANES:(c + 1) * LANES]
            return jnp.concatenate([slab_ref[c] for c in range(slabs)], axis=1).T
        kl_ref[...] = token_order_t(k)
        vl_ref[...] = token_order_t(v)


def _inproj_prompt(x, w_bf, keep, sample):
    bsz, seq, d_model = x.shape
    width = w_bf.shape[1] // 6
    rows = seq // PLANES
    tile = ROW_TILE // 2
    prow = tile // PLANES
    first_keep = (seq - keep) // tile
    steps = seq // tile
    sample_specs, sample_out_spec, sample_out = _attention_sample_specs(*sample, lambda b, i: b * steps + i)
    assert bsz * steps == sample[0].shape[0] * (width // (SAMPLE_PAIRS * LANES)), "one sample unit per grid step"
    plane = lambda dt: jax.ShapeDtypeStruct((bsz, PLANES, rows, width), dt)
    plane_spec = pl.BlockSpec((None, PLANES, prow, width), lambda b, i: (b, 0, i, 0))
    last = jax.ShapeDtypeStruct((bsz, width, keep), F32)
    last_spec = pl.BlockSpec((None, width, tile), lambda b, i: (b, 0, jnp.maximum(i - first_keep, 0)))
    perm = np.zeros((tile, tile), np.float32)
    perm[np.arange(tile), PLANES * (np.arange(tile) % prow) + np.arange(tile) // prow] = 1.0
    return pl.pallas_call(
        functools.partial(_inproj_kernel, first_keep, HEAD_DIM ** -0.5 * LOG2E),
        grid=(bsz, steps),
        in_specs=[pl.BlockSpec((None, tile, d_model), lambda b, i: (b, i, 0)),
                  pl.BlockSpec((tile, tile), lambda b, i: (0, 0), pipeline_mode=ONCE),
                  pl.BlockSpec((d_model, 3 * width), lambda b, i: (0, 0), pipeline_mode=ONCE),
                  pl.BlockSpec((d_model, width), lambda b, i: (0, 4), pipeline_mode=ONCE)] + sample_specs,
        out_specs=[plane_spec, plane_spec, plane_spec, plane_spec, last_spec, last_spec, sample_out_spec],
        out_shape=[plane(F32), plane(F32), plane(F32), plane(BF16), last, last, sample_out],
        scratch_shapes=[pltpu.VMEM((width // LANES, tile, LANES), F32)],
        compiler_params=_cparams(("arbitrary", "arbitrary")),
        name="inproj_prompt",
    )(x, jnp.asarray(perm, BF16), w_bf, w_bf, sample[0], sample[0], sample[0], *sample[1:])


ATTN_UNROLL = 2


def _attn_kernel(q_ref, k_ref, v_ref, bias_ref, o_ref, acc_ref, m_ref, l_ref, s_ref, p_ref, mn_ref):
    sup = pl.program_id(2)
    is_a = lax.broadcasted_iota(jnp.int32, (BLK, LANES), 1) < HEAD_DIM
    base = pl.multiple_of(sup * BLK, BLK)

    def raw_scores(q, k):
        kb = k.astype(BF16)
        out = []
        for head in range(2):
            qm = jnp.where(is_a if head == 0 else jnp.logical_not(is_a), q, 0.0).astype(BF16)
            out.append(lax.dot_general(qm, kb, (((1,), (1,)), ((), ())), preferred_element_type=F32))
        return out

    def both(a, b):
        return jnp.where(is_a, a, b)

    def weights(s, table, m_old):
        m_new, p = [], []
        for h in range(2):
            sh = s[h] + bias_ref[h, table]
            mb = jnp.broadcast_to(jnp.max(sh, axis=1, keepdims=True), (BLK, LANES))
            mh = mb if m_old is None else jnp.maximum(m_old[h], mb)
            m_new.append(mh)
            p.append(jnp.exp2(sh - jnp.concatenate([mh, mh], axis=1)).astype(BF16))
        return m_new, p

    ones = jnp.ones((2 * BLK, LANES), BF16)

    def combine(m_new, p, v, state):
        vext = jnp.concatenate([v.astype(BF16), ones], axis=1)
        res = [jnp.dot(p[h], vext, preferred_element_type=F32) for h in range(2)]
        acc_new = both(res[0][:, :LANES], res[1][:, :LANES])
        l_new = [res[h][:, LANES:] for h in range(2)]
        if state is not None:
            alpha = [jnp.exp2(state[0][h] - m_new[h]) for h in range(2)]
            l_new = [alpha[h] * state[1][h] + l_new[h] for h in range(2)]
            acc_new = both(alpha[0], alpha[1]) * state[2] + acc_new
        return m_new, l_new, acc_new

    cat = lambda parts: parts[0] if len(parts) == 1 else jnp.concatenate(parts, axis=0)

    def gather(ref, planes, start, size, *lead):
        return cat([ref[(*lead, pln, pl.ds(start, size), slice(None))] for pln in planes])

    def run_pattern(a, dil):
        npl = PLANES // dil
        qrows = BLK // npl
        per_res = BLK // qrows
        groups = PLANES // ATTN_UNROLL

        def geometry(grp, un):
            res, sub = divmod(grp * ATTN_UNROLL + un, per_res)
            off = sub * qrows
            kstart = pl.multiple_of(jnp.maximum(base + off - qrows, 0), qrows)
            table = 2 * a + (jnp.where(sup == 0, 1, 0) if sub == 0 else 0)
            return [res + dil * i for i in range(npl)], off, kstart, table

        def stage_scores(grp):
            stores = []
            for un in range(ATTN_UNROLL):
                planes, off, kstart, _ = geometry(grp, un)
                s = raw_scores(gather(q_ref, planes, base + off, qrows), gather(k_ref, planes, kstart, 2 * qrows))
                stores += [(s_ref, (grp % 2, un, h), s[h]) for h in range(2)]
            return stores

        def stage_softmax(grp):
            stores = []
            for un in range(ATTN_UNROLL):
                planes, off, _, table = geometry(grp, un)
                m_old = None if a == 0 else [gather(m_ref, planes, off, qrows, h) for h in range(2)]
                m_new, p = weights([s_ref[grp % 2, un, h] for h in range(2)], table, m_old)
                for h in range(2):
                    stores += [(p_ref, (grp % 2, un, h), p[h]), (mn_ref, (grp % 2, un, h), m_new[h])]
            return stores

        def stage_values(grp):
            stores = []
            for un in range(ATTN_UNROLL):
                planes, off, kstart, _ = geometry(grp, un)
                old = None if a == 0 else ([gather(m_ref, planes, off, qrows, h) for h in range(2)],
                                           [gather(l_ref, planes, off, qrows, h) for h in range(2)],
                                           gather(acc_ref, planes, off, qrows))
                m_new, l_new, acc_new = combine([mn_ref[grp % 2, un, h] for h in range(2)],
                                                [p_ref[grp % 2, un, h] for h in range(2)],
                                                gather(v_ref, planes, kstart, 2 * qrows), old)
                for i, pln in enumerate(planes):
                    part = slice(i * qrows, (i + 1) * qrows)
                    rows = (pln, pl.ds(off, qrows), slice(None))
                    for h in range(2):
                        stores += [(m_ref, (h,) + rows, m_new[h][part]), (l_ref, (h,) + rows, l_new[h][part])]
                    stores.append((acc_ref, rows, acc_new[part]))
            return stores

        for step in range(groups + 2):
            stores = []
            if 0 <= step - 2 < groups:
                stores += stage_values(step - 2)
            if 0 <= step - 1 < groups:
                stores += stage_softmax(step - 1)
            if step < groups:
                stores += stage_scores(step)
            for ref, idx, val in stores:
                ref[idx] = val

    for a, dil in enumerate(DILATIONS):
        run_pattern(a, dil)
    for pln in range(PLANES):
        o_ref[pl.ds(pln, BLK, stride=PLANES), :] = acc_ref[pln] / both(l_ref[0, pln], l_ref[1, pln])


def _attention_prompt(q, k, v, bias):
    bsz, _, rows, width = q.shape
    pairs = width // LANES
    ntab = bias.shape[1]
    qkv_spec = pl.BlockSpec((None, PLANES, rows, LANES), lambda b, h, s: (b, 0, 0, h))
    return pl.pallas_call(
        _attn_kernel,
        grid=(bsz, pairs, rows // BLK),
        in_specs=[qkv_spec, qkv_spec, qkv_spec,
                  pl.BlockSpec((2, ntab, BLK, 2 * BLK), lambda b, h, s: (h, 0, 0, 0))],
        out_specs=pl.BlockSpec((None, PLANES * BLK, LANES), lambda b, h, s: (b, s, h)),
        out_shape=jax.ShapeDtypeStruct((bsz, PLANES * rows, width), F32),
        scratch_shapes=[pltpu.VMEM((PLANES, BLK, LANES), F32),
                        pltpu.VMEM((2, PLANES, BLK, LANES), F32),
                        pltpu.VMEM((2, PLANES, BLK, LANES), F32),
                        pltpu.VMEM((2, ATTN_UNROLL, 2, BLK, 2 * BLK), F32),
                        pltpu.VMEM((2, ATTN_UNROLL, 2, BLK, 2 * BLK), BF16),
                        pltpu.VMEM((2, ATTN_UNROLL, 2, BLK, LANES), F32)],
        compiler_params=_cparams(("arbitrary", "arbitrary", "arbitrary")),
        name="attention_prompt",
    )(q, k, v, bias)


SSM_COLS = 4


def _ssm_kernel(u_ref, ws_ref, wx_ref, mlag_ref, coef_ref, d_ref, y_ref, sre_ref, sim_ref,
                ucat_ref, s_ref, x_ref, m_ref):
    chunks = u_ref.shape[1]

    @pl.when(pl.program_id(1) == 0)
    def _():
        zero_blk = jnp.zeros((LANES, LANES), BF16)
        for s in range(PLANES):
            for t in range(PLANES):
                blk = mlag_ref[(t - s) * LANES:(t - s + 1) * LANES, :] if t >= s else zero_blk
                m_ref[s * LANES:(s + 1) * LANES, t * LANES:(t + 1) * LANES] = blk

    for s in range(PLANES):
        ucat_ref[:, s * LANES:(s + 1) * LANES] = u_ref[s]
    s_ref[...] = jnp.dot(ucat_ref[...], ws_ref[...], preferred_element_type=F32)
    a_re = coef_ref[1:2, 0:STATE_HALF]
    a_im = coef_ref[1:2, STATE_HALF:2 * STATE_HALF]

    def step(c, carry):
        xr, xi = carry
        x_ref[pl.ds(c, 1), 0:STATE_HALF] = xr
        x_ref[pl.ds(c, 1), STATE_HALF:2 * STATE_HALF] = xi
        sr = s_ref[pl.ds(c, 1), 0:STATE_HALF]
        si = s_ref[pl.ds(c, 1), STATE_HALF:2 * STATE_HALF]
        return a_re * xr - a_im * xi + sr, a_re * xi + a_im * xr + si

    zero = jnp.zeros((1, STATE_HALF), F32)
    xr, xi = lax.fori_loop(0, chunks, step, (zero, zero))
    sre_ref[...] = xr
    sim_ref[...] = xi

    xb = x_ref[...].astype(BF16)
    d = d_ref[...]
    for g in range(PLANES // SSM_COLS):
        cols = slice(g * SSM_COLS * LANES, (g + 1) * SSM_COLS * LANES)
        used = (g + 1) * SSM_COLS * LANES
        y = jnp.dot(ucat_ref[:, :used], m_ref[:used, cols], preferred_element_type=F32)
        y = y + jnp.dot(xb, wx_ref[:, cols], preferred_element_type=F32)
        for t in range(SSM_COLS):
            tok = g * SSM_COLS + t
            y_ref[pl.ds(tok, chunks, stride=PLANES), :] = (y[:, t * LANES:(t + 1) * LANES]
                                                          + d * u_ref[tok].astype(F32))


def _ssm_prompt(u, ws, wx, mlag, coef, d_skip):
    bsz, _, chunks, width = u.shape
    tiles = width // LANES
    sh = STATE_HALF
    state = jax.ShapeDtypeStruct((bsz, tiles, 1, sh), F32)
    state_spec = pl.BlockSpec((None, None, 1, sh), lambda t, b: (b, t, 0, 0))
    per_tile = lambda r, c: pl.BlockSpec((None, r, c), lambda t, b: (t, 0, 0))
    return pl.pallas_call(
        _ssm_kernel,
        grid=(tiles, bsz),
        in_specs=[pl.BlockSpec((None, PLANES, chunks, LANES), lambda t, b: (b, 0, 0, t)),
                  per_tile(PLANES * LANES, 2 * sh), per_tile(2 * sh, PLANES * LANES),
                  per_tile(PLANES * LANES, LANES), per_tile(SUBLANES, 2 * sh),
                  pl.BlockSpec((1, LANES), lambda t, b: (0, t))],
        out_specs=[pl.BlockSpec((None, PLANES * chunks, LANES), lambda t, b: (b, 0, t)),
                   state_spec, state_spec],
        out_shape=[jax.ShapeDtypeStruct((bsz, PLANES * chunks, width), F32), state, state],
        scratch_shapes=[pltpu.VMEM((chunks, PLANES * LANES), BF16),
                        pltpu.VMEM((chunks, 2 * sh), F32),
                        pltpu.VMEM((chunks, 2 * sh), F32),
                        pltpu.VMEM((PLANES * LANES, PLANES * LANES), BF16)],
        compiler_params=_cparams(("arbitrary", "arbitrary")),
        name="ssm_prompt",
    )(u, ws, wx, mlag, coef, d_skip)


EPILOGUE_ROWS = 128


def _epilogue_kernel(alpha, attn_ref, y_ref, x_ref, wga_ref, wgs_ref, wglu_ref, bglu_ref, wo_ref, bo_ref,
                     g_ref, b_ref, o_ref):
    d_attn = attn_ref.shape[-1]
    rows = x_ref.shape[0]
    part = min(rows, EPILOGUE_ROWS)
    for r0 in range(0, rows, part):
        rs = slice(r0, r0 + part)
        x = x_ref[rs, :]
        xb = x.astype(BF16)
        g_attn = jnp.dot(xb, wga_ref[...], preferred_element_type=F32)
        br_a = (attn_ref[rs, :] * jax.nn.silu(g_attn)).astype(BF16)
        g_ssm = jnp.dot(xb, wgs_ref[...], preferred_element_type=F32)
        z = jax.nn.gelu(y_ref[rs, :])
        gate = jax.nn.sigmoid(jnp.dot(z.astype(BF16), wglu_ref[...], preferred_element_type=F32) + bglu_ref[...])
        br_s = (z * gate * jax.nn.silu(g_ssm)).astype(BF16)
        mix = (jnp.dot(br_a, wo_ref[0:d_attn, :], preferred_element_type=F32)
               + jnp.dot(br_s, wo_ref[d_attn:, :], preferred_element_type=F32) + bo_ref[...])
        t = alpha * x + mix
        mu = jnp.mean(t, axis=-1, keepdims=True)
        var = jnp.mean(jnp.square(t - mu), axis=-1, keepdims=True)
        o_ref[rs, :] = (t - mu) * lax.rsqrt(var + LN_EPS) * g_ref[...] + b_ref[...]


def _epilogue(alpha, name, attn, y, x, w_bf, weights, tile):
    lead = attn.shape[:-2]
    rows, width = attn.shape[-2:]
    d_model = x.shape[-1]
    grid = lead + (rows // tile,)
    none = (None,) * len(lead)
    tiled = lambda w: pl.BlockSpec(none + (tile, w), lambda *g: g + (0,))
    const = lambda shape, col=0: pl.BlockSpec(shape, lambda *g: (0, col), pipeline_mode=ONCE)
    return pl.pallas_call(
        functools.partial(_epilogue_kernel, alpha),
        grid=grid,
        in_specs=[tiled(width), tiled(width), tiled(d_model),
                  const((d_model, width), 3), const((d_model, width), 5),
                  const((width, width)), const((1, width)), const((2 * width, d_model)), const((1, d_model)),
                  const((1, d_model)), const((1, d_model))],
        out_specs=tiled(d_model),
        out_shape=jax.ShapeDtypeStruct(x.shape, F32),
        compiler_params=_cparams(("arbitrary",) * len(grid)),
        name=name,
    )(attn, y, x, w_bf, w_bf, *weights)


def _inproj_sample_kernel(scale, x_ref, w_ref, o_ref):
    acc = jnp.dot(x_ref[...].astype(BF16), w_ref[...], preferred_element_type=F32)
    o_ref[...] = acc * jnp.where(pl.program_id(0) == 0, scale, 1.0)


def _inproj_sample(x, w_bf):
    rows, d_model = x.shape
    width = w_bf.shape[1] // 6
    return pl.pallas_call(
        functools.partial(_inproj_sample_kernel, HEAD_DIM ** -0.5),
        grid=(4,),
        in_specs=[pl.BlockSpec((rows, d_model), lambda j: (0, 0)),
                  pl.BlockSpec((d_model, width), lambda j: (0, j + j // 3))],
        out_specs=pl.BlockSpec((rows, width), lambda j: (0, j)),
        out_shape=jax.ShapeDtypeStruct((rows, 4 * width), F32),
        compiler_params=_cparams(("arbitrary",)),
        name="inproj_sample",
    )(x, w_bf)


SAMPLE_PAIRS = 4


_NT = (((1,), (1,)), ((), ()))


def _sample_scores(q_ref, kn_ref, kt_ref):
    s_len = q_ref.shape[0]
    buf = kt_ref.shape[2]
    is_a = lax.broadcasted_iota(jnp.int32, (s_len, LANES), 1) < HEAD_DIM
    pad = jnp.zeros((s_len, LANES), F32)
    out = []
    for pp in range(SAMPLE_PAIRS):
        lanes = slice(pp * LANES, (pp + 1) * LANES)
        qp = q_ref[:, lanes]
        q2 = jnp.concatenate([jnp.where(is_a, qp, 0.0), jnp.where(is_a, 0.0, qp)], axis=0).astype(BF16)
        kt = kt_ref[2 * pp:2 * pp + 2].reshape(2 * HEAD_DIM, buf).astype(BF16)
        kn = jnp.concatenate([kn_ref[:, lanes], pad], axis=0).astype(BF16)
        out.append((jnp.dot(q2, kt, preferred_element_type=F32),
                    lax.dot_general(q2, kn, _NT, preferred_element_type=F32)))
    return out


def _sample_outputs(scores, vn_ref, vt_ref, tb_ref, tn_ref, o_ref):
    s_len = vn_ref.shape[0]
    buf = vt_ref.shape[2]
    npat = tb_ref.shape[0]
    is_a = lax.broadcasted_iota(jnp.int32, (s_len, LANES), 1) < HEAD_DIM
    pad = jnp.zeros((s_len, LANES), F32)
    for pp, (s_buf, s_new) in enumerate(scores):
        lanes = slice(pp * LANES, (pp + 1) * LANES)
        rows = slice(pp * 2 * s_len, (pp + 1) * 2 * s_len)
        vt = vt_ref[2 * pp:2 * pp + 2].reshape(2 * HEAD_DIM, buf).astype(BF16)
        vn = jnp.concatenate([vn_ref[:, lanes], pad], axis=0).astype(BF16)
        z_buf = [s_buf + tb_ref[a, rows, :] for a in range(npat)]
        z_new = [s_new + tn_ref[a, rows, 0:2 * s_len] for a in range(npat)]
        top = functools.reduce(jnp.maximum, [jnp.max(z, axis=1, keepdims=True) for z in z_buf + z_new])
        w_buf = functools.reduce(jnp.add, [jnp.exp(z - top) for z in z_buf])
        w_new = functools.reduce(jnp.add, [jnp.exp(z - top) for z in z_new])
        den = jnp.sum(w_buf, axis=1, keepdims=True) + jnp.sum(w_new, axis=1, keepdims=True)
        o = lax.dot_general(w_buf.astype(BF16), vt, _NT, preferred_element_type=F32)
        o = (o + jnp.dot(w_new.astype(BF16), vn, preferred_element_type=F32)) / den
        o_ref[:, lanes] = jnp.where(is_a, o[0:s_len], o[s_len:2 * s_len])


def _attention_sample_specs(h, kt, vt, tb, tn, unit_of):
    bsz, s_len, _ = h.shape
    heads, _, buf = kt.shape[1:]
    width = heads * HEAD_DIM
    gw = SAMPLE_PAIRS * LANES
    per_w = width // gw
    trows = SAMPLE_PAIRS * 2 * s_len
    where = lambda *g: divmod(unit_of(*g), per_w)
    new = lambda c: pl.BlockSpec((None, s_len, gw), lambda *g: (where(*g)[0], 0, c * per_w + where(*g)[1]))
    cache = pl.BlockSpec((None, 2 * SAMPLE_PAIRS, HEAD_DIM, buf), lambda *g: (where(*g)[0], where(*g)[1], 0, 0))
    specs = [new(0), new(1), new(2), cache, cache,
             pl.BlockSpec((tb.shape[0], trows, buf), lambda *g: (0, where(*g)[1], 0)),
             pl.BlockSpec((tn.shape[0], trows, LANES), lambda *g: (0, where(*g)[1], 0))]
    out_spec = pl.BlockSpec((None, s_len, gw), lambda *g: (where(*g)[0], 0, where(*g)[1]))
    return specs, out_spec, jax.ShapeDtypeStruct((bsz, s_len, width), F32)


def _ssm_sample_kernel(s_len, u_ref, b0_ref, c0_ref, coef_ref, d_ref, x0r_ref, x0i_ref,
                       y_ref, sre_ref, sim_ref, bu_ref, xs_ref):
    bsz = x0r_ref.shape[0]
    bu = jnp.dot(u_ref[...].astype(BF16), b0_ref[...], preferred_element_type=F32)
    slabs = bu_ref.shape[0]
    for c in range(slabs):
        bu_ref[c] = bu[:, c * LANES:(c + 1) * LANES]
    a_re = coef_ref[0:1, 0:STATE_HALF]
    a_im = coef_ref[0:1, STATE_HALF:2 * STATE_HALF]
    half = slabs // 2
    xr, xi = x0r_ref[...], x0i_ref[...]
    for s in range(s_len):
        step = lambda c: bu_ref[c, pl.ds(s, bsz, stride=s_len), :]
        br = jnp.concatenate([step(c) for c in range(half)], axis=1)
        bi = jnp.concatenate([step(c) for c in range(half, slabs)], axis=1)
        xr, xi = a_re * xr - a_im * xi + br, a_re * xi + a_im * xr + bi
        xs_ref[s * bsz:(s + 1) * bsz, :] = jnp.concatenate([xr, xi], axis=1)
    sre_ref[...] = xr
    sim_ref[...] = xi
    y = jnp.dot(xs_ref[...].astype(BF16), c0_ref[...], preferred_element_type=F32)
    d = d_ref[...]
    for s in range(s_len):
        y_ref[pl.ds(s, bsz, stride=s_len), :] = (y[s * bsz:(s + 1) * bsz, :]
                                                 + d * u_ref[pl.ds(s, bsz, stride=s_len), :])


def _ssm_sample(h, ws, c0, coef, d_skip, x0_re, x0_im, s_len):
    rows = h.shape[0]
    bsz = rows // s_len
    tiles = ws.shape[0]
    sh = STATE_HALF
    u_col0 = 3 * (h.shape[1] // 4) // LANES
    st_spec = pl.BlockSpec((bsz, sh), lambda t: (0, t))
    state = jax.ShapeDtypeStruct((bsz, tiles * sh), F32)
    return pl.pallas_call(
        functools.partial(_ssm_sample_kernel, s_len),
        grid=(tiles,),
        in_specs=[pl.BlockSpec((rows, LANES), lambda t: (0, u_col0 + t)),
                  pl.BlockSpec((None, LANES, 2 * sh), lambda t: (t, PLANES - 1, 0)),
                  pl.BlockSpec((None, 2 * sh, LANES), lambda t: (t, 0, 0)),
                  pl.BlockSpec((None, SUBLANES, 2 * sh), lambda t: (t, 0, 0)),
                  pl.BlockSpec((1, LANES), lambda t: (0, t)),
                  st_spec, st_spec],
        out_specs=[pl.BlockSpec((rows, LANES), lambda t: (0, t)), st_spec, st_spec],
        out_shape=[jax.ShapeDtypeStruct((rows, tiles * LANES), F32), state, state],
        scratch_shapes=[pltpu.VMEM((2 * sh // LANES, rows, LANES), F32),
                        pltpu.VMEM((rows, 2 * sh), F32)],
        compiler_params=_cparams(("arbitrary",)),
        name="ssm_sample",
    )(h, ws, c0, coef, d_skip, x0_re, x0_im)


def kernel(x_prompt, x_sample, cache_k, cache_v, state_ssm_re, state_ssm_im, w_in, w_out, b_out, rel_bias,
           lam_re, lam_im, log_dt, b_re, b_im, c_re, c_im, d_skip, w_glu, b_glu, ln_g, ln_b):
    depth = w_in.shape[0]
    assert depth == 1, "one layer per step"
    bsz, seq, d_model = x_prompt.shape
    dbsz, s_len, _ = x_sample.shape
    buf, heads = cache_k.shape[2], cache_k.shape[3]
    width = heads * HEAD_DIM
    groups, nstate = lam_re.shape[1], lam_re.shape[2]
    keep = min(MAX_DISTANCE, seq)
    assert seq % (PLANES * BLK) == 0 and seq >= 2 * PLANES * BLK and keep % ROW_TILE == 0
    assert buf == KPER * max(DILATIONS) and s_len <= min(DILATIONS[:-1])
    assert nstate == SSM_STATE and width == groups * SSM_CH and heads % (2 * SAMPLE_PAIRS) == 0
    alpha = (2 * depth) ** 0.25
    npat = len(DILATIONS)

    w_bf = w_in[0].astype(BF16)
    row = lambda v: v.reshape(1, -1)
    weights = (w_glu[0].astype(BF16), row(b_glu[0]), w_out[0].astype(BF16), row(b_out[0]),
               row(ln_g[0]), row(ln_b[0]))
    d_row = row(d_skip[0])

    rbt = rel_bias.T
    ptab = _bias_tables(jnp.asarray(_prompt_bucket_tables().reshape(1, -1)), rbt, 8192, LOG2E)
    ptab = ptab.reshape(heads, 2 * npat, BLK, 2 * BLK)
    key_w = buf + 2 * LANES
    stab = _bias_tables(jnp.asarray(_sample_bucket_tables(buf, s_len, key_w)), rbt, key_w)
    stab = jnp.transpose(stab.reshape(heads, npat, s_len, key_w), (1, 0, 2, 3)).reshape(npat, heads * s_len, key_w)
    stab_buf, stab_new = stab[:, :, :buf], stab[:, :, buf:buf + LANES]
    ws, wx, m_intra, c0, coef = _ssm_prep(lam_re[0], lam_im[0], log_dt[0], b_re[0], b_im[0], c_re[0], c_im[0])

    xs = x_sample.reshape(dbsz * s_len, d_model)
    hs = _inproj_sample(xs, w_bf)
    pos_minor = lambda c: jnp.transpose(c[0], (0, 2, 3, 1))
    sample = (hs.reshape(dbsz, s_len, -1), pos_minor(cache_k), pos_minor(cache_v), stab_buf, stab_new)

    q, k, v, u, kl_t, vl_t, attn_s = _inproj_prompt(x_prompt, w_bf, keep, sample)
    attn = _attention_prompt(q, k, v, ptab)
    y, sre_p, sim_p = _ssm_prompt(u, ws, wx, m_intra, coef, d_row)
    y_prompt = _epilogue(alpha, "epilogue_prompt", attn, y, x_prompt, w_bf, weights, ROW_TILE)
    last = lambda t: jnp.transpose(t.reshape(bsz, heads, HEAD_DIM, keep), (0, 3, 1, 2))[None]
    st_shape = (1, bsz, groups, nstate)

    y_s, sre_s, sim_s = _ssm_sample(hs, ws, c0, coef, d_row,
                                    state_ssm_re[0].astype(F32).reshape(dbsz, groups * nstate),
                                    state_ssm_im[0].astype(F32).reshape(dbsz, groups * nstate), s_len)
    y_sample = _epilogue(alpha, "epilogue_sample", attn_s.reshape(dbsz * s_len, width), y_s, xs, w_bf, weights,
                         dbsz * s_len)
    new_shape = (1, dbsz, s_len, heads, HEAD_DIM)
    sst_shape = (1, dbsz, groups, nstate)
    return (y_prompt, y_sample.reshape(dbsz, s_len, d_model), last(kl_t), last(vl_t),
            sre_p.reshape(st_shape), sim_p.reshape(st_shape),
            hs[:, width:2 * width].reshape(new_shape), hs[:, 2 * width:3 * width].reshape(new_shape),
            sre_s.reshape(sst_shape), sim_s.reshape(sst_shape))
```

```python
import functools
import math

import jax
import jax.numpy as jnp
import numpy as np
from jax import lax
from jax.experimental import pallas as pl
from jax.experimental.pallas import tpu as pltpu

F32 = jnp.float32
BF16 = jnp.bfloat16

HEAD_DIM = 64
SSM_CH = 16
SSM_STATE = 64
NUM_BUCKETS = 32
MAX_DISTANCE = 2048
KPER = 128
BLK = 128
DILATIONS = (16, 4, 1)
LN_EPS = 1e-5
NEG = -1e30
LOG2E = math.log2(math.e)

LANES = 128
SUBLANES = 8
PLANES = 16
GROUPS_PER_TILE = LANES // SSM_CH
STATE_HALF = GROUPS_PER_TILE * SSM_STATE
VMEM_LIMIT = 56 * 1024 * 1024
ROW_TILE = 512
ONCE = pl.Buffered(1)


def _cparams(sem, vmem=VMEM_LIMIT):
    return pltpu.CompilerParams(dimension_semantics=sem, vmem_limit_bytes=vmem)


def _bucket_np(dist):
    exact = NUM_BUCKETS // 2
    d_f = np.maximum(dist, 1).astype(np.float32)
    large = exact + (np.log(d_f / np.float32(exact)) / np.float32(math.log(MAX_DISTANCE / exact))
                     * np.float32(NUM_BUCKETS - exact)).astype(np.int32)
    large = np.minimum(large, NUM_BUCKETS - 1)
    return np.where(dist < exact, dist, large).astype(np.int32)


def _prompt_rel(dil, first):
    i = np.arange(BLK)[:, None]
    j = np.arange(2 * BLK)[None, :]
    npl = PLANES // dil
    qrows = BLK // npl
    pq, ml = i // qrows, i % qrows
    pk, jl = j // (2 * qrows), j % (2 * qrows)
    back = 0 if first else qrows
    return npl * (ml - jl + back) + (pq - pk)


def _prompt_bucket_tables():
    tabs = []
    for dil in DILATIONS:
        for first in (False, True):
            rel = _prompt_rel(dil, first)
            valid = (rel >= 0) & (rel <= KPER)
            tabs.append(np.where(valid, _bucket_np(np.clip(rel, 0, KPER) * dil), NUM_BUCKETS))
    return np.stack(tabs).reshape(len(DILATIONS) * 2, BLK * 2 * BLK).astype(np.int32)


def _sample_bucket_tables(buf, s_len, width):
    pos = np.arange(buf + s_len)
    tabs = np.full((len(DILATIONS), s_len, width), NUM_BUCKETS, np.int32)
    for a, dil in enumerate(DILATIONS):
        for s in range(s_len):
            dist = buf + s - pos
            valid = (dist >= 0) & (dist % dil == 0) & (dist // dil <= KPER)
            tabs[a, s, :len(pos)] = np.where(valid, _bucket_np(np.maximum(dist, 0)), NUM_BUCKETS)
    return tabs.reshape(1, -1)


def _bias_kernel(scale, idx_ref, rbt_ref, o_ref):
    idx = idx_ref[...]
    onehot = (lax.broadcasted_iota(jnp.int32, (NUM_BUCKETS, idx.shape[1]), 0) == idx).astype(BF16)
    rb = rbt_ref[...]
    n_heads = rb.shape[0]
    hi = rb.astype(BF16)
    lo = (rb - hi.astype(F32)).astype(BF16)
    res = jnp.dot(jnp.concatenate([hi, lo], axis=0), onehot, preferred_element_type=F32)
    tab = res[:n_heads] + res[n_heads:]
    o_ref[...] = jnp.where(idx < NUM_BUCKETS, tab * scale, NEG)


def _bias_tables(idx, rel_bias_t, chunk, scale=1.0):
    n_heads = rel_bias_t.shape[0]
    total = idx.shape[1]
    return pl.pallas_call(
        functools.partial(_bias_kernel, scale),
        grid=(total // chunk,),
        in_specs=[pl.BlockSpec((1, chunk), lambda c: (0, c)),
                  pl.BlockSpec((n_heads, NUM_BUCKETS), lambda c: (0, 0))],
        out_specs=pl.BlockSpec((n_heads, chunk), lambda c: (0, c)),
        out_shape=jax.ShapeDtypeStruct((n_heads, total), F32),
        compiler_params=_cparams(("arbitrary",)),
        name="bias_tables",
    )(idx, rel_bias_t)


def _discretize(lr, li, ldt):
    lr = jnp.minimum(lr, -1e-4)
    dt = jnp.exp(ldt)
    mag = jnp.exp(lr * dt)
    ab_re, ab_im = mag * jnp.cos(li * dt), mag * jnp.sin(li * dt)
    den = lr * lr + li * li
    inv_re, inv_im = lr / den, -li / den
    n_re, n_im = ab_re - 1.0, ab_im
    cf_re = n_re * inv_re - n_im * inv_im
    cf_im = n_re * inv_im + n_im * inv_re
    return ab_re, ab_im, cf_re, cf_im


def _ssm_prep_kernel(lam_row_ref, lam_col_ref, bt_re_ref, bt_im_ref, ct_re_ref, ct_im_ref,
                     ws_ref, wx_ref, m_ref, c0_ref, coef_ref):
    row = lam_row_ref[0]
    ab_re, ab_im, cf_re, cf_im = _discretize(row[0:1], row[1:2], row[2:3])
    bt_re, bt_im = bt_re_ref[0], bt_im_ref[0]
    bb_re = cf_re * bt_re - cf_im * bt_im
    bb_im = cf_re * bt_im + cf_im * bt_re
    col = lam_col_ref[0]
    abc_re, abc_im, _, _ = _discretize(col[:, 0:1], col[:, 1:2], col[:, 2:3])
    ct_re, ct_im = ct_re_ref[0], ct_im_ref[0]
    c0 = jnp.concatenate([ct_re, -ct_im], axis=0)
    c0_ref[0] = c0.astype(BF16)

    pr, pi = jnp.ones_like(ab_re), jnp.zeros_like(ab_re)
    qr, qi = abc_re, abc_im
    c0b = c0.astype(BF16)
    for lag in range(PLANES):
        w = jnp.concatenate([pr * bb_re - pi * bb_im, pr * bb_im + pi * bb_re], axis=1).astype(BF16)
        s = PLANES - 1 - lag
        ws_ref[0, s * LANES:(s + 1) * LANES, :] = w
        m_ref[0, lag * LANES:(lag + 1) * LANES, :] = jnp.dot(w, c0b, preferred_element_type=F32).astype(BF16)
        wx_ref[0, 0:STATE_HALF, lag * LANES:(lag + 1) * LANES] = (ct_re * qr - ct_im * qi).astype(BF16)
        wx_ref[0, STATE_HALF:2 * STATE_HALF, lag * LANES:(lag + 1) * LANES] = (-(ct_re * qi + ct_im * qr)).astype(BF16)
        pr, pi = pr * ab_re - pi * ab_im, pr * ab_im + pi * ab_re
        qr, qi = qr * abc_re - qi * abc_im, qr * abc_im + qi * abc_re
    coef_ref[0] = jnp.concatenate([
        jnp.concatenate([ab_re, ab_im], axis=1),
        jnp.concatenate([pr, pi], axis=1),
        jnp.zeros((SUBLANES - 2, 2 * STATE_HALF), F32)], axis=0)


def _ssm_prep(lam_re, lam_im, log_dt, b_re, b_im, c_re, c_im):
    groups, n = lam_re.shape
    tiles = groups // GROUPS_PER_TILE
    gpt = GROUPS_PER_TILE
    eye = jnp.eye(gpt, dtype=F32)

    def rows(v):
        return v.reshape(tiles, gpt * n)

    ldt = jnp.broadcast_to(log_dt[:, None], (groups, n))
    lam_row = jnp.stack([rows(lam_re), rows(lam_im), rows(ldt)], axis=1)
    lam_col = jnp.transpose(lam_row, (0, 2, 1))

    def bt(b):
        b = jnp.transpose(b.reshape(tiles, gpt, n, SSM_CH), (0, 1, 3, 2))
        return (b[:, :, :, None, :] * eye[None, :, None, :, None]).reshape(tiles, gpt * SSM_CH, gpt * n)

    def ct(c):
        c = jnp.transpose(c.reshape(tiles, gpt, SSM_CH, n), (0, 1, 3, 2))
        return (c[:, :, :, None, :] * eye[None, :, None, :, None]).reshape(tiles, gpt * n, gpt * SSM_CH)

    sh, ln = STATE_HALF, LANES
    tile3 = lambda a, b: pl.BlockSpec((1, a, b), lambda t: (t, 0, 0))
    return pl.pallas_call(
        _ssm_prep_kernel,
        grid=(tiles,),
        in_specs=[tile3(3, sh), tile3(sh, 3), tile3(ln, sh), tile3(ln, sh), tile3(sh, ln), tile3(sh, ln)],
        out_specs=[tile3(PLANES * ln, 2 * sh), tile3(2 * sh, PLANES * ln), tile3(PLANES * ln, ln),
                   tile3(2 * sh, ln), tile3(SUBLANES, 2 * sh)],
        out_shape=[jax.ShapeDtypeStruct((tiles, PLANES * ln, 2 * sh), BF16),
                   jax.ShapeDtypeStruct((tiles, 2 * sh, PLANES * ln), BF16),
                   jax.ShapeDtypeStruct((tiles, PLANES * ln, ln), BF16),
                   jax.ShapeDtypeStruct((tiles, 2 * sh, ln), BF16),
                   jax.ShapeDtypeStruct((tiles, SUBLANES, 2 * sh), F32)],
        compiler_params=_cparams(("arbitrary",)),
        name="ssm_prep",
    )(lam_row, lam_col, bt(b_re), bt(b_im), ct(c_re), ct(c_im))


def _inproj_kernel(first_keep, scale, x_ref, perm_ref, wqkv_ref, wu_ref, *rest):
    sample_in, (q_ref, k_ref, v_ref, u_ref, kl_ref, vl_ref, sample_out, slab_ref) = rest[:7], rest[7:]
    slabs = slab_ref.shape[0]
    prow = q_ref.shape[1]
    width = u_ref.shape[-1]
    xp = jnp.dot(perm_ref[...], x_ref[...].astype(BF16), preferred_element_type=F32).astype(BF16)

    def to_planes(val, out_ref):
        for r in range(PLANES):
            out_ref[r] = val[r * prow:(r + 1) * prow].astype(out_ref.dtype)

    sq_ref, skn_ref, svn_ref, skt_ref, svt_ref, stb_ref, stn_ref = sample_in
    to_planes(jnp.dot(xp, wqkv_ref[:, 0:width], preferred_element_type=F32) * scale, q_ref)
    sample_scores = _sample_scores(sq_ref, skn_ref, skt_ref)
    k = jnp.dot(xp, wqkv_ref[:, width:2 * width], preferred_element_type=F32)
    to_planes(k, k_ref)
    v = jnp.dot(xp, wqkv_ref[:, 2 * width:3 * width], preferred_element_type=F32)
    to_planes(v, v_ref)
    _sample_outputs(sample_scores, svn_ref, svt_ref, stb_ref, stn_ref, sample_out)
    to_planes(jnp.dot(xp, wu_ref[...], preferred_element_type=F32), u_ref)

    @pl.when(pl.program_id(1) >= first_keep)
    def _():
        def token_order_t(val):
            for r in range(PLANES):
                for c in range(slabs):
                    slab_ref[c, pl.ds(r, prow, stride=PLANES), :] = val[r * prow:(r + 1) * prow,
                                                                        c * LANES:(c + 1) * LANES]
            return jnp.concatenate([slab_ref[c] for c in range(slabs)], axis=1).T
        kl_ref[...] = token_order_t(k)
        vl_ref[...] = token_order_t(v)


def _inproj_prompt(x, w_bf, keep, sample):
    bsz, seq, d_model = x.shape
    width = w_bf.shape[1] // 6
    rows = seq // PLANES
    tile = ROW_TILE // 2
    prow = tile // PLANES
    first_keep = (seq - keep) // tile
    steps = seq // tile
    sample_specs, sample_out_spec, sample_out = _attention_sample_specs(*sample, lambda b, i: b * steps + i)
    assert bsz * steps == sample[0].shape[0] * (width // (SAMPLE_PAIRS * LANES)), "one sample unit per grid step"
    plane = lambda dt: jax.ShapeDtypeStruct((bsz, PLANES, rows, width), dt)
    plane_spec = pl.BlockSpec((None, PLANES, prow, width), lambda b, i: (b, 0, i, 0))
    last = jax.ShapeDtypeStruct((bsz, width, keep), F32)
    last_spec = pl.BlockSpec((None, width, tile), lambda b, i: (b, 0, jnp.maximum(i - first_keep, 0)))
    perm = np.zeros((tile, tile), np.float32)
    perm[np.arange(tile), PLANES * (np.arange(tile) % prow) + np.arange(tile) // prow] = 1.0
    return pl.pallas_call(
        functools.partial(_inproj_kernel, first_keep, HEAD_DIM ** -0.5 * LOG2E),
        grid=(bsz, steps),
        in_specs=[pl.BlockSpec((None, tile, d_model), lambda b, i: (b, i, 0)),
                  pl.BlockSpec((tile, tile), lambda b, i: (0, 0), pipeline_mode=ONCE),
                  pl.BlockSpec((d_model, 3 * width), lambda b, i: (0, 0), pipeline_mode=ONCE),
                  pl.BlockSpec((d_model, width), lambda b, i: (0, 4), pipeline_mode=ONCE)] + sample_specs,
        out_specs=[plane_spec, plane_spec, plane_spec, plane_spec, last_spec, last_spec, sample_out_spec],
        out_shape=[plane(F32), plane(F32), plane(F32), plane(BF16), last, last, sample_out],
        scratch_shapes=[pltpu.VMEM((width // LANES, tile, LANES), F32)],
        compiler_params=_cparams(("arbitrary", "arbitrary")),
        name="inproj_prompt",
    )(x, jnp.asarray(perm, BF16), w_bf, w_bf, sample[0], sample[0], sample[0], *sample[1:])


ATTN_UNROLL = 2


def _attn_kernel(q_ref, k_ref, v_ref, bias_ref, o_ref, acc_ref, m_ref, l_ref, s_ref, p_ref, mn_ref):
    sup = pl.program_id(2)
    is_a = lax.broadcasted_iota(jnp.int32, (BLK, LANES), 1) < HEAD_DIM
    base = pl.multiple_of(sup * BLK, BLK)

    def raw_scores(q, k):
        kb = k.astype(BF16)
        out = []
        for head in range(2):
            qm = jnp.where(is_a if head == 0 else jnp.logical_not(is_a), q, 0.0).astype(BF16)
            out.append(lax.dot_general(qm, kb, (((1,), (1,)), ((), ())), preferred_element_type=F32))
        return out

    def both(a, b):
        return jnp.where(is_a, a, b)

    def weights(s, table, m_old):
        m_new, p = [], []
        for h in range(2):
            sh = s[h] + bias_ref[h, table]
            mb = jnp.broadcast_to(jnp.max(sh, axis=1, keepdims=True), (BLK, LANES))
            mh = mb if m_old is None else jnp.maximum(m_old[h], mb)
            m_new.append(mh)
            p.append(jnp.exp2(sh - jnp.concatenate([mh, mh], axis=1)).astype(BF16))
        return m_new, p

    ones = jnp.ones((2 * BLK, LANES), BF16)

    def combine(m_new, p, v, state):
        vext = jnp.concatenate([v.astype(BF16), ones], axis=1)
        res = [jnp.dot(p[h], vext, preferred_element_type=F32) for h in range(2)]
        acc_new = both(res[0][:, :LANES], res[1][:, :LANES])
        l_new = [res[h][:, LANES:] for h in range(2)]
        if state is not None:
            alpha = [jnp.exp2(state[0][h] - m_new[h]) for h in range(2)]
            l_new = [alpha[h] * state[1][h] + l_new[h] for h in range(2)]
            acc_new = both(alpha[0], alpha[1]) * state[2] + acc_new
        return m_new, l_new, acc_new

    cat = lambda parts: parts[0] if len(parts) == 1 else jnp.concatenate(parts, axis=0)

    def gather(ref, planes, start, size, *lead):
        return cat([ref[(*lead, pln, pl.ds(start, size), slice(None))] for pln in planes])

    def run_pattern(a, dil):
        npl = PLANES // dil
        qrows = BLK // npl
        per_res = BLK // qrows
        groups = PLANES // ATTN_UNROLL

        def geometry(grp, un):
            res, sub = divmod(grp * ATTN_UNROLL + un, per_res)
            off = sub * qrows
            kstart = pl.multiple_of(jnp.maximum(base + off - qrows, 0), qrows)
            table = 2 * a + (jnp.where(sup == 0, 1, 0) if sub == 0 else 0)
            return [res + dil * i for i in range(npl)], off, kstart, table

        def stage_scores(grp):
            stores = []
            for un in range(ATTN_UNROLL):
                planes, off, kstart, _ = geometry(grp, un)
                s = raw_scores(gather(q_ref, planes, base + off, qrows), gather(k_ref, planes, kstart, 2 * qrows))
                stores += [(s_ref, (grp % 2, un, h), s[h]) for h in range(2)]
            return stores

        def stage_softmax(grp):
            stores = []
            for un in range(ATTN_UNROLL):
                planes, off, _, table = geometry(grp, un)
                m_old = None if a == 0 else [gather(m_ref, planes, off, qrows, h) for h in range(2)]
                m_new, p = weights([s_ref[grp % 2, un, h] for h in range(2)], table, m_old)
                for h in range(2):
                    stores += [(p_ref, (grp % 2, un, h), p[h]), (mn_ref, (grp % 2, un, h), m_new[h])]
            return stores

        def stage_values(grp):
            stores = []
            for un in range(ATTN_UNROLL):
                planes, off, kstart, _ = geometry(grp, un)
                old = None if a == 0 else ([gather(m_ref, planes, off, qrows, h) for h in range(2)],
                                           [gather(l_ref, planes, off, qrows, h) for h in range(2)],
                                           gather(acc_ref, planes, off, qrows))
                m_new, l_new, acc_new = combine([mn_ref[grp % 2, un, h] for h in range(2)],
                                                [p_ref[grp % 2, un, h] for h in range(2)],
                                                gather(v_ref, planes, kstart, 2 * qrows), old)
                for i, pln in enumerate(planes):
                    part = slice(i * qrows, (i + 1) * qrows)
                    rows = (pln, pl.ds(off, qrows), slice(None))
                    for h in range(2):
                        stores += [(m_ref, (h,) + rows, m_new[h][part]), (l_ref, (h,) + rows, l_new[h][part])]
                    stores.append((acc_ref, rows, acc_new[part]))
            return stores

        for step in range(groups + 2):
            stores = []
            if 0 <= step - 2 < groups:
                stores += stage_values(step - 2)
            if 0 <= step - 1 < groups:
                stores += stage_softmax(step - 1)
            if step < groups:
                stores += stage_scores(step)
            for ref, idx, val in stores:
                ref[idx] = val

    for a, dil in enumerate(DILATIONS):
        run_pattern(a, dil)
    for pln in range(PLANES):
        o_ref[pl.ds(pln, BLK, stride=PLANES), :] = acc_ref[pln] / both(l_ref[0, pln], l_ref[1, pln])


def _attention_prompt(q, k, v, bias):
    bsz, _, rows, width = q.shape
    pairs = width // LANES
    ntab = bias.shape[1]
    qkv_spec = pl.BlockSpec((None, PLANES, rows, LANES), lambda b, h, s: (b, 0, 0, h))
    return pl.pallas_call(
        _attn_kernel,
        grid=(bsz, pairs, rows // BLK),
        in_specs=[qkv_spec, qkv_spec, qkv_spec,
                  pl.BlockSpec((2, ntab, BLK, 2 * BLK), lambda b, h, s: (h, 0, 0, 0))],
        out_specs=pl.BlockSpec((None, PLANES * BLK, LANES), lambda b, h, s: (b, s, h)),
        out_shape=jax.ShapeDtypeStruct((bsz, PLANES * rows, width), F32),
        scratch_shapes=[pltpu.VMEM((PLANES, BLK, LANES), F32),
                        pltpu.VMEM((2, PLANES, BLK, LANES), F32),
                        pltpu.VMEM((2, PLANES, BLK, LANES), F32),
                        pltpu.VMEM((2, ATTN_UNROLL, 2, BLK, 2 * BLK), F32),
                        pltpu.VMEM((2, ATTN_UNROLL, 2, BLK, 2 * BLK), BF16),
                        pltpu.VMEM((2, ATTN_UNROLL, 2, BLK, LANES), F32)],
        compiler_params=_cparams(("arbitrary", "arbitrary", "arbitrary")),
        name="attention_prompt",
    )(q, k, v, bias)


SSM_COLS = 4


def _ssm_kernel(u_ref, ws_ref, wx_ref, mlag_ref, coef_ref, d_ref, y_ref, sre_ref, sim_ref,
                ucat_ref, s_ref, x_ref, m_ref):
    chunks = u_ref.shape[1]

    @pl.when(pl.program_id(1) == 0)
    def _():
        zero_blk = jnp.zeros((LANES, LANES), BF16)
        for s in range(PLANES):
            for t in range(PLANES):
                blk = mlag_ref[(t - s) * LANES:(t - s + 1) * LANES, :] if t >= s else zero_blk
                m_ref[s * LANES:(s + 1) * LANES, t * LANES:(t + 1) * LANES] = blk

    for s in range(PLANES):
        ucat_ref[:, s * LANES:(s + 1) * LANES] = u_ref[s]
    s_ref[...] = jnp.dot(ucat_ref[...], ws_ref[...], preferred_element_type=F32)
    a_re = coef_ref[1:2, 0:STATE_HALF]
    a_im = coef_ref[1:2, STATE_HALF:2 * STATE_HALF]

    def step(c, carry):
        xr, xi = carry
        x_ref[pl.ds(c, 1), 0:STATE_HALF] = xr
        x_ref[pl.ds(c, 1), STATE_HALF:2 * STATE_HALF] = xi
        sr = s_ref[pl.ds(c, 1), 0:STATE_HALF]
        si = s_ref[pl.ds(c, 1), STATE_HALF:2 * STATE_HALF]
        return a_re * xr - a_im * xi + sr, a_re * xi + a_im * xr + si

    zero = jnp.zeros((1, STATE_HALF), F32)
    xr, xi = lax.fori_loop(0, chunks, step, (zero, zero))
    sre_ref[...] = xr
    sim_ref[...] = xi

    xb = x_ref[...].astype(BF16)
    d = d_ref[...]
    for g in range(PLANES // SSM_COLS):
        cols = slice(g * SSM_COLS * LANES, (g + 1) * SSM_COLS * LANES)
        used = (g + 1) * SSM_COLS * LANES
        y = jnp.dot(ucat_ref[:, :used], m_ref[:used, cols], preferred_element_type=F32)
        y = y + jnp.dot(xb, wx_ref[:, cols], preferred_element_type=F32)
        for t in range(SSM_COLS):
            tok = g * SSM_COLS + t
            y_ref[pl.ds(tok, chunks, stride=PLANES), :] = (y[:, t * LANES:(t + 1) * LANES]
                                                          + d * u_ref[tok].astype(F32))


def _ssm_prompt(u, ws, wx, mlag, coef, d_skip):
    bsz, _, chunks, width = u.shape
    tiles = width // LANES
    sh = STATE_HALF
    state = jax.ShapeDtypeStruct((bsz, tiles, 1, sh), F32)
    state_spec = pl.BlockSpec((None, None, 1, sh), lambda t, b: (b, t, 0, 0))
    per_tile = lambda r, c: pl.BlockSpec((None, r, c), lambda t, b: (t, 0, 0))
    return pl.pallas_call(
        _ssm_kernel,
        grid=(tiles, bsz),
        in_specs=[pl.BlockSpec((None, PLANES, chunks, LANES), lambda t, b: (b, 0, 0, t)),
                  per_tile(PLANES * LANES, 2 * sh), per_tile(2 * sh, PLANES * LANES),
                  per_tile(PLANES * LANES, LANES), per_tile(SUBLANES, 2 * sh),
                  pl.BlockSpec((1, LANES), lambda t, b: (0, t))],
        out_specs=[pl.BlockSpec((None, PLANES * chunks, LANES), lambda t, b: (b, 0, t)),
                   state_spec, state_spec],
        out_shape=[jax.ShapeDtypeStruct((bsz, PLANES * chunks, width), F32), state, state],
        scratch_shapes=[pltpu.VMEM((chunks, PLANES * LANES), BF16),
                        pltpu.VMEM((chunks, 2 * sh), F32),
                        pltpu.VMEM((chunks, 2 * sh), F32),
                        pltpu.VMEM((PLANES * LANES, PLANES * LANES), BF16)],
        compiler_params=_cparams(("arbitrary", "arbitrary")),
        name="ssm_prompt",
    )(u, ws, wx, mlag, coef, d_skip)


EPILOGUE_ROWS = 256


def _epilogue_kernel(alpha, attn_ref, y_ref, x_ref, wga_ref, wgs_ref, wglu_ref, bglu_ref, wo_ref, bo_ref,
                     g_ref, b_ref, o_ref):
    d_attn = attn_ref.shape[-1]
    rows = x_ref.shape[0]
    part = min(rows, EPILOGUE_ROWS)
    for r0 in range(0, rows, part):
        rs = slice(r0, r0 + part)
        x = x_ref[rs, :]
        xb = x.astype(BF16)
        g_attn = jnp.dot(xb, wga_ref[...], preferred_element_type=F32)
        br_a = (attn_ref[rs, :] * jax.nn.silu(g_attn)).astype(BF16)
        g_ssm = jnp.dot(xb, wgs_ref[...], preferred_element_type=F32)
        z = jax.nn.gelu(y_ref[rs, :])
        gate = jax.nn.sigmoid(jnp.dot(z.astype(BF16), wglu_ref[...], preferred_element_type=F32) + bglu_ref[...])
        br_s = (z * gate * jax.nn.silu(g_ssm)).astype(BF16)
        mix = (jnp.dot(br_a, wo_ref[0:d_attn, :], preferred_element_type=F32)
               + jnp.dot(br_s, wo_ref[d_attn:, :], preferred_element_type=F32) + bo_ref[...])
        t = alpha * x + mix
        mu = jnp.mean(t, axis=-1, keepdims=True)
        var = jnp.mean(jnp.square(t - mu), axis=-1, keepdims=True)
        o_ref[rs, :] = (t - mu) * lax.rsqrt(var + LN_EPS) * g_ref[...] + b_ref[...]


def _epilogue(alpha, name, attn, y, x, w_bf, weights, tile):
    lead = attn.shape[:-2]
    rows, width = attn.shape[-2:]
    d_model = x.shape[-1]
    grid = lead + (rows // tile,)
    none = (None,) * len(lead)
    tiled = lambda w: pl.BlockSpec(none + (tile, w), lambda *g: g + (0,))
    const = lambda shape, col=0: pl.BlockSpec(shape, lambda *g: (0, col), pipeline_mode=ONCE)
    return pl.pallas_call(
        functools.partial(_epilogue_kernel, alpha),
        grid=grid,
        in_specs=[tiled(width), tiled(width), tiled(d_model),
                  const((d_model, width), 3), const((d_model, width), 5),
                  const((width, width)), const((1, width)), const((2 * width, d_model)), const((1, d_model)),
                  const((1, d_model)), const((1, d_model))],
        out_specs=tiled(d_model),
        out_shape=jax.ShapeDtypeStruct(x.shape, F32),
        compiler_params=_cparams(("arbitrary",) * len(grid)),
        name=name,
    )(attn, y, x, w_bf, w_bf, *weights)


def _inproj_sample_kernel(scale, x_ref, w_ref, o_ref):
    acc = jnp.dot(x_ref[...].astype(BF16), w_ref[...], preferred_element_type=F32)
    o_ref[...] = acc * jnp.where(pl.program_id(0) == 0, scale, 1.0)


def _inproj_sample(x, w_bf):
    rows, d_model = x.shape
    width = w_bf.shape[1] // 6
    return pl.pallas_call(
        functools.partial(_inproj_sample_kernel, HEAD_DIM ** -0.5),
        grid=(4,),
        in_specs=[pl.BlockSpec((rows, d_model), lambda j: (0, 0)),
                  pl.BlockSpec((d_model, width), lambda j: (0, j + j // 3))],
        out_specs=pl.BlockSpec((rows, width), lambda j: (0, j)),
        out_shape=jax.ShapeDtypeStruct((rows, 4 * width), F32),
        compiler_params=_cparams(("arbitrary",)),
        name="inproj_sample",
    )(x, w_bf)


SAMPLE_PAIRS = 4


_NT = (((1,), (1,)), ((), ()))


def _sample_scores(q_ref, kn_ref, kt_ref):
    s_len = q_ref.shape[0]
    buf = kt_ref.shape[2]
    is_a = lax.broadcasted_iota(jnp.int32, (s_len, LANES), 1) < HEAD_DIM
    out = []
    for pp in range(SAMPLE_PAIRS):
        lanes = slice(pp * LANES, (pp + 1) * LANES)
        qp = q_ref[:, lanes]
        q2 = jnp.concatenate([jnp.where(is_a, qp, 0.0), jnp.where(is_a, 0.0, qp)], axis=0)
        kt = kt_ref[2 * pp:2 * pp + 2].reshape(2 * HEAD_DIM, buf).astype(BF16)
        kn = kn_ref[:, lanes]
        s_new = [jnp.sum(q2 * kn[j:j + 1, :], axis=1, keepdims=True) for j in range(s_len)]
        out.append((jnp.dot(q2.astype(BF16), kt, preferred_element_type=F32), s_new))
    return out


def _sample_outputs(scores, vn_ref, vt_ref, tb_ref, tn_ref, o_ref):
    s_len = vn_ref.shape[0]
    buf = vt_ref.shape[2]
    npat = tb_ref.shape[0]
    is_a = lax.broadcasted_iota(jnp.int32, (s_len, LANES), 1) < HEAD_DIM
    for pp, (s_buf, s_new) in enumerate(scores):
        lanes = slice(pp * LANES, (pp + 1) * LANES)
        rows = slice(pp * 2 * s_len, (pp + 1) * 2 * s_len)
        vt = vt_ref[2 * pp:2 * pp + 2].reshape(2 * HEAD_DIM, buf).astype(BF16)
        vn = vn_ref[:, lanes]
        z_buf = [s_buf + tb_ref[a, rows, :] for a in range(npat)]
        z_new = [[s_new[j] + tn_ref[a, rows, j:j + 1] for a in range(npat)] for j in range(s_len)]
        top = functools.reduce(jnp.maximum, [jnp.max(z, axis=1, keepdims=True) for z in z_buf]
                               + [z for zs in z_new for z in zs])
        w_buf = functools.reduce(jnp.add, [jnp.exp(z - top) for z in z_buf])
        w_new = [functools.reduce(jnp.add, [jnp.exp(z - top) for z in zs]) for zs in z_new]
        den = jnp.sum(w_buf, axis=1, keepdims=True) + functools.reduce(jnp.add, w_new)
        o = lax.dot_general(vt, w_buf.astype(BF16), _NT, preferred_element_type=F32).T
        o = (o + functools.reduce(jnp.add, [w_new[j] * vn[j:j + 1, :] for j in range(s_len)])) / den
        o_ref[:, lanes] = jnp.where(is_a, o[0:s_len], o[s_len:2 * s_len])


def _attention_sample_specs(h, kt, vt, tb, tn, unit_of):
    bsz, s_len, _ = h.shape
    heads, _, buf = kt.shape[1:]
    width = heads * HEAD_DIM
    gw = SAMPLE_PAIRS * LANES
    per_w = width // gw
    trows = SAMPLE_PAIRS * 2 * s_len
    where = lambda *g: divmod(unit_of(*g), per_w)
    new = lambda c: pl.BlockSpec((None, s_len, gw), lambda *g: (where(*g)[0], 0, c * per_w + where(*g)[1]))
    cache = pl.BlockSpec((None, 2 * SAMPLE_PAIRS, HEAD_DIM, buf), lambda *g: (where(*g)[0], where(*g)[1], 0, 0))
    specs = [new(0), new(1), new(2), cache, cache,
             pl.BlockSpec((tb.shape[0], trows, buf), lambda *g: (0, where(*g)[1], 0)),
             pl.BlockSpec((tn.shape[0], trows, LANES), lambda *g: (0, where(*g)[1], 0))]
    out_spec = pl.BlockSpec((None, s_len, gw), lambda *g: (where(*g)[0], 0, where(*g)[1]))
    return specs, out_spec, jax.ShapeDtypeStruct((bsz, s_len, width), F32)


def _ssm_sample_kernel(s_len, u_ref, b0_ref, c0_ref, coef_ref, d_ref, x0r_ref, x0i_ref,
                       y_ref, sre_ref, sim_ref, bu_ref, xs_ref):
    bsz = x0r_ref.shape[0]
    bu = jnp.dot(u_ref[...].astype(BF16), b0_ref[...], preferred_element_type=F32)
    slabs = bu_ref.shape[0]
    for c in range(slabs):
        bu_ref[c] = bu[:, c * LANES:(c + 1) * LANES]
    a_re = coef_ref[0:1, 0:STATE_HALF]
    a_im = coef_ref[0:1, STATE_HALF:2 * STATE_HALF]
    half = slabs // 2
    xr, xi = x0r_ref[...], x0i_ref[...]
    for s in range(s_len):
        step = lambda c: bu_ref[c, pl.ds(s, bsz, stride=s_len), :]
        br = jnp.concatenate([step(c) for c in range(half)], axis=1)
        bi = jnp.concatenate([step(c) for c in range(half, slabs)], axis=1)
        xr, xi = a_re * xr - a_im * xi + br, a_re * xi + a_im * xr + bi
        xs_ref[s * bsz:(s + 1) * bsz, :] = jnp.concatenate([xr, xi], axis=1)
    sre_ref[...] = xr
    sim_ref[...] = xi
    y = jnp.dot(xs_ref[...].astype(BF16), c0_ref[...], preferred_element_type=F32)
    d = d_ref[...]
    for s in range(s_len):
        y_ref[pl.ds(s, bsz, stride=s_len), :] = (y[s * bsz:(s + 1) * bsz, :]
                                                 + d * u_ref[pl.ds(s, bsz, stride=s_len), :])


def _ssm_sample(h, ws, c0, coef, d_skip, x0_re, x0_im, s_len):
    rows = h.shape[0]
    bsz = rows // s_len
    tiles = ws.shape[0]
    sh = STATE_HALF
    u_col0 = 3 * (h.shape[1] // 4) // LANES
    st_spec = pl.BlockSpec((bsz, sh), lambda t: (0, t))
    state = jax.ShapeDtypeStruct((bsz, tiles * sh), F32)
    return pl.pallas_call(
        functools.partial(_ssm_sample_kernel, s_len),
        grid=(tiles,),
        in_specs=[pl.BlockSpec((rows, LANES), lambda t: (0, u_col0 + t)),
                  pl.BlockSpec((None, LANES, 2 * sh), lambda t: (t, PLANES - 1, 0)),
                  pl.BlockSpec((None, 2 * sh, LANES), lambda t: (t, 0, 0)),
                  pl.BlockSpec((None, SUBLANES, 2 * sh), lambda t: (t, 0, 0)),
                  pl.BlockSpec((1, LANES), lambda t: (0, t)),
                  st_spec, st_spec],
        out_specs=[pl.BlockSpec((rows, LANES), lambda t: (0, t)), st_spec, st_spec],
        out_shape=[jax.ShapeDtypeStruct((rows, tiles * LANES), F32), state, state],
        scratch_shapes=[pltpu.VMEM((2 * sh // LANES, rows, LANES), F32),
                        pltpu.VMEM((rows, 2 * sh), F32)],
        compiler_params=_cparams(("arbitrary",)),
        name="ssm_sample",
    )(h, ws, c0, coef, d_skip, x0_re, x0_im)


def kernel(x_prompt, x_sample, cache_k, cache_v, state_ssm_re, state_ssm_im, w_in, w_out, b_out, rel_bias,
           lam_re, lam_im, log_dt, b_re, b_im, c_re, c_im, d_skip, w_glu, b_glu, ln_g, ln_b):
    depth = w_in.shape[0]
    assert depth == 1, "one layer per step"
    bsz, seq, d_model = x_prompt.shape
    dbsz, s_len, _ = x_sample.shape
    buf, heads = cache_k.shape[2], cache_k.shape[3]
    width = heads * HEAD_DIM
    groups, nstate = lam_re.shape[1], lam_re.shape[2]
    keep = min(MAX_DISTANCE, seq)
    assert seq % (PLANES * BLK) == 0 and seq >= 2 * PLANES * BLK and keep % ROW_TILE == 0
    assert buf == KPER * max(DILATIONS) and s_len <= min(DILATIONS[:-1])
    assert nstate == SSM_STATE and width == groups * SSM_CH and heads % (2 * SAMPLE_PAIRS) == 0
    alpha = (2 * depth) ** 0.25
    npat = len(DILATIONS)

    w_bf = w_in[0].astype(BF16)
    row = lambda v: v.reshape(1, -1)
    weights = (w_glu[0].astype(BF16), row(b_glu[0]), w_out[0].astype(BF16), row(b_out[0]),
               row(ln_g[0]), row(ln_b[0]))
    d_row = row(d_skip[0])

    rbt = rel_bias.T
    ptab = _bias_tables(jnp.asarray(_prompt_bucket_tables().reshape(1, -1)), rbt, 8192, LOG2E)
    ptab = ptab.reshape(heads, 2 * npat, BLK, 2 * BLK)
    key_w = buf + 2 * LANES
    stab = _bias_tables(jnp.asarray(_sample_bucket_tables(buf, s_len, key_w)), rbt, key_w)
    stab = jnp.transpose(stab.reshape(heads, npat, s_len, key_w), (1, 0, 2, 3)).reshape(npat, heads * s_len, key_w)
    stab_buf, stab_new = stab[:, :, :buf], stab[:, :, buf:buf + LANES]
    ws, wx, m_intra, c0, coef = _ssm_prep(lam_re[0], lam_im[0], log_dt[0], b_re[0], b_im[0], c_re[0], c_im[0])

    xs = x_sample.reshape(dbsz * s_len, d_model)
    hs = _inproj_sample(xs, w_bf)
    pos_minor = lambda c: jnp.transpose(c[0], (0, 2, 3, 1))
    sample = (hs.reshape(dbsz, s_len, -1), pos_minor(cache_k), pos_minor(cache_v), stab_buf, stab_new)

    q, k, v, u, kl_t, vl_t, attn_s = _inproj_prompt(x_prompt, w_bf, keep, sample)
    attn = _attention_prompt(q, k, v, ptab)
    y, sre_p, sim_p = _ssm_prompt(u, ws, wx, m_intra, coef, d_row)
    y_prompt = _epilogue(alpha, "epilogue_prompt", attn, y, x_prompt, w_bf, weights, ROW_TILE // 2)
    last = lambda t: jnp.transpose(t.reshape(bsz, heads, HEAD_DIM, keep), (0, 3, 1, 2))[None]
    st_shape = (1, bsz, groups, nstate)

    y_s, sre_s, sim_s = _ssm_sample(hs, ws, c0, coef, d_row,
                                    state_ssm_re[0].astype(F32).reshape(dbsz, groups * nstate),
                                    state_ssm_im[0].astype(F32).reshape(dbsz, groups * nstate), s_len)
    y_sample = _epilogue(alpha, "epilogue_sample", attn_s.reshape(dbsz * s_len, width), y_s, xs, w_bf, weights,
                         dbsz * s_len)
    new_shape = (1, dbsz, s_len, heads, HEAD_DIM)
    sst_shape = (1, dbsz, groups, nstate)
    return (y_prompt, y_sample.reshape(dbsz, s_len, d_model), last(kl_t), last(vl_t),
            sre_p.reshape(st_shape), sim_p.reshape(st_shape),
            hs[:, width:2 * width].reshape(new_shape), hs[:, 2 * width:3 * width].reshape(new_shape),
            sre_s.reshape(sst_shape), sim_s.reshape(sst_shape))
```

```python
import functools
import math

import jax
import jax.numpy as jnp
import numpy as np
from jax import lax
from jax.experimental import pallas as pl
from jax.experimental.pallas import tpu as pltpu

F32 = jnp.float32
BF16 = jnp.bfloat16

HEAD_DIM = 64
SSM_CH = 16
SSM_STATE = 64
NUM_BUCKETS = 32
MAX_DISTANCE = 2048
KPER = 128
BLK = 128
DILATIONS = (16, 4, 1)
LN_EPS = 1e-5
NEG = -1e30
LOG2E = math.log2(math.e)

LANES = 128
SUBLANES = 8
PLANES = 16
GROUPS_PER_TILE = LANES // SSM_CH
STATE_HALF = GROUPS_PER_TILE * SSM_STATE
VMEM_LIMIT = 56 * 1024 * 1024
ROW_TILE = 512
ONCE = pl.Buffered(1)


def _cparams(sem, vmem=VMEM_LIMIT):
    return pltpu.CompilerParams(dimension_semantics=sem, vmem_limit_bytes=vmem)


def _bucket_np(dist):
    exact = NUM_BUCKETS // 2
    d_f = np.maximum(dist, 1).astype(np.float32)
    large = exact + (np.log(d_f / np.float32(exact)) / np.float32(math.log(MAX_DISTANCE / exact))
                     * np.float32(NUM_BUCKETS - exact)).astype(np.int32)
    large = np.minimum(large, NUM_BUCKETS - 1)
    return np.where(dist < exact, dist, large).astype(np.int32)


def _prompt_rel(dil):
    i = np.arange(BLK)[:, None]
    j = np.arange(2 * BLK)[None, :]
    npl = PLANES // dil
    qrows = BLK // npl
    pq, ml = i // qrows, i % qrows
    pk, jl = j // (2 * qrows), j % (2 * qrows)
    return npl * (ml - jl + qrows) + (pq - pk), np.broadcast_to(jl < qrows, (BLK, 2 * BLK))


def _prompt_bucket_tables():
    tabs = []
    for dil in DILATIONS:
        rel, earlier = _prompt_rel(dil)
        for first in (False, True):
            valid = (rel >= 0) & (rel <= KPER) & np.logical_not(earlier & first)
            tabs.append(np.where(valid, _bucket_np(np.clip(rel, 0, KPER) * dil), NUM_BUCKETS))
    return np.stack(tabs).reshape(len(DILATIONS) * 2, BLK * 2 * BLK).astype(np.int32)


def _sample_bucket_tables(buf, s_len, width):
    pos = np.arange(buf + s_len)
    tabs = np.full((len(DILATIONS), s_len, width), NUM_BUCKETS, np.int32)
    for a, dil in enumerate(DILATIONS):
        for s in range(s_len):
            dist = buf + s - pos
            valid = (dist >= 0) & (dist % dil == 0) & (dist // dil <= KPER)
            tabs[a, s, :len(pos)] = np.where(valid, _bucket_np(np.maximum(dist, 0)), NUM_BUCKETS)
    return tabs.reshape(1, -1)


def _bias_kernel(scale, idx_ref, rbt_ref, o_ref):
    idx = idx_ref[...]
    onehot = (lax.broadcasted_iota(jnp.int32, (NUM_BUCKETS, idx.shape[1]), 0) == idx).astype(BF16)
    rb = rbt_ref[...]
    n_heads = rb.shape[0]
    hi = rb.astype(BF16)
    lo = (rb - hi.astype(F32)).astype(BF16)
    res = jnp.dot(jnp.concatenate([hi, lo], axis=0), onehot, preferred_element_type=F32)
    tab = res[:n_heads] + res[n_heads:]
    o_ref[...] = jnp.where(idx < NUM_BUCKETS, tab * scale, NEG)


def _bias_tables(idx, rel_bias_t, chunk, scale=1.0):
    n_heads = rel_bias_t.shape[0]
    total = idx.shape[1]
    return pl.pallas_call(
        functools.partial(_bias_kernel, scale),
        grid=(total // chunk,),
        in_specs=[pl.BlockSpec((1, chunk), lambda c: (0, c)),
                  pl.BlockSpec((n_heads, NUM_BUCKETS), lambda c: (0, 0))],
        out_specs=pl.BlockSpec((n_heads, chunk), lambda c: (0, c)),
        out_shape=jax.ShapeDtypeStruct((n_heads, total), F32),
        compiler_params=_cparams(("arbitrary",)),
        name="bias_tables",
    )(idx, rel_bias_t)


def _discretize(lr, li, ldt):
    lr = jnp.minimum(lr, -1e-4)
    dt = jnp.exp(ldt)
    mag = jnp.exp(lr * dt)
    ab_re, ab_im = mag * jnp.cos(li * dt), mag * jnp.sin(li * dt)
    den = lr * lr + li * li
    inv_re, inv_im = lr / den, -li / den
    n_re, n_im = ab_re - 1.0, ab_im
    cf_re = n_re * inv_re - n_im * inv_im
    cf_im = n_re * inv_im + n_im * inv_re
    return ab_re, ab_im, cf_re, cf_im


def _ssm_prep_kernel(lam_row_ref, lam_col_ref, bt_re_ref, bt_im_ref, ct_re_ref, ct_im_ref,
                     ws_ref, wx_ref, m_ref, c0_ref, coef_ref):
    row = lam_row_ref[0]
    ab_re, ab_im, cf_re, cf_im = _discretize(row[0:1], row[1:2], row[2:3])
    bt_re, bt_im = bt_re_ref[0], bt_im_ref[0]
    bb_re = cf_re * bt_re - cf_im * bt_im
    bb_im = cf_re * bt_im + cf_im * bt_re
    col = lam_col_ref[0]
    abc_re, abc_im, _, _ = _discretize(col[:, 0:1], col[:, 1:2], col[:, 2:3])
    ct_re, ct_im = ct_re_ref[0], ct_im_ref[0]
    c0 = jnp.concatenate([ct_re, -ct_im], axis=0)
    c0_ref[0] = c0.astype(BF16)

    pr, pi = jnp.ones_like(ab_re), jnp.zeros_like(ab_re)
    qr, qi = abc_re, abc_im
    c0b = c0.astype(BF16)
    for lag in range(PLANES):
        w = jnp.concatenate([pr * bb_re - pi * bb_im, pr * bb_im + pi * bb_re], axis=1).astype(BF16)
        s = PLANES - 1 - lag
        ws_ref[0, s * LANES:(s + 1) * LANES, :] = w
        m_ref[0, lag * LANES:(lag + 1) * LANES, :] = jnp.dot(w, c0b, preferred_element_type=F32).astype(BF16)
        wx_ref[0, 0:STATE_HALF, lag * LANES:(lag + 1) * LANES] = (ct_re * qr - ct_im * qi).astype(BF16)
        wx_ref[0, STATE_HALF:2 * STATE_HALF, lag * LANES:(lag + 1) * LANES] = (-(ct_re * qi + ct_im * qr)).astype(BF16)
        pr, pi = pr * ab_re - pi * ab_im, pr * ab_im + pi * ab_re
        qr, qi = qr * abc_re - qi * abc_im, qr * abc_im + qi * abc_re
    coef_ref[0] = jnp.concatenate([
        jnp.concatenate([ab_re, ab_im], axis=1),
        jnp.concatenate([pr, pi], axis=1),
        jnp.zeros((SUBLANES - 2, 2 * STATE_HALF), F32)], axis=0)


def _ssm_prep(lam_re, lam_im, log_dt, b_re, b_im, c_re, c_im):
    groups, n = lam_re.shape
    tiles = groups // GROUPS_PER_TILE
    gpt = GROUPS_PER_TILE
    eye = jnp.eye(gpt, dtype=F32)

    def rows(v):
        return v.reshape(tiles, gpt * n)

    ldt = jnp.broadcast_to(log_dt[:, None], (groups, n))
    lam_row = jnp.stack([rows(lam_re), rows(lam_im), rows(ldt)], axis=1)
    lam_col = jnp.transpose(lam_row, (0, 2, 1))

    def bt(b):
        b = jnp.transpose(b.reshape(tiles, gpt, n, SSM_CH), (0, 1, 3, 2))
        return (b[:, :, :, None, :] * eye[None, :, None, :, None]).reshape(tiles, gpt * SSM_CH, gpt * n)

    def ct(c):
        c = jnp.transpose(c.reshape(tiles, gpt, SSM_CH, n), (0, 1, 3, 2))
        return (c[:, :, :, None, :] * eye[None, :, None, :, None]).reshape(tiles, gpt * n, gpt * SSM_CH)

    sh, ln = STATE_HALF, LANES
    tile3 = lambda a, b: pl.BlockSpec((1, a, b), lambda t: (t, 0, 0))
    return pl.pallas_call(
        _ssm_prep_kernel,
        grid=(tiles,),
        in_specs=[tile3(3, sh), tile3(sh, 3), tile3(ln, sh), tile3(ln, sh), tile3(sh, ln), tile3(sh, ln)],
        out_specs=[tile3(PLANES * ln, 2 * sh), tile3(2 * sh, PLANES * ln), tile3(PLANES * ln, ln),
                   tile3(2 * sh, ln), tile3(SUBLANES, 2 * sh)],
        out_shape=[jax.ShapeDtypeStruct((tiles, PLANES * ln, 2 * sh), BF16),
                   jax.ShapeDtypeStruct((tiles, 2 * sh, PLANES * ln), BF16),
                   jax.ShapeDtypeStruct((tiles, PLANES * ln, ln), BF16),
                   jax.ShapeDtypeStruct((tiles, 2 * sh, ln), BF16),
                   jax.ShapeDtypeStruct((tiles, SUBLANES, 2 * sh), F32)],
        compiler_params=_cparams(("arbitrary",)),
        name="ssm_prep",
    )(lam_row, lam_col, bt(b_re), bt(b_im), ct(c_re), ct(c_im))


def _inproj_kernel(first_keep, scale, x_ref, perm_ref, wqkv_ref, wu_ref, *rest):
    sample_in, (q_ref, k_ref, v_ref, u_ref, kl_ref, vl_ref, sample_out, slab_ref) = rest[:7], rest[7:]
    slabs = slab_ref.shape[0]
    prow = q_ref.shape[1]
    width = u_ref.shape[-1]
    xp = jnp.dot(perm_ref[...], x_ref[...].astype(BF16), preferred_element_type=F32).astype(BF16)

    def to_planes(val, out_ref):
        for r in range(PLANES):
            out_ref[r] = val[r * prow:(r + 1) * prow].astype(out_ref.dtype)

    sq_ref, skn_ref, svn_ref, skt_ref, svt_ref, stb_ref, stn_ref = sample_in
    to_planes(jnp.dot(xp, wqkv_ref[:, 0:width], preferred_element_type=F32) * scale, q_ref)
    sample_scores = _sample_scores(sq_ref, skn_ref, skt_ref)
    k = jnp.dot(xp, wqkv_ref[:, width:2 * width], preferred_element_type=F32)
    to_planes(k, k_ref)
    v = jnp.dot(xp, wqkv_ref[:, 2 * width:3 * width], preferred_element_type=F32)
    to_planes(v, v_ref)
    _sample_outputs(sample_scores, svn_ref, svt_ref, stb_ref, stn_ref, sample_out)
    to_planes(jnp.dot(xp, wu_ref[...], preferred_element_type=F32), u_ref)

    @pl.when(pl.program_id(1) >= first_keep)
    def _():
        def token_order_t(val):
            for r in range(PLANES):
                for c in range(slabs):
                    slab_ref[c, pl.ds(r, prow, stride=PLANES), :] = val[r * prow:(r + 1) * prow,
                                                                        c * LANES:(c + 1) * LANES]
            return jnp.concatenate([slab_ref[c] for c in range(slabs)], axis=1).T
        kl_ref[...] = token_order_t(k)
        vl_ref[...] = token_order_t(v)


def _inproj_prompt(x, w_bf, keep, sample):
    bsz, seq, d_model = x.shape
    width = w_bf.shape[1] // 6
    rows = seq // PLANES
    tile = ROW_TILE // 2
    prow = tile // PLANES
    first_keep = (seq - keep) // tile
    steps = seq // tile
    sample_specs, sample_out_spec, sample_out = _attention_sample_specs(*sample, lambda b, i: b * steps + i)
    assert bsz * steps == sample[0].shape[0] * (width // (SAMPLE_PAIRS * LANES)), "one sample unit per grid step"
    plane = lambda dt: jax.ShapeDtypeStruct((bsz, PLANES, rows, width), dt)
    plane_spec = pl.BlockSpec((None, PLANES, prow, width), lambda b, i: (b, 0, i, 0))
    last = jax.ShapeDtypeStruct((bsz, width, keep), F32)
    last_spec = pl.BlockSpec((None, width, tile), lambda b, i: (b, 0, jnp.maximum(i - first_keep, 0)))
    perm = np.zeros((tile, tile), np.float32)
    perm[np.arange(tile), PLANES * (np.arange(tile) % prow) + np.arange(tile) // prow] = 1.0
    return pl.pallas_call(
        functools.partial(_inproj_kernel, first_keep, HEAD_DIM ** -0.5 * LOG2E),
        grid=(bsz, steps),
        in_specs=[pl.BlockSpec((None, tile, d_model), lambda b, i: (b, i, 0)),
                  pl.BlockSpec((tile, tile), lambda b, i: (0, 0), pipeline_mode=ONCE),
                  pl.BlockSpec((d_model, 3 * width), lambda b, i: (0, 0), pipeline_mode=ONCE),
                  pl.BlockSpec((d_model, width), lambda b, i: (0, 4), pipeline_mode=ONCE)] + sample_specs,
        out_specs=[plane_spec, plane_spec, plane_spec, plane_spec, last_spec, last_spec, sample_out_spec],
        out_shape=[plane(F32), plane(F32), plane(F32), plane(BF16), last, last, sample_out],
        scratch_shapes=[pltpu.VMEM((width // LANES, tile, LANES), F32)],
        compiler_params=_cparams(("arbitrary", "arbitrary")),
        name="inproj_prompt",
    )(x, jnp.asarray(perm, BF16), w_bf, w_bf, sample[0], sample[0], sample[0], *sample[1:])


ATTN_UNROLL = 2


def _attn_kernel(q_ref, kp_ref, kc_ref, vp_ref, vc_ref, bias_ref, o_ref, acc_ref, m_ref, l_ref, s_ref, p_ref, mn_ref):
    sup = pl.program_id(2)
    is_a = lax.broadcasted_iota(jnp.int32, (BLK, LANES), 1) < HEAD_DIM

    def raw_scores(q, k):
        kb = k.astype(BF16)
        out = []
        for head in range(2):
            qm = jnp.where(is_a if head == 0 else jnp.logical_not(is_a), q, 0.0).astype(BF16)
            out.append(lax.dot_general(qm, kb, (((1,), (1,)), ((), ())), preferred_element_type=F32))
        return out

    def both(a, b):
        return jnp.where(is_a, a, b)

    def weights(s, table, m_old):
        m_new, p = [], []
        for h in range(2):
            sh = s[h] + bias_ref[h, table]
            mb = jnp.broadcast_to(jnp.max(sh, axis=1, keepdims=True), (BLK, LANES))
            mh = mb if m_old is None else jnp.maximum(m_old[h], mb)
            m_new.append(mh)
            p.append(jnp.exp2(sh - jnp.concatenate([mh, mh], axis=1)).astype(BF16))
        return m_new, p

    ones = jnp.ones((2 * BLK, LANES), BF16)

    def combine(m_new, p, v, state):
        vext = jnp.concatenate([v.astype(BF16), ones], axis=1)
        res = [jnp.dot(p[h], vext, preferred_element_type=F32) for h in range(2)]
        acc_new = both(res[0][:, :LANES], res[1][:, :LANES])
        l_new = [res[h][:, LANES:] for h in range(2)]
        if state is not None:
            alpha = [jnp.exp2(state[0][h] - m_new[h]) for h in range(2)]
            l_new = [alpha[h] * state[1][h] + l_new[h] for h in range(2)]
            acc_new = both(alpha[0], alpha[1]) * state[2] + acc_new
        return m_new, l_new, acc_new

    cat = lambda parts: parts[0] if len(parts) == 1 else jnp.concatenate(parts, axis=0)

    def gather(ref, planes, start, size, *lead):
        return cat([ref[(*lead, pln, pl.ds(start, size), slice(None))] for pln in planes])

    def run_pattern(a, dil):
        npl = PLANES // dil
        qrows = BLK // npl
        per_res = BLK // qrows
        groups = PLANES // ATTN_UNROLL

        def geometry(grp, un):
            res, sub = divmod(grp * ATTN_UNROLL + un, per_res)
            off = sub * qrows
            table = 2 * a + (jnp.where(sup == 0, 1, 0) if sub == 0 else 0)
            return [res + dil * i for i in range(npl)], off, table

        def keys(prev_ref, cur_ref, planes, off):
            if off:
                return cat([cur_ref[pln, off - qrows:off + qrows, :] for pln in planes])
            return cat([part for pln in planes for part in (prev_ref[pln, BLK - qrows:BLK, :], cur_ref[pln, 0:qrows, :])])

        def stage_scores(grp):
            stores = []
            for un in range(ATTN_UNROLL):
                planes, off, _ = geometry(grp, un)
                s = raw_scores(gather(q_ref, planes, off, qrows), keys(kp_ref, kc_ref, planes, off))
                stores += [(s_ref, (grp % 2, un, h), s[h]) for h in range(2)]
            return stores

        def stage_softmax(grp):
            stores = []
            for un in range(ATTN_UNROLL):
                planes, off, table = geometry(grp, un)
                m_old = None if a == 0 else [gather(m_ref, planes, off, qrows, h) for h in range(2)]
                m_new, p = weights([s_ref[grp % 2, un, h] for h in range(2)], table, m_old)
                for h in range(2):
                    stores += [(p_ref, (grp % 2, un, h), p[h]), (mn_ref, (grp % 2, un, h), m_new[h])]
            return stores

        def stage_values(grp):
            stores = []
            for un in range(ATTN_UNROLL):
                planes, off, _ = geometry(grp, un)
                old = None if a == 0 else ([gather(m_ref, planes, off, qrows, h) for h in range(2)],
                                           [gather(l_ref, planes, off, qrows, h) for h in range(2)],
                                           gather(acc_ref, planes, off, qrows))
                m_new, l_new, acc_new = combine([mn_ref[grp % 2, un, h] for h in range(2)],
                                                [p_ref[grp % 2, un, h] for h in range(2)],
                                                keys(vp_ref, vc_ref, planes, off), old)
                for i, pln in enumerate(planes):
                    part = slice(i * qrows, (i + 1) * qrows)
                    rows = (pln, pl.ds(off, qrows), slice(None))
                    for h in range(2):
                        stores += [(m_ref, (h,) + rows, m_new[h][part]), (l_ref, (h,) + rows, l_new[h][part])]
                    stores.append((acc_ref, rows, acc_new[part]))
            return stores

        for step in range(groups + 2):
            stores = []
            if 0 <= step - 2 < groups:
                stores += stage_values(step - 2)
            if 0 <= step - 1 < groups:
                stores += stage_softmax(step - 1)
            if step < groups:
                stores += stage_scores(step)
            for ref, idx, val in stores:
                ref[idx] = val

    for a, dil in enumerate(DILATIONS):
        run_pattern(a, dil)
    for pln in range(PLANES):
        o_ref[pl.ds(pln, BLK, stride=PLANES), :] = acc_ref[pln] / both(l_ref[0, pln], l_ref[1, pln])


def _attention_prompt(q, k, v, bias):
    bsz, _, rows, width = q.shape
    pairs = width // LANES
    ntab = bias.shape[1]
    cur = pl.BlockSpec((None, PLANES, BLK, LANES), lambda b, h, s: (b, 0, s, h))
    prev = pl.BlockSpec((None, PLANES, BLK, LANES), lambda b, h, s: (b, 0, jnp.maximum(s - 1, 0), h))
    return pl.pallas_call(
        _attn_kernel,
        grid=(bsz, pairs, rows // BLK),
        in_specs=[cur, prev, cur, prev, cur,
                  pl.BlockSpec((2, ntab, BLK, 2 * BLK), lambda b, h, s: (h, 0, 0, 0))],
        out_specs=pl.BlockSpec((None, PLANES * BLK, LANES), lambda b, h, s: (b, s, h)),
        out_shape=jax.ShapeDtypeStruct((bsz, PLANES * rows, width), F32),
        scratch_shapes=[pltpu.VMEM((PLANES, BLK, LANES), F32),
                        pltpu.VMEM((2, PLANES, BLK, LANES), F32),
                        pltpu.VMEM((2, PLANES, BLK, LANES), F32),
                        pltpu.VMEM((2, ATTN_UNROLL, 2, BLK, 2 * BLK), F32),
                        pltpu.VMEM((2, ATTN_UNROLL, 2, BLK, 2 * BLK), BF16),
                        pltpu.VMEM((2, ATTN_UNROLL, 2, BLK, LANES), F32)],
        compiler_params=_cparams(("arbitrary", "arbitrary", "arbitrary")),
        name="attention_prompt",
    )(q, k, k, v, v, bias)


SSM_COLS = 4


def _ssm_kernel(u_ref, ws_ref, wx_ref, mlag_ref, coef_ref, d_ref, y_ref, sre_ref, sim_ref,
                ucat_ref, s_ref, x_ref, m_ref):
    chunks = u_ref.shape[1]

    @pl.when(pl.program_id(1) == 0)
    def _():
        zero_blk = jnp.zeros((LANES, LANES), BF16)
        for s in range(PLANES):
            for t in range(PLANES):
                blk = mlag_ref[(t - s) * LANES:(t - s + 1) * LANES, :] if t >= s else zero_blk
                m_ref[s * LANES:(s + 1) * LANES, t * LANES:(t + 1) * LANES] = blk

    for s in range(PLANES):
        ucat_ref[:, s * LANES:(s + 1) * LANES] = u_ref[s]
    s_ref[...] = jnp.dot(ucat_ref[...], ws_ref[...], preferred_element_type=F32)
    a_re = coef_ref[1:2, 0:STATE_HALF]
    a_im = coef_ref[1:2, STATE_HALF:2 * STATE_HALF]

    def step(c, carry):
        xr, xi = carry
        x_ref[pl.ds(c, 1), 0:STATE_HALF] = xr
        x_ref[pl.ds(c, 1), STATE_HALF:2 * STATE_HALF] = xi
        sr = s_ref[pl.ds(c, 1), 0:STATE_HALF]
        si = s_ref[pl.ds(c, 1), STATE_HALF:2 * STATE_HALF]
        return a_re * xr - a_im * xi + sr, a_re * xi + a_im * xr + si

    zero = jnp.zeros((1, STATE_HALF), F32)
    xr, xi = lax.fori_loop(0, chunks, step, (zero, zero))
    sre_ref[...] = xr
    sim_ref[...] = xi

    xb = x_ref[...].astype(BF16)
    d = d_ref[...]
    for g in range(PLANES // SSM_COLS):
        cols = slice(g * SSM_COLS * LANES, (g + 1) * SSM_COLS * LANES)
        used = (g + 1) * SSM_COLS * LANES
        y = jnp.dot(ucat_ref[:, :used], m_ref[:used, cols], preferred_element_type=F32)
        y = y + jnp.dot(xb, wx_ref[:, cols], preferred_element_type=F32)
        for t in range(SSM_COLS):
            tok = g * SSM_COLS + t
            y_ref[pl.ds(tok, chunks, stride=PLANES), :] = (y[:, t * LANES:(t + 1) * LANES]
                                                          + d * u_ref[tok].astype(F32))


def _ssm_prompt(u, ws, wx, mlag, coef, d_skip):
    bsz, _, chunks, width = u.shape
    tiles = width // LANES
    sh = STATE_HALF
    state = jax.ShapeDtypeStruct((bsz, tiles, 1, sh), F32)
    state_spec = pl.BlockSpec((None, None, 1, sh), lambda t, b: (b, t, 0, 0))
    per_tile = lambda r, c: pl.BlockSpec((None, r, c), lambda t, b: (t, 0, 0))
    return pl.pallas_call(
        _ssm_kernel,
        grid=(tiles, bsz),
        in_specs=[pl.BlockSpec((None, PLANES, chunks, LANES), lambda t, b: (b, 0, 0, t)),
                  per_tile(PLANES * LANES, 2 * sh), per_tile(2 * sh, PLANES * LANES),
                  per_tile(PLANES * LANES, LANES), per_tile(SUBLANES, 2 * sh),
                  pl.BlockSpec((1, LANES), lambda t, b: (0, t))],
        out_specs=[pl.BlockSpec((None, PLANES * chunks, LANES), lambda t, b: (b, 0, t)),
                   state_spec, state_spec],
        out_shape=[jax.ShapeDtypeStruct((bsz, PLANES * chunks, width), F32), state, state],
        scratch_shapes=[pltpu.VMEM((chunks, PLANES * LANES), BF16),
                        pltpu.VMEM((chunks, 2 * sh), F32),
                        pltpu.VMEM((chunks, 2 * sh), F32),
                        pltpu.VMEM((PLANES * LANES, PLANES * LANES), BF16)],
        compiler_params=_cparams(("arbitrary", "arbitrary")),
        name="ssm_prompt",
    )(u, ws, wx, mlag, coef, d_skip)


EPILOGUE_ROWS = 256


def _epilogue_kernel(alpha, attn_ref, y_ref, x_ref, wga_ref, wgs_ref, wglu_ref, bglu_ref, wo_ref, bo_ref,
                     g_ref, b_ref, o_ref):
    d_attn = attn_ref.shape[-1]
    rows = x_ref.shape[0]
    part = min(rows, EPILOGUE_ROWS)
    for r0 in range(0, rows, part):
        rs = slice(r0, r0 + part)
        x = x_ref[rs, :]
        xb = x.astype(BF16)
        g_attn = jnp.dot(xb, wga_ref[...], preferred_element_type=F32)
        br_a = (attn_ref[rs, :] * jax.nn.silu(g_attn)).astype(BF16)
        g_ssm = jnp.dot(xb, wgs_ref[...], preferred_element_type=F32)
        z = jax.nn.gelu(y_ref[rs, :])
        gate = jax.nn.sigmoid(jnp.dot(z.astype(BF16), wglu_ref[...], preferred_element_type=F32) + bglu_ref[...])
        br_s = (z * gate * jax.nn.silu(g_ssm)).astype(BF16)
        mix = (jnp.dot(br_a, wo_ref[0:d_attn, :], preferred_element_type=F32)
               + jnp.dot(br_s, wo_ref[d_attn:, :], preferred_element_type=F32) + bo_ref[...])
        t = alpha * x + mix
        mu = jnp.mean(t, axis=-1, keepdims=True)
        var = jnp.mean(jnp.square(t - mu), axis=-1, keepdims=True)
        o_ref[rs, :] = (t - mu) * lax.rsqrt(var + LN_EPS) * g_ref[...] + b_ref[...]


def _epilogue(alpha, name, attn, y, x, w_bf, weights, tile):
    lead = attn.shape[:-2]
    rows, width = attn.shape[-2:]
    d_model = x.shape[-1]
    grid = lead + (rows // tile,)
    none = (None,) * len(lead)
    tiled = lambda w: pl.BlockSpec(none + (tile, w), lambda *g: g + (0,))
    const = lambda shape, col=0: pl.BlockSpec(shape, lambda *g: (0, col), pipeline_mode=ONCE)
    return pl.pallas_call(
        functools.partial(_epilogue_kernel, alpha),
        grid=grid,
        in_specs=[tiled(width), tiled(width), tiled(d_model),
                  const((d_model, width), 3), const((d_model, width), 5),
                  const((width, width)), const((1, width)), const((2 * width, d_model)), const((1, d_model)),
                  const((1, d_model)), const((1, d_model))],
        out_specs=tiled(d_model),
        out_shape=jax.ShapeDtypeStruct(x.shape, F32),
        compiler_params=_cparams(("arbitrary",) * len(grid)),
        name=name,
    )(attn, y, x, w_bf, w_bf, *weights)


def _inproj_sample_kernel(scale, x_ref, w_ref, o_ref):
    acc = jnp.dot(x_ref[...].astype(BF16), w_ref[...], preferred_element_type=F32)
    o_ref[...] = acc * jnp.where(pl.program_id(0) == 0, scale, 1.0)


def _inproj_sample(x, w_bf):
    rows, d_model = x.shape
    width = w_bf.shape[1] // 6
    return pl.pallas_call(
        functools.partial(_inproj_sample_kernel, HEAD_DIM ** -0.5),
        grid=(4,),
        in_specs=[pl.BlockSpec((rows, d_model), lambda j: (0, 0)),
                  pl.BlockSpec((d_model, width), lambda j: (0, j + j // 3))],
        out_specs=pl.BlockSpec((rows, width), lambda j: (0, j)),
        out_shape=jax.ShapeDtypeStruct((rows, 4 * width), F32),
        compiler_params=_cparams(("arbitrary",)),
        name="inproj_sample",
    )(x, w_bf)


SAMPLE_PAIRS = 4


_NT = (((1,), (1,)), ((), ()))


def _sample_scores(q_ref, kn_ref, kt_ref):
    s_len = q_ref.shape[0]
    buf = kt_ref.shape[2]
    is_a = lax.broadcasted_iota(jnp.int32, (s_len, LANES), 1) < HEAD_DIM
    out = []
    for pp in range(SAMPLE_PAIRS):
        lanes = slice(pp * LANES, (pp + 1) * LANES)
        qp = q_ref[:, lanes]
        q2 = jnp.concatenate([jnp.where(is_a, qp, 0.0), jnp.where(is_a, 0.0, qp)], axis=0)
        kt = kt_ref[2 * pp:2 * pp + 2].reshape(2 * HEAD_DIM, buf).astype(BF16)
        kn = kn_ref[:, lanes]
        s_new = [jnp.sum(q2 * kn[j:j + 1, :], axis=1, keepdims=True) for j in range(s_len)]
        out.append((jnp.dot(q2.astype(BF16), kt, preferred_element_type=F32), s_new))
    return out


def _sample_outputs(scores, vn_ref, vt_ref, tb_ref, tn_ref, o_ref):
    s_len = vn_ref.shape[0]
    buf = vt_ref.shape[2]
    npat = tb_ref.shape[0]
    is_a = lax.broadcasted_iota(jnp.int32, (s_len, LANES), 1) < HEAD_DIM
    for pp, (s_buf, s_new) in enumerate(scores):
        lanes = slice(pp * LANES, (pp + 1) * LANES)
        rows = slice(pp * 2 * s_len, (pp + 1) * 2 * s_len)
        vt = vt_ref[2 * pp:2 * pp + 2].reshape(2 * HEAD_DIM, buf).astype(BF16)
        vn = vn_ref[:, lanes]
        z_buf = [s_buf + tb_ref[a, rows, :] for a in range(npat)]
        z_new = [[s_new[j] + tn_ref[a, rows, j:j + 1] for a in range(npat)] for j in range(s_len)]
        top = functools.reduce(jnp.maximum, [jnp.max(z, axis=1, keepdims=True) for z in z_buf]
                               + [z for zs in z_new for z in zs])
        w_buf = functools.reduce(jnp.add, [jnp.exp(z - top) for z in z_buf])
        w_new = [functools.reduce(jnp.add, [jnp.exp(z - top) for z in zs]) for zs in z_new]
        den = jnp.sum(w_buf, axis=1, keepdims=True) + functools.reduce(jnp.add, w_new)
        o = lax.dot_general(vt, w_buf.astype(BF16), _NT, preferred_element_type=F32).T
        o = (o + functools.reduce(jnp.add, [w_new[j] * vn[j:j + 1, :] for j in range(s_len)])) / den
        o_ref[:, lanes] = jnp.where(is_a, o[0:s_len], o[s_len:2 * s_len])


def _attention_sample_specs(h, kt, vt, tb, tn, unit_of):
    bsz, s_len, _ = h.shape
    heads, _, buf = kt.shape[1:]
    width = heads * HEAD_DIM
    gw = SAMPLE_PAIRS * LANES
    per_w = width // gw
    trows = SAMPLE_PAIRS * 2 * s_len
    where = lambda *g: divmod(unit_of(*g), per_w)
    new = lambda c: pl.BlockSpec((None, s_len, gw), lambda *g: (where(*g)[0], 0, c * per_w + where(*g)[1]))
    cache = pl.BlockSpec((None, 2 * SAMPLE_PAIRS, HEAD_DIM, buf), lambda *g: (where(*g)[0], where(*g)[1], 0, 0))
    specs = [new(0), new(1), new(2), cache, cache,
             pl.BlockSpec((tb.shape[0], trows, buf), lambda *g: (0, where(*g)[1], 0)),
             pl.BlockSpec((tn.shape[0], trows, LANES), lambda *g: (0, where(*g)[1], 0))]
    out_spec = pl.BlockSpec((None, s_len, gw), lambda *g: (where(*g)[0], 0, where(*g)[1]))
    return specs, out_spec, jax.ShapeDtypeStruct((bsz, s_len, width), F32)


def _ssm_sample_kernel(s_len, u_ref, b0_ref, c0_ref, coef_ref, d_ref, x0r_ref, x0i_ref,
                       y_ref, sre_ref, sim_ref, bu_ref, xs_ref):
    bsz = x0r_ref.shape[0]
    bu = jnp.dot(u_ref[...].astype(BF16), b0_ref[...], preferred_element_type=F32)
    slabs = bu_ref.shape[0]
    for c in range(slabs):
        bu_ref[c] = bu[:, c * LANES:(c + 1) * LANES]
    a_re = coef_ref[0:1, 0:STATE_HALF]
    a_im = coef_ref[0:1, STATE_HALF:2 * STATE_HALF]
    half = slabs // 2
    xr, xi = x0r_ref[...], x0i_ref[...]
    for s in range(s_len):
        step = lambda c: bu_ref[c, pl.ds(s, bsz, stride=s_len), :]
        br = jnp.concatenate([step(c) for c in range(half)], axis=1)
        bi = jnp.concatenate([step(c) for c in range(half, slabs)], axis=1)
        xr, xi = a_re * xr - a_im * xi + br, a_re * xi + a_im * xr + bi
        xs_ref[s * bsz:(s + 1) * bsz, :] = jnp.concatenate([xr, xi], axis=1)
    sre_ref[...] = xr
    sim_ref[...] = xi
    y = jnp.dot(xs_ref[...].astype(BF16), c0_ref[...], preferred_element_type=F32)
    d = d_ref[...]
    for s in range(s_len):
        y_ref[pl.ds(s, bsz, stride=s_len), :] = (y[s * bsz:(s + 1) * bsz, :]
                                                 + d * u_ref[pl.ds(s, bsz, stride=s_len), :])


def _ssm_sample(h, ws, c0, coef, d_skip, x0_re, x0_im, s_len):
    rows = h.shape[0]
    bsz = rows // s_len
    tiles = ws.shape[0]
    sh = STATE_HALF
    u_col0 = 3 * (h.shape[1] // 4) // LANES
    st_spec = pl.BlockSpec((bsz, sh), lambda t: (0, t))
    state = jax.ShapeDtypeStruct((bsz, tiles * sh), F32)
    return pl.pallas_call(
        functools.partial(_ssm_sample_kernel, s_len),
        grid=(tiles,),
        in_specs=[pl.BlockSpec((rows, LANES), lambda t: (0, u_col0 + t)),
                  pl.BlockSpec((None, LANES, 2 * sh), lambda t: (t, PLANES - 1, 0)),
                  pl.BlockSpec((None, 2 * sh, LANES), lambda t: (t, 0, 0)),
                  pl.BlockSpec((None, SUBLANES, 2 * sh), lambda t: (t, 0, 0)),
                  pl.BlockSpec((1, LANES), lambda t: (0, t)),
                  st_spec, st_spec],
        out_specs=[pl.BlockSpec((rows, LANES), lambda t: (0, t)), st_spec, st_spec],
        out_shape=[jax.ShapeDtypeStruct((rows, tiles * LANES), F32), state, state],
        scratch_shapes=[pltpu.VMEM((2 * sh // LANES, rows, LANES), F32),
                        pltpu.VMEM((rows, 2 * sh), F32)],
        compiler_params=_cparams(("arbitrary",)),
        name="ssm_sample",
    )(h, ws, c0, coef, d_skip, x0_re, x0_im)


def kernel(x_prompt, x_sample, cache_k, cache_v, state_ssm_re, state_ssm_im, w_in, w_out, b_out, rel_bias,
           lam_re, lam_im, log_dt, b_re, b_im, c_re, c_im, d_skip, w_glu, b_glu, ln_g, ln_b):
    depth = w_in.shape[0]
    assert depth == 1, "one layer per step"
    bsz, seq, d_model = x_prompt.shape
    dbsz, s_len, _ = x_sample.shape
    buf, heads = cache_k.shape[2], cache_k.shape[3]
    width = heads * HEAD_DIM
    groups, nstate = lam_re.shape[1], lam_re.shape[2]
    keep = min(MAX_DISTANCE, seq)
    assert seq % (PLANES * BLK) == 0 and seq >= 2 * PLANES * BLK and keep % ROW_TILE == 0
    assert buf == KPER * max(DILATIONS) and s_len <= min(DILATIONS[:-1])
    assert nstate == SSM_STATE and width == groups * SSM_CH and heads % (2 * SAMPLE_PAIRS) == 0
    alpha = (2 * depth) ** 0.25
    npat = len(DILATIONS)

    w_bf = w_in[0].astype(BF16)
    row = lambda v: v.reshape(1, -1)
    weights = (w_glu[0].astype(BF16), row(b_glu[0]), w_out[0].astype(BF16), row(b_out[0]),
               row(ln_g[0]), row(ln_b[0]))
    d_row = row(d_skip[0])

    rbt = rel_bias.T
    ptab = _bias_tables(jnp.asarray(_prompt_bucket_tables().reshape(1, -1)), rbt, BLK * 2 * BLK, LOG2E)
    ptab = ptab.reshape(heads, 2 * npat, BLK, 2 * BLK)
    key_w = buf + 2 * LANES
    stab = _bias_tables(jnp.asarray(_sample_bucket_tables(buf, s_len, key_w)), rbt, npat * s_len * key_w)
    stab = jnp.transpose(stab.reshape(heads, npat, s_len, key_w), (1, 0, 2, 3)).reshape(npat, heads * s_len, key_w)
    stab_buf, stab_new = stab[:, :, :buf], stab[:, :, buf:buf + LANES]
    ws, wx, m_intra, c0, coef = _ssm_prep(lam_re[0], lam_im[0], log_dt[0], b_re[0], b_im[0], c_re[0], c_im[0])

    xs = x_sample.reshape(dbsz * s_len, d_model)
    hs = _inproj_sample(xs, w_bf)
    pos_minor = lambda c: jnp.transpose(c[0], (0, 2, 3, 1))
    sample = (hs.reshape(dbsz, s_len, -1), pos_minor(cache_k), pos_minor(cache_v), stab_buf, stab_new)

    q, k, v, u, kl_t, vl_t, attn_s = _inproj_prompt(x_prompt, w_bf, keep, sample)
    attn = _attention_prompt(q, k, v, ptab)
    y, sre_p, sim_p = _ssm_prompt(u, ws, wx, m_intra, coef, d_row)
    y_prompt = _epilogue(alpha, "epilogue_prompt", attn, y, x_prompt, w_bf, weights, ROW_TILE // 2)
    last = lambda t: jnp.transpose(t.reshape(bsz, heads, HEAD_DIM, keep), (0, 3, 1, 2))[None]
    st_shape = (1, bsz, groups, nstate)

    y_s, sre_s, sim_s = _ssm_sample(hs, ws, c0, coef, d_row,
                                    state_ssm_re[0].astype(F32).reshape(dbsz, groups * nstate),
                                    state_ssm_im[0].astype(F32).reshape(dbsz, groups * nstate), s_len)
    y_sample = _epilogue(alpha, "epilogue_sample", attn_s.reshape(dbsz * s_len, width), y_s, xs, w_bf, weights,
                         dbsz * s_len)
    new_shape = (1, dbsz, s_len, heads, HEAD_DIM)
    sst_shape = (1, dbsz, groups, nstate)
    return (y_prompt, y_sample.reshape(dbsz, s_len, d_model), last(kl_t), last(vl_t),
            sre_p.reshape(st_shape), sim_p.reshape(st_shape),
            hs[:, width:2 * width].reshape(new_shape), hs[:, 2 * width:3 * width].reshape(new_shape),
            sre_s.reshape(sst_shape), sim_s.reshape(sst_shape))
```

```python
import functools
import math

import jax
import jax.numpy as jnp
import numpy as np
from jax import lax
from jax.experimental import pallas as pl
from jax.experimental.pallas import tpu as pltpu

F32 = jnp.float32
BF16 = jnp.bfloat16

HEAD_DIM = 64
SSM_CH = 16
SSM_STATE = 64
NUM_BUCKETS = 32
MAX_DISTANCE = 2048
KPER = 128
BLK = 128
DILATIONS = (16, 4, 1)
LN_EPS = 1e-5
NEG = -1e30
LOG2E = math.log2(math.e)

LANES = 128
SUBLANES = 8
PLANES = 16
GROUPS_PER_TILE = LANES // SSM_CH
STATE_HALF = GROUPS_PER_TILE * SSM_STATE
VMEM_LIMIT = 56 * 1024 * 1024
ROW_TILE = 512
ONCE = pl.Buffered(1)


def _cparams(sem, vmem=VMEM_LIMIT):
    return pltpu.CompilerParams(dimension_semantics=sem, vmem_limit_bytes=vmem)


def _bucket_np(dist):
    exact = NUM_BUCKETS // 2
    d_f = np.maximum(dist, 1).astype(np.float32)
    large = exact + (np.log(d_f / np.float32(exact)) / np.float32(math.log(MAX_DISTANCE / exact))
                     * np.float32(NUM_BUCKETS - exact)).astype(np.int32)
    large = np.minimum(large, NUM_BUCKETS - 1)
    return np.where(dist < exact, dist, large).astype(np.int32)


def _prompt_rel(dil):
    i = np.arange(BLK)[:, None]
    j = np.arange(2 * BLK)[None, :]
    npl = PLANES // dil
    qrows = BLK // npl
    pq, ml = i // qrows, i % qrows
    pk, jl = j // (2 * qrows), j % (2 * qrows)
    return npl * (ml - jl + qrows) + (pq - pk), np.broadcast_to(jl < qrows, (BLK, 2 * BLK))


def _prompt_bucket_tables():
    tabs = []
    for dil in DILATIONS:
        rel, earlier = _prompt_rel(dil)
        for first in (False, True):
            valid = (rel >= 0) & (rel <= KPER) & np.logical_not(earlier & first)
            tabs.append(np.where(valid, _bucket_np(np.clip(rel, 0, KPER) * dil), NUM_BUCKETS))
    return np.stack(tabs).reshape(len(DILATIONS) * 2, BLK * 2 * BLK).astype(np.int32)


def _sample_bucket_tables(buf, s_len, width):
    pos = np.arange(buf + s_len)
    tabs = np.full((len(DILATIONS), s_len, width), NUM_BUCKETS, np.int32)
    for a, dil in enumerate(DILATIONS):
        for s in range(s_len):
            dist = buf + s - pos
            valid = (dist >= 0) & (dist % dil == 0) & (dist // dil <= KPER)
            tabs[a, s, :len(pos)] = np.where(valid, _bucket_np(np.maximum(dist, 0)), NUM_BUCKETS)
    return tabs.reshape(1, -1)


def _bias_kernel(scale, idx_ref, rbt_ref, o_ref):
    idx = idx_ref[...]
    onehot = (lax.broadcasted_iota(jnp.int32, (NUM_BUCKETS, idx.shape[1]), 0) == idx).astype(BF16)
    rb = rbt_ref[...]
    n_heads = rb.shape[0]
    hi = rb.astype(BF16)
    lo = (rb - hi.astype(F32)).astype(BF16)
    res = jnp.dot(jnp.concatenate([hi, lo], axis=0), onehot, preferred_element_type=F32)
    tab = res[:n_heads] + res[n_heads:]
    o_ref[...] = jnp.where(idx < NUM_BUCKETS, tab * scale, NEG)


def _bias_tables(idx, rel_bias_t, chunk, scale=1.0):
    n_heads = rel_bias_t.shape[0]
    total = idx.shape[1]
    return pl.pallas_call(
        functools.partial(_bias_kernel, scale),
        grid=(total // chunk,),
        in_specs=[pl.BlockSpec((1, chunk), lambda c: (0, c)),
                  pl.BlockSpec((n_heads, NUM_BUCKETS), lambda c: (0, 0))],
        out_specs=pl.BlockSpec((n_heads, chunk), lambda c: (0, c)),
        out_shape=jax.ShapeDtypeStruct((n_heads, total), F32),
        compiler_params=_cparams(("arbitrary",)),
        name="bias_tables",
    )(idx, rel_bias_t)


def _discretize(lr, li, ldt):
    lr = jnp.minimum(lr, -1e-4)
    dt = jnp.exp(ldt)
    mag = jnp.exp(lr * dt)
    ab_re, ab_im = mag * jnp.cos(li * dt), mag * jnp.sin(li * dt)
    den = lr * lr + li * li
    inv_re, inv_im = lr / den, -li / den
    n_re, n_im = ab_re - 1.0, ab_im
    cf_re = n_re * inv_re - n_im * inv_im
    cf_im = n_re * inv_im + n_im * inv_re
    return ab_re, ab_im, cf_re, cf_im


def _ssm_prep_kernel(lam_row_ref, bt_re_ref, bt_im_ref, ct_re_ref, ct_im_ref,
                     ws_ref, wx_ref, m_ref, c0_ref, coef_ref):
    row = lam_row_ref[0]
    ab_re, ab_im, cf_re, cf_im = _discretize(row[0:1], row[1:2], row[2:3])
    bt_re, bt_im = bt_re_ref[0], bt_im_ref[0]
    bb_re = cf_re * bt_re - cf_im * bt_im
    bb_im = cf_re * bt_im + cf_im * bt_re
    ct_re, ct_im = ct_re_ref[0], ct_im_ref[0]
    c0 = jnp.concatenate([ct_re, -ct_im], axis=0)
    c0b = c0.astype(BF16)
    c0_ref[0] = c0b

    powers = [(jnp.ones_like(ab_re), jnp.zeros_like(ab_re))]
    for _ in range(PLANES):
        pr, pi = powers[-1]
        powers.append((pr * ab_re - pi * ab_im, pr * ab_im + pi * ab_re))
    cols = jnp.concatenate([part for pw in powers[1:] for part in pw], axis=0).T
    for lag in range(PLANES):
        pr, pi = powers[lag]
        w = jnp.concatenate([pr * bb_re - pi * bb_im, pr * bb_im + pi * bb_re], axis=1).astype(BF16)
        s = PLANES - 1 - lag
        ws_ref[0, s * LANES:(s + 1) * LANES, :] = w
        m_ref[0, lag * LANES:(lag + 1) * LANES, :] = jnp.dot(w, c0b, preferred_element_type=F32).astype(BF16)
        qr, qi = cols[:, 2 * lag:2 * lag + 1], cols[:, 2 * lag + 1:2 * lag + 2]
        wx_ref[0, 0:STATE_HALF, lag * LANES:(lag + 1) * LANES] = (ct_re * qr - ct_im * qi).astype(BF16)
        wx_ref[0, STATE_HALF:2 * STATE_HALF, lag * LANES:(lag + 1) * LANES] = (-(ct_re * qi + ct_im * qr)).astype(BF16)
    coef_ref[0] = jnp.concatenate([
        jnp.concatenate([ab_re, ab_im], axis=1),
        jnp.concatenate(powers[PLANES], axis=1),
        jnp.zeros((SUBLANES - 2, 2 * STATE_HALF), F32)], axis=0)


def _ssm_prep(lam_re, lam_im, log_dt, b_re, b_im, c_re, c_im):
    groups, n = lam_re.shape
    tiles = groups // GROUPS_PER_TILE
    gpt = GROUPS_PER_TILE
    eye = jnp.eye(gpt, dtype=F32)

    def rows(v):
        return v.reshape(tiles, gpt * n)

    ldt = jnp.broadcast_to(log_dt[:, None], (groups, n))
    lam_row = jnp.stack([rows(lam_re), rows(lam_im), rows(ldt)], axis=1)

    def bt(b):
        b = jnp.transpose(b.reshape(tiles, gpt, n, SSM_CH), (0, 1, 3, 2))
        return (b[:, :, :, None, :] * eye[None, :, None, :, None]).reshape(tiles, gpt * SSM_CH, gpt * n)

    def ct(c):
        c = jnp.transpose(c.reshape(tiles, gpt, SSM_CH, n), (0, 1, 3, 2))
        return (c[:, :, :, None, :] * eye[None, :, None, :, None]).reshape(tiles, gpt * n, gpt * SSM_CH)

    sh, ln = STATE_HALF, LANES
    tile3 = lambda a, b: pl.BlockSpec((1, a, b), lambda t: (t, 0, 0))
    return pl.pallas_call(
        _ssm_prep_kernel,
        grid=(tiles,),
        in_specs=[tile3(3, sh), tile3(ln, sh), tile3(ln, sh), tile3(sh, ln), tile3(sh, ln)],
        out_specs=[tile3(PLANES * ln, 2 * sh), tile3(2 * sh, PLANES * ln), tile3(PLANES * ln, ln),
                   tile3(2 * sh, ln), tile3(SUBLANES, 2 * sh)],
        out_shape=[jax.ShapeDtypeStruct((tiles, PLANES * ln, 2 * sh), BF16),
                   jax.ShapeDtypeStruct((tiles, 2 * sh, PLANES * ln), BF16),
                   jax.ShapeDtypeStruct((tiles, PLANES * ln, ln), BF16),
                   jax.ShapeDtypeStruct((tiles, 2 * sh, ln), BF16),
                   jax.ShapeDtypeStruct((tiles, SUBLANES, 2 * sh), F32)],
        compiler_params=_cparams(("arbitrary",)),
        name="ssm_prep",
    )(lam_row, bt(b_re), bt(b_im), ct(c_re), ct(c_im))


def _inproj_kernel(first_keep, scale, x_ref, perm_ref, wqkv_ref, wu_ref, *rest):
    sample_in, (q_ref, k_ref, v_ref, u_ref, kl_ref, vl_ref, sample_out, slab_ref) = rest[:7], rest[7:]
    slabs = slab_ref.shape[0]
    prow = q_ref.shape[1]
    width = u_ref.shape[-1]
    xp = jnp.dot(perm_ref[...], x_ref[...].astype(BF16), preferred_element_type=F32).astype(BF16)

    def to_planes(val, out_ref):
        for r in range(PLANES):
            out_ref[r] = val[r * prow:(r + 1) * prow].astype(out_ref.dtype)

    sq_ref, skn_ref, svn_ref, skt_ref, svt_ref, stb_ref, stn_ref = sample_in
    to_planes(jnp.dot(xp, wqkv_ref[:, 0:width], preferred_element_type=F32) * scale, q_ref)
    sample_scores = _sample_scores(sq_ref, skn_ref, skt_ref)
    k = jnp.dot(xp, wqkv_ref[:, width:2 * width], preferred_element_type=F32)
    to_planes(k, k_ref)
    v = jnp.dot(xp, wqkv_ref[:, 2 * width:3 * width], preferred_element_type=F32)
    to_planes(v, v_ref)
    _sample_outputs(sample_scores, svn_ref, svt_ref, stb_ref, stn_ref, sample_out)
    to_planes(jnp.dot(xp, wu_ref[...], preferred_element_type=F32), u_ref)

    @pl.when(pl.program_id(1) >= first_keep)
    def _():
        def token_order_t(val):
            for r in range(PLANES):
                for c in range(slabs):
                    slab_ref[c, pl.ds(r, prow, stride=PLANES), :] = val[r * prow:(r + 1) * prow,
                                                                        c * LANES:(c + 1) * LANES]
            return jnp.concatenate([slab_ref[c] for c in range(slabs)], axis=1).T
        kl_ref[...] = token_order_t(k)
        vl_ref[...] = token_order_t(v)


def _inproj_prompt(x, w_bf, keep, sample):
    bsz, seq, d_model = x.shape
    width = w_bf.shape[1] // 6
    rows = seq // PLANES
    tile = ROW_TILE // 2
    prow = tile // PLANES
    first_keep = (seq - keep) // tile
    steps = seq // tile
    sample_specs, sample_out_spec, sample_out = _attention_sample_specs(*sample, lambda b, i: b * steps + i)
    assert bsz * steps == sample[0].shape[0] * (width // (SAMPLE_PAIRS * LANES)), "one sample unit per grid step"
    plane = lambda dt: jax.ShapeDtypeStruct((bsz, PLANES, rows, width), dt)
    plane_spec = pl.BlockSpec((None, PLANES, prow, width), lambda b, i: (b, 0, i, 0))
    last = jax.ShapeDtypeStruct((bsz, width, keep), F32)
    last_spec = pl.BlockSpec((None, width, tile), lambda b, i: (b, 0, jnp.maximum(i - first_keep, 0)))
    perm = np.zeros((tile, tile), np.float32)
    perm[np.arange(tile), PLANES * (np.arange(tile) % prow) + np.arange(tile) // prow] = 1.0
    return pl.pallas_call(
        functools.partial(_inproj_kernel, first_keep, HEAD_DIM ** -0.5 * LOG2E),
        grid=(bsz, steps),
        in_specs=[pl.BlockSpec((None, tile, d_model), lambda b, i: (b, i, 0)),
                  pl.BlockSpec((tile, tile), lambda b, i: (0, 0), pipeline_mode=ONCE),
                  pl.BlockSpec((d_model, 3 * width), lambda b, i: (0, 0), pipeline_mode=ONCE),
                  pl.BlockSpec((d_model, width), lambda b, i: (0, 4), pipeline_mode=ONCE)] + sample_specs,
        out_specs=[plane_spec, plane_spec, plane_spec, plane_spec, last_spec, last_spec, sample_out_spec],
        out_shape=[plane(F32), plane(F32), plane(F32), plane(BF16), last, last, sample_out],
        scratch_shapes=[pltpu.VMEM((width // LANES, tile, LANES), F32)],
        compiler_params=_cparams(("arbitrary", "arbitrary")),
        name="inproj_prompt",
    )(x, jnp.asarray(perm, BF16), w_bf, w_bf, sample[0], sample[0], sample[0], *sample[1:])


ATTN_UNROLL = 2


def _attn_kernel(q_ref, kp_ref, kc_ref, vp_ref, vc_ref, bias_ref, o_ref, acc_ref, m_ref, l_ref, s_ref, p_ref, mn_ref):
    sup = pl.program_id(2)
    is_a = lax.broadcasted_iota(jnp.int32, (BLK, LANES), 1) < HEAD_DIM

    def raw_scores(q, k):
        kb = k.astype(BF16)
        out = []
        for head in range(2):
            qm = jnp.where(is_a if head == 0 else jnp.logical_not(is_a), q, 0.0).astype(BF16)
            out.append(lax.dot_general(qm, kb, (((1,), (1,)), ((), ())), preferred_element_type=F32))
        return out

    def both(a, b):
        return jnp.where(is_a, a, b)

    def weights(s, table, m_old):
        m_new, p = [], []
        for h in range(2):
            sh = s[h] + bias_ref[h, table]
            mb = jnp.broadcast_to(jnp.max(sh, axis=1, keepdims=True), (BLK, LANES))
            mh = mb if m_old is None else jnp.maximum(m_old[h], mb)
            m_new.append(mh)
            p.append(jnp.exp2(sh - jnp.concatenate([mh, mh], axis=1)).astype(BF16))
        return m_new, p

    ones = jnp.ones((2 * BLK, LANES), BF16)

    def combine(m_new, p, v, state):
        vext = jnp.concatenate([v.astype(BF16), ones], axis=1)
        res = [jnp.dot(p[h], vext, preferred_element_type=F32) for h in range(2)]
        acc_new = both(res[0][:, :LANES], res[1][:, :LANES])
        l_new = [res[h][:, LANES:] for h in range(2)]
        if state is not None:
            alpha = [jnp.exp2(state[0][h] - m_new[h]) for h in range(2)]
            l_new = [alpha[h] * state[1][h] + l_new[h] for h in range(2)]
            acc_new = both(alpha[0], alpha[1]) * state[2] + acc_new
        return m_new, l_new, acc_new

    cat = lambda parts: parts[0] if len(parts) == 1 else jnp.concatenate(parts, axis=0)

    def gather(ref, planes, start, size, *lead):
        return cat([ref[(*lead, pln, pl.ds(start, size), slice(None))] for pln in planes])

    def pattern_stages(a, dil):
        npl = PLANES // dil
        qrows = BLK // npl
        per_res = BLK // qrows

        def geometry(grp, un):
            res, sub = divmod(grp * ATTN_UNROLL + un, per_res)
            off = sub * qrows
            table = 2 * a + (jnp.where(sup == 0, 1, 0) if sub == 0 else 0)
            return [res + dil * i for i in range(npl)], off, table

        def keys(prev_ref, cur_ref, planes, off):
            if off:
                return cat([cur_ref[pln, off - qrows:off + qrows, :] for pln in planes])
            return cat([part for pln in planes for part in (prev_ref[pln, BLK - qrows:BLK, :], cur_ref[pln, 0:qrows, :])])

        def stage_scores(grp, slot):
            stores = []
            for un in range(ATTN_UNROLL):
                planes, off, _ = geometry(grp, un)
                s = raw_scores(gather(q_ref, planes, off, qrows), keys(kp_ref, kc_ref, planes, off))
                stores += [(s_ref, (slot, un, h), s[h]) for h in range(2)]
            return stores

        def stage_softmax(grp, slot):
            stores = []
            for un in range(ATTN_UNROLL):
                planes, off, table = geometry(grp, un)
                m_old = None if a == 0 else [gather(m_ref, planes, off, qrows, h) for h in range(2)]
                m_new, p = weights([s_ref[slot, un, h] for h in range(2)], table, m_old)
                for h in range(2):
                    stores += [(p_ref, (slot, un, h), p[h]), (mn_ref, (slot, un, h), m_new[h])]
            return stores

        def stage_values(grp, slot):
            stores = []
            for un in range(ATTN_UNROLL):
                planes, off, _ = geometry(grp, un)
                old = None if a == 0 else ([gather(m_ref, planes, off, qrows, h) for h in range(2)],
                                           [gather(l_ref, planes, off, qrows, h) for h in range(2)],
                                           gather(acc_ref, planes, off, qrows))
                m_new, l_new, acc_new = combine([mn_ref[slot, un, h] for h in range(2)],
                                                [p_ref[slot, un, h] for h in range(2)],
                                                keys(vp_ref, vc_ref, planes, off), old)
                for i, pln in enumerate(planes):
                    part = slice(i * qrows, (i + 1) * qrows)
                    rows = (pln, pl.ds(off, qrows), slice(None))
                    for h in range(2):
                        stores += [(m_ref, (h,) + rows, m_new[h][part]), (l_ref, (h,) + rows, l_new[h][part])]
                    stores.append((acc_ref, rows, acc_new[part]))
            return stores

        return stage_scores, stage_softmax, stage_values

    groups = PLANES // ATTN_UNROLL
    work = [(stages, grp) for stages in (pattern_stages(a, dil) for a, dil in enumerate(DILATIONS))
            for grp in range(groups)]
    for step in range(len(work) + 2):
        stores = []
        for stage, lag in ((2, 2), (1, 1), (0, 0)):
            j = step - lag
            if 0 <= j < len(work):
                stages, grp = work[j]
                stores += stages[stage](grp, j % 2)
        for ref, idx, val in stores:
            ref[idx] = val
    for pln in range(PLANES):
        o_ref[pl.ds(pln, BLK, stride=PLANES), :] = acc_ref[pln] / both(l_ref[0, pln], l_ref[1, pln])


def _attention_prompt(q, k, v, bias):
    bsz, _, rows, width = q.shape
    pairs = width // LANES
    ntab = bias.shape[1]
    cur = pl.BlockSpec((None, PLANES, BLK, LANES), lambda b, h, s: (b, 0, s, h))
    prev = pl.BlockSpec((None, PLANES, BLK, LANES), lambda b, h, s: (b, 0, jnp.maximum(s - 1, 0), h))
    return pl.pallas_call(
        _attn_kernel,
        grid=(bsz, pairs, rows // BLK),
        in_specs=[cur, prev, cur, prev, cur,
                  pl.BlockSpec((2, ntab, BLK, 2 * BLK), lambda b, h, s: (h, 0, 0, 0))],
        out_specs=pl.BlockSpec((None, PLANES * BLK, LANES), lambda b, h, s: (b, s, h)),
        out_shape=jax.ShapeDtypeStruct((bsz, PLANES * rows, width), F32),
        scratch_shapes=[pltpu.VMEM((PLANES, BLK, LANES), F32),
                        pltpu.VMEM((2, PLANES, BLK, LANES), F32),
                        pltpu.VMEM((2, PLANES, BLK, LANES), F32),
                        pltpu.VMEM((2, ATTN_UNROLL, 2, BLK, 2 * BLK), F32),
                        pltpu.VMEM((2, ATTN_UNROLL, 2, BLK, 2 * BLK), BF16),
                        pltpu.VMEM((2, ATTN_UNROLL, 2, BLK, LANES), F32)],
        compiler_params=_cparams(("arbitrary", "arbitrary", "arbitrary")),
        name="attention_prompt",
    )(q, k, k, v, v, bias)


SSM_COLS = 4


def _ssm_kernel(u_ref, ws_ref, wx_ref, mlag_ref, coef_ref, d_ref, y_ref, sre_ref, sim_ref,
                ucat_ref, s_ref, x_ref, m_ref):
    chunks = u_ref.shape[1]

    @pl.when(pl.program_id(1) == 0)
    def _():
        zero_blk = jnp.zeros((LANES, LANES), BF16)
        for s in range(PLANES):
            for t in range(PLANES):
                blk = mlag_ref[(t - s) * LANES:(t - s + 1) * LANES, :] if t >= s else zero_blk
                m_ref[s * LANES:(s + 1) * LANES, t * LANES:(t + 1) * LANES] = blk

    for s in range(PLANES):
        ucat_ref[:, s * LANES:(s + 1) * LANES] = u_ref[s]
    s_ref[...] = jnp.dot(ucat_ref[...], ws_ref[...], preferred_element_type=F32)
    a_re = coef_ref[1:2, 0:STATE_HALF]
    a_im = coef_ref[1:2, STATE_HALF:2 * STATE_HALF]

    def step(c, carry):
        xr, xi = carry
        x_ref[pl.ds(c, 1), 0:STATE_HALF] = xr
        x_ref[pl.ds(c, 1), STATE_HALF:2 * STATE_HALF] = xi
        sr = s_ref[pl.ds(c, 1), 0:STATE_HALF]
        si = s_ref[pl.ds(c, 1), STATE_HALF:2 * STATE_HALF]
        return a_re * xr - a_im * xi + sr, a_re * xi + a_im * xr + si

    zero = jnp.zeros((1, STATE_HALF), F32)
    xr, xi = lax.fori_loop(0, chunks, step, (zero, zero))
    sre_ref[...] = xr
    sim_ref[...] = xi

    xb = x_ref[...].astype(BF16)
    d = d_ref[...]
    for g in range(PLANES // SSM_COLS):
        cols = slice(g * SSM_COLS * LANES, (g + 1) * SSM_COLS * LANES)
        used = (g + 1) * SSM_COLS * LANES
        y = jnp.dot(ucat_ref[:, :used], m_ref[:used, cols], preferred_element_type=F32)
        y = y + jnp.dot(xb, wx_ref[:, cols], preferred_element_type=F32)
        for t in range(SSM_COLS):
            tok = g * SSM_COLS + t
            y_ref[pl.ds(tok, chunks, stride=PLANES), :] = (y[:, t * LANES:(t + 1) * LANES]
                                                          + d * u_ref[tok].astype(F32))


def _ssm_prompt(u, ws, wx, mlag, coef, d_skip):
    bsz, _, chunks, width = u.shape
    tiles = width // LANES
    sh = STATE_HALF
    state = jax.ShapeDtypeStruct((bsz, tiles, 1, sh), F32)
    state_spec = pl.BlockSpec((None, None, 1, sh), lambda t, b: (b, t, 0, 0))
    per_tile = lambda r, c: pl.BlockSpec((None, r, c), lambda t, b: (t, 0, 0))
    return pl.pallas_call(
        _ssm_kernel,
        grid=(tiles, bsz),
        in_specs=[pl.BlockSpec((None, PLANES, chunks, LANES), lambda t, b: (b, 0, 0, t)),
                  per_tile(PLANES * LANES, 2 * sh), per_tile(2 * sh, PLANES * LANES),
                  per_tile(PLANES * LANES, LANES), per_tile(SUBLANES, 2 * sh),
                  pl.BlockSpec((1, LANES), lambda t, b: (0, t))],
        out_specs=[pl.BlockSpec((None, PLANES * chunks, LANES), lambda t, b: (b, 0, t)),
                   state_spec, state_spec],
        out_shape=[jax.ShapeDtypeStruct((bsz, PLANES * chunks, width), F32), state, state],
        scratch_shapes=[pltpu.VMEM((chunks, PLANES * LANES), BF16),
                        pltpu.VMEM((chunks, 2 * sh), F32),
                        pltpu.VMEM((chunks, 2 * sh), F32),
                        pltpu.VMEM((PLANES * LANES, PLANES * LANES), BF16)],
        compiler_params=_cparams(("arbitrary", "arbitrary")),
        name="ssm_prompt",
    )(u, ws, wx, mlag, coef, d_skip)


EPILOGUE_ROWS = 256


def _epilogue_kernel(alpha, attn_ref, y_ref, x_ref, wga_ref, wgs_ref, wglu_ref, bglu_ref, wo_ref, bo_ref,
                     g_ref, b_ref, o_ref):
    rows = x_ref.shape[0]
    part = min(rows, EPILOGUE_ROWS)
    for r0 in range(0, rows, part):
        rs = slice(r0, r0 + part)
        x = x_ref[rs, :]
        xb = x.astype(BF16)
        residual = alpha * x + bo_ref[...]
        g_attn = jnp.dot(xb, wga_ref[...], preferred_element_type=F32)
        br_a = (attn_ref[rs, :] * jax.nn.silu(g_attn)).astype(BF16)
        g_ssm = jnp.dot(xb, wgs_ref[...], preferred_element_type=F32)
        z = jax.nn.gelu(y_ref[rs, :])
        gate = jax.nn.sigmoid(jnp.dot(z.astype(BF16), wglu_ref[...], preferred_element_type=F32) + bglu_ref[...])
        br_s = (z * gate * jax.nn.silu(g_ssm)).astype(BF16)
        t = residual + jnp.dot(jnp.concatenate([br_a, br_s], axis=1), wo_ref[...], preferred_element_type=F32)
        mu = jnp.mean(t, axis=-1, keepdims=True)
        var = jnp.mean(jnp.square(t - mu), axis=-1, keepdims=True)
        o_ref[rs, :] = (t - mu) * lax.rsqrt(var + LN_EPS) * g_ref[...] + b_ref[...]


def _epilogue(alpha, name, attn, y, x, w_bf, weights, tile):
    lead = attn.shape[:-2]
    rows, width = attn.shape[-2:]
    d_model = x.shape[-1]
    grid = lead + (rows // tile,)
    none = (None,) * len(lead)
    tiled = lambda w: pl.BlockSpec(none + (tile, w), lambda *g: g + (0,))
    const = lambda shape, col=0: pl.BlockSpec(shape, lambda *g: (0, col), pipeline_mode=ONCE)
    return pl.pallas_call(
        functools.partial(_epilogue_kernel, alpha),
        grid=grid,
        in_specs=[tiled(width), tiled(width), tiled(d_model),
                  const((d_model, width), 3), const((d_model, width), 5),
                  const((width, width)), const((1, width)), const((2 * width, d_model)), const((1, d_model)),
                  const((1, d_model)), const((1, d_model))],
        out_specs=tiled(d_model),
        out_shape=jax.ShapeDtypeStruct(x.shape, F32),
        compiler_params=_cparams(("arbitrary",) * len(grid)),
        name=name,
    )(attn, y, x, w_bf, w_bf, *weights)


def _inproj_sample_kernel(scale, x_ref, w_ref, o_ref):
    acc = jnp.dot(x_ref[...].astype(BF16), w_ref[...], preferred_element_type=F32)
    o_ref[...] = acc * jnp.where(pl.program_id(0) == 0, scale, 1.0)


def _inproj_sample(x, w_bf):
    rows, d_model = x.shape
    width = w_bf.shape[1] // 6
    return pl.pallas_call(
        functools.partial(_inproj_sample_kernel, HEAD_DIM ** -0.5),
        grid=(4,),
        in_specs=[pl.BlockSpec((rows, d_model), lambda j: (0, 0)),
                  pl.BlockSpec((d_model, width), lambda j: (0, j + j // 3))],
        out_specs=pl.BlockSpec((rows, width), lambda j: (0, j)),
        out_shape=jax.ShapeDtypeStruct((rows, 4 * width), F32),
        compiler_params=_cparams(("arbitrary",)),
        name="inproj_sample",
    )(x, w_bf)


SAMPLE_PAIRS = 4


_NT = (((1,), (1,)), ((), ()))


def _sample_scores(q_ref, kn_ref, kt_ref):
    s_len = q_ref.shape[0]
    buf = kt_ref.shape[2]
    is_a = lax.broadcasted_iota(jnp.int32, (s_len, LANES), 1) < HEAD_DIM
    out = []
    for pp in range(SAMPLE_PAIRS):
        lanes = slice(pp * LANES, (pp + 1) * LANES)
        qp = q_ref[:, lanes]
        q2 = jnp.concatenate([jnp.where(is_a, qp, 0.0), jnp.where(is_a, 0.0, qp)], axis=0)
        kt = kt_ref[2 * pp:2 * pp + 2].reshape(2 * HEAD_DIM, buf).astype(BF16)
        kn = kn_ref[:, lanes]
        s_new = [jnp.sum(q2 * kn[j:j + 1, :], axis=1, keepdims=True) for j in range(s_len)]
        out.append((jnp.dot(q2.astype(BF16), kt, preferred_element_type=F32), s_new))
    return out


def _sample_outputs(scores, vn_ref, vt_ref, tb_ref, tn_ref, o_ref):
    s_len = vn_ref.shape[0]
    buf = vt_ref.shape[2]
    npat = tb_ref.shape[0]
    is_a = lax.broadcasted_iota(jnp.int32, (s_len, LANES), 1) < HEAD_DIM
    for pp, (s_buf, s_new) in enumerate(scores):
        lanes = slice(pp * LANES, (pp + 1) * LANES)
        rows = slice(pp * 2 * s_len, (pp + 1) * 2 * s_len)
        vt = vt_ref[2 * pp:2 * pp + 2].reshape(2 * HEAD_DIM, buf).astype(BF16)
        vn = vn_ref[:, lanes]
        z_buf = [s_buf + tb_ref[a, rows, :] for a in range(npat)]
        z_new = [[s_new[j] + tn_ref[a, rows, j:j + 1] for a in range(npat)] for j in range(s_len)]
        top = functools.reduce(jnp.maximum, [jnp.max(z, axis=1, keepdims=True) for z in z_buf]
                               + [z for zs in z_new for z in zs])
        w_buf = functools.reduce(jnp.add, [jnp.exp(z - top) for z in z_buf])
        w_new = [functools.reduce(jnp.add, [jnp.exp(z - top) for z in zs]) for zs in z_new]
        den = jnp.sum(w_buf, axis=1, keepdims=True) + functools.reduce(jnp.add, w_new)
        o = lax.dot_general(vt, w_buf.astype(BF16), _NT, preferred_element_type=F32).T
        o = (o + functools.reduce(jnp.add, [w_new[j] * vn[j:j + 1, :] for j in range(s_len)])) / den
        o_ref[:, lanes] = jnp.where(is_a, o[0:s_len], o[s_len:2 * s_len])


def _attention_sample_specs(h, kt, vt, tb, tn, unit_of):
    bsz, s_len, _ = h.shape
    heads, _, buf = kt.shape[1:]
    width = heads * HEAD_DIM
    gw = SAMPLE_PAIRS * LANES
    per_w = width // gw
    trows = SAMPLE_PAIRS * 2 * s_len
    where = lambda *g: divmod(unit_of(*g), per_w)
    new = lambda c: pl.BlockSpec((None, s_len, gw), lambda *g: (where(*g)[0], 0, c * per_w + where(*g)[1]))
    cache = pl.BlockSpec((None, 2 * SAMPLE_PAIRS, HEAD_DIM, buf), lambda *g: (where(*g)[0], where(*g)[1], 0, 0))
    specs = [new(0), new(1), new(2), cache, cache,
             pl.BlockSpec((tb.shape[0], trows, buf), lambda *g: (0, where(*g)[1], 0)),
             pl.BlockSpec((tn.shape[0], trows, LANES), lambda *g: (0, where(*g)[1], 0))]
    out_spec = pl.BlockSpec((None, s_len, gw), lambda *g: (where(*g)[0], 0, where(*g)[1]))
    return specs, out_spec, jax.ShapeDtypeStruct((bsz, s_len, width), F32)


def _ssm_sample_kernel(s_len, u_ref, b0_ref, c0_ref, coef_ref, d_ref, x0r_ref, x0i_ref,
                       y_ref, sre_ref, sim_ref, bu_ref, xs_ref):
    bsz = x0r_ref.shape[0]
    bu = jnp.dot(u_ref[...].astype(BF16), b0_ref[...], preferred_element_type=F32)
    slabs = bu_ref.shape[0]
    for c in range(slabs):
        bu_ref[c] = bu[:, c * LANES:(c + 1) * LANES]
    a_re = coef_ref[0:1, 0:STATE_HALF]
    a_im = coef_ref[0:1, STATE_HALF:2 * STATE_HALF]
    half = slabs // 2
    xr, xi = x0r_ref[...], x0i_ref[...]
    for s in range(s_len):
        step = lambda c: bu_ref[c, pl.ds(s, bsz, stride=s_len), :]
        br = jnp.concatenate([step(c) for c in range(half)], axis=1)
        bi = jnp.concatenate([step(c) for c in range(half, slabs)], axis=1)
        xr, xi = a_re * xr - a_im * xi + br, a_re * xi + a_im * xr + bi
        xs_ref[s * bsz:(s + 1) * bsz, :] = jnp.concatenate([xr, xi], axis=1)
    sre_ref[...] = xr
    sim_ref[...] = xi
    y = jnp.dot(xs_ref[...].astype(BF16), c0_ref[...], preferred_element_type=F32)
    d = d_ref[...]
    for s in range(s_len):
        y_ref[pl.ds(s, bsz, stride=s_len), :] = (y[s * bsz:(s + 1) * bsz, :]
                                                 + d * u_ref[pl.ds(s, bsz, stride=s_len), :])


def _ssm_sample(h, ws, c0, coef, d_skip, x0_re, x0_im, s_len):
    rows = h.shape[0]
    bsz = rows // s_len
    tiles = ws.shape[0]
    sh = STATE_HALF
    u_col0 = 3 * (h.shape[1] // 4) // LANES
    st_spec = pl.BlockSpec((bsz, sh), lambda t: (0, t))
    state = jax.ShapeDtypeStruct((bsz, tiles * sh), F32)
    return pl.pallas_call(
        functools.partial(_ssm_sample_kernel, s_len),
        grid=(tiles,),
        in_specs=[pl.BlockSpec((rows, LANES), lambda t: (0, u_col0 + t)),
                  pl.BlockSpec((None, LANES, 2 * sh), lambda t: (t, PLANES - 1, 0)),
                  pl.BlockSpec((None, 2 * sh, LANES), lambda t: (t, 0, 0)),
                  pl.BlockSpec((None, SUBLANES, 2 * sh), lambda t: (t, 0, 0)),
                  pl.BlockSpec((1, LANES), lambda t: (0, t)),
                  st_spec, st_spec],
        out_specs=[pl.BlockSpec((rows, LANES), lambda t: (0, t)), st_spec, st_spec],
        out_shape=[jax.ShapeDtypeStruct((rows, tiles * LANES), F32), state, state],
        scratch_shapes=[pltpu.VMEM((2 * sh // LANES, rows, LANES), F32),
                        pltpu.VMEM((rows, 2 * sh), F32)],
        compiler_params=_cparams(("arbitrary",)),
        name="ssm_sample",
    )(h, ws, c0, coef, d_skip, x0_re, x0_im)


def kernel(x_prompt, x_sample, cache_k, cache_v, state_ssm_re, state_ssm_im, w_in, w_out, b_out, rel_bias,
           lam_re, lam_im, log_dt, b_re, b_im, c_re, c_im, d_skip, w_glu, b_glu, ln_g, ln_b):
    depth = w_in.shape[0]
    assert depth == 1, "one layer per step"
    bsz, seq, d_model = x_prompt.shape
    dbsz, s_len, _ = x_sample.shape
    buf, heads = cache_k.shape[2], cache_k.shape[3]
    width = heads * HEAD_DIM
    groups, nstate = lam_re.shape[1], lam_re.shape[2]
    keep = min(MAX_DISTANCE, seq)
    assert seq % (PLANES * BLK) == 0 and seq >= 2 * PLANES * BLK and keep % ROW_TILE == 0
    assert buf == KPER * max(DILATIONS) and s_len <= min(DILATIONS[:-1])
    assert nstate == SSM_STATE and width == groups * SSM_CH and heads % (2 * SAMPLE_PAIRS) == 0
    alpha = (2 * depth) ** 0.25
    npat = len(DILATIONS)

    w_bf = w_in[0].astype(BF16)
    row = lambda v: v.reshape(1, -1)
    weights = (w_glu[0].astype(BF16), row(b_glu[0]), w_out[0].astype(BF16), row(b_out[0]),
               row(ln_g[0]), row(ln_b[0]))
    d_row = row(d_skip[0])

    rbt = rel_bias.T
    ptab = _bias_tables(jnp.asarray(_prompt_bucket_tables().reshape(1, -1)), rbt, BLK * 2 * BLK, LOG2E)
    ptab = ptab.reshape(heads, 2 * npat, BLK, 2 * BLK)
    key_w = buf + 2 * LANES
    stab = _bias_tables(jnp.asarray(_sample_bucket_tables(buf, s_len, key_w)), rbt, npat * s_len * key_w)
    stab = jnp.transpose(stab.reshape(heads, npat, s_len, key_w), (1, 0, 2, 3)).reshape(npat, heads * s_len, key_w)
    stab_buf, stab_new = stab[:, :, :buf], stab[:, :, buf:buf + LANES]
    ws, wx, m_intra, c0, coef = _ssm_prep(lam_re[0], lam_im[0], log_dt[0], b_re[0], b_im[0], c_re[0], c_im[0])

    xs = x_sample.reshape(dbsz * s_len, d_model)
    hs = _inproj_sample(xs, w_bf)
    pos_minor = lambda c: jnp.transpose(c[0], (0, 2, 3, 1))
    sample = (hs.reshape(dbsz, s_len, -1), pos_minor(cache_k), pos_minor(cache_v), stab_buf, stab_new)

    q, k, v, u, kl_t, vl_t, attn_s = _inproj_prompt(x_prompt, w_bf, keep, sample)
    attn = _attention_prompt(q, k, v, ptab)
    y, sre_p, sim_p = _ssm_prompt(u, ws, wx, m_intra, coef, d_row)
    y_prompt = _epilogue(alpha, "epilogue_prompt", attn, y, x_prompt, w_bf, weights, ROW_TILE // 2)
    last = lambda t: jnp.transpose(t.reshape(bsz, heads, HEAD_DIM, keep), (0, 3, 1, 2))[None]
    st_shape = (1, bsz, groups, nstate)

    y_s, sre_s, sim_s = _ssm_sample(hs, ws, c0, coef, d_row,
                                    state_ssm_re[0].astype(F32).reshape(dbsz, groups * nstate),
                                    state_ssm_im[0].astype(F32).reshape(dbsz, groups * nstate), s_len)
    y_sample = _epilogue(alpha, "epilogue_sample", attn_s.reshape(dbsz * s_len, width), y_s, xs, w_bf, weights,
                         dbsz * s_len)
    new_shape = (1, dbsz, s_len, heads, HEAD_DIM)
    sst_shape = (1, dbsz, groups, nstate)
    return (y_prompt, y_sample.reshape(dbsz, s_len, d_model), last(kl_t), last(vl_t),
            sre_p.reshape(st_shape), sim_p.reshape(st_shape),
            hs[:, width:2 * width].reshape(new_shape), hs[:, 2 * width:3 * width].reshape(new_shape),
            sre_s.reshape(sst_shape), sim_s.reshape(sst_shape))
```

```python
import functools
import math

import jax
import jax.numpy as jnp
import numpy as np
from jax import lax
from jax.experimental import pallas as pl
from jax.experimental.pallas import tpu as pltpu

F32 = jnp.float32
BF16 = jnp.bfloat16

HEAD_DIM = 64
SSM_CH = 16
SSM_STATE = 64
NUM_BUCKETS = 32
MAX_DISTANCE = 2048
KPER = 128
BLK = 128
DILATIONS = (16, 4, 1)
LN_EPS = 1e-5
NEG = -1e30
LOG2E = math.log2(math.e)

LANES = 128
SUBLANES = 8
PLANES = 16
GROUPS_PER_TILE = LANES // SSM_CH
STATE_HALF = GROUPS_PER_TILE * SSM_STATE
VMEM_LIMIT = 56 * 1024 * 1024
ROW_TILE = 512
ONCE = pl.Buffered(1)


def _cparams(sem, vmem=VMEM_LIMIT):
    return pltpu.CompilerParams(dimension_semantics=sem, vmem_limit_bytes=vmem)


def _bucket_np(dist):
    exact = NUM_BUCKETS // 2
    d_f = np.maximum(dist, 1).astype(np.float32)
    large = exact + (np.log(d_f / np.float32(exact)) / np.float32(math.log(MAX_DISTANCE / exact))
                     * np.float32(NUM_BUCKETS - exact)).astype(np.int32)
    large = np.minimum(large, NUM_BUCKETS - 1)
    return np.where(dist < exact, dist, large).astype(np.int32)


def _prompt_rel(dil):
    i = np.arange(BLK)[:, None]
    j = np.arange(2 * BLK)[None, :]
    npl = PLANES // dil
    qrows = BLK // npl
    pq, ml = i // qrows, i % qrows
    pk, jl = j // (2 * qrows), j % (2 * qrows)
    return npl * (ml - jl + qrows) + (pq - pk), np.broadcast_to(jl < qrows, (BLK, 2 * BLK))


def _prompt_bucket_tables():
    tabs = []
    for dil in DILATIONS:
        rel, earlier = _prompt_rel(dil)
        for first in (False, True):
            valid = (rel >= 0) & (rel <= KPER) & np.logical_not(earlier & first)
            tabs.append(np.where(valid, _bucket_np(np.clip(rel, 0, KPER) * dil), NUM_BUCKETS))
    return np.stack(tabs).reshape(len(DILATIONS) * 2, BLK * 2 * BLK).astype(np.int32)


def _sample_bucket_tables(buf, s_len, width):
    pos = np.arange(buf + s_len)
    tabs = np.full((len(DILATIONS), s_len, width), NUM_BUCKETS, np.int32)
    for a, dil in enumerate(DILATIONS):
        for s in range(s_len):
            dist = buf + s - pos
            valid = (dist >= 0) & (dist % dil == 0) & (dist // dil <= KPER)
            tabs[a, s, :len(pos)] = np.where(valid, _bucket_np(np.maximum(dist, 0)), NUM_BUCKETS)
    return tabs.reshape(1, -1)


def _bias_kernel(scale, idx_ref, rbt_ref, o_ref):
    idx = idx_ref[...]
    onehot = (lax.broadcasted_iota(jnp.int32, (NUM_BUCKETS, idx.shape[1]), 0) == idx).astype(BF16)
    rb = rbt_ref[...]
    n_heads = rb.shape[0]
    hi = rb.astype(BF16)
    lo = (rb - hi.astype(F32)).astype(BF16)
    res = jnp.dot(jnp.concatenate([hi, lo], axis=0), onehot, preferred_element_type=F32)
    tab = res[:n_heads] + res[n_heads:]
    o_ref[...] = jnp.where(idx < NUM_BUCKETS, tab * scale, NEG)


def _bias_tables(idx, rel_bias_t, chunk, scale=1.0):
    n_heads = rel_bias_t.shape[0]
    total = idx.shape[1]
    return pl.pallas_call(
        functools.partial(_bias_kernel, scale),
        grid=(total // chunk,),
        in_specs=[pl.BlockSpec((1, chunk), lambda c: (0, c)),
                  pl.BlockSpec((n_heads, NUM_BUCKETS), lambda c: (0, 0))],
        out_specs=pl.BlockSpec((n_heads, chunk), lambda c: (0, c)),
        out_shape=jax.ShapeDtypeStruct((n_heads, total), F32),
        compiler_params=_cparams(("arbitrary",)),
        name="bias_tables",
    )(idx, rel_bias_t)


def _discretize(lr, li, ldt):
    lr = jnp.minimum(lr, -1e-4)
    dt = jnp.exp(ldt)
    mag = jnp.exp(lr * dt)
    ab_re, ab_im = mag * jnp.cos(li * dt), mag * jnp.sin(li * dt)
    den = lr * lr + li * li
    inv_re, inv_im = lr / den, -li / den
    n_re, n_im = ab_re - 1.0, ab_im
    cf_re = n_re * inv_re - n_im * inv_im
    cf_im = n_re * inv_im + n_im * inv_re
    return ab_re, ab_im, cf_re, cf_im


def _ssm_prep_kernel(lam_row_ref, bt_re_ref, bt_im_ref, ct_re_ref, ct_im_ref,
                     ws_ref, wx_ref, m_ref, c0_ref, coef_ref):
    row = lam_row_ref[0]
    ab_re, ab_im, cf_re, cf_im = _discretize(row[0:1], row[1:2], row[2:3])
    bt_re, bt_im = bt_re_ref[0], bt_im_ref[0]
    bb_re = cf_re * bt_re - cf_im * bt_im
    bb_im = cf_re * bt_im + cf_im * bt_re
    ct_re, ct_im = ct_re_ref[0], ct_im_ref[0]
    c0 = jnp.concatenate([ct_re, -ct_im], axis=0)
    c0b = c0.astype(BF16)
    c0_ref[0] = c0b

    powers = [(jnp.ones_like(ab_re), jnp.zeros_like(ab_re))]
    for _ in range(PLANES):
        pr, pi = powers[-1]
        powers.append((pr * ab_re - pi * ab_im, pr * ab_im + pi * ab_re))
    cols = jnp.concatenate([part for pw in powers[1:] for part in pw], axis=0).T
    for lag in range(PLANES):
        pr, pi = powers[lag]
        w = jnp.concatenate([pr * bb_re - pi * bb_im, pr * bb_im + pi * bb_re], axis=1).astype(BF16)
        s = PLANES - 1 - lag
        ws_ref[0, s * LANES:(s + 1) * LANES, :] = w
        m_ref[0, lag * LANES:(lag + 1) * LANES, :] = jnp.dot(w, c0b, preferred_element_type=F32).astype(BF16)
        qr, qi = cols[:, 2 * lag:2 * lag + 1], cols[:, 2 * lag + 1:2 * lag + 2]
        wx_ref[0, 0:STATE_HALF, lag * LANES:(lag + 1) * LANES] = (ct_re * qr - ct_im * qi).astype(BF16)
        wx_ref[0, STATE_HALF:2 * STATE_HALF, lag * LANES:(lag + 1) * LANES] = (-(ct_re * qi + ct_im * qr)).astype(BF16)
    coef_ref[0] = jnp.concatenate([
        jnp.concatenate([ab_re, ab_im], axis=1),
        jnp.concatenate(powers[PLANES], axis=1),
        jnp.zeros((SUBLANES - 2, 2 * STATE_HALF), F32)], axis=0)


def _ssm_prep(lam_re, lam_im, log_dt, b_re, b_im, c_re, c_im):
    groups, n = lam_re.shape
    tiles = groups // GROUPS_PER_TILE
    gpt = GROUPS_PER_TILE
    eye = jnp.eye(gpt, dtype=F32)

    def rows(v):
        return v.reshape(tiles, gpt * n)

    ldt = jnp.broadcast_to(log_dt[:, None], (groups, n))
    lam_row = jnp.stack([rows(lam_re), rows(lam_im), rows(ldt)], axis=1)

    def bt(b):
        b = jnp.transpose(b.reshape(tiles, gpt, n, SSM_CH), (0, 1, 3, 2))
        return (b[:, :, :, None, :] * eye[None, :, None, :, None]).reshape(tiles, gpt * SSM_CH, gpt * n)

    def ct(c):
        c = jnp.transpose(c.reshape(tiles, gpt, SSM_CH, n), (0, 1, 3, 2))
        return (c[:, :, :, None, :] * eye[None, :, None, :, None]).reshape(tiles, gpt * n, gpt * SSM_CH)

    sh, ln = STATE_HALF, LANES
    tile3 = lambda a, b: pl.BlockSpec((1, a, b), lambda t: (t, 0, 0))
    return pl.pallas_call(
        _ssm_prep_kernel,
        grid=(tiles,),
        in_specs=[tile3(3, sh), tile3(ln, sh), tile3(ln, sh), tile3(sh, ln), tile3(sh, ln)],
        out_specs=[tile3(PLANES * ln, 2 * sh), tile3(2 * sh, PLANES * ln), tile3(PLANES * ln, ln),
                   tile3(2 * sh, ln), tile3(SUBLANES, 2 * sh)],
        out_shape=[jax.ShapeDtypeStruct((tiles, PLANES * ln, 2 * sh), BF16),
                   jax.ShapeDtypeStruct((tiles, 2 * sh, PLANES * ln), BF16),
                   jax.ShapeDtypeStruct((tiles, PLANES * ln, ln), BF16),
                   jax.ShapeDtypeStruct((tiles, 2 * sh, ln), BF16),
                   jax.ShapeDtypeStruct((tiles, SUBLANES, 2 * sh), F32)],
        compiler_params=_cparams(("arbitrary",)),
        name="ssm_prep",
    )(lam_row, bt(b_re), bt(b_im), ct(c_re), ct(c_im))


def _inproj_kernel(first_keep, scale, x_ref, perm_ref, wqkv_ref, wu_ref, *rest):
    sample_in, (q_ref, k_ref, v_ref, u_ref, kl_ref, vl_ref, sample_out, slab_ref) = rest[:7], rest[7:]
    slabs = slab_ref.shape[0]
    prow = q_ref.shape[1]
    width = u_ref.shape[-1]
    xp = jnp.dot(perm_ref[...], x_ref[...].astype(BF16), preferred_element_type=F32).astype(BF16)

    def to_planes(val, out_ref):
        for r in range(PLANES):
            out_ref[r] = val[r * prow:(r + 1) * prow].astype(out_ref.dtype)

    sq_ref, skn_ref, svn_ref, skt_ref, svt_ref, stb_ref, stn_ref = sample_in
    to_planes(jnp.dot(xp, wqkv_ref[:, 0:width], preferred_element_type=F32) * scale, q_ref)
    sample_scores = _sample_scores(sq_ref, skn_ref, skt_ref)
    k = jnp.dot(xp, wqkv_ref[:, width:2 * width], preferred_element_type=F32)
    to_planes(k, k_ref)
    v = jnp.dot(xp, wqkv_ref[:, 2 * width:3 * width], preferred_element_type=F32)
    to_planes(v, v_ref)
    _sample_outputs(sample_scores, svn_ref, svt_ref, stb_ref, stn_ref, sample_out)
    to_planes(jnp.dot(xp, wu_ref[...], preferred_element_type=F32), u_ref)

    @pl.when(pl.program_id(1) >= first_keep)
    def _():
        def token_order_t(val):
            for r in range(PLANES):
                for c in range(slabs):
                    slab_ref[c, pl.ds(r, prow, stride=PLANES), :] = val[r * prow:(r + 1) * prow,
                                                                        c * LANES:(c + 1) * LANES]
            return jnp.concatenate([slab_ref[c] for c in range(slabs)], axis=1).T
        kl_ref[...] = token_order_t(k)
        vl_ref[...] = token_order_t(v)


def _inproj_prompt(x, w_bf, keep, sample):
    bsz, seq, d_model = x.shape
    width = w_bf.shape[1] // 4
    rows = seq // PLANES
    tile = ROW_TILE // 2
    prow = tile // PLANES
    first_keep = (seq - keep) // tile
    steps = seq // tile
    sample_specs, sample_out_spec, sample_out = _attention_sample_specs(*sample, lambda b, i: b * steps + i)
    assert bsz * steps == sample[0].shape[0] * (width // (SAMPLE_PAIRS * LANES)), "one sample unit per grid step"
    plane = lambda dt: jax.ShapeDtypeStruct((bsz, PLANES, rows, width), dt)
    plane_spec = pl.BlockSpec((None, PLANES, prow, width), lambda b, i: (b, 0, i, 0))
    last = jax.ShapeDtypeStruct((bsz, width, keep), F32)
    last_spec = pl.BlockSpec((None, width, tile), lambda b, i: (b, 0, jnp.maximum(i - first_keep, 0)))
    perm = np.zeros((tile, tile), np.float32)
    perm[np.arange(tile), PLANES * (np.arange(tile) % prow) + np.arange(tile) // prow] = 1.0
    return pl.pallas_call(
        functools.partial(_inproj_kernel, first_keep, HEAD_DIM ** -0.5 * LOG2E),
        grid=(bsz, steps),
        in_specs=[pl.BlockSpec((None, tile, d_model), lambda b, i: (b, i, 0)),
                  pl.BlockSpec((tile, tile), lambda b, i: (0, 0), pipeline_mode=ONCE),
                  pl.BlockSpec((d_model, 3 * width), lambda b, i: (0, 0), pipeline_mode=ONCE),
                  pl.BlockSpec((d_model, width), lambda b, i: (0, 3), pipeline_mode=ONCE)] + sample_specs,
        out_specs=[plane_spec, plane_spec, plane_spec, plane_spec, last_spec, last_spec, sample_out_spec],
        out_shape=[plane(F32), plane(F32), plane(F32), plane(BF16), last, last, sample_out],
        scratch_shapes=[pltpu.VMEM((width // LANES, tile, LANES), F32)],
        compiler_params=_cparams(("arbitrary", "arbitrary")),
        name="inproj_prompt",
    )(x, jnp.asarray(perm, BF16), w_bf, w_bf, sample[0], sample[0], sample[0], *sample[1:])


ATTN_UNROLL = 2


def _attn_group_rows(dil, grp):
    npl = PLANES // dil
    qrows = BLK // npl
    out = set()
    for un in range(ATTN_UNROLL):
        res, sub = divmod(grp * ATTN_UNROLL + un, BLK // qrows)
        out |= {(res + dil * i, r) for i in range(npl) for r in range(sub * qrows, (sub + 1) * qrows)}
    return out


def _attn_kernel(n_cast, q_ref, kp_ref, kc_ref, vp_ref, vc_ref, bias_ref, *rest):
    cast_in, o_ref, cast_out = rest[:n_cast], rest[n_cast], rest[n_cast + 1:2 * n_cast + 1]
    acc_ref, m_ref, l_ref, s_ref, p_ref, mn_ref = rest[2 * n_cast + 1:]
    for src, dst in zip(cast_in, cast_out):
        dst[...] = src[...].astype(BF16)
    sup = pl.program_id(2)
    is_a = lax.broadcasted_iota(jnp.int32, (BLK, LANES), 1) < HEAD_DIM

    def raw_scores(q, k):
        kb = k.astype(BF16)
        out = []
        for head in range(2):
            qm = jnp.where(is_a if head == 0 else jnp.logical_not(is_a), q, 0.0).astype(BF16)
            out.append(lax.dot_general(qm, kb, (((1,), (1,)), ((), ())), preferred_element_type=F32))
        return out

    def both(a, b):
        return jnp.where(is_a, a, b)

    def weights(s, table, m_old):
        m_new, p = [], []
        for h in range(2):
            sh = s[h] + bias_ref[h, table]
            mb = jnp.broadcast_to(jnp.max(sh, axis=1, keepdims=True), (BLK, LANES))
            mh = mb if m_old is None else jnp.maximum(m_old[h], mb)
            m_new.append(mh)
            p.append(jnp.exp2(sh - jnp.concatenate([mh, mh], axis=1)).astype(BF16))
        return m_new, p

    ones = jnp.ones((2 * BLK, LANES), BF16)

    def combine(m_new, p, v, state):
        vext = jnp.concatenate([v.astype(BF16), ones], axis=1)
        res = [jnp.dot(p[h], vext, preferred_element_type=F32) for h in range(2)]
        acc_new = both(res[0][:, :LANES], res[1][:, :LANES])
        l_new = [res[h][:, LANES:] for h in range(2)]
        if state is not None:
            alpha = [jnp.exp2(state[0][h] - m_new[h]) for h in range(2)]
            l_new = [alpha[h] * state[1][h] + l_new[h] for h in range(2)]
            acc_new = both(alpha[0], alpha[1]) * state[2] + acc_new
        return m_new, l_new, acc_new

    cat = lambda parts: parts[0] if len(parts) == 1 else jnp.concatenate(parts, axis=0)

    def gather(ref, planes, start, size, *lead):
        return cat([ref[(*lead, pln, pl.ds(start, size), slice(None))] for pln in planes])

    def pattern_stages(a, dil):
        npl = PLANES // dil
        qrows = BLK // npl
        per_res = BLK // qrows

        def geometry(grp, un):
            res, sub = divmod(grp * ATTN_UNROLL + un, per_res)
            off = sub * qrows
            table = 2 * a + (jnp.where(sup == 0, 1, 0) if sub == 0 else 0)
            return [res + dil * i for i in range(npl)], off, table

        def keys(prev_ref, cur_ref, planes, off):
            if off:
                return cat([cur_ref[pln, off - qrows:off + qrows, :] for pln in planes])
            return cat([part for pln in planes for part in (prev_ref[pln, BLK - qrows:BLK, :], cur_ref[pln, 0:qrows, :])])

        def stage_scores(grp, slot):
            stores = []
            for un in range(ATTN_UNROLL):
                planes, off, _ = geometry(grp, un)
                s = raw_scores(gather(q_ref, planes, off, qrows), keys(kp_ref, kc_ref, planes, off))
                stores += [(s_ref, (slot, un, h), s[h]) for h in range(2)]
            return stores

        def stage_softmax(grp, slot):
            stores = []
            for un in range(ATTN_UNROLL):
                planes, off, table = geometry(grp, un)
                m_old = None if a == 0 else [gather(m_ref, planes, off, qrows, h) for h in range(2)]
                m_new, p = weights([s_ref[slot, un, h] for h in range(2)], table, m_old)
                for h in range(2):
                    stores += [(p_ref, (slot, un, h), p[h]), (mn_ref, (slot, un, h), m_new[h])]
            return stores

        def stage_values(grp, slot):
            stores = []
            for un in range(ATTN_UNROLL):
                planes, off, _ = geometry(grp, un)
                old = None if a == 0 else ([gather(m_ref, planes, off, qrows, h) for h in range(2)],
                                           [gather(l_ref, planes, off, qrows, h) for h in range(2)],
                                           gather(acc_ref, planes, off, qrows))
                m_new, l_new, acc_new = combine([mn_ref[slot, un, h] for h in range(2)],
                                                [p_ref[slot, un, h] for h in range(2)],
                                                keys(vp_ref, vc_ref, planes, off), old)
                for i, pln in enumerate(planes):
                    part = slice(i * qrows, (i + 1) * qrows)
                    rows = (pln, pl.ds(off, qrows), slice(None))
                    for h in range(2):
                        stores += [(m_ref, (h,) + rows, m_new[h][part]), (l_ref, (h,) + rows, l_new[h][part])]
                    stores.append((acc_ref, rows, acc_new[part]))
            return stores

        return stage_scores, stage_softmax, stage_values

    groups = PLANES // ATTN_UNROLL
    work = [(stages, grp) for stages in (pattern_stages(a, dil) for a, dil in enumerate(DILATIONS))
            for grp in range(groups)]
    rows_of = [_attn_group_rows(dil, grp) for dil in DILATIONS for grp in range(groups)]
    assert all(not (rows_of[j] & rows_of[j - 1]) for j in range(1, len(work))), "groups sharing a step overlap"
    for step in range(len(work) + 2):
        stores = []
        for stage, lag in ((2, 2), (1, 1), (0, 0)):
            j = step - lag
            if 0 <= j < len(work):
                stages, grp = work[j]
                stores += stages[stage](grp, j % 2)
        for ref, idx, val in stores:
            ref[idx] = val
    for pln in range(PLANES):
        o_ref[pl.ds(pln, BLK, stride=PLANES), :] = acc_ref[pln] / both(l_ref[0, pln], l_ref[1, pln])


def _attention_prompt(q, k, v, bias, to_cast):
    bsz, _, rows, width = q.shape
    pairs = width // LANES
    ntab = bias.shape[1]
    sups = rows // BLK
    steps = bsz * pairs * sups
    unit = lambda b, h, s: (b * pairs + h) * sups + s
    cur = pl.BlockSpec((None, PLANES, BLK, LANES), lambda b, h, s: (b, 0, s, h))
    prev = pl.BlockSpec((None, PLANES, BLK, LANES), lambda b, h, s: (b, 0, jnp.maximum(s - 1, 0), h))
    cast_in = [pl.BlockSpec((m.shape[0] // steps, w), lambda b, h, s, c=c: (unit(b, h, s), c)) for m, w, c in to_cast]
    cast_out = [pl.BlockSpec((m.shape[0] // steps, w), lambda b, h, s: (unit(b, h, s), 0)) for m, w, _ in to_cast]
    return pl.pallas_call(
        functools.partial(_attn_kernel, len(to_cast)),
        grid=(bsz, pairs, sups),
        in_specs=[cur, prev, cur, prev, cur,
                  pl.BlockSpec((2, ntab, BLK, 2 * BLK), lambda b, h, s: (h, 0, 0, 0))] + cast_in,
        out_specs=[pl.BlockSpec((None, PLANES * BLK, LANES), lambda b, h, s: (b, s, h))] + cast_out,
        out_shape=[jax.ShapeDtypeStruct((bsz, PLANES * rows, width), F32)]
        + [jax.ShapeDtypeStruct((m.shape[0], w), BF16) for m, w, _ in to_cast],
        scratch_shapes=[pltpu.VMEM((PLANES, BLK, LANES), F32),
                        pltpu.VMEM((2, PLANES, BLK, LANES), F32),
                        pltpu.VMEM((2, PLANES, BLK, LANES), F32),
                        pltpu.VMEM((2, ATTN_UNROLL, 2, BLK, 2 * BLK), F32),
                        pltpu.VMEM((2, ATTN_UNROLL, 2, BLK, 2 * BLK), BF16),
                        pltpu.VMEM((2, ATTN_UNROLL, 2, BLK, LANES), F32)],
        compiler_params=_cparams(("arbitrary", "arbitrary", "arbitrary")),
        name="attention_prompt",
    )(q, k, k, v, v, bias, *[m for m, _, _ in to_cast])


SSM_COLS = 4


def _ssm_kernel(u_ref, ws_ref, wx_ref, mlag_ref, coef_ref, d_ref, y_ref, sre_ref, sim_ref,
                ucat_ref, s_ref, x_ref, m_ref):
    chunks = u_ref.shape[1]

    @pl.when(pl.program_id(1) == 0)
    def _():
        zero_blk = jnp.zeros((LANES, LANES), BF16)
        for s in range(PLANES):
            for t in range(PLANES):
                blk = mlag_ref[(t - s) * LANES:(t - s + 1) * LANES, :] if t >= s else zero_blk
                m_ref[s * LANES:(s + 1) * LANES, t * LANES:(t + 1) * LANES] = blk

    for s in range(PLANES):
        ucat_ref[:, s * LANES:(s + 1) * LANES] = u_ref[s]
    s_ref[...] = jnp.dot(ucat_ref[...], ws_ref[...], preferred_element_type=F32)
    a_re = coef_ref[1:2, 0:STATE_HALF]
    a_im = coef_ref[1:2, STATE_HALF:2 * STATE_HALF]

    def step(c, carry):
        xr, xi = carry
        x_ref[pl.ds(c, 1), 0:STATE_HALF] = xr
        x_ref[pl.ds(c, 1), STATE_HALF:2 * STATE_HALF] = xi
        sr = s_ref[pl.ds(c, 1), 0:STATE_HALF]
        si = s_ref[pl.ds(c, 1), STATE_HALF:2 * STATE_HALF]
        return a_re * xr - a_im * xi + sr, a_re * xi + a_im * xr + si

    zero = jnp.zeros((1, STATE_HALF), F32)
    xr, xi = lax.fori_loop(0, chunks, step, (zero, zero))
    sre_ref[...] = xr
    sim_ref[...] = xi

    xb = x_ref[...].astype(BF16)
    d = d_ref[...]
    for g in range(PLANES // SSM_COLS):
        cols = slice(g * SSM_COLS * LANES, (g + 1) * SSM_COLS * LANES)
        used = (g + 1) * SSM_COLS * LANES
        y = jnp.dot(ucat_ref[:, :used], m_ref[:used, cols], preferred_element_type=F32)
        y = y + jnp.dot(xb, wx_ref[:, cols], preferred_element_type=F32)
        for t in range(SSM_COLS):
            tok = g * SSM_COLS + t
            y_ref[pl.ds(tok, chunks, stride=PLANES), :] = (y[:, t * LANES:(t + 1) * LANES]
                                                          + d * u_ref[tok].astype(F32))


def _ssm_prompt(u, ws, wx, mlag, coef, d_skip):
    bsz, _, chunks, width = u.shape
    tiles = width // LANES
    sh = STATE_HALF
    state = jax.ShapeDtypeStruct((bsz, tiles, 1, sh), F32)
    state_spec = pl.BlockSpec((None, None, 1, sh), lambda t, b: (b, t, 0, 0))
    per_tile = lambda r, c: pl.BlockSpec((None, r, c), lambda t, b: (t, 0, 0))
    return pl.pallas_call(
        _ssm_kernel,
        grid=(tiles, bsz),
        in_specs=[pl.BlockSpec((None, PLANES, chunks, LANES), lambda t, b: (b, 0, 0, t)),
                  per_tile(PLANES * LANES, 2 * sh), per_tile(2 * sh, PLANES * LANES),
                  per_tile(PLANES * LANES, LANES), per_tile(SUBLANES, 2 * sh),
                  pl.BlockSpec((1, LANES), lambda t, b: (0, t))],
        out_specs=[pl.BlockSpec((None, PLANES * chunks, LANES), lambda t, b: (b, 0, t)),
                   state_spec, state_spec],
        out_shape=[jax.ShapeDtypeStruct((bsz, PLANES * chunks, width), F32), state, state],
        scratch_shapes=[pltpu.VMEM((chunks, PLANES * LANES), BF16),
                        pltpu.VMEM((chunks, 2 * sh), F32),
                        pltpu.VMEM((chunks, 2 * sh), F32),
                        pltpu.VMEM((PLANES * LANES, PLANES * LANES), BF16)],
        compiler_params=_cparams(("arbitrary", "arbitrary")),
        name="ssm_prompt",
    )(u, ws, wx, mlag, coef, d_skip)


def _epilogue_kernel(alpha, attn_ref, y_ref, x_ref, wga_ref, wgs_ref, wglu_ref, bglu_ref, wo_ref, bo_ref,
                     g_ref, b_ref, o_ref):
    x = x_ref[...]
    xb = x.astype(BF16)
    residual = alpha * x + bo_ref[...]
    g_attn = jnp.dot(xb, wga_ref[...], preferred_element_type=F32)
    br_a = (attn_ref[...] * jax.nn.silu(g_attn)).astype(BF16)
    g_ssm = jnp.dot(xb, wgs_ref[...], preferred_element_type=F32)
    z = jax.nn.gelu(y_ref[...])
    gate = jax.nn.sigmoid(jnp.dot(z.astype(BF16), wglu_ref[...], preferred_element_type=F32) + bglu_ref[...])
    br_s = (z * gate * jax.nn.silu(g_ssm)).astype(BF16)
    t = residual + jnp.dot(jnp.concatenate([br_a, br_s], axis=1), wo_ref[...], preferred_element_type=F32)
    mu = jnp.mean(t, axis=-1, keepdims=True)
    var = jnp.mean(jnp.square(t - mu), axis=-1, keepdims=True)
    o_ref[...] = (t - mu) * lax.rsqrt(var + LN_EPS) * g_ref[...] + b_ref[...]


def _epilogue(alpha, name, attn, y, x, weights, tile):
    lead = attn.shape[:-2]
    rows, width = attn.shape[-2:]
    d_model = x.shape[-1]
    grid = lead + (rows // tile,)
    none = (None,) * len(lead)
    tiled = lambda w: pl.BlockSpec(none + (tile, w), lambda *g: g + (0,))
    const = lambda shape: pl.BlockSpec(shape, lambda *g: (0, 0), pipeline_mode=ONCE)
    return pl.pallas_call(
        functools.partial(_epilogue_kernel, alpha),
        grid=grid,
        in_specs=[tiled(width), tiled(width), tiled(d_model),
                  const((d_model, width)), const((d_model, width)),
                  const((width, width)), const((1, width)), const((2 * width, d_model)), const((1, d_model)),
                  const((1, d_model)), const((1, d_model))],
        out_specs=tiled(d_model),
        out_shape=jax.ShapeDtypeStruct(x.shape, F32),
        compiler_params=_cparams(("arbitrary",) * len(grid)),
        name=name,
    )(attn, y, x, *weights)


def _inproj_sample_kernel(scale, x_ref, w_ref, o_ref):
    acc = jnp.dot(x_ref[...].astype(BF16), w_ref[...], preferred_element_type=F32)
    o_ref[...] = acc * jnp.where(pl.program_id(0) == 0, scale, 1.0)


def _inproj_sample(x, w_bf):
    rows, d_model = x.shape
    width = w_bf.shape[1] // 4
    return pl.pallas_call(
        functools.partial(_inproj_sample_kernel, HEAD_DIM ** -0.5),
        grid=(4,),
        in_specs=[pl.BlockSpec((rows, d_model), lambda j: (0, 0)),
                  pl.BlockSpec((d_model, width), lambda j: (0, j))],
        out_specs=pl.BlockSpec((rows, width), lambda j: (0, j)),
        out_shape=jax.ShapeDtypeStruct((rows, 4 * width), F32),
        compiler_params=_cparams(("arbitrary",)),
        name="inproj_sample",
    )(x, w_bf)


SAMPLE_PAIRS = 4


_NT = (((1,), (1,)), ((), ()))


def _sample_scores(q_ref, kn_ref, kt_ref):
    s_len = q_ref.shape[0]
    buf = kt_ref.shape[2]
    is_a = lax.broadcasted_iota(jnp.int32, (s_len, LANES), 1) < HEAD_DIM
    out = []
    for pp in range(SAMPLE_PAIRS):
        lanes = slice(pp * LANES, (pp + 1) * LANES)
        qp = q_ref[:, lanes]
        q2 = jnp.concatenate([jnp.where(is_a, qp, 0.0), jnp.where(is_a, 0.0, qp)], axis=0)
        kt = kt_ref[2 * pp:2 * pp + 2].reshape(2 * HEAD_DIM, buf).astype(BF16)
        kn = kn_ref[:, lanes]
        s_new = [jnp.sum(q2 * kn[j:j + 1, :], axis=1, keepdims=True) for j in range(s_len)]
        out.append((jnp.dot(q2.astype(BF16), kt, preferred_element_type=F32), s_new))
    return out


def _sample_outputs(scores, vn_ref, vt_ref, tb_ref, tn_ref, o_ref):
    s_len = vn_ref.shape[0]
    buf = vt_ref.shape[2]
    npat = tb_ref.shape[0]
    is_a = lax.broadcasted_iota(jnp.int32, (s_len, LANES), 1) < HEAD_DIM
    for pp, (s_buf, s_new) in enumerate(scores):
        lanes = slice(pp * LANES, (pp + 1) * LANES)
        rows = slice(pp * 2 * s_len, (pp + 1) * 2 * s_len)
        vt = vt_ref[2 * pp:2 * pp + 2].reshape(2 * HEAD_DIM, buf).astype(BF16)
        vn = vn_ref[:, lanes]
        z_buf = [s_buf + tb_ref[a, rows, :] for a in range(npat)]
        z_new = [[s_new[j] + tn_ref[a, rows, j:j + 1] for a in range(npat)] for j in range(s_len)]
        top = functools.reduce(jnp.maximum, [jnp.max(z, axis=1, keepdims=True) for z in z_buf]
                               + [z for zs in z_new for z in zs])
        w_buf = functools.reduce(jnp.add, [jnp.exp(z - top) for z in z_buf])
        w_new = [functools.reduce(jnp.add, [jnp.exp(z - top) for z in zs]) for zs in z_new]
        den = jnp.sum(w_buf, axis=1, keepdims=True) + functools.reduce(jnp.add, w_new)
        o = lax.dot_general(vt, w_buf.astype(BF16), _NT, preferred_element_type=F32).T
        o = (o + functools.reduce(jnp.add, [w_new[j] * vn[j:j + 1, :] for j in range(s_len)])) / den
        o_ref[:, lanes] = jnp.where(is_a, o[0:s_len], o[s_len:2 * s_len])


def _attention_sample_specs(h, kt, vt, tb, tn, unit_of):
    bsz, s_len, _ = h.shape
    heads, _, buf = kt.shape[1:]
    width = heads * HEAD_DIM
    gw = SAMPLE_PAIRS * LANES
    per_w = width // gw
    trows = SAMPLE_PAIRS * 2 * s_len
    where = lambda *g: divmod(unit_of(*g), per_w)
    new = lambda c: pl.BlockSpec((None, s_len, gw), lambda *g: (where(*g)[0], 0, c * per_w + where(*g)[1]))
    cache = pl.BlockSpec((None, 2 * SAMPLE_PAIRS, HEAD_DIM, buf), lambda *g: (where(*g)[0], where(*g)[1], 0, 0))
    specs = [new(0), new(1), new(2), cache, cache,
             pl.BlockSpec((tb.shape[0], trows, buf), lambda *g: (0, where(*g)[1], 0)),
             pl.BlockSpec((tn.shape[0], trows, LANES), lambda *g: (0, where(*g)[1], 0))]
    out_spec = pl.BlockSpec((None, s_len, gw), lambda *g: (where(*g)[0], 0, where(*g)[1]))
    return specs, out_spec, jax.ShapeDtypeStruct((bsz, s_len, width), F32)


def _ssm_sample_kernel(s_len, u_ref, b0_ref, c0_ref, coef_ref, d_ref, x0r_ref, x0i_ref,
                       y_ref, sre_ref, sim_ref, bu_ref, xs_ref):
    bsz = x0r_ref.shape[0]
    bu = jnp.dot(u_ref[...].astype(BF16), b0_ref[...], preferred_element_type=F32)
    slabs = bu_ref.shape[0]
    for c in range(slabs):
        bu_ref[c] = bu[:, c * LANES:(c + 1) * LANES]
    a_re = coef_ref[0:1, 0:STATE_HALF]
    a_im = coef_ref[0:1, STATE_HALF:2 * STATE_HALF]
    half = slabs // 2
    xr, xi = x0r_ref[...], x0i_ref[...]
    for s in range(s_len):
        step = lambda c: bu_ref[c, pl.ds(s, bsz, stride=s_len), :]
        br = jnp.concatenate([step(c) for c in range(half)], axis=1)
        bi = jnp.concatenate([step(c) for c in range(half, slabs)], axis=1)
        xr, xi = a_re * xr - a_im * xi + br, a_re * xi + a_im * xr + bi
        xs_ref[s * bsz:(s + 1) * bsz, :] = jnp.concatenate([xr, xi], axis=1)
    sre_ref[...] = xr
    sim_ref[...] = xi
    y = jnp.dot(xs_ref[...].astype(BF16), c0_ref[...], preferred_element_type=F32)
    d = d_ref[...]
    for s in range(s_len):
        y_ref[pl.ds(s, bsz, stride=s_len), :] = (y[s * bsz:(s + 1) * bsz, :]
                                                 + d * u_ref[pl.ds(s, bsz, stride=s_len), :])


def _ssm_sample(h, ws, c0, coef, d_skip, x0_re, x0_im, s_len):
    rows = h.shape[0]
    bsz = rows // s_len
    tiles = ws.shape[0]
    sh = STATE_HALF
    u_col0 = 3 * (h.shape[1] // 4) // LANES
    st_spec = pl.BlockSpec((bsz, sh), lambda t: (0, t))
    state = jax.ShapeDtypeStruct((bsz, tiles * sh), F32)
    return pl.pallas_call(
        functools.partial(_ssm_sample_kernel, s_len),
        grid=(tiles,),
        in_specs=[pl.BlockSpec((rows, LANES), lambda t: (0, u_col0 + t)),
                  pl.BlockSpec((None, LANES, 2 * sh), lambda t: (t, PLANES - 1, 0)),
                  pl.BlockSpec((None, 2 * sh, LANES), lambda t: (t, 0, 0)),
                  pl.BlockSpec((None, SUBLANES, 2 * sh), lambda t: (t, 0, 0)),
                  pl.BlockSpec((1, LANES), lambda t: (0, t)),
                  st_spec, st_spec],
        out_specs=[pl.BlockSpec((rows, LANES), lambda t: (0, t)), st_spec, st_spec],
        out_shape=[jax.ShapeDtypeStruct((rows, tiles * LANES), F32), state, state],
        scratch_shapes=[pltpu.VMEM((2 * sh // LANES, rows, LANES), F32),
                        pltpu.VMEM((rows, 2 * sh), F32)],
        compiler_params=_cparams(("arbitrary",)),
        name="ssm_sample",
    )(h, ws, c0, coef, d_skip, x0_re, x0_im)


def kernel(x_prompt, x_sample, cache_k, cache_v, state_ssm_re, state_ssm_im, w_in, w_out, b_out, rel_bias,
           lam_re, lam_im, log_dt, b_re, b_im, c_re, c_im, d_skip, w_glu, b_glu, ln_g, ln_b):
    depth = w_in.shape[0]
    assert depth == 1, "one layer per step"
    bsz, seq, d_model = x_prompt.shape
    dbsz, s_len, _ = x_sample.shape
    buf, heads = cache_k.shape[2], cache_k.shape[3]
    width = heads * HEAD_DIM
    groups, nstate = lam_re.shape[1], lam_re.shape[2]
    keep = min(MAX_DISTANCE, seq)
    assert seq % (PLANES * BLK) == 0 and seq >= 2 * PLANES * BLK and keep % ROW_TILE == 0
    assert buf == KPER * max(DILATIONS) and s_len <= min(DILATIONS[:-1])
    assert nstate == SSM_STATE and width == groups * SSM_CH and heads % (2 * SAMPLE_PAIRS) == 0
    alpha = (2 * depth) ** 0.25
    npat = len(DILATIONS)

    w_bf = jnp.concatenate([w_in[0][:, :3 * width], w_in[0][:, 4 * width:5 * width]], axis=1).astype(BF16)
    to_cast = [(w_in[0], width, 3), (w_in[0], width, 5), (w_glu[0], width, 0), (w_out[0], d_model, 0)]
    row = lambda v: v.reshape(1, -1)
    d_row = row(d_skip[0])

    rbt = rel_bias.T
    ptab = _bias_tables(jnp.asarray(_prompt_bucket_tables().reshape(1, -1)), rbt, BLK * 2 * BLK, LOG2E)
    ptab = ptab.reshape(heads, 2 * npat, BLK, 2 * BLK)
    key_w = buf + 2 * LANES
    stab = _bias_tables(jnp.asarray(_sample_bucket_tables(buf, s_len, key_w)), rbt, npat * s_len * key_w)
    stab = jnp.transpose(stab.reshape(heads, npat, s_len, key_w), (1, 0, 2, 3)).reshape(npat, heads * s_len, key_w)
    stab_buf, stab_new = stab[:, :, :buf], stab[:, :, buf:buf + LANES]
    ws, wx, m_intra, c0, coef = _ssm_prep(lam_re[0], lam_im[0], log_dt[0], b_re[0], b_im[0], c_re[0], c_im[0])

    xs = x_sample.reshape(dbsz * s_len, d_model)
    hs = _inproj_sample(xs, w_bf)
    pos_minor = lambda c: jnp.transpose(c[0], (0, 2, 3, 1))
    sample = (hs.reshape(dbsz, s_len, -1), pos_minor(cache_k), pos_minor(cache_v), stab_buf, stab_new)

    q, k, v, u, kl_t, vl_t, attn_s = _inproj_prompt(x_prompt, w_bf, keep, sample)
    attn, w_ga, w_gs, w_glu_bf, w_out_bf = _attention_prompt(q, k, v, ptab, to_cast)
    weights = (w_ga, w_gs, w_glu_bf, row(b_glu[0]), w_out_bf, row(b_out[0]), row(ln_g[0]), row(ln_b[0]))
    y, sre_p, sim_p = _ssm_prompt(u, ws, wx, m_intra, coef, d_row)
    y_prompt = _epilogue(alpha, "epilogue_prompt", attn, y, x_prompt, weights, ROW_TILE // 2)
    last = lambda t: jnp.transpose(t.reshape(bsz, heads, HEAD_DIM, keep), (0, 3, 1, 2))[None]
    st_shape = (1, bsz, groups, nstate)

    y_s, sre_s, sim_s = _ssm_sample(hs, ws, c0, coef, d_row,
                                    state_ssm_re[0].astype(F32).reshape(dbsz, groups * nstate),
                                    state_ssm_im[0].astype(F32).reshape(dbsz, groups * nstate), s_len)
    y_sample = _epilogue(alpha, "epilogue_sample", attn_s.reshape(dbsz * s_len, width), y_s, xs, weights,
                         dbsz * s_len)
    new_shape = (1, dbsz, s_len, heads, HEAD_DIM)
    sst_shape = (1, dbsz, groups, nstate)
    return (y_prompt, y_sample.reshape(dbsz, s_len, d_model), last(kl_t), last(vl_t),
            sre_p.reshape(st_shape), sim_p.reshape(st_shape),
            hs[:, width:2 * width].reshape(new_shape), hs[:, 2 * width:3 * width].reshape(new_shape),
            sre_s.reshape(sst_shape), sim_s.reshape(sst_shape))
```

```python
import functools
import math

import jax
import jax.numpy as jnp
import numpy as np
from jax import lax
from jax.experimental import pallas as pl
from jax.experimental.pallas import tpu as pltpu

F32 = jnp.float32
BF16 = jnp.bfloat16

HEAD_DIM = 64
SSM_CH = 16
SSM_STATE = 64
NUM_BUCKETS = 32
MAX_DISTANCE = 2048
KPER = 128
BLK = 128
DILATIONS = (16, 4, 1)
LN_EPS = 1e-5
NEG = -1e30
LOG2E = math.log2(math.e)

LANES = 128
SUBLANES = 8
PLANES = 16
GROUPS_PER_TILE = LANES // SSM_CH
STATE_HALF = GROUPS_PER_TILE * SSM_STATE
VMEM_LIMIT = 56 * 1024 * 1024
ROW_TILE = 512
ONCE = pl.Buffered(1)


def _cparams(sem, vmem=VMEM_LIMIT):
    return pltpu.CompilerParams(dimension_semantics=sem, vmem_limit_bytes=vmem)


def _bucket_np(dist):
    exact = NUM_BUCKETS // 2
    d_f = np.maximum(dist, 1).astype(np.float32)
    large = exact + (np.log(d_f / np.float32(exact)) / np.float32(math.log(MAX_DISTANCE / exact))
                     * np.float32(NUM_BUCKETS - exact)).astype(np.int32)
    large = np.minimum(large, NUM_BUCKETS - 1)
    return np.where(dist < exact, dist, large).astype(np.int32)


def _prompt_rel(dil):
    i = np.arange(BLK)[:, None]
    j = np.arange(2 * BLK)[None, :]
    npl = PLANES // dil
    qrows = BLK // npl
    pq, ml = i // qrows, i % qrows
    pk, jl = j // (2 * qrows), j % (2 * qrows)
    return npl * (ml - jl + qrows) + (pq - pk), np.broadcast_to(jl < qrows, (BLK, 2 * BLK))


def _prompt_bucket_tables():
    tabs = []
    for dil in DILATIONS:
        rel, earlier = _prompt_rel(dil)
        for first in (False, True):
            valid = (rel >= 0) & (rel <= KPER) & np.logical_not(earlier & first)
            tabs.append(np.where(valid, _bucket_np(np.clip(rel, 0, KPER) * dil), NUM_BUCKETS))
    return np.stack(tabs).reshape(len(DILATIONS) * 2, BLK * 2 * BLK).astype(np.int32)


def _sample_bucket_tables(buf, s_len, width):
    pos = np.arange(buf + s_len)
    tabs = np.full((len(DILATIONS), s_len, width), NUM_BUCKETS, np.int32)
    for a, dil in enumerate(DILATIONS):
        for s in range(s_len):
            dist = buf + s - pos
            valid = (dist >= 0) & (dist % dil == 0) & (dist // dil <= KPER)
            tabs[a, s, :len(pos)] = np.where(valid, _bucket_np(np.maximum(dist, 0)), NUM_BUCKETS)
    return tabs.reshape(1, -1)


def _bias_kernel(scale, idx_ref, rbt_ref, o_ref):
    idx = idx_ref[...]
    onehot = (lax.broadcasted_iota(jnp.int32, (NUM_BUCKETS, idx.shape[1]), 0) == idx).astype(BF16)
    rb = rbt_ref[...]
    n_heads = rb.shape[0]
    hi = rb.astype(BF16)
    lo = (rb - hi.astype(F32)).astype(BF16)
    res = jnp.dot(jnp.concatenate([hi, lo], axis=0), onehot, preferred_element_type=F32)
    tab = res[:n_heads] + res[n_heads:]
    o_ref[...] = jnp.where(idx < NUM_BUCKETS, tab * scale, NEG)


def _bias_tables(idx, rel_bias_t, chunk, scale=1.0):
    n_heads = rel_bias_t.shape[0]
    total = idx.shape[1]
    return pl.pallas_call(
        functools.partial(_bias_kernel, scale),
        grid=(total // chunk,),
        in_specs=[pl.BlockSpec((1, chunk), lambda c: (0, c)),
                  pl.BlockSpec((n_heads, NUM_BUCKETS), lambda c: (0, 0))],
        out_specs=pl.BlockSpec((n_heads, chunk), lambda c: (0, c)),
        out_shape=jax.ShapeDtypeStruct((n_heads, total), F32),
        compiler_params=_cparams(("arbitrary",)),
        name="bias_tables",
    )(idx, rel_bias_t)


def _discretize(lr, li, ldt):
    lr = jnp.minimum(lr, -1e-4)
    dt = jnp.exp(ldt)
    mag = jnp.exp(lr * dt)
    ab_re, ab_im = mag * jnp.cos(li * dt), mag * jnp.sin(li * dt)
    den = lr * lr + li * li
    inv_re, inv_im = lr / den, -li / den
    n_re, n_im = ab_re - 1.0, ab_im
    cf_re = n_re * inv_re - n_im * inv_im
    cf_im = n_re * inv_im + n_im * inv_re
    return ab_re, ab_im, cf_re, cf_im


def _ssm_prep_kernel(lam_row_ref, bt_re_ref, bt_im_ref, ct_re_ref, ct_im_ref,
                     ws_ref, wx_ref, m_ref, c0_ref, coef_ref):
    row = lam_row_ref[0]
    ab_re, ab_im, cf_re, cf_im = _discretize(row[0:1], row[1:2], row[2:3])
    bt_re, bt_im = bt_re_ref[0], bt_im_ref[0]
    bb_re = cf_re * bt_re - cf_im * bt_im
    bb_im = cf_re * bt_im + cf_im * bt_re
    ct_re, ct_im = ct_re_ref[0], ct_im_ref[0]
    c0 = jnp.concatenate([ct_re, -ct_im], axis=0)
    c0b = c0.astype(BF16)
    c0_ref[0] = c0b

    powers = [(jnp.ones_like(ab_re), jnp.zeros_like(ab_re))]
    for _ in range(PLANES):
        pr, pi = powers[-1]
        powers.append((pr * ab_re - pi * ab_im, pr * ab_im + pi * ab_re))
    cols = jnp.concatenate([part for pw in powers[1:] for part in pw], axis=0).T
    for lag in range(PLANES):
        pr, pi = powers[lag]
        w = jnp.concatenate([pr * bb_re - pi * bb_im, pr * bb_im + pi * bb_re], axis=1).astype(BF16)
        s = PLANES - 1 - lag
        ws_ref[0, s * LANES:(s + 1) * LANES, :] = w
        m_ref[0, lag * LANES:(lag + 1) * LANES, :] = jnp.dot(w, c0b, preferred_element_type=F32).astype(BF16)
        qr, qi = cols[:, 2 * lag:2 * lag + 1], cols[:, 2 * lag + 1:2 * lag + 2]
        wx_ref[0, 0:STATE_HALF, lag * LANES:(lag + 1) * LANES] = (ct_re * qr - ct_im * qi).astype(BF16)
        wx_ref[0, STATE_HALF:2 * STATE_HALF, lag * LANES:(lag + 1) * LANES] = (-(ct_re * qi + ct_im * qr)).astype(BF16)
    coef_ref[0] = jnp.concatenate([
        jnp.concatenate([ab_re, ab_im], axis=1),
        jnp.concatenate(powers[PLANES], axis=1),
        jnp.zeros((SUBLANES - 2, 2 * STATE_HALF), F32)], axis=0)


def _ssm_prep(lam_re, lam_im, log_dt, b_re, b_im, c_re, c_im):
    groups, n = lam_re.shape
    tiles = groups // GROUPS_PER_TILE
    gpt = GROUPS_PER_TILE
    eye = jnp.eye(gpt, dtype=F32)

    def rows(v):
        return v.reshape(tiles, gpt * n)

    ldt = jnp.broadcast_to(log_dt[:, None], (groups, n))
    lam_row = jnp.stack([rows(lam_re), rows(lam_im), rows(ldt)], axis=1)

    def bt(b):
        b = jnp.transpose(b.reshape(tiles, gpt, n, SSM_CH), (0, 1, 3, 2))
        return (b[:, :, :, None, :] * eye[None, :, None, :, None]).reshape(tiles, gpt * SSM_CH, gpt * n)

    def ct(c):
        c = jnp.transpose(c.reshape(tiles, gpt, SSM_CH, n), (0, 1, 3, 2))
        return (c[:, :, :, None, :] * eye[None, :, None, :, None]).reshape(tiles, gpt * n, gpt * SSM_CH)

    sh, ln = STATE_HALF, LANES
    tile3 = lambda a, b: pl.BlockSpec((1, a, b), lambda t: (t, 0, 0))
    return pl.pallas_call(
        _ssm_prep_kernel,
        grid=(tiles,),
        in_specs=[tile3(3, sh), tile3(ln, sh), tile3(ln, sh), tile3(sh, ln), tile3(sh, ln)],
        out_specs=[tile3(PLANES * ln, 2 * sh), tile3(2 * sh, PLANES * ln), tile3(PLANES * ln, ln),
                   tile3(2 * sh, ln), tile3(SUBLANES, 2 * sh)],
        out_shape=[jax.ShapeDtypeStruct((tiles, PLANES * ln, 2 * sh), BF16),
                   jax.ShapeDtypeStruct((tiles, 2 * sh, PLANES * ln), BF16),
                   jax.ShapeDtypeStruct((tiles, PLANES * ln, ln), BF16),
                   jax.ShapeDtypeStruct((tiles, 2 * sh, ln), BF16),
                   jax.ShapeDtypeStruct((tiles, SUBLANES, 2 * sh), F32)],
        compiler_params=_cparams(("arbitrary",)),
        name="ssm_prep",
    )(lam_row, bt(b_re), bt(b_im), ct(c_re), ct(c_im))


def _inproj_kernel(first_keep, scale, x_ref, perm_ref, wqkv_ref, wu_ref, *rest):
    sample_in, (q_ref, k_ref, v_ref, u_ref, kl_ref, vl_ref, sample_out, slab_ref) = rest[:7], rest[7:]
    slabs = slab_ref.shape[0]
    prow = q_ref.shape[1]
    width = u_ref.shape[-1]
    xp = jnp.dot(perm_ref[...], x_ref[...].astype(BF16), preferred_element_type=F32).astype(BF16)

    def to_planes(val, out_ref):
        for r in range(PLANES):
            out_ref[r] = val[r * prow:(r + 1) * prow].astype(out_ref.dtype)

    sq_ref, skn_ref, svn_ref, skt_ref, svt_ref, stb_ref, stn_ref = sample_in
    to_planes(jnp.dot(xp, wqkv_ref[:, 0:width], preferred_element_type=F32) * scale, q_ref)
    sample_scores = _sample_scores(sq_ref, skn_ref, skt_ref)
    k = jnp.dot(xp, wqkv_ref[:, width:2 * width], preferred_element_type=F32)
    to_planes(k, k_ref)
    v = jnp.dot(xp, wqkv_ref[:, 2 * width:3 * width], preferred_element_type=F32)
    to_planes(v, v_ref)
    _sample_outputs(sample_scores, svn_ref, svt_ref, stb_ref, stn_ref, sample_out)
    to_planes(jnp.dot(xp, wu_ref[...], preferred_element_type=F32), u_ref)

    @pl.when(pl.program_id(1) >= first_keep)
    def _():
        def token_order_t(val):
            for r in range(PLANES):
                for c in range(slabs):
                    slab_ref[c, pl.ds(r, prow, stride=PLANES), :] = val[r * prow:(r + 1) * prow,
                                                                        c * LANES:(c + 1) * LANES]
            return jnp.concatenate([slab_ref[c] for c in range(slabs)], axis=1).T
        kl_ref[...] = token_order_t(k)
        vl_ref[...] = token_order_t(v)


def _inproj_prompt(x, w_bf, keep, sample):
    bsz, seq, d_model = x.shape
    width = w_bf.shape[1] // 6
    rows = seq // PLANES
    tile = ROW_TILE // 2
    prow = tile // PLANES
    first_keep = (seq - keep) // tile
    steps = seq // tile
    sample_specs, sample_out_spec, sample_out = _attention_sample_specs(*sample, lambda b, i: b * steps + i)
    assert bsz * steps == sample[0].shape[0] * (width // (SAMPLE_PAIRS * LANES)), "one sample unit per grid step"
    plane = lambda dt: jax.ShapeDtypeStruct((bsz, PLANES, rows, width), dt)
    plane_spec = pl.BlockSpec((None, PLANES, prow, width), lambda b, i: (b, 0, i, 0))
    last = jax.ShapeDtypeStruct((bsz, width, keep), F32)
    last_spec = pl.BlockSpec((None, width, tile), lambda b, i: (b, 0, jnp.maximum(i - first_keep, 0)))
    perm = np.zeros((tile, tile), np.float32)
    perm[np.arange(tile), PLANES * (np.arange(tile) % prow) + np.arange(tile) // prow] = 1.0
    return pl.pallas_call(
        functools.partial(_inproj_kernel, first_keep, HEAD_DIM ** -0.5 * LOG2E),
        grid=(bsz, steps),
        in_specs=[pl.BlockSpec((None, tile, d_model), lambda b, i: (b, i, 0)),
                  pl.BlockSpec((tile, tile), lambda b, i: (0, 0), pipeline_mode=ONCE),
                  pl.BlockSpec((d_model, 3 * width), lambda b, i: (0, 0), pipeline_mode=ONCE),
                  pl.BlockSpec((d_model, width), lambda b, i: (0, 4), pipeline_mode=ONCE)] + sample_specs,
        out_specs=[plane_spec, plane_spec, plane_spec, plane_spec, last_spec, last_spec, sample_out_spec],
        out_shape=[plane(F32), plane(F32), plane(F32), plane(BF16), last, last, sample_out],
        scratch_shapes=[pltpu.VMEM((width // LANES, tile, LANES), F32)],
        compiler_params=_cparams(("arbitrary", "arbitrary")),
        name="inproj_prompt",
    )(x, jnp.asarray(perm, BF16), w_bf, w_bf, sample[0], sample[0], sample[0], *sample[1:])


ATTN_UNROLL = 2


def _attn_group_rows(dil, grp):
    npl = PLANES // dil
    qrows = BLK // npl
    out = set()
    for un in range(ATTN_UNROLL):
        res, sub = divmod(grp * ATTN_UNROLL + un, BLK // qrows)
        out |= {(res + dil * i, r) for i in range(npl) for r in range(sub * qrows, (sub + 1) * qrows)}
    return out


def _attn_kernel(n_cast, q_ref, kp_ref, kc_ref, vp_ref, vc_ref, bias_ref, *rest):
    cast_in, o_ref, cast_out = rest[:n_cast], rest[n_cast], rest[n_cast + 1:2 * n_cast + 1]
    acc_ref, m_ref, l_ref, s_ref, p_ref, mn_ref = rest[2 * n_cast + 1:]
    for src, dst in zip(cast_in, cast_out):
        dst[...] = src[...].astype(BF16)
    sup = pl.program_id(2)
    is_a = lax.broadcasted_iota(jnp.int32, (BLK, LANES), 1) < HEAD_DIM

    def raw_scores(q, k):
        kb = k.astype(BF16)
        out = []
        for head in range(2):
            qm = jnp.where(is_a if head == 0 else jnp.logical_not(is_a), q, 0.0).astype(BF16)
            out.append(lax.dot_general(qm, kb, (((1,), (1,)), ((), ())), preferred_element_type=F32))
        return out

    def both(a, b):
        return jnp.where(is_a, a, b)

    def weights(s, table, m_old):
        m_new, p = [], []
        for h in range(2):
            sh = s[h] + bias_ref[h, table]
            mb = jnp.broadcast_to(jnp.max(sh, axis=1, keepdims=True), (BLK, LANES))
            mh = mb if m_old is None else jnp.maximum(m_old[h], mb)
            m_new.append(mh)
            p.append(jnp.exp2(sh - jnp.concatenate([mh, mh], axis=1)).astype(BF16))
        return m_new, p

    ones = jnp.ones((2 * BLK, LANES), BF16)

    def combine(m_new, p, v, state):
        vext = jnp.concatenate([v.astype(BF16), ones], axis=1)
        res = [jnp.dot(p[h], vext, preferred_element_type=F32) for h in range(2)]
        acc_new = both(res[0][:, :LANES], res[1][:, :LANES])
        l_new = [res[h][:, LANES:] for h in range(2)]
        if state is not None:
            alpha = [jnp.exp2(state[0][h] - m_new[h]) for h in range(2)]
            l_new = [alpha[h] * state[1][h] + l_new[h] for h in range(2)]
            acc_new = both(alpha[0], alpha[1]) * state[2] + acc_new
        return m_new, l_new, acc_new

    cat = lambda parts: parts[0] if len(parts) == 1 else jnp.concatenate(parts, axis=0)

    def gather(ref, planes, start, size, *lead):
        return cat([ref[(*lead, pln, pl.ds(start, size), slice(None))] for pln in planes])

    def pattern_stages(a, dil):
        npl = PLANES // dil
        qrows = BLK // npl
        per_res = BLK // qrows

        def geometry(grp, un):
            res, sub = divmod(grp * ATTN_UNROLL + un, per_res)
            off = sub * qrows
            table = 2 * a + (jnp.where(sup == 0, 1, 0) if sub == 0 else 0)
            return [res + dil * i for i in range(npl)], off, table

        def keys(prev_ref, cur_ref, planes, off):
            if off:
                return cat([cur_ref[pln, off - qrows:off + qrows, :] for pln in planes])
            return cat([part for pln in planes for part in (prev_ref[pln, BLK - qrows:BLK, :], cur_ref[pln, 0:qrows, :])])

        def stage_scores(grp, slot):
            stores = []
            for un in range(ATTN_UNROLL):
                planes, off, _ = geometry(grp, un)
                s = raw_scores(gather(q_ref, planes, off, qrows), keys(kp_ref, kc_ref, planes, off))
                stores += [(s_ref, (slot, un, h), s[h]) for h in range(2)]
            return stores

        def stage_softmax(grp, slot):
            stores = []
            for un in range(ATTN_UNROLL):
                planes, off, table = geometry(grp, un)
                m_old = None if a == 0 else [gather(m_ref, planes, off, qrows, h) for h in range(2)]
                m_new, p = weights([s_ref[slot, un, h] for h in range(2)], table, m_old)
                for h in range(2):
                    stores += [(p_ref, (slot, un, h), p[h]), (mn_ref, (slot, un, h), m_new[h])]
            return stores

        def stage_values(grp, slot):
            stores = []
            for un in range(ATTN_UNROLL):
                planes, off, _ = geometry(grp, un)
                old = None if a == 0 else ([gather(m_ref, planes, off, qrows, h) for h in range(2)],
                                           [gather(l_ref, planes, off, qrows, h) for h in range(2)],
                                           gather(acc_ref, planes, off, qrows))
                m_new, l_new, acc_new = combine([mn_ref[slot, un, h] for h in range(2)],
                                                [p_ref[slot, un, h] for h in range(2)],
                                                keys(vp_ref, vc_ref, planes, off), old)
                for i, pln in enumerate(planes):
                    part = slice(i * qrows, (i + 1) * qrows)
                    rows = (pln, pl.ds(off, qrows), slice(None))
                    for h in range(2):
                        stores += [(m_ref, (h,) + rows, m_new[h][part]), (l_ref, (h,) + rows, l_new[h][part])]
                    stores.append((acc_ref, rows, acc_new[part]))
            return stores

        return stage_scores, stage_softmax, stage_values

    groups = PLANES // ATTN_UNROLL
    work = [(stages, grp) for stages in (pattern_stages(a, dil) for a, dil in enumerate(DILATIONS))
            for grp in range(groups)]
    rows_of = [_attn_group_rows(dil, grp) for dil in DILATIONS for grp in range(groups)]
    assert all(not (rows_of[j] & rows_of[j - 1]) for j in range(1, len(work))), "groups sharing a step overlap"
    for step in range(len(work) + 2):
        stores = []
        for stage, lag in ((2, 2), (1, 1), (0, 0)):
            j = step - lag
            if 0 <= j < len(work):
                stages, grp = work[j]
                stores += stages[stage](grp, j % 2)
        for ref, idx, val in stores:
            ref[idx] = val
    for pln in range(PLANES):
        o_ref[pl.ds(pln, BLK, stride=PLANES), :] = acc_ref[pln] / both(l_ref[0, pln], l_ref[1, pln])


def _attention_prompt(q, k, v, bias, to_cast):
    bsz, _, rows, width = q.shape
    pairs = width // LANES
    ntab = bias.shape[1]
    sups = rows // BLK
    steps = bsz * pairs * sups
    unit = lambda b, h, s: (b * pairs + h) * sups + s
    cur = pl.BlockSpec((None, PLANES, BLK, LANES), lambda b, h, s: (b, 0, s, h))
    prev = pl.BlockSpec((None, PLANES, BLK, LANES), lambda b, h, s: (b, 0, jnp.maximum(s - 1, 0), h))
    cast_in = [pl.BlockSpec((m.shape[0] // steps, w), lambda b, h, s, c=c: (unit(b, h, s), c)) for m, w, c in to_cast]
    cast_out = [pl.BlockSpec((m.shape[0] // steps, w), lambda b, h, s: (unit(b, h, s), 0)) for m, w, _ in to_cast]
    return pl.pallas_call(
        functools.partial(_attn_kernel, len(to_cast)),
        grid=(bsz, pairs, sups),
        in_specs=[cur, prev, cur, prev, cur,
                  pl.BlockSpec((2, ntab, BLK, 2 * BLK), lambda b, h, s: (h, 0, 0, 0))] + cast_in,
        out_specs=[pl.BlockSpec((None, PLANES * BLK, LANES), lambda b, h, s: (b, s, h))] + cast_out,
        out_shape=[jax.ShapeDtypeStruct((bsz, PLANES * rows, width), F32)]
        + [jax.ShapeDtypeStruct((m.shape[0], w), BF16) for m, w, _ in to_cast],
        scratch_shapes=[pltpu.VMEM((PLANES, BLK, LANES), F32),
                        pltpu.VMEM((2, PLANES, BLK, LANES), F32),
                        pltpu.VMEM((2, PLANES, BLK, LANES), F32),
                        pltpu.VMEM((2, ATTN_UNROLL, 2, BLK, 2 * BLK), F32),
                        pltpu.VMEM((2, ATTN_UNROLL, 2, BLK, 2 * BLK), BF16),
                        pltpu.VMEM((2, ATTN_UNROLL, 2, BLK, LANES), F32)],
        compiler_params=_cparams(("arbitrary", "arbitrary", "arbitrary")),
        name="attention_prompt",
    )(q, k, k, v, v, bias, *[m for m, _, _ in to_cast])


SSM_COLS = 4


def _ssm_kernel(u_ref, ws_ref, wx_ref, mlag_ref, coef_ref, d_ref, y_ref, sre_ref, sim_ref,
                ucat_ref, s_ref, x_ref, m_ref):
    chunks = u_ref.shape[1]

    @pl.when(pl.program_id(1) == 0)
    def _():
        zero_blk = jnp.zeros((LANES, LANES), BF16)
        for s in range(PLANES):
            for t in range(PLANES):
                blk = mlag_ref[(t - s) * LANES:(t - s + 1) * LANES, :] if t >= s else zero_blk
                m_ref[s * LANES:(s + 1) * LANES, t * LANES:(t + 1) * LANES] = blk

    for s in range(PLANES):
        ucat_ref[:, s * LANES:(s + 1) * LANES] = u_ref[s]
    s_ref[...] = jnp.dot(ucat_ref[...], ws_ref[...], preferred_element_type=F32)
    a_re = coef_ref[1:2, 0:STATE_HALF]
    a_im = coef_ref[1:2, STATE_HALF:2 * STATE_HALF]

    def step(c, carry):
        xr, xi = carry
        x_ref[pl.ds(c, 1), 0:STATE_HALF] = xr
        x_ref[pl.ds(c, 1), STATE_HALF:2 * STATE_HALF] = xi
        sr = s_ref[pl.ds(c, 1), 0:STATE_HALF]
        si = s_ref[pl.ds(c, 1), STATE_HALF:2 * STATE_HALF]
        return a_re * xr - a_im * xi + sr, a_re * xi + a_im * xr + si

    zero = jnp.zeros((1, STATE_HALF), F32)
    xr, xi = lax.fori_loop(0, chunks, step, (zero, zero))
    sre_ref[...] = xr
    sim_ref[...] = xi

    xb = x_ref[...].astype(BF16)
    d = d_ref[...]
    for g in range(PLANES // SSM_COLS):
        cols = slice(g * SSM_COLS * LANES, (g + 1) * SSM_COLS * LANES)
        used = (g + 1) * SSM_COLS * LANES
        y = jnp.dot(ucat_ref[:, :used], m_ref[:used, cols], preferred_element_type=F32)
        y = y + jnp.dot(xb, wx_ref[:, cols], preferred_element_type=F32)
        for t in range(SSM_COLS):
            tok = g * SSM_COLS + t
            y_ref[pl.ds(tok, chunks, stride=PLANES), :] = (y[:, t * LANES:(t + 1) * LANES]
                                                          + d * u_ref[tok].astype(F32))


def _ssm_prompt(u, ws, wx, mlag, coef, d_skip):
    bsz, _, chunks, width = u.shape
    tiles = width // LANES
    sh = STATE_HALF
    state = jax.ShapeDtypeStruct((bsz, tiles, 1, sh), F32)
    state_spec = pl.BlockSpec((None, None, 1, sh), lambda t, b: (b, t, 0, 0))
    per_tile = lambda r, c: pl.BlockSpec((None, r, c), lambda t, b: (t, 0, 0))
    return pl.pallas_call(
        _ssm_kernel,
        grid=(tiles, bsz),
        in_specs=[pl.BlockSpec((None, PLANES, chunks, LANES), lambda t, b: (b, 0, 0, t)),
                  per_tile(PLANES * LANES, 2 * sh), per_tile(2 * sh, PLANES * LANES),
                  per_tile(PLANES * LANES, LANES), per_tile(SUBLANES, 2 * sh),
                  pl.BlockSpec((1, LANES), lambda t, b: (0, t))],
        out_specs=[pl.BlockSpec((None, PLANES * chunks, LANES), lambda t, b: (b, 0, t)),
                   state_spec, state_spec],
        out_shape=[jax.ShapeDtypeStruct((bsz, PLANES * chunks, width), F32), state, state],
        scratch_shapes=[pltpu.VMEM((chunks, PLANES * LANES), BF16),
                        pltpu.VMEM((chunks, 2 * sh), F32),
                        pltpu.VMEM((chunks, 2 * sh), F32),
                        pltpu.VMEM((PLANES * LANES, PLANES * LANES), BF16)],
        compiler_params=_cparams(("arbitrary", "arbitrary")),
        name="ssm_prompt",
    )(u, ws, wx, mlag, coef, d_skip)


def _epilogue_kernel(alpha, attn_ref, y_ref, x_ref, wga_ref, wgs_ref, wglu_ref, bglu_ref, wo_ref, bo_ref,
                     g_ref, b_ref, o_ref):
    x = x_ref[...]
    xb = x.astype(BF16)
    residual = alpha * x + bo_ref[...]
    g_attn = jnp.dot(xb, wga_ref[...], preferred_element_type=F32)
    br_a = (attn_ref[...] * jax.nn.silu(g_attn)).astype(BF16)
    g_ssm = jnp.dot(xb, wgs_ref[...], preferred_element_type=F32)
    z = jax.nn.gelu(y_ref[...])
    gate = jax.nn.sigmoid(jnp.dot(z.astype(BF16), wglu_ref[...], preferred_element_type=F32) + bglu_ref[...])
    br_s = (z * gate * jax.nn.silu(g_ssm)).astype(BF16)
    t = residual + jnp.dot(jnp.concatenate([br_a, br_s], axis=1), wo_ref[...], preferred_element_type=F32)
    mu = jnp.mean(t, axis=-1, keepdims=True)
    var = jnp.mean(jnp.square(t - mu), axis=-1, keepdims=True)
    o_ref[...] = (t - mu) * lax.rsqrt(var + LN_EPS) * g_ref[...] + b_ref[...]


def _epilogue(alpha, name, attn, y, x, weights, tile):
    lead = attn.shape[:-2]
    rows, width = attn.shape[-2:]
    d_model = x.shape[-1]
    grid = lead + (rows // tile,)
    none = (None,) * len(lead)
    tiled = lambda w: pl.BlockSpec(none + (tile, w), lambda *g: g + (0,))
    const = lambda shape, col=0: pl.BlockSpec(shape, lambda *g: (0, col), pipeline_mode=ONCE)
    return pl.pallas_call(
        functools.partial(_epilogue_kernel, alpha),
        grid=grid,
        in_specs=[tiled(width), tiled(width), tiled(d_model),
                  const((d_model, width), 3), const((d_model, width), 5),
                  const((width, width)), const((1, width)), const((2 * width, d_model)), const((1, d_model)),
                  const((1, d_model)), const((1, d_model))],
        out_specs=tiled(d_model),
        out_shape=jax.ShapeDtypeStruct(x.shape, F32),
        compiler_params=_cparams(("arbitrary",) * len(grid)),
        name=name,
    )(attn, y, x, *weights)


def _inproj_sample_kernel(scale, x_ref, w_ref, o_ref):
    acc = jnp.dot(x_ref[...].astype(BF16), w_ref[...], preferred_element_type=F32)
    o_ref[...] = acc * jnp.where(pl.program_id(0) == 0, scale, 1.0)


def _inproj_sample(x, w_bf):
    rows, d_model = x.shape
    width = w_bf.shape[1] // 6
    return pl.pallas_call(
        functools.partial(_inproj_sample_kernel, HEAD_DIM ** -0.5),
        grid=(4,),
        in_specs=[pl.BlockSpec((rows, d_model), lambda j: (0, 0)),
                  pl.BlockSpec((d_model, width), lambda j: (0, j + j // 3))],
        out_specs=pl.BlockSpec((rows, width), lambda j: (0, j)),
        out_shape=jax.ShapeDtypeStruct((rows, 4 * width), F32),
        compiler_params=_cparams(("arbitrary",)),
        name="inproj_sample",
    )(x, w_bf)


SAMPLE_PAIRS = 4


_NT = (((1,), (1,)), ((), ()))


def _sample_scores(q_ref, kn_ref, kt_ref):
    s_len = q_ref.shape[0]
    buf = kt_ref.shape[2]
    is_a = lax.broadcasted_iota(jnp.int32, (s_len, LANES), 1) < HEAD_DIM
    out = []
    for pp in range(SAMPLE_PAIRS):
        lanes = slice(pp * LANES, (pp + 1) * LANES)
        qp = q_ref[:, lanes]
        q2 = jnp.concatenate([jnp.where(is_a, qp, 0.0), jnp.where(is_a, 0.0, qp)], axis=0)
        kt = kt_ref[2 * pp:2 * pp + 2].reshape(2 * HEAD_DIM, buf).astype(BF16)
        kn = kn_ref[:, lanes]
        s_new = [jnp.sum(q2 * kn[j:j + 1, :], axis=1, keepdims=True) for j in range(s_len)]
        out.append((jnp.dot(q2.astype(BF16), kt, preferred_element_type=F32), s_new))
    return out


def _sample_outputs(scores, vn_ref, vt_ref, tb_ref, tn_ref, o_ref):
    s_len = vn_ref.shape[0]
    buf = vt_ref.shape[2]
    npat = tb_ref.shape[0]
    is_a = lax.broadcasted_iota(jnp.int32, (s_len, LANES), 1) < HEAD_DIM
    for pp, (s_buf, s_new) in enumerate(scores):
        lanes = slice(pp * LANES, (pp + 1) * LANES)
        rows = slice(pp * 2 * s_len, (pp + 1) * 2 * s_len)
        vt = vt_ref[2 * pp:2 * pp + 2].reshape(2 * HEAD_DIM, buf).astype(BF16)
        vn = vn_ref[:, lanes]
        z_buf = [s_buf + tb_ref[a, rows, :] for a in range(npat)]
        z_new = [[s_new[j] + tn_ref[a, rows, j:j + 1] for a in range(npat)] for j in range(s_len)]
        top = functools.reduce(jnp.maximum, [jnp.max(z, axis=1, keepdims=True) for z in z_buf]
                               + [z for zs in z_new for z in zs])
        w_buf = functools.reduce(jnp.add, [jnp.exp(z - top) for z in z_buf])
        w_new = [functools.reduce(jnp.add, [jnp.exp(z - top) for z in zs]) for zs in z_new]
        den = jnp.sum(w_buf, axis=1, keepdims=True) + functools.reduce(jnp.add, w_new)
        o = lax.dot_general(vt, w_buf.astype(BF16), _NT, preferred_element_type=F32).T
        o = (o + functools.reduce(jnp.add, [w_new[j] * vn[j:j + 1, :] for j in range(s_len)])) / den
        o_ref[:, lanes] = jnp.where(is_a, o[0:s_len], o[s_len:2 * s_len])


def _attention_sample_specs(h, kt, vt, tb, tn, unit_of):
    bsz, s_len, _ = h.shape
    heads, _, buf = kt.shape[1:]
    width = heads * HEAD_DIM
    gw = SAMPLE_PAIRS * LANES
    per_w = width // gw
    trows = SAMPLE_PAIRS * 2 * s_len
    where = lambda *g: divmod(unit_of(*g), per_w)
    new = lambda c: pl.BlockSpec((None, s_len, gw), lambda *g: (where(*g)[0], 0, c * per_w + where(*g)[1]))
    cache = pl.BlockSpec((None, 2 * SAMPLE_PAIRS, HEAD_DIM, buf), lambda *g: (where(*g)[0], where(*g)[1], 0, 0))
    specs = [new(0), new(1), new(2), cache, cache,
             pl.BlockSpec((tb.shape[0], trows, buf), lambda *g: (0, where(*g)[1], 0)),
             pl.BlockSpec((tn.shape[0], trows, LANES), lambda *g: (0, where(*g)[1], 0))]
    out_spec = pl.BlockSpec((None, s_len, gw), lambda *g: (where(*g)[0], 0, where(*g)[1]))
    return specs, out_spec, jax.ShapeDtypeStruct((bsz, s_len, width), F32)


def _ssm_sample_kernel(s_len, u_ref, b0_ref, c0_ref, coef_ref, d_ref, x0r_ref, x0i_ref,
                       y_ref, sre_ref, sim_ref, bu_ref, xs_ref):
    bsz = x0r_ref.shape[0]
    bu = jnp.dot(u_ref[...].astype(BF16), b0_ref[...], preferred_element_type=F32)
    slabs = bu_ref.shape[0]
    for c in range(slabs):
        bu_ref[c] = bu[:, c * LANES:(c + 1) * LANES]
    a_re = coef_ref[0:1, 0:STATE_HALF]
    a_im = coef_ref[0:1, STATE_HALF:2 * STATE_HALF]
    half = slabs // 2
    xr, xi = x0r_ref[...], x0i_ref[...]
    for s in range(s_len):
        step = lambda c: bu_ref[c, pl.ds(s, bsz, stride=s_len), :]
        br = jnp.concatenate([step(c) for c in range(half)], axis=1)
        bi = jnp.concatenate([step(c) for c in range(half, slabs)], axis=1)
        xr, xi = a_re * xr - a_im * xi + br, a_re * xi + a_im * xr + bi
        xs_ref[s * bsz:(s + 1) * bsz, :] = jnp.concatenate([xr, xi], axis=1)
    sre_ref[...] = xr
    sim_ref[...] = xi
    y = jnp.dot(xs_ref[...].astype(BF16), c0_ref[...], preferred_element_type=F32)
    d = d_ref[...]
    for s in range(s_len):
        y_ref[pl.ds(s, bsz, stride=s_len), :] = (y[s * bsz:(s + 1) * bsz, :]
                                                 + d * u_ref[pl.ds(s, bsz, stride=s_len), :])


def _ssm_sample(h, ws, c0, coef, d_skip, x0_re, x0_im, s_len):
    rows = h.shape[0]
    bsz = rows // s_len
    tiles = ws.shape[0]
    sh = STATE_HALF
    u_col0 = 3 * (h.shape[1] // 4) // LANES
    st_spec = pl.BlockSpec((bsz, sh), lambda t: (0, t))
    state = jax.ShapeDtypeStruct((bsz, tiles * sh), F32)
    return pl.pallas_call(
        functools.partial(_ssm_sample_kernel, s_len),
        grid=(tiles,),
        in_specs=[pl.BlockSpec((rows, LANES), lambda t: (0, u_col0 + t)),
                  pl.BlockSpec((None, LANES, 2 * sh), lambda t: (t, PLANES - 1, 0)),
                  pl.BlockSpec((None, 2 * sh, LANES), lambda t: (t, 0, 0)),
                  pl.BlockSpec((None, SUBLANES, 2 * sh), lambda t: (t, 0, 0)),
                  pl.BlockSpec((1, LANES), lambda t: (0, t)),
                  st_spec, st_spec],
        out_specs=[pl.BlockSpec((rows, LANES), lambda t: (0, t)), st_spec, st_spec],
        out_shape=[jax.ShapeDtypeStruct((rows, tiles * LANES), F32), state, state],
        scratch_shapes=[pltpu.VMEM((2 * sh // LANES, rows, LANES), F32),
                        pltpu.VMEM((rows, 2 * sh), F32)],
        compiler_params=_cparams(("arbitrary",)),
        name="ssm_sample",
    )(h, ws, c0, coef, d_skip, x0_re, x0_im)


def kernel(x_prompt, x_sample, cache_k, cache_v, state_ssm_re, state_ssm_im, w_in, w_out, b_out, rel_bias,
           lam_re, lam_im, log_dt, b_re, b_im, c_re, c_im, d_skip, w_glu, b_glu, ln_g, ln_b):
    depth = w_in.shape[0]
    assert depth == 1, "one layer per step"
    bsz, seq, d_model = x_prompt.shape
    dbsz, s_len, _ = x_sample.shape
    buf, heads = cache_k.shape[2], cache_k.shape[3]
    width = heads * HEAD_DIM
    groups, nstate = lam_re.shape[1], lam_re.shape[2]
    keep = min(MAX_DISTANCE, seq)
    assert seq % (PLANES * BLK) == 0 and seq >= 2 * PLANES * BLK and keep % ROW_TILE == 0
    assert buf == KPER * max(DILATIONS) and s_len <= min(DILATIONS[:-1])
    assert nstate == SSM_STATE and width == groups * SSM_CH and heads % (2 * SAMPLE_PAIRS) == 0
    alpha = (2 * depth) ** 0.25
    npat = len(DILATIONS)

    w_bf = w_in[0].astype(BF16)
    to_cast = [(w_glu[0], width, 0), (w_out[0], d_model, 0)]
    row = lambda v: v.reshape(1, -1)
    d_row = row(d_skip[0])

    rbt = rel_bias.T
    ptab = _bias_tables(jnp.asarray(_prompt_bucket_tables().reshape(1, -1)), rbt, BLK * 2 * BLK, LOG2E)
    ptab = ptab.reshape(heads, 2 * npat, BLK, 2 * BLK)
    key_w = buf + 2 * LANES
    stab = _bias_tables(jnp.asarray(_sample_bucket_tables(buf, s_len, key_w)), rbt, npat * s_len * key_w)
    stab = jnp.transpose(stab.reshape(heads, npat, s_len, key_w), (1, 0, 2, 3)).reshape(npat, heads * s_len, key_w)
    stab_buf, stab_new = stab[:, :, :buf], stab[:, :, buf:buf + LANES]
    ws, wx, m_intra, c0, coef = _ssm_prep(lam_re[0], lam_im[0], log_dt[0], b_re[0], b_im[0], c_re[0], c_im[0])

    xs = x_sample.reshape(dbsz * s_len, d_model)
    hs = _inproj_sample(xs, w_bf)
    pos_minor = lambda c: jnp.transpose(c[0], (0, 2, 3, 1))
    sample = (hs.reshape(dbsz, s_len, -1), pos_minor(cache_k), pos_minor(cache_v), stab_buf, stab_new)

    q, k, v, u, kl_t, vl_t, attn_s = _inproj_prompt(x_prompt, w_bf, keep, sample)
    attn, w_glu_bf, w_out_bf = _attention_prompt(q, k, v, ptab, to_cast)
    weights = (w_bf, w_bf, w_glu_bf, row(b_glu[0]), w_out_bf, row(b_out[0]), row(ln_g[0]), row(ln_b[0]))
    y, sre_p, sim_p = _ssm_prompt(u, ws, wx, m_intra, coef, d_row)
    y_prompt = _epilogue(alpha, "epilogue_prompt", attn, y, x_prompt, weights, ROW_TILE // 2)
    last = lambda t: jnp.transpose(t.reshape(bsz, heads, HEAD_DIM, keep), (0, 3, 1, 2))[None]
    st_shape = (1, bsz, groups, nstate)

    y_s, sre_s, sim_s = _ssm_sample(hs, ws, c0, coef, d_row,
                                    state_ssm_re[0].astype(F32).reshape(dbsz, groups * nstate),
                                    state_ssm_im[0].astype(F32).reshape(dbsz, groups * nstate), s_len)
    y_sample = _epilogue(alpha, "epilogue_sample", attn_s.reshape(dbsz * s_len, width), y_s, xs, weights,
                         dbsz * s_len)
    new_shape = (1, dbsz, s_len, heads, HEAD_DIM)
    sst_shape = (1, dbsz, groups, nstate)
    return (y_prompt, y_sample.reshape(dbsz, s_len, d_model), last(kl_t), last(vl_t),
            sre_p.reshape(st_shape), sim_p.reshape(st_shape),
            hs[:, width:2 * width].reshape(new_shape), hs[:, 2 * width:3 * width].reshape(new_shape),
            sre_s.reshape(sst_shape), sim_s.reshape(sst_shape))
```

```python
import functools
import math

import jax
import jax.numpy as jnp
import numpy as np
from jax import lax
from jax.experimental import pallas as pl
from jax.experimental.pallas import tpu as pltpu

F32 = jnp.float32
BF16 = jnp.bfloat16

HEAD_DIM = 64
SSM_CH = 16
SSM_STATE = 64
NUM_BUCKETS = 32
MAX_DISTANCE = 2048
KPER = 128
BLK = 128
DILATIONS = (16, 4, 1)
LN_EPS = 1e-5
NEG = -1e30
LOG2E = math.log2(math.e)

LANES = 128
SUBLANES = 8
PLANES = 16
GROUPS_PER_TILE = LANES // SSM_CH
STATE_HALF = GROUPS_PER_TILE * SSM_STATE
VMEM_LIMIT = 56 * 1024 * 1024
INPROJ_VMEM_LIMIT = 60 * 1024 * 1024
ROW_TILE = 512
ONCE = pl.Buffered(1)


def _cparams(sem, vmem=VMEM_LIMIT):
    return pltpu.CompilerParams(dimension_semantics=sem, vmem_limit_bytes=vmem)


def _bucket_np(dist):
    exact = NUM_BUCKETS // 2
    d_f = np.maximum(dist, 1).astype(np.float32)
    large = exact + (np.log(d_f / np.float32(exact)) / np.float32(math.log(MAX_DISTANCE / exact))
                     * np.float32(NUM_BUCKETS - exact)).astype(np.int32)
    large = np.minimum(large, NUM_BUCKETS - 1)
    return np.where(dist < exact, dist, large).astype(np.int32)


def _prompt_rel(dil):
    i = np.arange(BLK)[:, None]
    j = np.arange(2 * BLK)[None, :]
    npl = PLANES // dil
    qrows = BLK // npl
    pq, ml = i // qrows, i % qrows
    pk, jl = j // (2 * qrows), j % (2 * qrows)
    return npl * (ml - jl + qrows) + (pq - pk), np.broadcast_to(jl < qrows, (BLK, 2 * BLK))


def _prompt_bucket_tables():
    tabs = []
    for dil in DILATIONS:
        rel, earlier = _prompt_rel(dil)
        for first in (False, True):
            valid = (rel >= 0) & (rel <= KPER) & np.logical_not(earlier & first)
            tabs.append(np.where(valid, _bucket_np(np.clip(rel, 0, KPER) * dil), NUM_BUCKETS))
    return np.stack(tabs).reshape(len(DILATIONS) * 2, BLK * 2 * BLK).astype(np.int32)


def _sample_bucket_tables(buf, s_len, width):
    pos = np.arange(buf + s_len)
    tabs = np.full((len(DILATIONS), s_len, width), NUM_BUCKETS, np.int32)
    for a, dil in enumerate(DILATIONS):
        for s in range(s_len):
            dist = buf + s - pos
            valid = (dist >= 0) & (dist % dil == 0) & (dist // dil <= KPER)
            tabs[a, s, :len(pos)] = np.where(valid, _bucket_np(np.maximum(dist, 0)), NUM_BUCKETS)
    return tabs.reshape(1, -1)


def _bias_kernel(scale, idx_ref, rbt_ref, o_ref):
    idx = idx_ref[...]
    onehot = (lax.broadcasted_iota(jnp.int32, (NUM_BUCKETS, idx.shape[1]), 0) == idx).astype(BF16)
    rb = rbt_ref[...]
    n_heads = rb.shape[0]
    hi = rb.astype(BF16)
    lo = (rb - hi.astype(F32)).astype(BF16)
    res = jnp.dot(jnp.concatenate([hi, lo], axis=0), onehot, preferred_element_type=F32)
    tab = res[:n_heads] + res[n_heads:]
    o_ref[...] = jnp.where(idx < NUM_BUCKETS, tab * scale, NEG)


def _bias_tables(idx, rel_bias_t, chunk, scale=1.0):
    n_heads = rel_bias_t.shape[0]
    total = idx.shape[1]
    return pl.pallas_call(
        functools.partial(_bias_kernel, scale),
        grid=(total // chunk,),
        in_specs=[pl.BlockSpec((1, chunk), lambda c: (0, c)),
                  pl.BlockSpec((n_heads, NUM_BUCKETS), lambda c: (0, 0))],
        out_specs=pl.BlockSpec((n_heads, chunk), lambda c: (0, c)),
        out_shape=jax.ShapeDtypeStruct((n_heads, total), F32),
        compiler_params=_cparams(("arbitrary",)),
        name="bias_tables",
    )(idx, rel_bias_t)


def _discretize(lr, li, ldt):
    lr = jnp.minimum(lr, -1e-4)
    dt = jnp.exp(ldt)
    mag = jnp.exp(lr * dt)
    ab_re, ab_im = mag * jnp.cos(li * dt), mag * jnp.sin(li * dt)
    den = lr * lr + li * li
    inv_re, inv_im = lr / den, -li / den
    n_re, n_im = ab_re - 1.0, ab_im
    cf_re = n_re * inv_re - n_im * inv_im
    cf_im = n_re * inv_im + n_im * inv_re
    return ab_re, ab_im, cf_re, cf_im


PREP_LAGS = 2


def _ssm_prep_part(part, lam_row_ref, bt_re_ref, bt_im_ref, ct_re_ref, ct_im_ref,
                   ws_ref, wx_ref, m_ref, c0_ref, coef_ref):
    row = lam_row_ref[...]
    ab_re, ab_im, cf_re, cf_im = _discretize(row[0:1], row[1:2], row[2:3])
    bt_re, bt_im = bt_re_ref[...], bt_im_ref[...]
    bb_re = cf_re * bt_re - cf_im * bt_im
    bb_im = cf_re * bt_im + cf_im * bt_re
    ct_re, ct_im = ct_re_ref[...], ct_im_ref[...]
    c0b = jnp.concatenate([ct_re, -ct_im], axis=0).astype(BF16)
    c0_ref[...] = c0b

    cmul = lambda a, b: (a[0] * b[0] - a[1] * b[1], a[0] * b[1] + a[1] * b[0])
    squares = [(ab_re, ab_im)]
    while len(squares) < 5:
        squares.append(cmul(squares[-1], squares[-1]))
    base = (jnp.ones_like(ab_re), jnp.zeros_like(ab_re))
    for bit in range(3):
        factor = squares[bit + 1]
        take = ((part >> bit) & 1) == 1
        prod = cmul(base, factor)
        base = (jnp.where(take, prod[0], base[0]), jnp.where(take, prod[1], base[1]))
    powers = [base, cmul(base, squares[0]), cmul(base, squares[1])]
    cols = jnp.concatenate([p for pw in powers[1:] for p in pw] + [jnp.zeros((SUBLANES - 4, STATE_HALF), F32)],
                           axis=0).T
    for d in range(PREP_LAGS):
        pr, pi = powers[d]
        w = jnp.concatenate([pr * bb_re - pi * bb_im, pr * bb_im + pi * bb_re], axis=1).astype(BF16)
        ws_ref[(PREP_LAGS - 1 - d) * LANES:(PREP_LAGS - d) * LANES, :] = w
        m_ref[d * LANES:(d + 1) * LANES, :] = jnp.dot(w, c0b, preferred_element_type=F32).astype(BF16)
        qr, qi = cols[:, 2 * d:2 * d + 1], cols[:, 2 * d + 1:2 * d + 2]
        wx_ref[0:STATE_HALF, d * LANES:(d + 1) * LANES] = (ct_re * qr - ct_im * qi).astype(BF16)
        wx_ref[STATE_HALF:2 * STATE_HALF, d * LANES:(d + 1) * LANES] = (-(ct_re * qi + ct_im * qr)).astype(BF16)
    coef_ref[...] = jnp.concatenate([
        jnp.concatenate([ab_re, ab_im], axis=1),
        jnp.concatenate(squares[4], axis=1),
        jnp.zeros((SUBLANES - 2, 2 * STATE_HALF), F32)], axis=0)


def _ssm_prep(lam_re, lam_im, log_dt, b_re, b_im, c_re, c_im, unit_of):
    groups, n = lam_re.shape
    tiles = groups // GROUPS_PER_TILE
    gpt = GROUPS_PER_TILE
    eye = jnp.eye(gpt, dtype=F32)

    def rows(v):
        return v.reshape(tiles, gpt * n)

    ldt = jnp.broadcast_to(log_dt[:, None], (groups, n))
    lam_row = jnp.stack([rows(lam_re), rows(lam_im), rows(ldt)], axis=1)

    def bt(b):
        b = jnp.transpose(b.reshape(tiles, gpt, n, SSM_CH), (0, 1, 3, 2))
        return (b[:, :, :, None, :] * eye[None, :, None, :, None]).reshape(tiles, gpt * SSM_CH, gpt * n)

    def ct(c):
        c = jnp.transpose(c.reshape(tiles, gpt, SSM_CH, n), (0, 1, 3, 2))
        return (c[:, :, :, None, :] * eye[None, :, None, :, None]).reshape(tiles, gpt * n, gpt * SSM_CH)

    sh, ln = STATE_HALF, LANES
    parts = PLANES // PREP_LAGS
    where = lambda *g: divmod(unit_of(*g), parts)
    tile_in = lambda a, b: pl.BlockSpec((None, a, b), lambda *g: (where(*g)[0], 0, 0))
    in_specs = [tile_in(3, sh), tile_in(ln, sh), tile_in(ln, sh), tile_in(sh, ln), tile_in(sh, ln)]
    out_specs = [pl.BlockSpec((None, PREP_LAGS * ln, 2 * sh), lambda *g: (where(*g)[0], parts - 1 - where(*g)[1], 0)),
                 pl.BlockSpec((None, 2 * sh, PREP_LAGS * ln), lambda *g: (where(*g)[0], 0, where(*g)[1])),
                 pl.BlockSpec((None, PREP_LAGS * ln, ln), lambda *g: (where(*g)[0], where(*g)[1], 0)),
                 tile_in(2 * sh, ln), tile_in(SUBLANES, 2 * sh)]
    out_shape = [jax.ShapeDtypeStruct((tiles, PLANES * ln, 2 * sh), BF16),
                 jax.ShapeDtypeStruct((tiles, 2 * sh, PLANES * ln), BF16),
                 jax.ShapeDtypeStruct((tiles, PLANES * ln, ln), BF16),
                 jax.ShapeDtypeStruct((tiles, 2 * sh, ln), BF16),
                 jax.ShapeDtypeStruct((tiles, SUBLANES, 2 * sh), F32)]
    return (lam_row, bt(b_re), bt(b_im), ct(c_re), ct(c_im)), in_specs, out_specs, out_shape, tiles * parts


def _inproj_kernel(first_keep, scale, x_ref, perm_ref, wqkv_ref, wu_ref, *rest):
    sample_in, prep_in = rest[:7], rest[7:12]
    q_ref, k_ref, v_ref, u_ref, kl_ref, vl_ref, sample_out = rest[12:19]
    prep_out, slab_ref = rest[19:24], rest[24]
    unit = pl.program_id(0) * pl.num_programs(1) + pl.program_id(1)
    _ssm_prep_part(unit % (PLANES // PREP_LAGS), *prep_in, *prep_out)
    slabs = slab_ref.shape[0]
    prow = q_ref.shape[1]
    width = u_ref.shape[-1]
    xp = jnp.dot(perm_ref[...], x_ref[...].astype(BF16), preferred_element_type=F32).astype(BF16)

    def to_planes(val, out_ref):
        for r in range(PLANES):
            out_ref[r] = val[r * prow:(r + 1) * prow].astype(out_ref.dtype)

    sq_ref, skn_ref, svn_ref, skt_ref, svt_ref, stb_ref, stn_ref = sample_in
    to_planes(jnp.dot(xp, wqkv_ref[:, 0:width], preferred_element_type=F32) * scale, q_ref)
    sample_scores = _sample_scores(sq_ref, skn_ref, skt_ref)
    k = jnp.dot(xp, wqkv_ref[:, width:2 * width], preferred_element_type=F32)
    to_planes(k, k_ref)
    v = jnp.dot(xp, wqkv_ref[:, 2 * width:3 * width], preferred_element_type=F32)
    to_planes(v, v_ref)
    _sample_outputs(sample_scores, svn_ref, svt_ref, stb_ref, stn_ref, sample_out)
    to_planes(jnp.dot(xp, wu_ref[...], preferred_element_type=F32), u_ref)

    @pl.when(pl.program_id(1) >= first_keep)
    def _():
        def token_order_t(val):
            for r in range(PLANES):
                for c in range(slabs):
                    slab_ref[c, pl.ds(r, prow, stride=PLANES), :] = val[r * prow:(r + 1) * prow,
                                                                        c * LANES:(c + 1) * LANES]
            return jnp.concatenate([slab_ref[c] for c in range(slabs)], axis=1).T
        kl_ref[...] = token_order_t(k)
        vl_ref[...] = token_order_t(v)


def _inproj_prompt(x, w_bf, keep, sample, ssm_params):
    bsz, seq, d_model = x.shape
    width = w_bf.shape[1] // 6
    rows = seq // PLANES
    tile = ROW_TILE // 2
    prow = tile // PLANES
    first_keep = (seq - keep) // tile
    steps = seq // tile
    unit = lambda b, i: b * steps + i
    sample_specs, sample_out_spec, sample_out = _attention_sample_specs(*sample, unit)
    assert bsz * steps == sample[0].shape[0] * (width // (SAMPLE_PAIRS * LANES)), "one sample unit per grid step"
    prep_args, prep_in, prep_out, prep_shapes, prep_units = _ssm_prep(*ssm_params, unit)
    assert prep_units == bsz * steps, "one slice of the state-space weight prep per grid step"
    plane = lambda dt: jax.ShapeDtypeStruct((bsz, PLANES, rows, width), dt)
    plane_spec = pl.BlockSpec((None, PLANES, prow, width), lambda b, i: (b, 0, i, 0))
    last = jax.ShapeDtypeStruct((bsz, width, keep), F32)
    last_spec = pl.BlockSpec((None, width, tile), lambda b, i: (b, 0, jnp.maximum(i - first_keep, 0)))
    perm = np.zeros((tile, tile), np.float32)
    perm[np.arange(tile), PLANES * (np.arange(tile) % prow) + np.arange(tile) // prow] = 1.0
    return pl.pallas_call(
        functools.partial(_inproj_kernel, first_keep, HEAD_DIM ** -0.5 * LOG2E),
        grid=(bsz, steps),
        in_specs=[pl.BlockSpec((None, tile, d_model), lambda b, i: (b, i, 0)),
                  pl.BlockSpec((tile, tile), lambda b, i: (0, 0), pipeline_mode=ONCE),
                  pl.BlockSpec((d_model, 3 * width), lambda b, i: (0, 0), pipeline_mode=ONCE),
                  pl.BlockSpec((d_model, width), lambda b, i: (0, 4), pipeline_mode=ONCE)] + sample_specs + prep_in,
        out_specs=[plane_spec, plane_spec, plane_spec, plane_spec, last_spec, last_spec, sample_out_spec] + prep_out,
        out_shape=[plane(F32), plane(F32), plane(F32), plane(BF16), last, last, sample_out] + prep_shapes,
        scratch_shapes=[pltpu.VMEM((width // LANES, tile, LANES), F32)],
        compiler_params=_cparams(("arbitrary", "arbitrary"), INPROJ_VMEM_LIMIT),
        name="inproj_prompt",
    )(x, jnp.asarray(perm, BF16), w_bf, w_bf, sample[0], sample[0], sample[0], *sample[1:], *prep_args)


ATTN_UNROLL = 2


def _attn_group_rows(dil, grp):
    npl = PLANES // dil
    qrows = BLK // npl
    out = set()
    for un in range(ATTN_UNROLL):
        res, sub = divmod(grp * ATTN_UNROLL + un, BLK // qrows)
        out |= {(res + dil * i, r) for i in range(npl) for r in range(sub * qrows, (sub + 1) * qrows)}
    return out


def _attn_kernel(n_cast, q_ref, kp_ref, kc_ref, vp_ref, vc_ref, bias_ref, *rest):
    cast_in, o_ref, cast_out = rest[:n_cast], rest[n_cast], rest[n_cast + 1:2 * n_cast + 1]
    acc_ref, m_ref, l_ref, s_ref, p_ref, mn_ref = rest[2 * n_cast + 1:]
    for src, dst in zip(cast_in, cast_out):
        dst[...] = src[...].astype(BF16)
    sup = pl.program_id(2)
    is_a = lax.broadcasted_iota(jnp.int32, (BLK, LANES), 1) < HEAD_DIM

    def raw_scores(q, k):
        kb = k.astype(BF16)
        out = []
        for head in range(2):
            qm = jnp.where(is_a if head == 0 else jnp.logical_not(is_a), q, 0.0).astype(BF16)
            out.append(lax.dot_general(qm, kb, (((1,), (1,)), ((), ())), preferred_element_type=F32))
        return out

    def both(a, b):
        return jnp.where(is_a, a, b)

    def weights(s, table, m_old):
        m_new, p = [], []
        for h in range(2):
            sh = s[h] + bias_ref[h, table]
            mb = jnp.broadcast_to(jnp.max(sh, axis=1, keepdims=True), (BLK, LANES))
            mh = mb if m_old is None else jnp.maximum(m_old[h], mb)
            m_new.append(mh)
            p.append(jnp.exp2(sh - jnp.concatenate([mh, mh], axis=1)).astype(BF16))
        return m_new, p

    ones = jnp.ones((2 * BLK, LANES), BF16)

    def combine(m_new, p, v, state):
        vext = jnp.concatenate([v.astype(BF16), ones], axis=1)
        res = [jnp.dot(p[h], vext, preferred_element_type=F32) for h in range(2)]
        acc_new = both(res[0][:, :LANES], res[1][:, :LANES])
        l_new = [res[h][:, LANES:] for h in range(2)]
        if state is not None:
            alpha = [jnp.exp2(state[0][h] - m_new[h]) for h in range(2)]
            l_new = [alpha[h] * state[1][h] + l_new[h] for h in range(2)]
            acc_new = both(alpha[0], alpha[1]) * state[2] + acc_new
        return m_new, l_new, acc_new

    cat = lambda parts: parts[0] if len(parts) == 1 else jnp.concatenate(parts, axis=0)

    def gather(ref, planes, start, size, *lead):
        return cat([ref[(*lead, pln, pl.ds(start, size), slice(None))] for pln in planes])

    def pattern_stages(a, dil):
        npl = PLANES // dil
        qrows = BLK // npl
        per_res = BLK // qrows

        def geometry(grp, un):
            res, sub = divmod(grp * ATTN_UNROLL + un, per_res)
            off = sub * qrows
            table = 2 * a + (jnp.where(sup == 0, 1, 0) if sub == 0 else 0)
            return [res + dil * i for i in range(npl)], off, table

        def keys(prev_ref, cur_ref, planes, off):
            if off:
                return cat([cur_ref[pln, off - qrows:off + qrows, :] for pln in planes])
            return cat([part for pln in planes for part in (prev_ref[pln, BLK - qrows:BLK, :], cur_ref[pln, 0:qrows, :])])

        def stage_scores(grp, slot):
            stores = []
            for un in range(ATTN_UNROLL):
                planes, off, _ = geometry(grp, un)
                s = raw_scores(gather(q_ref, planes, off, qrows), keys(kp_ref, kc_ref, planes, off))
                stores += [(s_ref, (slot, un, h), s[h]) for h in range(2)]
            return stores

        def stage_softmax(grp, slot):
            stores = []
            for un in range(ATTN_UNROLL):
                planes, off, table = geometry(grp, un)
                m_old = None if a == 0 else [gather(m_ref, planes, off, qrows, h) for h in range(2)]
                m_new, p = weights([s_ref[slot, un, h] for h in range(2)], table, m_old)
                for h in range(2):
                    stores += [(p_ref, (slot, un, h), p[h]), (mn_ref, (slot, un, h), m_new[h])]
            return stores

        def stage_values(grp, slot):
            stores = []
            for un in range(ATTN_UNROLL):
                planes, off, _ = geometry(grp, un)
                old = None if a == 0 else ([gather(m_ref, planes, off, qrows, h) for h in range(2)],
                                           [gather(l_ref, planes, off, qrows, h) for h in range(2)],
                                           gather(acc_ref, planes, off, qrows))
                m_new, l_new, acc_new = combine([mn_ref[slot, un, h] for h in range(2)],
                                                [p_ref[slot, un, h] for h in range(2)],
                                                keys(vp_ref, vc_ref, planes, off), old)
                for i, pln in enumerate(planes):
                    part = slice(i * qrows, (i + 1) * qrows)
                    rows = (pln, pl.ds(off, qrows), slice(None))
                    for h in range(2):
                        stores += [(m_ref, (h,) + rows, m_new[h][part]), (l_ref, (h,) + rows, l_new[h][part])]
                    stores.append((acc_ref, rows, acc_new[part]))
            return stores

        return stage_scores, stage_softmax, stage_values

    groups = PLANES // ATTN_UNROLL
    work = [(stages, grp) for stages in (pattern_stages(a, dil) for a, dil in enumerate(DILATIONS))
            for grp in range(groups)]
    rows_of = [_attn_group_rows(dil, grp) for dil in DILATIONS for grp in range(groups)]
    assert all(not (rows_of[j] & rows_of[j - 1]) for j in range(1, len(work))), "groups sharing a step overlap"
    for step in range(len(work) + 2):
        stores = []
        for stage, lag in ((2, 2), (1, 1), (0, 0)):
            j = step - lag
            if 0 <= j < len(work):
                stages, grp = work[j]
                stores += stages[stage](grp, j % 2)
        for ref, idx, val in stores:
            ref[idx] = val
    for pln in range(PLANES):
        o_ref[pl.ds(pln, BLK, stride=PLANES), :] = acc_ref[pln] / both(l_ref[0, pln], l_ref[1, pln])


def _attention_prompt(q, k, v, bias, to_cast):
    bsz, _, rows, width = q.shape
    pairs = width // LANES
    ntab = bias.shape[1]
    sups = rows // BLK
    steps = bsz * pairs * sups
    unit = lambda b, h, s: (b * pairs + h) * sups + s
    cur = pl.BlockSpec((None, PLANES, BLK, LANES), lambda b, h, s: (b, 0, s, h))
    prev = pl.BlockSpec((None, PLANES, BLK, LANES), lambda b, h, s: (b, 0, jnp.maximum(s - 1, 0), h))
    cast_in = [pl.BlockSpec((m.shape[0] // steps, w), lambda b, h, s, c=c: (unit(b, h, s), c)) for m, w, c in to_cast]
    cast_out = [pl.BlockSpec((m.shape[0] // steps, w), lambda b, h, s: (unit(b, h, s), 0)) for m, w, _ in to_cast]
    return pl.pallas_call(
        functools.partial(_attn_kernel, len(to_cast)),
        grid=(bsz, pairs, sups),
        in_specs=[cur, prev, cur, prev, cur,
                  pl.BlockSpec((2, ntab, BLK, 2 * BLK), lambda b, h, s: (h, 0, 0, 0))] + cast_in,
        out_specs=[pl.BlockSpec((None, PLANES * BLK, LANES), lambda b, h, s: (b, s, h))] + cast_out,
        out_shape=[jax.ShapeDtypeStruct((bsz, PLANES * rows, width), F32)]
        + [jax.ShapeDtypeStruct((m.shape[0], w), BF16) for m, w, _ in to_cast],
        scratch_shapes=[pltpu.VMEM((PLANES, BLK, LANES), F32),
                        pltpu.VMEM((2, PLANES, BLK, LANES), F32),
                        pltpu.VMEM((2, PLANES, BLK, LANES), F32),
                        pltpu.VMEM((2, ATTN_UNROLL, 2, BLK, 2 * BLK), F32),
                        pltpu.VMEM((2, ATTN_UNROLL, 2, BLK, 2 * BLK), BF16),
                        pltpu.VMEM((2, ATTN_UNROLL, 2, BLK, LANES), F32)],
        compiler_params=_cparams(("arbitrary", "arbitrary", "arbitrary")),
        name="attention_prompt",
    )(q, k, k, v, v, bias, *[m for m, _, _ in to_cast])


SSM_COLS = 4


def _ssm_kernel(u_ref, ws_ref, wx_ref, mlag_ref, coef_ref, d_ref, y_ref, sre_ref, sim_ref,
                ucat_ref, s_ref, x_ref, m_ref):
    chunks = u_ref.shape[1]

    @pl.when(pl.program_id(1) == 0)
    def _():
        zero_blk = jnp.zeros((LANES, LANES), BF16)
        for s in range(PLANES):
            for t in range(PLANES):
                blk = mlag_ref[(t - s) * LANES:(t - s + 1) * LANES, :] if t >= s else zero_blk
                m_ref[s * LANES:(s + 1) * LANES, t * LANES:(t + 1) * LANES] = blk

    for s in range(PLANES):
        ucat_ref[:, s * LANES:(s + 1) * LANES] = u_ref[s]
    s_ref[...] = jnp.dot(ucat_ref[...], ws_ref[...], preferred_element_type=F32)
    a_re = coef_ref[1:2, 0:STATE_HALF]
    a_im = coef_ref[1:2, STATE_HALF:2 * STATE_HALF]

    def step(c, carry):
        xr, xi = carry
        x_ref[pl.ds(c, 1), 0:STATE_HALF] = xr
        x_ref[pl.ds(c, 1), STATE_HALF:2 * STATE_HALF] = xi
        sr = s_ref[pl.ds(c, 1), 0:STATE_HALF]
        si = s_ref[pl.ds(c, 1), STATE_HALF:2 * STATE_HALF]
        return a_re * xr - a_im * xi + sr, a_re * xi + a_im * xr + si

    zero = jnp.zeros((1, STATE_HALF), F32)
    xr, xi = lax.fori_loop(0, chunks, step, (zero, zero))
    sre_ref[...] = xr
    sim_ref[...] = xi

    xb = x_ref[...].astype(BF16)
    d = d_ref[...]
    for g in range(PLANES // SSM_COLS):
        cols = slice(g * SSM_COLS * LANES, (g + 1) * SSM_COLS * LANES)
        used = (g + 1) * SSM_COLS * LANES
        y = jnp.dot(ucat_ref[:, :used], m_ref[:used, cols], preferred_element_type=F32)
        y = y + jnp.dot(xb, wx_ref[:, cols], preferred_element_type=F32)
        for t in range(SSM_COLS):
            tok = g * SSM_COLS + t
            y_ref[pl.ds(tok, chunks, stride=PLANES), :] = (y[:, t * LANES:(t + 1) * LANES]
                                                          + d * u_ref[tok].astype(F32))


def _ssm_prompt(u, ws, wx, mlag, coef, d_skip):
    bsz, _, chunks, width = u.shape
    tiles = width // LANES
    sh = STATE_HALF
    state = jax.ShapeDtypeStruct((bsz, tiles, 1, sh), F32)
    state_spec = pl.BlockSpec((None, None, 1, sh), lambda t, b: (b, t, 0, 0))
    per_tile = lambda r, c: pl.BlockSpec((None, r, c), lambda t, b: (t, 0, 0))
    return pl.pallas_call(
        _ssm_kernel,
        grid=(tiles, bsz),
        in_specs=[pl.BlockSpec((None, PLANES, chunks, LANES), lambda t, b: (b, 0, 0, t)),
                  per_tile(PLANES * LANES, 2 * sh), per_tile(2 * sh, PLANES * LANES),
                  per_tile(PLANES * LANES, LANES), per_tile(SUBLANES, 2 * sh),
                  pl.BlockSpec((1, LANES), lambda t, b: (0, t))],
        out_specs=[pl.BlockSpec((None, PLANES * chunks, LANES), lambda t, b: (b, 0, t)),
                   state_spec, state_spec],
        out_shape=[jax.ShapeDtypeStruct((bsz, PLANES * chunks, width), F32), state, state],
        scratch_shapes=[pltpu.VMEM((chunks, PLANES * LANES), BF16),
                        pltpu.VMEM((chunks, 2 * sh), F32),
                        pltpu.VMEM((chunks, 2 * sh), F32),
                        pltpu.VMEM((PLANES * LANES, PLANES * LANES), BF16)],
        compiler_params=_cparams(("arbitrary", "arbitrary")),
        name="ssm_prompt",
    )(u, ws, wx, mlag, coef, d_skip)


def _epilogue_kernel(alpha, attn_ref, y_ref, x_ref, wga_ref, wgs_ref, wglu_ref, bglu_ref, wo_ref, bo_ref,
                     g_ref, b_ref, o_ref):
    x = x_ref[...]
    xb = x.astype(BF16)
    residual = alpha * x + bo_ref[...]
    g_attn = jnp.dot(xb, wga_ref[...], preferred_element_type=F32)
    br_a = (attn_ref[...] * jax.nn.silu(g_attn)).astype(BF16)
    g_ssm = jnp.dot(xb, wgs_ref[...], preferred_element_type=F32)
    z = jax.nn.gelu(y_ref[...])
    gate = jax.nn.sigmoid(jnp.dot(z.astype(BF16), wglu_ref[...], preferred_element_type=F32) + bglu_ref[...])
    br_s = (z * gate * jax.nn.silu(g_ssm)).astype(BF16)
    t = residual + jnp.dot(jnp.concatenate([br_a, br_s], axis=1), wo_ref[...], preferred_element_type=F32)
    mu = jnp.mean(t, axis=-1, keepdims=True)
    var = jnp.mean(jnp.square(t - mu), axis=-1, keepdims=True)
    o_ref[...] = (t - mu) * lax.rsqrt(var + LN_EPS) * g_ref[...] + b_ref[...]


def _epilogue(alpha, name, attn, y, x, weights, tile):
    lead = attn.shape[:-2]
    rows, width = attn.shape[-2:]
    d_model = x.shape[-1]
    grid = lead + (rows // tile,)
    none = (None,) * len(lead)
    tiled = lambda w: pl.BlockSpec(none + (tile, w), lambda *g: g + (0,))
    const = lambda shape, col=0: pl.BlockSpec(shape, lambda *g: (0, col), pipeline_mode=ONCE)
    return pl.pallas_call(
        functools.partial(_epilogue_kernel, alpha),
        grid=grid,
        in_specs=[tiled(width), tiled(width), tiled(d_model),
                  const((d_model, width), 3), const((d_model, width), 5),
                  const((width, width)), const((1, width)), const((2 * width, d_model)), const((1, d_model)),
                  const((1, d_model)), const((1, d_model))],
        out_specs=tiled(d_model),
        out_shape=jax.ShapeDtypeStruct(x.shape, F32),
        compiler_params=_cparams(("arbitrary",) * len(grid)),
        name=name,
    )(attn, y, x, *weights)


def _inproj_sample_kernel(scale, x_ref, w_ref, o_ref):
    acc = jnp.dot(x_ref[...].astype(BF16), w_ref[...], preferred_element_type=F32)
    o_ref[...] = acc * jnp.where(pl.program_id(0) == 0, scale, 1.0)


def _inproj_sample(x, w_bf):
    rows, d_model = x.shape
    width = w_bf.shape[1] // 6
    return pl.pallas_call(
        functools.partial(_inproj_sample_kernel, HEAD_DIM ** -0.5),
        grid=(4,),
        in_specs=[pl.BlockSpec((rows, d_model), lambda j: (0, 0)),
                  pl.BlockSpec((d_model, width), lambda j: (0, j + j // 3))],
        out_specs=pl.BlockSpec((rows, width), lambda j: (0, j)),
        out_shape=jax.ShapeDtypeStruct((rows, 4 * width), F32),
        compiler_params=_cparams(("arbitrary",)),
        name="inproj_sample",
    )(x, w_bf)


SAMPLE_PAIRS = 4


_NT = (((1,), (1,)), ((), ()))


def _sample_scores(q_ref, kn_ref, kt_ref):
    s_len = q_ref.shape[0]
    buf = kt_ref.shape[2]
    is_a = lax.broadcasted_iota(jnp.int32, (s_len, LANES), 1) < HEAD_DIM
    out = []
    for pp in range(SAMPLE_PAIRS):
        lanes = slice(pp * LANES, (pp + 1) * LANES)
        qp = q_ref[:, lanes]
        q2 = jnp.concatenate([jnp.where(is_a, qp, 0.0), jnp.where(is_a, 0.0, qp)], axis=0)
        kt = kt_ref[2 * pp:2 * pp + 2].reshape(2 * HEAD_DIM, buf).astype(BF16)
        kn = kn_ref[:, lanes]
        s_new = [jnp.sum(q2 * kn[j:j + 1, :], axis=1, keepdims=True) for j in range(s_len)]
        out.append((jnp.dot(q2.astype(BF16), kt, preferred_element_type=F32), s_new))
    return out


def _sample_outputs(scores, vn_ref, vt_ref, tb_ref, tn_ref, o_ref):
    s_len = vn_ref.shape[0]
    buf = vt_ref.shape[2]
    npat = tb_ref.shape[0]
    is_a = lax.broadcasted_iota(jnp.int32, (s_len, LANES), 1) < HEAD_DIM
    for pp, (s_buf, s_new) in enumerate(scores):
        lanes = slice(pp * LANES, (pp + 1) * LANES)
        rows = slice(pp * 2 * s_len, (pp + 1) * 2 * s_len)
        vt = vt_ref[2 * pp:2 * pp + 2].reshape(2 * HEAD_DIM, buf).astype(BF16)
        vn = vn_ref[:, lanes]
        z_buf = [s_buf + tb_ref[a, rows, :] for a in range(npat)]
        z_new = [[s_new[j] + tn_ref[a, rows, j:j + 1] for a in range(npat)] for j in range(s_len)]
        top = functools.reduce(jnp.maximum, [jnp.max(z, axis=1, keepdims=True) for z in z_buf]
                               + [z for zs in z_new for z in zs])
        w_buf = functools.reduce(jnp.add, [jnp.exp(z - top) for z in z_buf])
        w_new = [functools.reduce(jnp.add, [jnp.exp(z - top) for z in zs]) for zs in z_new]
        den = jnp.sum(w_buf, axis=1, keepdims=True) + functools.reduce(jnp.add, w_new)
        o = lax.dot_general(vt, w_buf.astype(BF16), _NT, preferred_element_type=F32).T
        o = (o + functools.reduce(jnp.add, [w_new[j] * vn[j:j + 1, :] for j in range(s_len)])) / den
        o_ref[:, lanes] = jnp.where(is_a, o[0:s_len], o[s_len:2 * s_len])


def _attention_sample_specs(h, kt, vt, tb, tn, unit_of):
    bsz, s_len, _ = h.shape
    heads, _, buf = kt.shape[1:]
    width = heads * HEAD_DIM
    gw = SAMPLE_PAIRS * LANES
    per_w = width // gw
    trows = SAMPLE_PAIRS * 2 * s_len
    where = lambda *g: divmod(unit_of(*g), per_w)
    new = lambda c: pl.BlockSpec((None, s_len, gw), lambda *g: (where(*g)[0], 0, c * per_w + where(*g)[1]))
    cache = pl.BlockSpec((None, 2 * SAMPLE_PAIRS, HEAD_DIM, buf), lambda *g: (where(*g)[0], where(*g)[1], 0, 0))
    specs = [new(0), new(1), new(2), cache, cache,
             pl.BlockSpec((tb.shape[0], trows, buf), lambda *g: (0, where(*g)[1], 0)),
             pl.BlockSpec((tn.shape[0], trows, LANES), lambda *g: (0, where(*g)[1], 0))]
    out_spec = pl.BlockSpec((None, s_len, gw), lambda *g: (where(*g)[0], 0, where(*g)[1]))
    return specs, out_spec, jax.ShapeDtypeStruct((bsz, s_len, width), F32)


def _ssm_sample_kernel(s_len, u_ref, b0_ref, c0_ref, coef_ref, d_ref, x0r_ref, x0i_ref,
                       y_ref, sre_ref, sim_ref, bu_ref, xs_ref):
    bsz = x0r_ref.shape[0]
    bu = jnp.dot(u_ref[...].astype(BF16), b0_ref[...], preferred_element_type=F32)
    slabs = bu_ref.shape[0]
    for c in range(slabs):
        bu_ref[c] = bu[:, c * LANES:(c + 1) * LANES]
    a_re = coef_ref[0:1, 0:STATE_HALF]
    a_im = coef_ref[0:1, STATE_HALF:2 * STATE_HALF]
    half = slabs // 2
    xr, xi = x0r_ref[...], x0i_ref[...]
    for s in range(s_len):
        step = lambda c: bu_ref[c, pl.ds(s, bsz, stride=s_len), :]
        br = jnp.concatenate([step(c) for c in range(half)], axis=1)
        bi = jnp.concatenate([step(c) for c in range(half, slabs)], axis=1)
        xr, xi = a_re * xr - a_im * xi + br, a_re * xi + a_im * xr + bi
        xs_ref[s * bsz:(s + 1) * bsz, :] = jnp.concatenate([xr, xi], axis=1)
    sre_ref[...] = xr
    sim_ref[...] = xi
    y = jnp.dot(xs_ref[...].astype(BF16), c0_ref[...], preferred_element_type=F32)
    d = d_ref[...]
    for s in range(s_len):
        y_ref[pl.ds(s, bsz, stride=s_len), :] = (y[s * bsz:(s + 1) * bsz, :]
                                                 + d * u_ref[pl.ds(s, bsz, stride=s_len), :])


def _ssm_sample(h, ws, c0, coef, d_skip, x0_re, x0_im, s_len):
    rows = h.shape[0]
    bsz = rows // s_len
    tiles = ws.shape[0]
    sh = STATE_HALF
    u_col0 = 3 * (h.shape[1] // 4) // LANES
    st_spec = pl.BlockSpec((bsz, sh), lambda t: (0, t))
    state = jax.ShapeDtypeStruct((bsz, tiles * sh), F32)
    return pl.pallas_call(
        functools.partial(_ssm_sample_kernel, s_len),
        grid=(tiles,),
        in_specs=[pl.BlockSpec((rows, LANES), lambda t: (0, u_col0 + t)),
                  pl.BlockSpec((None, LANES, 2 * sh), lambda t: (t, PLANES - 1, 0)),
                  pl.BlockSpec((None, 2 * sh, LANES), lambda t: (t, 0, 0)),
                  pl.BlockSpec((None, SUBLANES, 2 * sh), lambda t: (t, 0, 0)),
                  pl.BlockSpec((1, LANES), lambda t: (0, t)),
                  st_spec, st_spec],
        out_specs=[pl.BlockSpec((rows, LANES), lambda t: (0, t)), st_spec, st_spec],
        out_shape=[jax.ShapeDtypeStruct((rows, tiles * LANES), F32), state, state],
        scratch_shapes=[pltpu.VMEM((2 * sh // LANES, rows, LANES), F32),
                        pltpu.VMEM((rows, 2 * sh), F32)],
        compiler_params=_cparams(("arbitrary",)),
        name="ssm_sample",
    )(h, ws, c0, coef, d_skip, x0_re, x0_im)


def kernel(x_prompt, x_sample, cache_k, cache_v, state_ssm_re, state_ssm_im, w_in, w_out, b_out, rel_bias,
           lam_re, lam_im, log_dt, b_re, b_im, c_re, c_im, d_skip, w_glu, b_glu, ln_g, ln_b):
    depth = w_in.shape[0]
    assert depth == 1, "one layer per step"
    bsz, seq, d_model = x_prompt.shape
    dbsz, s_len, _ = x_sample.shape
    buf, heads = cache_k.shape[2], cache_k.shape[3]
    width = heads * HEAD_DIM
    groups, nstate = lam_re.shape[1], lam_re.shape[2]
    keep = min(MAX_DISTANCE, seq)
    assert seq % (PLANES * BLK) == 0 and seq >= 2 * PLANES * BLK and keep % ROW_TILE == 0
    assert buf == KPER * max(DILATIONS) and s_len <= min(DILATIONS[:-1])
    assert nstate == SSM_STATE and width == groups * SSM_CH and heads % (2 * SAMPLE_PAIRS) == 0
    alpha = (2 * depth) ** 0.25
    npat = len(DILATIONS)

    w_bf = w_in[0].astype(BF16)
    to_cast = [(w_glu[0], width, 0), (w_out[0], d_model, 0)]
    row = lambda v: v.reshape(1, -1)
    d_row = row(d_skip[0])

    rbt = rel_bias.T
    ptab = _bias_tables(jnp.asarray(_prompt_bucket_tables().reshape(1, -1)), rbt, BLK * 2 * BLK, LOG2E)
    ptab = ptab.reshape(heads, 2 * npat, BLK, 2 * BLK)
    key_w = buf + 2 * LANES
    stab = _bias_tables(jnp.asarray(_sample_bucket_tables(buf, s_len, key_w)), rbt, npat * s_len * key_w)
    stab = jnp.transpose(stab.reshape(heads, npat, s_len, key_w), (1, 0, 2, 3)).reshape(npat, heads * s_len, key_w)
    stab_buf, stab_new = stab[:, :, :buf], stab[:, :, buf:buf + LANES]
    ssm_params = (lam_re[0], lam_im[0], log_dt[0], b_re[0], b_im[0], c_re[0], c_im[0])

    xs = x_sample.reshape(dbsz * s_len, d_model)
    hs = _inproj_sample(xs, w_bf)
    pos_minor = lambda c: jnp.transpose(c[0], (0, 2, 3, 1))
    sample = (hs.reshape(dbsz, s_len, -1), pos_minor(cache_k), pos_minor(cache_v), stab_buf, stab_new)

    q, k, v, u, kl_t, vl_t, attn_s, ws, wx, m_intra, c0, coef = _inproj_prompt(x_prompt, w_bf, keep, sample,
                                                                              ssm_params)
    attn, w_glu_bf, w_out_bf = _attention_prompt(q, k, v, ptab, to_cast)
    weights = (w_bf, w_bf, w_glu_bf, row(b_glu[0]), w_out_bf, row(b_out[0]), row(ln_g[0]), row(ln_b[0]))
    y, sre_p, sim_p = _ssm_prompt(u, ws, wx, m_intra, coef, d_row)
    y_prompt = _epilogue(alpha, "epilogue_prompt", attn, y, x_prompt, weights, ROW_TILE // 2)
    last = lambda t: jnp.transpose(t.reshape(bsz, heads, HEAD_DIM, keep), (0, 3, 1, 2))[None]
    st_shape = (1, bsz, groups, nstate)

    y_s, sre_s, sim_s = _ssm_sample(hs, ws, c0, coef, d_row,
                                    state_ssm_re[0].astype(F32).reshape(dbsz, groups * nstate),
                                    state_ssm_im[0].astype(F32).reshape(dbsz, groups * nstate), s_len)
    y_sample = _epilogue(alpha, "epilogue_sample", attn_s.reshape(dbsz * s_len, width), y_s, xs, weights,
                         dbsz * s_len)
    new_shape = (1, dbsz, s_len, heads, HEAD_DIM)
    sst_shape = (1, dbsz, groups, nstate)
    return (y_prompt, y_sample.reshape(dbsz, s_len, d_model), last(kl_t), last(vl_t),
            sre_p.reshape(st_shape), sim_p.reshape(st_shape),
            hs[:, width:2 * width].reshape(new_shape), hs[:, 2 * width:3 * width].reshape(new_shape),
            sre_s.reshape(sst_shape), sim_s.reshape(sst_shape))
```

```python
import functools
import math

import jax
import jax.numpy as jnp
import numpy as np
from jax import lax
from jax.experimental import pallas as pl
from jax.experimental.pallas import tpu as pltpu

F32 = jnp.float32
BF16 = jnp.bfloat16

HEAD_DIM = 64
SSM_CH = 16
SSM_STATE = 64
NUM_BUCKETS = 32
MAX_DISTANCE = 2048
KPER = 128
BLK = 128
DILATIONS = (16, 4, 1)
LN_EPS = 1e-5
NEG = -1e30
LOG2E = math.log2(math.e)

LANES = 128
SUBLANES = 8
PLANES = 16
GROUPS_PER_TILE = LANES // SSM_CH
STATE_HALF = GROUPS_PER_TILE * SSM_STATE
VMEM_LIMIT = 56 * 1024 * 1024
INPROJ_VMEM_LIMIT = 60 * 1024 * 1024
ROW_TILE = 512
ONCE = pl.Buffered(1)


def _cparams(sem, vmem=VMEM_LIMIT):
    return pltpu.CompilerParams(dimension_semantics=sem, vmem_limit_bytes=vmem)


def _bucket_np(dist):
    exact = NUM_BUCKETS // 2
    d_f = np.maximum(dist, 1).astype(np.float32)
    large = exact + (np.log(d_f / np.float32(exact)) / np.float32(math.log(MAX_DISTANCE / exact))
                     * np.float32(NUM_BUCKETS - exact)).astype(np.int32)
    large = np.minimum(large, NUM_BUCKETS - 1)
    return np.where(dist < exact, dist, large).astype(np.int32)


def _prompt_rel(dil):
    i = np.arange(BLK)[:, None]
    j = np.arange(2 * BLK)[None, :]
    npl = PLANES // dil
    qrows = BLK // npl
    pq, ml = i // qrows, i % qrows
    pk, jl = j // (2 * qrows), j % (2 * qrows)
    return npl * (ml - jl + qrows) + (pq - pk), np.broadcast_to(jl < qrows, (BLK, 2 * BLK))


def _prompt_bucket_tables():
    tabs = []
    for dil in DILATIONS:
        rel, earlier = _prompt_rel(dil)
        for first in (False, True):
            valid = (rel >= 0) & (rel <= KPER) & np.logical_not(earlier & first)
            tabs.append(np.where(valid, _bucket_np(np.clip(rel, 0, KPER) * dil), NUM_BUCKETS))
    return np.stack(tabs).reshape(len(DILATIONS) * 2, BLK * 2 * BLK).astype(np.int32)


def _sample_bucket_tables(buf, s_len, width):
    pos = np.arange(buf + s_len)
    tabs = np.full((len(DILATIONS), s_len, width), NUM_BUCKETS, np.int32)
    for a, dil in enumerate(DILATIONS):
        for s in range(s_len):
            dist = buf + s - pos
            valid = (dist >= 0) & (dist % dil == 0) & (dist // dil <= KPER)
            tabs[a, s, :len(pos)] = np.where(valid, _bucket_np(np.maximum(dist, 0)), NUM_BUCKETS)
    return tabs.reshape(1, -1)


def _bias_kernel(scale, idx_ref, rbt_ref, o_ref):
    idx = idx_ref[...]
    onehot = (lax.broadcasted_iota(jnp.int32, (NUM_BUCKETS, idx.shape[1]), 0) == idx).astype(BF16)
    rb = rbt_ref[...]
    n_heads = rb.shape[0]
    hi = rb.astype(BF16)
    lo = (rb - hi.astype(F32)).astype(BF16)
    res = jnp.dot(jnp.concatenate([hi, lo], axis=0), onehot, preferred_element_type=F32)
    tab = res[:n_heads] + res[n_heads:]
    o_ref[...] = jnp.where(idx < NUM_BUCKETS, tab * scale, NEG)


def _bias_tables(idx, rel_bias_t, chunk, scale=1.0):
    n_heads = rel_bias_t.shape[0]
    total = idx.shape[1]
    return pl.pallas_call(
        functools.partial(_bias_kernel, scale),
        grid=(total // chunk,),
        in_specs=[pl.BlockSpec((1, chunk), lambda c: (0, c)),
                  pl.BlockSpec((n_heads, NUM_BUCKETS), lambda c: (0, 0))],
        out_specs=pl.BlockSpec((n_heads, chunk), lambda c: (0, c)),
        out_shape=jax.ShapeDtypeStruct((n_heads, total), F32),
        compiler_params=_cparams(("arbitrary",)),
        name="bias_tables",
    )(idx, rel_bias_t)


def _discretize(lr, li, ldt):
    lr = jnp.minimum(lr, -1e-4)
    dt = jnp.exp(ldt)
    mag = jnp.exp(lr * dt)
    ab_re, ab_im = mag * jnp.cos(li * dt), mag * jnp.sin(li * dt)
    den = lr * lr + li * li
    inv_re, inv_im = lr / den, -li / den
    n_re, n_im = ab_re - 1.0, ab_im
    cf_re = n_re * inv_re - n_im * inv_im
    cf_im = n_re * inv_im + n_im * inv_re
    return ab_re, ab_im, cf_re, cf_im


PREP_LAGS = 2


def _ssm_prep_part(part, lam_row_ref, bt_re_ref, bt_im_ref, ct_re_ref, ct_im_ref,
                   ws_ref, wx_ref, m_ref, c0_ref, coef_ref):
    row = lam_row_ref[...]
    ab_re, ab_im, cf_re, cf_im = _discretize(row[0:1], row[1:2], row[2:3])
    bt_re, bt_im = bt_re_ref[...], bt_im_ref[...]
    bb_re = cf_re * bt_re - cf_im * bt_im
    bb_im = cf_re * bt_im + cf_im * bt_re
    ct_re, ct_im = ct_re_ref[...], ct_im_ref[...]
    c0b = jnp.concatenate([ct_re, -ct_im], axis=0).astype(BF16)
    c0_ref[...] = c0b

    cmul = lambda a, b: (a[0] * b[0] - a[1] * b[1], a[0] * b[1] + a[1] * b[0])
    squares = [(ab_re, ab_im)]
    while len(squares) < 5:
        squares.append(cmul(squares[-1], squares[-1]))
    base = (jnp.ones_like(ab_re), jnp.zeros_like(ab_re))
    for bit in range(3):
        factor = squares[bit + 1]
        take = ((part >> bit) & 1) == 1
        prod = cmul(base, factor)
        base = (jnp.where(take, prod[0], base[0]), jnp.where(take, prod[1], base[1]))
    powers = [base, cmul(base, squares[0]), cmul(base, squares[1])]
    cols = jnp.concatenate([p for pw in powers[1:] for p in pw] + [jnp.zeros((SUBLANES - 4, STATE_HALF), F32)],
                           axis=0).T
    for d in range(PREP_LAGS):
        pr, pi = powers[d]
        w = jnp.concatenate([pr * bb_re - pi * bb_im, pr * bb_im + pi * bb_re], axis=1).astype(BF16)
        ws_ref[(PREP_LAGS - 1 - d) * LANES:(PREP_LAGS - d) * LANES, :] = w
        m_ref[d * LANES:(d + 1) * LANES, :] = jnp.dot(w, c0b, preferred_element_type=F32).astype(BF16)
        qr, qi = cols[:, 2 * d:2 * d + 1], cols[:, 2 * d + 1:2 * d + 2]
        wx_ref[0:STATE_HALF, d * LANES:(d + 1) * LANES] = (ct_re * qr - ct_im * qi).astype(BF16)
        wx_ref[STATE_HALF:2 * STATE_HALF, d * LANES:(d + 1) * LANES] = (-(ct_re * qi + ct_im * qr)).astype(BF16)
    coef_ref[...] = jnp.concatenate([
        jnp.concatenate([ab_re, ab_im], axis=1),
        jnp.concatenate(squares[4], axis=1),
        jnp.zeros((SUBLANES - 2, 2 * STATE_HALF), F32)], axis=0)


def _ssm_prep(lam_re, lam_im, log_dt, b_re, b_im, c_re, c_im, unit_of):
    groups, n = lam_re.shape
    tiles = groups // GROUPS_PER_TILE
    gpt = GROUPS_PER_TILE
    eye = jnp.eye(gpt, dtype=F32)

    def rows(v):
        return v.reshape(tiles, gpt * n)

    ldt = jnp.broadcast_to(log_dt[:, None], (groups, n))
    lam_row = jnp.stack([rows(lam_re), rows(lam_im), rows(ldt)], axis=1)

    def bt(b):
        b = jnp.transpose(b.reshape(tiles, gpt, n, SSM_CH), (0, 1, 3, 2))
        return (b[:, :, :, None, :] * eye[None, :, None, :, None]).reshape(tiles, gpt * SSM_CH, gpt * n)

    def ct(c):
        c = jnp.transpose(c.reshape(tiles, gpt, SSM_CH, n), (0, 1, 3, 2))
        return (c[:, :, :, None, :] * eye[None, :, None, :, None]).reshape(tiles, gpt * n, gpt * SSM_CH)

    sh, ln = STATE_HALF, LANES
    parts = PLANES // PREP_LAGS
    where = lambda *g: divmod(unit_of(*g), parts)
    tile_in = lambda a, b: pl.BlockSpec((None, a, b), lambda *g: (where(*g)[0], 0, 0))
    in_specs = [tile_in(3, sh), tile_in(ln, sh), tile_in(ln, sh), tile_in(sh, ln), tile_in(sh, ln)]
    out_specs = [pl.BlockSpec((None, PREP_LAGS * ln, 2 * sh), lambda *g: (where(*g)[0], parts - 1 - where(*g)[1], 0)),
                 pl.BlockSpec((None, 2 * sh, PREP_LAGS * ln), lambda *g: (where(*g)[0], 0, where(*g)[1])),
                 pl.BlockSpec((None, PREP_LAGS * ln, ln), lambda *g: (where(*g)[0], where(*g)[1], 0)),
                 tile_in(2 * sh, ln), tile_in(SUBLANES, 2 * sh)]
    out_shape = [jax.ShapeDtypeStruct((tiles, PLANES * ln, 2 * sh), BF16),
                 jax.ShapeDtypeStruct((tiles, 2 * sh, PLANES * ln), BF16),
                 jax.ShapeDtypeStruct((tiles, PLANES * ln, ln), BF16),
                 jax.ShapeDtypeStruct((tiles, 2 * sh, ln), BF16),
                 jax.ShapeDtypeStruct((tiles, SUBLANES, 2 * sh), F32)]
    return (lam_row, bt(b_re), bt(b_im), ct(c_re), ct(c_im)), in_specs, out_specs, out_shape, tiles * parts


def _inproj_kernel(first_keep, scale, x_ref, perm_ref, wqkv_ref, wu_ref, *rest):
    sample_in, prep_in = rest[:7], rest[7:12]
    q_ref, k_ref, v_ref, u_ref, kl_ref, vl_ref, sample_out = rest[12:19]
    prep_out, slab_ref = rest[19:24], rest[24]
    slabs = slab_ref.shape[0]
    prow = q_ref.shape[1]
    width = u_ref.shape[-1]
    xp = jnp.dot(perm_ref[...], x_ref[...].astype(BF16), preferred_element_type=F32).astype(BF16)

    def to_planes(val, out_ref):
        for r in range(PLANES):
            out_ref[r] = val[r * prow:(r + 1) * prow].astype(out_ref.dtype)

    sq_ref, skn_ref, svn_ref, skt_ref, svt_ref, stb_ref, stn_ref = sample_in
    to_planes(jnp.dot(xp, wqkv_ref[:, 0:width], preferred_element_type=F32) * scale, q_ref)
    sample_scores = _sample_scores(sq_ref, skn_ref, skt_ref)
    k = jnp.dot(xp, wqkv_ref[:, width:2 * width], preferred_element_type=F32)
    to_planes(k, k_ref)
    v = jnp.dot(xp, wqkv_ref[:, 2 * width:3 * width], preferred_element_type=F32)
    to_planes(v, v_ref)
    _sample_outputs(sample_scores, svn_ref, svt_ref, stb_ref, stn_ref, sample_out)
    to_planes(jnp.dot(xp, wu_ref[...], preferred_element_type=F32), u_ref)
    unit = pl.program_id(0) * pl.num_programs(1) + pl.program_id(1)
    _ssm_prep_part(unit % (PLANES // PREP_LAGS), *prep_in, *prep_out)

    @pl.when(pl.program_id(1) >= first_keep)
    def _():
        def token_order_t(val):
            for r in range(PLANES):
                for c in range(slabs):
                    slab_ref[c, pl.ds(r, prow, stride=PLANES), :] = val[r * prow:(r + 1) * prow,
                                                                        c * LANES:(c + 1) * LANES]
            return jnp.concatenate([slab_ref[c] for c in range(slabs)], axis=1).T
        kl_ref[...] = token_order_t(k)
        vl_ref[...] = token_order_t(v)


def _inproj_prompt(x, w_bf, keep, sample, ssm_params):
    bsz, seq, d_model = x.shape
    width = w_bf.shape[1] // 6
    rows = seq // PLANES
    tile = ROW_TILE // 2
    prow = tile // PLANES
    first_keep = (seq - keep) // tile
    steps = seq // tile
    unit = lambda b, i: b * steps + i
    sample_specs, sample_out_spec, sample_out = _attention_sample_specs(*sample, unit)
    assert bsz * steps == sample[0].shape[0] * (width // (SAMPLE_PAIRS * LANES)), "one sample unit per grid step"
    prep_args, prep_in, prep_out, prep_shapes, prep_units = _ssm_prep(*ssm_params, unit)
    assert prep_units == bsz * steps, "one slice of the state-space weight prep per grid step"
    plane = lambda dt: jax.ShapeDtypeStruct((bsz, PLANES, rows, width), dt)
    plane_spec = pl.BlockSpec((None, PLANES, prow, width), lambda b, i: (b, 0, i, 0))
    last = jax.ShapeDtypeStruct((bsz, width, keep), F32)
    last_spec = pl.BlockSpec((None, width, tile), lambda b, i: (b, 0, jnp.maximum(i - first_keep, 0)))
    perm = np.zeros((tile, tile), np.float32)
    perm[np.arange(tile), PLANES * (np.arange(tile) % prow) + np.arange(tile) // prow] = 1.0
    return pl.pallas_call(
        functools.partial(_inproj_kernel, first_keep, HEAD_DIM ** -0.5 * LOG2E),
        grid=(bsz, steps),
        in_specs=[pl.BlockSpec((None, tile, d_model), lambda b, i: (b, i, 0)),
                  pl.BlockSpec((tile, tile), lambda b, i: (0, 0), pipeline_mode=ONCE),
                  pl.BlockSpec((d_model, 3 * width), lambda b, i: (0, 0), pipeline_mode=ONCE),
                  pl.BlockSpec((d_model, width), lambda b, i: (0, 4), pipeline_mode=ONCE)] + sample_specs + prep_in,
        out_specs=[plane_spec, plane_spec, plane_spec, plane_spec, last_spec, last_spec, sample_out_spec] + prep_out,
        out_shape=[plane(F32), plane(F32), plane(F32), plane(BF16), last, last, sample_out] + prep_shapes,
        scratch_shapes=[pltpu.VMEM((width // LANES, tile, LANES), F32)],
        compiler_params=_cparams(("arbitrary", "arbitrary"), INPROJ_VMEM_LIMIT),
        name="inproj_prompt",
    )(x, jnp.asarray(perm, BF16), w_bf, w_bf, sample[0], sample[0], sample[0], *sample[1:], *prep_args)


ATTN_UNROLL = 2


def _attn_group_rows(dil, grp):
    npl = PLANES // dil
    qrows = BLK // npl
    out = set()
    for un in range(ATTN_UNROLL):
        res, sub = divmod(grp * ATTN_UNROLL + un, BLK // qrows)
        out |= {(res + dil * i, r) for i in range(npl) for r in range(sub * qrows, (sub + 1) * qrows)}
    return out


def _attn_kernel(n_cast, q_ref, kp_ref, kc_ref, vp_ref, vc_ref, bias_ref, *rest):
    cast_in, o_ref, cast_out = rest[:n_cast], rest[n_cast], rest[n_cast + 1:2 * n_cast + 1]
    acc_ref, m_ref, l_ref, s_ref, p_ref, mn_ref = rest[2 * n_cast + 1:]
    for src, dst in zip(cast_in, cast_out):
        dst[...] = src[...].astype(BF16)
    sup = pl.program_id(2)
    is_a = lax.broadcasted_iota(jnp.int32, (BLK, LANES), 1) < HEAD_DIM

    def raw_scores(q, k):
        kb = k.astype(BF16)
        out = []
        for head in range(2):
            qm = jnp.where(is_a if head == 0 else jnp.logical_not(is_a), q, 0.0).astype(BF16)
            out.append(lax.dot_general(qm, kb, (((1,), (1,)), ((), ())), preferred_element_type=F32))
        return out

    def both(a, b):
        return jnp.where(is_a, a, b)

    def weights(s, table, m_old):
        m_new, p = [], []
        for h in range(2):
            sh = s[h] + bias_ref[h, table]
            mb = jnp.broadcast_to(jnp.max(sh, axis=1, keepdims=True), (BLK, LANES))
            mh = mb if m_old is None else jnp.maximum(m_old[h], mb)
            m_new.append(mh)
            p.append(jnp.exp2(sh - jnp.concatenate([mh, mh], axis=1)).astype(BF16))
        return m_new, p

    ones = jnp.ones((2 * BLK, LANES), BF16)

    def combine(m_new, p, v, state):
        vext = jnp.concatenate([v.astype(BF16), ones], axis=1)
        res = [jnp.dot(p[h], vext, preferred_element_type=F32) for h in range(2)]
        acc_new = both(res[0][:, :LANES], res[1][:, :LANES])
        l_new = [res[h][:, LANES:] for h in range(2)]
        if state is not None:
            alpha = [jnp.exp2(state[0][h] - m_new[h]) for h in range(2)]
            l_new = [alpha[h] * state[1][h] + l_new[h] for h in range(2)]
            acc_new = both(alpha[0], alpha[1]) * state[2] + acc_new
        return m_new, l_new, acc_new

    cat = lambda parts: parts[0] if len(parts) == 1 else jnp.concatenate(parts, axis=0)

    def gather(ref, planes, start, size, *lead):
        return cat([ref[(*lead, pln, pl.ds(start, size), slice(None))] for pln in planes])

    def pattern_stages(a, dil):
        npl = PLANES // dil
        qrows = BLK // npl
        per_res = BLK // qrows

        def geometry(grp, un):
            res, sub = divmod(grp * ATTN_UNROLL + un, per_res)
            off = sub * qrows
            table = 2 * a + (jnp.where(sup == 0, 1, 0) if sub == 0 else 0)
            return [res + dil * i for i in range(npl)], off, table

        def keys(prev_ref, cur_ref, planes, off):
            if off:
                return cat([cur_ref[pln, off - qrows:off + qrows, :] for pln in planes])
            return cat([part for pln in planes for part in (prev_ref[pln, BLK - qrows:BLK, :], cur_ref[pln, 0:qrows, :])])

        def stage_scores(grp, slot):
            stores = []
            for un in range(ATTN_UNROLL):
                planes, off, _ = geometry(grp, un)
                s = raw_scores(gather(q_ref, planes, off, qrows), keys(kp_ref, kc_ref, planes, off))
                stores += [(s_ref, (slot, un, h), s[h]) for h in range(2)]
            return stores

        def stage_softmax(grp, slot):
            stores = []
            for un in range(ATTN_UNROLL):
                planes, off, table = geometry(grp, un)
                m_old = None if a == 0 else [gather(m_ref, planes, off, qrows, h) for h in range(2)]
                m_new, p = weights([s_ref[slot, un, h] for h in range(2)], table, m_old)
                for h in range(2):
                    stores += [(p_ref, (slot, un, h), p[h]), (mn_ref, (slot, un, h), m_new[h])]
            return stores

        def stage_values(grp, slot):
            stores = []
            for un in range(ATTN_UNROLL):
                planes, off, _ = geometry(grp, un)
                old = None if a == 0 else ([gather(m_ref, planes, off, qrows, h) for h in range(2)],
                                           [gather(l_ref, planes, off, qrows, h) for h in range(2)],
                                           gather(acc_ref, planes, off, qrows))
                m_new, l_new, acc_new = combine([mn_ref[slot, un, h] for h in range(2)],
                                                [p_ref[slot, un, h] for h in range(2)],
                                                keys(vp_ref, vc_ref, planes, off), old)
                for i, pln in enumerate(planes):
                    part = slice(i * qrows, (i + 1) * qrows)
                    rows = (pln, pl.ds(off, qrows), slice(None))
                    for h in range(2):
                        stores += [(m_ref, (h,) + rows, m_new[h][part]), (l_ref, (h,) + rows, l_new[h][part])]
                    stores.append((acc_ref, rows, acc_new[part]))
            return stores

        return stage_scores, stage_softmax, stage_values

    groups = PLANES // ATTN_UNROLL
    work = [(stages, grp) for stages in (pattern_stages(a, dil) for a, dil in enumerate(DILATIONS))
            for grp in range(groups)]
    rows_of = [_attn_group_rows(dil, grp) for dil in DILATIONS for grp in range(groups)]
    assert all(not (rows_of[j] & rows_of[j - 1]) for j in range(1, len(work))), "groups sharing a step overlap"
    for step in range(len(work) + 2):
        stores = []
        for stage, lag in ((2, 2), (1, 1), (0, 0)):
            j = step - lag
            if 0 <= j < len(work):
                stages, grp = work[j]
                stores += stages[stage](grp, j % 2)
        for ref, idx, val in stores:
            ref[idx] = val
    for pln in range(PLANES):
        o_ref[pl.ds(pln, BLK, stride=PLANES), :] = acc_ref[pln] / both(l_ref[0, pln], l_ref[1, pln])


def _attention_prompt(q, k, v, bias, to_cast):
    bsz, _, rows, width = q.shape
    pairs = width // LANES
    ntab = bias.shape[1]
    sups = rows // BLK
    steps = bsz * pairs * sups
    unit = lambda b, h, s: (b * pairs + h) * sups + s
    cur = pl.BlockSpec((None, PLANES, BLK, LANES), lambda b, h, s: (b, 0, s, h))
    prev = pl.BlockSpec((None, PLANES, BLK, LANES), lambda b, h, s: (b, 0, jnp.maximum(s - 1, 0), h))
    cast_in = [pl.BlockSpec((m.shape[0] // steps, w), lambda b, h, s, c=c: (unit(b, h, s), c)) for m, w, c in to_cast]
    cast_out = [pl.BlockSpec((m.shape[0] // steps, w), lambda b, h, s: (unit(b, h, s), 0)) for m, w, _ in to_cast]
    return pl.pallas_call(
        functools.partial(_attn_kernel, len(to_cast)),
        grid=(bsz, pairs, sups),
        in_specs=[cur, prev, cur, prev, cur,
                  pl.BlockSpec((2, ntab, BLK, 2 * BLK), lambda b, h, s: (h, 0, 0, 0))] + cast_in,
        out_specs=[pl.BlockSpec((None, PLANES * BLK, LANES), lambda b, h, s: (b, s, h))] + cast_out,
        out_shape=[jax.ShapeDtypeStruct((bsz, PLANES * rows, width), F32)]
        + [jax.ShapeDtypeStruct((m.shape[0], w), BF16) for m, w, _ in to_cast],
        scratch_shapes=[pltpu.VMEM((PLANES, BLK, LANES), F32),
                        pltpu.VMEM((2, PLANES, BLK, LANES), F32),
                        pltpu.VMEM((2, PLANES, BLK, LANES), F32),
                        pltpu.VMEM((2, ATTN_UNROLL, 2, BLK, 2 * BLK), F32),
                        pltpu.VMEM((2, ATTN_UNROLL, 2, BLK, 2 * BLK), BF16),
                        pltpu.VMEM((2, ATTN_UNROLL, 2, BLK, LANES), F32)],
        compiler_params=_cparams(("arbitrary", "arbitrary", "arbitrary")),
        name="attention_prompt",
    )(q, k, k, v, v, bias, *[m for m, _, _ in to_cast])


SSM_COLS = 4


def _ssm_kernel(u_ref, ws_ref, wx_ref, mlag_ref, coef_ref, d_ref, y_ref, sre_ref, sim_ref,
                ucat_ref, s_ref, x_ref, m_ref):
    chunks = u_ref.shape[1]

    @pl.when(pl.program_id(1) == 0)
    def _():
        zero_blk = jnp.zeros((LANES, LANES), BF16)
        for s in range(PLANES):
            for t in range(PLANES):
                blk = mlag_ref[(t - s) * LANES:(t - s + 1) * LANES, :] if t >= s else zero_blk
                m_ref[s * LANES:(s + 1) * LANES, t * LANES:(t + 1) * LANES] = blk

    for s in range(PLANES):
        ucat_ref[:, s * LANES:(s + 1) * LANES] = u_ref[s]
    s_ref[...] = jnp.dot(ucat_ref[...], ws_ref[...], preferred_element_type=F32)
    a_re = coef_ref[1:2, 0:STATE_HALF]
    a_im = coef_ref[1:2, STATE_HALF:2 * STATE_HALF]

    def step(c, carry):
        xr, xi = carry
        x_ref[pl.ds(c, 1), 0:STATE_HALF] = xr
        x_ref[pl.ds(c, 1), STATE_HALF:2 * STATE_HALF] = xi
        sr = s_ref[pl.ds(c, 1), 0:STATE_HALF]
        si = s_ref[pl.ds(c, 1), STATE_HALF:2 * STATE_HALF]
        return a_re * xr - a_im * xi + sr, a_re * xi + a_im * xr + si

    zero = jnp.zeros((1, STATE_HALF), F32)
    xr, xi = lax.fori_loop(0, chunks, step, (zero, zero))
    sre_ref[...] = xr
    sim_ref[...] = xi

    xb = x_ref[...].astype(BF16)
    d = d_ref[...]
    for g in range(PLANES // SSM_COLS):
        cols = slice(g * SSM_COLS * LANES, (g + 1) * SSM_COLS * LANES)
        used = (g + 1) * SSM_COLS * LANES
        y = jnp.dot(ucat_ref[:, :used], m_ref[:used, cols], preferred_element_type=F32)
        y = y + jnp.dot(xb, wx_ref[:, cols], preferred_element_type=F32)
        for t in range(SSM_COLS):
            tok = g * SSM_COLS + t
            y_ref[pl.ds(tok, chunks, stride=PLANES), :] = (y[:, t * LANES:(t + 1) * LANES]
                                                          + d * u_ref[tok].astype(F32))


def _ssm_prompt(u, ws, wx, mlag, coef, d_skip):
    bsz, _, chunks, width = u.shape
    tiles = width // LANES
    sh = STATE_HALF
    state = jax.ShapeDtypeStruct((bsz, tiles, 1, sh), F32)
    state_spec = pl.BlockSpec((None, None, 1, sh), lambda t, b: (b, t, 0, 0))
    per_tile = lambda r, c: pl.BlockSpec((None, r, c), lambda t, b: (t, 0, 0))
    return pl.pallas_call(
        _ssm_kernel,
        grid=(tiles, bsz),
        in_specs=[pl.BlockSpec((None, PLANES, chunks, LANES), lambda t, b: (b, 0, 0, t)),
                  per_tile(PLANES * LANES, 2 * sh), per_tile(2 * sh, PLANES * LANES),
                  per_tile(PLANES * LANES, LANES), per_tile(SUBLANES, 2 * sh),
                  pl.BlockSpec((1, LANES), lambda t, b: (0, t))],
        out_specs=[pl.BlockSpec((None, PLANES * chunks, LANES), lambda t, b: (b, 0, t)),
                   state_spec, state_spec],
        out_shape=[jax.ShapeDtypeStruct((bsz, PLANES * chunks, width), F32), state, state],
        scratch_shapes=[pltpu.VMEM((chunks, PLANES * LANES), BF16),
                        pltpu.VMEM((chunks, 2 * sh), F32),
                        pltpu.VMEM((chunks, 2 * sh), F32),
                        pltpu.VMEM((PLANES * LANES, PLANES * LANES), BF16)],
        compiler_params=_cparams(("arbitrary", "arbitrary")),
        name="ssm_prompt",
    )(u, ws, wx, mlag, coef, d_skip)


def _epilogue_kernel(alpha, attn_ref, y_ref, x_ref, wga_ref, wgs_ref, wglu_ref, bglu_ref, wo_ref, bo_ref,
                     g_ref, b_ref, o_ref):
    x = x_ref[...]
    xb = x.astype(BF16)
    residual = alpha * x + bo_ref[...]
    g_attn = jnp.dot(xb, wga_ref[...], preferred_element_type=F32)
    br_a = (attn_ref[...] * jax.nn.silu(g_attn)).astype(BF16)
    g_ssm = jnp.dot(xb, wgs_ref[...], preferred_element_type=F32)
    z = jax.nn.gelu(y_ref[...])
    gate = jax.nn.sigmoid(jnp.dot(z.astype(BF16), wglu_ref[...], preferred_element_type=F32) + bglu_ref[...])
    br_s = (z * gate * jax.nn.silu(g_ssm)).astype(BF16)
    t = residual + jnp.dot(jnp.concatenate([br_a, br_s], axis=1), wo_ref[...], preferred_element_type=F32)
    mu = jnp.mean(t, axis=-1, keepdims=True)
    var = jnp.mean(jnp.square(t - mu), axis=-1, keepdims=True)
    o_ref[...] = (t - mu) * lax.rsqrt(var + LN_EPS) * g_ref[...] + b_ref[...]


def _epilogue(alpha, name, attn, y, x, weights, tile):
    lead = attn.shape[:-2]
    rows, width = attn.shape[-2:]
    d_model = x.shape[-1]
    grid = lead + (rows // tile,)
    none = (None,) * len(lead)
    tiled = lambda w: pl.BlockSpec(none + (tile, w), lambda *g: g + (0,))
    const = lambda shape, col=0: pl.BlockSpec(shape, lambda *g: (0, col), pipeline_mode=ONCE)
    return pl.pallas_call(
        functools.partial(_epilogue_kernel, alpha),
        grid=grid,
        in_specs=[tiled(width), tiled(width), tiled(d_model),
                  const((d_model, width), 3), const((d_model, width), 5),
                  const((width, width)), const((1, width)), const((2 * width, d_model)), const((1, d_model)),
                  const((1, d_model)), const((1, d_model))],
        out_specs=tiled(d_model),
        out_shape=jax.ShapeDtypeStruct(x.shape, F32),
        compiler_params=_cparams(("arbitrary",) * len(grid)),
        name=name,
    )(attn, y, x, *weights)


def _inproj_sample_kernel(scale, x_ref, w_ref, o_ref):
    acc = jnp.dot(x_ref[...].astype(BF16), w_ref[...], preferred_element_type=F32)
    o_ref[...] = acc * jnp.where(pl.program_id(0) == 0, scale, 1.0)


def _inproj_sample(x, w_bf):
    rows, d_model = x.shape
    width = w_bf.shape[1] // 6
    return pl.pallas_call(
        functools.partial(_inproj_sample_kernel, HEAD_DIM ** -0.5),
        grid=(4,),
        in_specs=[pl.BlockSpec((rows, d_model), lambda j: (0, 0)),
                  pl.BlockSpec((d_model, width), lambda j: (0, j + j // 3))],
        out_specs=pl.BlockSpec((rows, width), lambda j: (0, j)),
        out_shape=jax.ShapeDtypeStruct((rows, 4 * width), F32),
        compiler_params=_cparams(("arbitrary",)),
        name="inproj_sample",
    )(x, w_bf)


SAMPLE_PAIRS = 4


_NT = (((1,), (1,)), ((), ()))


def _sample_scores(q_ref, kn_ref, kt_ref):
    s_len = q_ref.shape[0]
    buf = kt_ref.shape[2]
    is_a = lax.broadcasted_iota(jnp.int32, (s_len, LANES), 1) < HEAD_DIM
    out = []
    for pp in range(SAMPLE_PAIRS):
        lanes = slice(pp * LANES, (pp + 1) * LANES)
        qp = q_ref[:, lanes]
        q2 = jnp.concatenate([jnp.where(is_a, qp, 0.0), jnp.where(is_a, 0.0, qp)], axis=0)
        kt = kt_ref[2 * pp:2 * pp + 2].reshape(2 * HEAD_DIM, buf).astype(BF16)
        kn = kn_ref[:, lanes]
        s_new = [jnp.sum(q2 * kn[j:j + 1, :], axis=1, keepdims=True) for j in range(s_len)]
        out.append((jnp.dot(q2.astype(BF16), kt, preferred_element_type=F32), s_new))
    return out


def _sample_outputs(scores, vn_ref, vt_ref, tb_ref, tn_ref, o_ref):
    s_len = vn_ref.shape[0]
    buf = vt_ref.shape[2]
    npat = tb_ref.shape[0]
    is_a = lax.broadcasted_iota(jnp.int32, (s_len, LANES), 1) < HEAD_DIM
    for pp, (s_buf, s_new) in enumerate(scores):
        lanes = slice(pp * LANES, (pp + 1) * LANES)
        rows = slice(pp * 2 * s_len, (pp + 1) * 2 * s_len)
        vt = vt_ref[2 * pp:2 * pp + 2].reshape(2 * HEAD_DIM, buf).astype(BF16)
        vn = vn_ref[:, lanes]
        z_buf = [s_buf + tb_ref[a, rows, :] for a in range(npat)]
        z_new = [[s_new[j] + tn_ref[a, rows, j:j + 1] for a in range(npat)] for j in range(s_len)]
        top = functools.reduce(jnp.maximum, [jnp.max(z, axis=1, keepdims=True) for z in z_buf]
                               + [z for zs in z_new for z in zs])
        w_buf = functools.reduce(jnp.add, [jnp.exp(z - top) for z in z_buf])
        w_new = [functools.reduce(jnp.add, [jnp.exp(z - top) for z in zs]) for zs in z_new]
        den = jnp.sum(w_buf, axis=1, keepdims=True) + functools.reduce(jnp.add, w_new)
        o = lax.dot_general(vt, w_buf.astype(BF16), _NT, preferred_element_type=F32).T
        o = (o + functools.reduce(jnp.add, [w_new[j] * vn[j:j + 1, :] for j in range(s_len)])) / den
        o_ref[:, lanes] = jnp.where(is_a, o[0:s_len], o[s_len:2 * s_len])


def _attention_sample_specs(h, kt, vt, tb, tn, unit_of):
    bsz, s_len, _ = h.shape
    heads, _, buf = kt.shape[1:]
    width = heads * HEAD_DIM
    gw = SAMPLE_PAIRS * LANES
    per_w = width // gw
    trows = SAMPLE_PAIRS * 2 * s_len
    where = lambda *g: divmod(unit_of(*g), per_w)
    new = lambda c: pl.BlockSpec((None, s_len, gw), lambda *g: (where(*g)[0], 0, c * per_w + where(*g)[1]))
    cache = pl.BlockSpec((None, 2 * SAMPLE_PAIRS, HEAD_DIM, buf), lambda *g: (where(*g)[0], where(*g)[1], 0, 0))
    specs = [new(0), new(1), new(2), cache, cache,
             pl.BlockSpec((tb.shape[0], trows, buf), lambda *g: (0, where(*g)[1], 0)),
             pl.BlockSpec((tn.shape[0], trows, LANES), lambda *g: (0, where(*g)[1], 0))]
    out_spec = pl.BlockSpec((None, s_len, gw), lambda *g: (where(*g)[0], 0, where(*g)[1]))
    return specs, out_spec, jax.ShapeDtypeStruct((bsz, s_len, width), F32)


def _ssm_sample_kernel(s_len, u_ref, b0_ref, c0_ref, coef_ref, d_ref, x0r_ref, x0i_ref,
                       y_ref, sre_ref, sim_ref, bu_ref, xs_ref):
    bsz = x0r_ref.shape[0]
    bu = jnp.dot(u_ref[...].astype(BF16), b0_ref[...], preferred_element_type=F32)
    slabs = bu_ref.shape[0]
    for c in range(slabs):
        bu_ref[c] = bu[:, c * LANES:(c + 1) * LANES]
    a_re = coef_ref[0:1, 0:STATE_HALF]
    a_im = coef_ref[0:1, STATE_HALF:2 * STATE_HALF]
    half = slabs // 2
    xr, xi = x0r_ref[...], x0i_ref[...]
    for s in range(s_len):
        step = lambda c: bu_ref[c, pl.ds(s, bsz, stride=s_len), :]
        br = jnp.concatenate([step(c) for c in range(half)], axis=1)
        bi = jnp.concatenate([step(c) for c in range(half, slabs)], axis=1)
        xr, xi = a_re * xr - a_im * xi + br, a_re * xi + a_im * xr + bi
        xs_ref[s * bsz:(s + 1) * bsz, :] = jnp.concatenate([xr, xi], axis=1)
    sre_ref[...] = xr
    sim_ref[...] = xi
    y = jnp.dot(xs_ref[...].astype(BF16), c0_ref[...], preferred_element_type=F32)
    d = d_ref[...]
    for s in range(s_len):
        y_ref[pl.ds(s, bsz, stride=s_len), :] = (y[s * bsz:(s + 1) * bsz, :]
                                                 + d * u_ref[pl.ds(s, bsz, stride=s_len), :])


def _ssm_sample(h, ws, c0, coef, d_skip, x0_re, x0_im, s_len):
    rows = h.shape[0]
    bsz = rows // s_len
    tiles = ws.shape[0]
    sh = STATE_HALF
    u_col0 = 3 * (h.shape[1] // 4) // LANES
    st_spec = pl.BlockSpec((bsz, sh), lambda t: (0, t))
    state = jax.ShapeDtypeStruct((bsz, tiles * sh), F32)
    return pl.pallas_call(
        functools.partial(_ssm_sample_kernel, s_len),
        grid=(tiles,),
        in_specs=[pl.BlockSpec((rows, LANES), lambda t: (0, u_col0 + t)),
                  pl.BlockSpec((None, LANES, 2 * sh), lambda t: (t, PLANES - 1, 0)),
                  pl.BlockSpec((None, 2 * sh, LANES), lambda t: (t, 0, 0)),
                  pl.BlockSpec((None, SUBLANES, 2 * sh), lambda t: (t, 0, 0)),
                  pl.BlockSpec((1, LANES), lambda t: (0, t)),
                  st_spec, st_spec],
        out_specs=[pl.BlockSpec((rows, LANES), lambda t: (0, t)), st_spec, st_spec],
        out_shape=[jax.ShapeDtypeStruct((rows, tiles * LANES), F32), state, state],
        scratch_shapes=[pltpu.VMEM((2 * sh // LANES, rows, LANES), F32),
                        pltpu.VMEM((rows, 2 * sh), F32)],
        compiler_params=_cparams(("arbitrary",)),
        name="ssm_sample",
    )(h, ws, c0, coef, d_skip, x0_re, x0_im)


def kernel(x_prompt, x_sample, cache_k, cache_v, state_ssm_re, state_ssm_im, w_in, w_out, b_out, rel_bias,
           lam_re, lam_im, log_dt, b_re, b_im, c_re, c_im, d_skip, w_glu, b_glu, ln_g, ln_b):
    depth = w_in.shape[0]
    assert depth == 1, "one layer per step"
    bsz, seq, d_model = x_prompt.shape
    dbsz, s_len, _ = x_sample.shape
    buf, heads = cache_k.shape[2], cache_k.shape[3]
    width = heads * HEAD_DIM
    groups, nstate = lam_re.shape[1], lam_re.shape[2]
    keep = min(MAX_DISTANCE, seq)
    assert seq % (PLANES * BLK) == 0 and seq >= 2 * PLANES * BLK and keep % ROW_TILE == 0
    assert buf == KPER * max(DILATIONS) and s_len <= min(DILATIONS[:-1])
    assert nstate == SSM_STATE and width == groups * SSM_CH and heads % (2 * SAMPLE_PAIRS) == 0
    alpha = (2 * depth) ** 0.25
    npat = len(DILATIONS)

    w_bf = w_in[0].astype(BF16)
    to_cast = [(w_glu[0], width, 0), (w_out[0], d_model, 0)]
    row = lambda v: v.reshape(1, -1)
    d_row = row(d_skip[0])

    rbt = rel_bias.T
    ptab = _bias_tables(jnp.asarray(_prompt_bucket_tables().reshape(1, -1)), rbt, BLK * 2 * BLK, LOG2E)
    ptab = ptab.reshape(heads, 2 * npat, BLK, 2 * BLK)
    key_w = buf + 2 * LANES
    stab = _bias_tables(jnp.asarray(_sample_bucket_tables(buf, s_len, key_w)), rbt, npat * s_len * key_w)
    stab = jnp.transpose(stab.reshape(heads, npat, s_len, key_w), (1, 0, 2, 3)).reshape(npat, heads * s_len, key_w)
    stab_buf, stab_new = stab[:, :, :buf], stab[:, :, buf:buf + LANES]
    ssm_params = (lam_re[0], lam_im[0], log_dt[0], b_re[0], b_im[0], c_re[0], c_im[0])

    xs = x_sample.reshape(dbsz * s_len, d_model)
    hs = _inproj_sample(xs, w_bf)
    pos_minor = lambda c: jnp.transpose(c[0], (0, 2, 3, 1))
    sample = (hs.reshape(dbsz, s_len, -1), pos_minor(cache_k), pos_minor(cache_v), stab_buf, stab_new)

    q, k, v, u, kl_t, vl_t, attn_s, ws, wx, m_intra, c0, coef = _inproj_prompt(x_prompt, w_bf, keep, sample,
                                                                              ssm_params)
    attn, w_glu_bf, w_out_bf = _attention_prompt(q, k, v, ptab, to_cast)
    weights = (w_bf, w_bf, w_glu_bf, row(b_glu[0]), w_out_bf, row(b_out[0]), row(ln_g[0]), row(ln_b[0]))
    y, sre_p, sim_p = _ssm_prompt(u, ws, wx, m_intra, coef, d_row)
    y_prompt = _epilogue(alpha, "epilogue_prompt", attn, y, x_prompt, weights, ROW_TILE // 2)
    last = lambda t: jnp.transpose(t.reshape(bsz, heads, HEAD_DIM, keep), (0, 3, 1, 2))[None]
    st_shape = (1, bsz, groups, nstate)

    y_s, sre_s, sim_s = _ssm_sample(hs, ws, c0, coef, d_row,
                                    state_ssm_re[0].astype(F32).reshape(dbsz, groups * nstate),
                                    state_ssm_im[0].astype(F32).reshape(dbsz, groups * nstate), s_len)
    y_sample = _epilogue(alpha, "epilogue_sample", attn_s.reshape(dbsz * s_len, width), y_s, xs, weights,
                         dbsz * s_len)
    new_shape = (1, dbsz, s_len, heads, HEAD_DIM)
    sst_shape = (1, dbsz, groups, nstate)
    return (y_prompt, y_sample.reshape(dbsz, s_len, d_model), last(kl_t), last(vl_t),
            sre_p.reshape(st_shape), sim_p.reshape(st_shape),
            hs[:, width:2 * width].reshape(new_shape), hs[:, 2 * width:3 * width].reshape(new_shape),
            sre_s.reshape(sst_shape), sim_s.reshape(sst_shape))
```
